```python
import numpy as np
import jax
import jax.numpy as jnp
from jax import lax

D_MODEL = 4096
BATCH = 1
SEQ = 8192
DEPTH = 1
DEC_BATCH = 128
DEC_SEQ = 8
PAST_LEN = 8192
PAGE_SIZE = 128

HEAD_DIM = 64
N_Q_HEADS = 32
N_KV_HEADS = 8
Q_PER_KV = N_Q_HEADS // N_KV_HEADS
ATTN_WIDTH = N_Q_HEADS * HEAD_DIM
KV_WIDTH = N_KV_HEADS * HEAD_DIM
WINDOW = 128
ROT_DIM = HEAD_DIM // 4
ROPE_THETA = 500000.0
POOL_WINDOWS = (2, 4, 8, 16)
N_POOL_GROUPS = 4
POOL_WIDTH = D_MODEL // 2
POOL_GROUP = POOL_WIDTH // N_POOL_GROUPS
POOL_BUF = max(POOL_WINDOWS) - 1
IN_WIDTH = ATTN_WIDTH + 2 * KV_WIDTH + POOL_WIDTH + 2 * D_MODEL
N_GROUPS = 4
EXPERTS_PER_GROUP = 4
N_EXPERTS = N_GROUPS * EXPERTS_PER_GROUP
TOP_K = 2
D_EXPERT = D_MODEL // 4
PLE_DIM = 256
RMS_EPS = 1e-6

kernel_name = 'hybrid_swa_sink_pool_hmoe_step'

F32 = jnp.float32
NEG_INF = -1e30


def rmsnorm(x, g):
    xf = x.astype(F32)
    y = xf * lax.rsqrt(jnp.mean(xf * xf, axis=-1, keepdims=True) + RMS_EPS)
    return (y * g.astype(F32)).astype(x.dtype)


def partial_rope(x, pos):
    half = ROT_DIM // 2
    inv_freq = ROPE_THETA ** (-jnp.arange(half, dtype=F32) * (2.0 / ROT_DIM))
    ang = pos[:, None] * inv_freq[None, :]
    cos = jnp.cos(ang)[None, :, None, :]
    sin = jnp.sin(ang)[None, :, None, :]
    xr = x[..., :ROT_DIM].astype(F32)
    x1, x2 = xr[..., :half], xr[..., half:]
    rot = jnp.concatenate([x1 * cos - x2 * sin, x2 * cos + x1 * sin], axis=-1)
    return jnp.concatenate([rot.astype(x.dtype), x[..., ROT_DIM:]], axis=-1)


def sink_attention(q, k, v, mask, sinks):
    s = jnp.einsum('bnqhgd,bnkhd->bnhgqk', q, k, preferred_element_type=F32) * (HEAD_DIM ** -0.5)
    s = jnp.where(mask[None, :, None, None], s, NEG_INF)
    sink_col = jnp.broadcast_to(sinks.astype(F32)[None, None, :, :, None, None], s.shape[:-1] + (1,))
    probs = jax.nn.softmax(jnp.concatenate([s, sink_col], axis=-1), axis=-1)[..., :-1]
    return jnp.einsum('bnhgqk,bnkhd->bnqhgd', probs.astype(v.dtype), v)


def band_attention_prompt(q, k, v, sinks):
    B, S = q.shape[0], q.shape[1]
    nb = S // WINDOW
    qb = q.reshape(B, nb, WINDOW, N_KV_HEADS, Q_PER_KV, HEAD_DIM)
    kb = k.reshape(B, nb, WINDOW, N_KV_HEADS, HEAD_DIM)
    vb = v.reshape(B, nb, WINDOW, N_KV_HEADS, HEAD_DIM)

    def with_prev(t):
        prev = jnp.concatenate([jnp.zeros_like(t[:, :1]), t[:, :-1]], axis=1)
        return jnp.concatenate([prev, t], axis=2)

    i = np.arange(WINDOW)[None, :, None]
    j = np.arange(2 * WINDOW)[None, None, :]
    n = np.arange(nb)[:, None, None]
    dist = i + WINDOW - j
    mask = (dist >= 0) & (dist < WINDOW) & (n * WINDOW - WINDOW + j >= 0)
    o = sink_attention(qb, with_prev(kb), with_prev(vb), jnp.asarray(mask), sinks)
    return o.reshape(B, S, ATTN_WIDTH)


def window_attention_sample(q, k, v, k_prev, v_prev, sinks):
    B, T = q.shape[0], q.shape[1]
    C = k_prev.shape[1]
    k_all = jnp.concatenate([k_prev, k], axis=1)
    v_all = jnp.concatenate([v_prev, v], axis=1)
    qpos = np.arange(T) + C
    kpos = np.arange(C + T)
    dist = qpos[:, None] - kpos[None, :]
    mask = (dist >= 0) & (dist < WINDOW)
    o = sink_attention(q[:, None], k_all[:, None], v_all[:, None], jnp.asarray(mask[None]), sinks)
    return o.reshape(B, T, ATTN_WIDTH), k_all, v_all


def multiscale_pool(full, T, w_grp, scale):
    B, L, C = full.shape
    P = L - T
    ff = full.astype(F32)
    cs = jnp.concatenate([jnp.zeros((B, 1, C), F32), jnp.cumsum(ff, axis=1)], axis=1)
    ends = np.arange(P + 1, L + 1)
    parts = []
    for g, w in enumerate(POOL_WINDOWS):
        lo = np.maximum(ends - w, 0)
        csg = cs[:, :, g * POOL_GROUP:(g + 1) * POOL_GROUP]
        cnt = jnp.asarray((ends - lo).astype(np.float32))[None, :, None]
        parts.append((csg[:, ends] - csg[:, lo]) / cnt)
    pooled = (jnp.concatenate(parts, axis=-1) - ff[:, P:]).astype(full.dtype)
    mixed = jnp.einsum('btgc,gcd->btgd', pooled.reshape(B, T, N_POOL_GROUPS, POOL_GROUP), w_grp)
    return mixed.reshape(B, T, POOL_WIDTH) * scale


def hier_moe(x, w_rg, b_rg, w_re, b_re, w_g, w_u, w_d):
    B, T, D = x.shape
    xf = x.reshape(B * T, D)
    g_prob = jax.nn.softmax(jnp.dot(xf, w_rg).astype(F32) + b_rg.astype(F32), axis=-1)
    g_p, g_idx = lax.top_k(g_prob, 1)
    e_logits = (jnp.dot(xf, w_re).astype(F32) + b_re.astype(F32)).reshape(-1, N_GROUPS, EXPERTS_PER_GROUP)
    in_grp = jnp.take_along_axis(e_logits, g_idx[:, :, None], axis=1)[:, 0]
    e_top, e_idx = lax.top_k(in_grp, TOP_K)
    combine = g_p * jax.nn.softmax(e_top, axis=-1)
    expert_id = g_idx * EXPERTS_PER_GROUP + e_idx
    dense_w = jnp.sum(jax.nn.one_hot(expert_id, N_EXPERTS, dtype=F32) * combine[..., None], axis=1)
    hid = jax.nn.silu(jnp.einsum('nd,edf->nef', xf, w_g)) * jnp.einsum('nd,edf->nef', xf, w_u)
    hid = hid * dense_w[:, :, None].astype(hid.dtype)
    return jnp.einsum('nef,efd->nd', hid, w_d).reshape(B, T, D)


def trunk_layer(h, p_l, pos, k_prev, v_prev, pool_prev, lw, n_win, n_pool):
    B, T = h.shape[0], h.shape[1]
    xn = rmsnorm(h, lw['g_mix'])
    u = jnp.einsum('btd,de->bte', xn, lw['w_in'])
    o1 = ATTN_WIDTH
    o2 = o1 + KV_WIDTH
    o3 = o2 + KV_WIDTH
    o4 = o3 + POOL_WIDTH
    q = partial_rope(u[..., :o1].reshape(B, T, N_Q_HEADS, HEAD_DIM), pos)
    q = q.reshape(B, T, N_KV_HEADS, Q_PER_KV, HEAD_DIM)
    k = partial_rope(u[..., o1:o2].reshape(B, T, N_KV_HEADS, HEAD_DIM), pos)
    v = u[..., o2:o3].reshape(B, T, N_KV_HEADS, HEAD_DIM)
    z = u[..., o3:o4]
    gates = u[..., o4:] + lw['b_gate']
    sinks = lw['attn_sinks'].reshape(N_KV_HEADS, Q_PER_KV)
    if k_prev is None:
        attn = band_attention_prompt(q, k, v, sinks)
        k_all, v_all, z_all = k, v, z
    else:
        attn, k_all, v_all = window_attention_sample(q, k, v, k_prev, v_prev, sinks)
        z_all = jnp.concatenate([pool_prev, z], axis=1)
    pooled = multiscale_pool(z_all, T, lw['w_pool_grp'], lw['pool_scale'])
    a_up = jnp.einsum('bte,ed->btd', attn, lw['w_attn_up'])
    b_up = jnp.einsum('bte,ed->btd', pooled, lw['w_pool_up'])
    merged = jax.nn.sigmoid(gates[..., :D_MODEL]) * a_up + jax.nn.sigmoid(gates[..., D_MODEL:]) * b_up
    h = h + jnp.einsum('btd,de->bte', merged, lw['w_out'])
    h = h + hier_moe(rmsnorm(h, lw['g_ffn']), lw['w_route_group'], lw['b_route_group'],
                     lw['w_route_expert'], lw['b_route_expert'],
                     lw['w_exp_gate'], lw['w_exp_up'], lw['w_exp_down'])
    ple = jnp.einsum('btp,pd->btd', p_l, lw['w_ple_proj'])
    ple_gate = jax.nn.sigmoid(jnp.einsum('btd,de->bte', rmsnorm(h, lw['g_ple']), lw['w_ple_gate']) + lw['b_ple_gate'])
    h = h + ple_gate * ple
    return h, k_all[:, -n_win:], v_all[:, -n_win:], z_all[:, -n_pool:]


def setup_inputs(seed: int = 0) -> dict:
    key = jax.random.key(seed)
    ks = jax.random.split(key, 32)
    n_win = min(WINDOW, PAST_LEN)
    n_pool = min(POOL_BUF, PAST_LEN)

    def nrm(k, shape, scale):
        return jax.random.normal(k, shape, F32) * scale

    return {
        'x_prompt': nrm(ks[0], (BATCH, SEQ, D_MODEL), 1.0),
        'x_sample': nrm(ks[1], (DEC_BATCH, DEC_SEQ, D_MODEL), 1.0),
        'cache_k': nrm(ks[2], (DEPTH, DEC_BATCH, n_win, N_KV_HEADS, HEAD_DIM), 1.0),
        'cache_v': nrm(ks[3], (DEPTH, DEC_BATCH, n_win, N_KV_HEADS, HEAD_DIM), 1.0),
        'state_pool': nrm(ks[4], (DEPTH, DEC_BATCH, n_pool, POOL_WIDTH), 1.0),
        'p_prompt': nrm(ks[5], (DEPTH, BATCH, SEQ, PLE_DIM), 1.0),
        'p_sample': nrm(ks[6], (DEPTH, DEC_BATCH, DEC_SEQ, PLE_DIM), 1.0),
        'g_mix': 1.0 + nrm(ks[7], (DEPTH, D_MODEL), 0.05),
        'w_in': nrm(ks[8], (DEPTH, D_MODEL, IN_WIDTH), D_MODEL ** -0.5),
        'b_gate': nrm(ks[9], (DEPTH, 2 * D_MODEL), 0.02),
        'attn_sinks': nrm(ks[10], (DEPTH, N_Q_HEADS), 0.5),
        'w_pool_grp': nrm(ks[11], (DEPTH, N_POOL_GROUPS, POOL_GROUP, POOL_GROUP), POOL_GROUP ** -0.5),
        'pool_scale': 1.0 + nrm(ks[12], (DEPTH, POOL_WIDTH), 0.1),
        'w_attn_up': nrm(ks[13], (DEPTH, ATTN_WIDTH, D_MODEL), ATTN_WIDTH ** -0.5),
        'w_pool_up': nrm(ks[14], (DEPTH, POOL_WIDTH, D_MODEL), POOL_WIDTH ** -0.5),
        'w_out': nrm(ks[15], (DEPTH, D_MODEL, D_MODEL), D_MODEL ** -0.5),
        'g_ffn': 1.0 + nrm(ks[16], (DEPTH, D_MODEL), 0.05),
        'w_route_group': nrm(ks[17], (DEPTH, D_MODEL, N_GROUPS), D_MODEL ** -0.5),
        'b_route_group': nrm(ks[18], (DEPTH, N_GROUPS), 0.01),
        'w_route_expert': nrm(ks[19], (DEPTH, D_MODEL, N_EXPERTS), D_MODEL ** -0.5),
        'b_route_expert': nrm(ks[20], (DEPTH, N_EXPERTS), 0.01),
        'w_exp_gate': nrm(ks[21], (DEPTH, N_EXPERTS, D_MODEL, D_EXPERT), D_MODEL ** -0.5),
        'w_exp_up': nrm(ks[22], (DEPTH, N_EXPERTS, D_MODEL, D_EXPERT), D_MODEL ** -0.5),
        'w_exp_down': nrm(ks[23], (DEPTH, N_EXPERTS, D_EXPERT, D_MODEL), D_EXPERT ** -0.5),
        'g_ple': 1.0 + nrm(ks[24], (DEPTH, D_MODEL), 0.05),
        'w_ple_gate': nrm(ks[25], (DEPTH, D_MODEL, D_MODEL), D_MODEL ** -0.5),
        'b_ple_gate': nrm(ks[26], (DEPTH, D_MODEL), 0.02),
        'w_ple_proj': nrm(ks[27], (DEPTH, PLE_DIM, D_MODEL), PLE_DIM ** -0.5),
        'g_final': 1.0 + nrm(ks[28], (D_MODEL,), 0.05),
    }


def reference(x_prompt, x_sample, cache_k, cache_v, state_pool, p_prompt, p_sample,
              g_mix, w_in, b_gate, attn_sinks, w_pool_grp, pool_scale, w_attn_up, w_pool_up,
              w_out, g_ffn, w_route_group, b_route_group, w_route_expert, b_route_expert,
              w_exp_gate, w_exp_up, w_exp_down, g_ple, w_ple_gate, b_ple_gate, w_ple_proj, g_final):
    n_win = cache_k.shape[2]
    n_pool = state_pool.shape[2]
    pos_p = jnp.arange(x_prompt.shape[1], dtype=F32)
    pos_s = PAST_LEN + jnp.arange(x_sample.shape[1], dtype=F32)
    hp, hs = x_prompt, x_sample
    kp_l, vp_l, zp_l, ks_l, vs_l, zs_l = [], [], [], [], [], []
    for l in range(DEPTH):
        lw = {
            'g_mix': g_mix[l], 'w_in': w_in[l], 'b_gate': b_gate[l], 'attn_sinks': attn_sinks[l],
            'w_pool_grp': w_pool_grp[l], 'pool_scale': pool_scale[l], 'w_attn_up': w_attn_up[l],
            'w_pool_up': w_pool_up[l], 'w_out': w_out[l], 'g_ffn': g_ffn[l],
            'w_route_group': w_route_group[l], 'b_route_group': b_route_group[l],
            'w_route_expert': w_route_expert[l], 'b_route_expert': b_route_expert[l],
            'w_exp_gate': w_exp_gate[l], 'w_exp_up': w_exp_up[l], 'w_exp_down': w_exp_down[l],
            'g_ple': g_ple[l], 'w_ple_gate': w_ple_gate[l], 'b_ple_gate': b_ple_gate[l],
            'w_ple_proj': w_ple_proj[l],
        }
        hp, kp, vp, zp = trunk_layer(hp, p_prompt[l], pos_p, None, None, None, lw, n_win, n_pool)
        hs, ksn, vsn, zsn = trunk_layer(hs, p_sample[l], pos_s, cache_k[l], cache_v[l], state_pool[l], lw, n_win, n_pool)
        kp_l.append(kp)
        vp_l.append(vp)
        zp_l.append(zp)
        ks_l.append(ksn)
        vs_l.append(vsn)
        zs_l.append(zsn)
    y_prompt = rmsnorm(hp, g_final)
    y_sample = rmsnorm(hs, g_final)
    new_k_prompt = jnp.stack(kp_l)
    new_v_prompt = jnp.stack(vp_l)
    new_pool_prompt = jnp.stack(zp_l)
    new_k_sample = jnp.stack(ks_l)
    new_v_sample = jnp.stack(vs_l)
    new_pool_sample = jnp.stack(zs_l)
    return (y_prompt, y_sample, new_k_prompt, new_v_prompt, new_pool_prompt, new_k_sample, new_v_sample, new_pool_sample)
```

```python
import functools

import numpy as np
import jax
import jax.numpy as jnp
from jax import lax
from jax.experimental import pallas as pl
from jax.experimental.pallas import tpu as pltpu

F32 = jnp.float32
BF16 = jnp.bfloat16
I32 = jnp.int32

HEAD_DIM = 64
Q_PER_KV = 4
WINDOW = 128
PAST_LEN = 8192
ROT_DIM = HEAD_DIM // 4
ROPE_THETA = 500000.0
POOL_WINDOWS = (2, 4, 8, 16)
POOL_HALO = 16
N_GROUPS = 4
EXPERTS_PER_GROUP = 4
N_EXPERTS = N_GROUPS * EXPERTS_PER_GROUP
TOP_K = 2
RMS_EPS = 1e-6
NEG_INF = -1e30
LANES = 128
MIB = 1024 * 1024

TM = 1024
TN = 512
TM_ROW = 256
TM_E = 512
SEQS_PER_STEP = 4


def _params(n_axes, vmem_mib):
    return pltpu.CompilerParams(dimension_semantics=("arbitrary",) * n_axes,
                                vmem_limit_bytes=vmem_mib * MIB)


def _rms_rows(x, g):
    ms = jnp.mean(x * x, axis=-1, keepdims=True)
    return x * lax.rsqrt(ms + RMS_EPS) * g


def _norm_kernel(x_ref, g_ref, o_ref):
    o_ref[...] = _rms_rows(x_ref[...], g_ref[...]).astype(o_ref.dtype)


def _rmsnorm(x, g, out_dtype, name):
    n, d = x.shape
    return pl.pallas_call(
        _norm_kernel,
        grid=(n // TM_ROW,),
        in_specs=[pl.BlockSpec((TM_ROW, d), lambda i: (i, 0)),
                  pl.BlockSpec((1, d), lambda i: (0, 0))],
        out_specs=pl.BlockSpec((TM_ROW, d), lambda i: (i, 0)),
        out_shape=jax.ShapeDtypeStruct((n, d), out_dtype),
        compiler_params=_params(1, 32),
        name=name,
    )(x, g.reshape(1, d))


def _proj_kernel(a_ref, w_ref, o_ref, wb_ref):
    @pl.when(pl.program_id(1) == 0)
    def _():
        wb_ref[...] = w_ref[...].astype(BF16)

    o_ref[...] = jnp.dot(a_ref[...], wb_ref[...], preferred_element_type=F32)


def _in_proj(xn, w, col0, width, name):
    n, d = xn.shape
    cb0 = col0 // TN
    return pl.pallas_call(
        _proj_kernel,
        grid=(width // TN, n // TM),
        in_specs=[pl.BlockSpec((TM, d), lambda j, i: (i, 0)),
                  pl.BlockSpec((d, TN), lambda j, i: (0, cb0 + j))],
        out_specs=pl.BlockSpec((TM, TN), lambda j, i: (i, j)),
        out_shape=jax.ShapeDtypeStruct((n, width), F32),
        scratch_shapes=[pltpu.VMEM((d, TN), BF16)],
        compiler_params=_params(2, 56),
        name=name,
    )(xn, w)


def _rope_tables(pos):
    half = ROT_DIM // 2
    inv_freq = ROPE_THETA ** (-jnp.arange(half, dtype=F32) * (2.0 / ROT_DIM))
    ang = pos[:, None] * inv_freq[None, :]
    cos, sin = jnp.cos(ang), jnp.sin(ang)
    t = pos.shape[0]
    pad = jnp.zeros((t, HEAD_DIM - ROT_DIM), F32)
    zeros = jnp.zeros((t, half), F32)
    cos_h = jnp.concatenate([cos, cos, pad + 1.0], axis=1)
    sa_h = jnp.concatenate([zeros, sin, pad], axis=1)
    sb_h = jnp.concatenate([-sin, zeros, pad], axis=1)
    reps = LANES // HEAD_DIM
    return tuple(jnp.tile(a, (1, reps)) for a in (cos_h, sa_h, sb_h))


def _rope(x, cos_t, sa_t, sb_t):
    pieces = []
    for c in range(x.shape[1] // LANES):
        xc = x[:, c * LANES:(c + 1) * LANES]
        pieces.append(xc * cos_t
                      + pltpu.roll(xc, ROT_DIM // 2, 1) * sa_t
                      + pltpu.roll(xc, LANES - ROT_DIM // 2, 1) * sb_t)
    return jnp.concatenate(pieces, axis=1)


def _softmax_with_sink(s, sink):
    m = jnp.maximum(jnp.max(s, axis=-1, keepdims=True), sink)
    p = jnp.exp(s - m)
    den = jnp.sum(p, axis=-1, keepdims=True) + jnp.exp(sink - m)
    return p / den


def _attn_prompt_kernel(sink_ref, q_ref, k_ref, v_ref, cos_ref, sa_ref, sb_ref,
                        o_ref, ko_ref, kprev_ref, vprev_ref, *, n_kv):
    b = pl.program_id(0)

    @pl.when(b == 0)
    def _():
        kprev_ref[...] = jnp.zeros_like(kprev_ref)
        vprev_ref[...] = jnp.zeros_like(vprev_ref)

    cos_t, sa_t, sb_t = cos_ref[...], sa_ref[...], sb_ref[...]
    q = _rope(q_ref[...], cos_t, sa_t, sb_t)
    k = _rope(k_ref[...], cos_t, sa_t, sb_t)
    v = v_ref[...]
    ko_ref[...] = k
    kcat = jnp.concatenate([kprev_ref[...], k], axis=0).astype(BF16)
    vcat = jnp.concatenate([vprev_ref[...], v], axis=0).astype(BF16)
    kprev_ref[...] = k
    vprev_ref[...] = v

    rows = Q_PER_KV * WINDOW
    qi = lax.broadcasted_iota(I32, (rows, 2 * WINDOW), 0) & (WINDOW - 1)
    kj = lax.broadcasted_iota(I32, (rows, 2 * WINDOW), 1)
    first_key = jnp.where(b > 0, 0, WINDOW)
    mask = (kj > qi) & (kj <= qi + WINDOW) & (kj >= first_key)

    outs = []
    for g in range(n_kv):
        qg = jnp.concatenate(
            [q[:, (g * Q_PER_KV + r) * HEAD_DIM:(g * Q_PER_KV + r + 1) * HEAD_DIM] for r in range(Q_PER_KV)],
            axis=0).astype(BF16)
        kg = kcat[:, g * HEAD_DIM:(g + 1) * HEAD_DIM]
        vg = vcat[:, g * HEAD_DIM:(g + 1) * HEAD_DIM]
        s = lax.dot_general(qg, kg, (((1,), (1,)), ((), ())), preferred_element_type=F32) * (HEAD_DIM ** -0.5)
        s = jnp.where(mask, s, NEG_INF)
        sink = jnp.concatenate(
            [jnp.full((WINDOW, 1), sink_ref[g * Q_PER_KV + r], F32) for r in range(Q_PER_KV)], axis=0)
        probs = _softmax_with_sink(s, sink).astype(BF16)
        og = jnp.dot(probs, vg, preferred_element_type=F32)
        for r in range(Q_PER_KV):
            outs.append(og[r * WINDOW:(r + 1) * WINDOW])
    o_ref[...] = jnp.concatenate(outs, axis=1).astype(o_ref.dtype)


def _attn_prompt(qkv, sinks, tables, n_rows, attn_w, kv_w):
    nb = n_rows // WINDOW
    kb, vb = attn_w // kv_w, attn_w // kv_w + 1
    tab_spec = pl.BlockSpec((WINDOW, LANES), lambda b, s: (b, 0))
    return pl.pallas_call(
        functools.partial(_attn_prompt_kernel, n_kv=kv_w // HEAD_DIM),
        grid_spec=pltpu.PrefetchScalarGridSpec(
            num_scalar_prefetch=1,
            grid=(nb,),
            in_specs=[pl.BlockSpec((WINDOW, attn_w), lambda b, s: (b, 0)),
                      pl.BlockSpec((WINDOW, kv_w), lambda b, s: (b, kb)),
                      pl.BlockSpec((WINDOW, kv_w), lambda b, s: (b, vb)),
                      tab_spec, tab_spec, tab_spec],
            out_specs=[pl.BlockSpec((WINDOW, attn_w), lambda b, s: (b, 0)),
                       pl.BlockSpec((WINDOW, kv_w), lambda b, s: (b, 0))],
            scratch_shapes=[pltpu.VMEM((WINDOW, kv_w), F32), pltpu.VMEM((WINDOW, kv_w), F32)]),
        out_shape=[jax.ShapeDtypeStruct((n_rows, attn_w), BF16),
                   jax.ShapeDtypeStruct((n_rows, kv_w), F32)],
        compiler_params=_params(1, 32),
        name="attn_prompt",
    )(sinks, qkv, qkv, qkv, *tables)


def _attn_sample_kernel(sink_ref, q_ref, k_ref, v_ref, ck_ref, cv_ref, cos_ref, sa_ref, sb_ref,
                        o_ref, nk_ref, nv_ref, *, n_kv, t_new):
    cos_t, sa_t, sb_t = cos_ref[...], sa_ref[...], sb_ref[...]
    q = _rope(q_ref[...], cos_t, sa_t, sb_t)
    k = _rope(k_ref[...], cos_t, sa_t, sb_t)
    v = v_ref[...]
    n_cache = ck_ref.shape[1]
    n_keys = 2 * WINDOW
    rows = Q_PER_KV * t_new
    qt = lax.broadcasted_iota(I32, (rows, n_keys), 0) & (t_new - 1)
    kj = lax.broadcasted_iota(I32, (rows, n_keys), 1)
    mask = (kj > qt + (n_cache - WINDOW)) & (kj <= qt + n_cache)
    zpad = jnp.zeros((n_keys - n_cache - t_new, k.shape[1]), F32)

    seq_outs = []
    for s_i in range(q.shape[0] // t_new):
        ck, cv = ck_ref[s_i], cv_ref[s_i]
        kn, vn = k[s_i * t_new:(s_i + 1) * t_new], v[s_i * t_new:(s_i + 1) * t_new]
        nk_ref[s_i] = jnp.concatenate([ck[t_new:], kn], axis=0)
        nv_ref[s_i] = jnp.concatenate([cv[t_new:], vn], axis=0)
        kall = jnp.concatenate([ck, kn, zpad], axis=0).astype(BF16)
        vall = jnp.concatenate([cv, vn, zpad], axis=0).astype(BF16)
        qs = q[s_i * t_new:(s_i + 1) * t_new]
        outs = []
        for g in range(n_kv):
            qg = jnp.concatenate(
                [qs[:, (g * Q_PER_KV + r) * HEAD_DIM:(g * Q_PER_KV + r + 1) * HEAD_DIM] for r in range(Q_PER_KV)],
                axis=0).astype(BF16)
            kg = kall[:, g * HEAD_DIM:(g + 1) * HEAD_DIM]
            vg = vall[:, g * HEAD_DIM:(g + 1) * HEAD_DIM]
            s = lax.dot_general(qg, kg, (((1,), (1,)), ((), ())), preferred_element_type=F32) * (HEAD_DIM ** -0.5)
            s = jnp.where(mask, s, NEG_INF)
            sink = jnp.concatenate(
                [jnp.full((t_new, 1), sink_ref[g * Q_PER_KV + r], F32) for r in range(Q_PER_KV)], axis=0)
            probs = _softmax_with_sink(s, sink).astype(BF16)
            og = jnp.dot(probs, vg, preferred_element_type=F32)
            for r in range(Q_PER_KV):
                outs.append(og[r * t_new:(r + 1) * t_new])
        seq_outs.append(jnp.concatenate(outs, axis=1))
    o_ref[...] = jnp.concatenate(seq_outs, axis=0).astype(o_ref.dtype)


def _attn_sample(qkv, row0, cache_k, cache_v, sinks, tables, n_seq, t_new, attn_w, kv_w):
    rows = SEQS_PER_STEP * t_new
    rb0 = row0 // rows
    kb, vb = attn_w // kv_w, attn_w // kv_w + 1
    n_cache = cache_k.shape[1]
    tab_spec = pl.BlockSpec((rows, LANES), lambda i, s: (0, 0))
    cache_spec = pl.BlockSpec((SEQS_PER_STEP, n_cache, kv_w), lambda i, s: (i, 0, 0))
    return pl.pallas_call(
        functools.partial(_attn_sample_kernel, n_kv=kv_w // HEAD_DIM, t_new=t_new),
        grid_spec=pltpu.PrefetchScalarGridSpec(
            num_scalar_prefetch=1,
            grid=(n_seq // SEQS_PER_STEP,),
            in_specs=[pl.BlockSpec((rows, attn_w), lambda i, s: (rb0 + i, 0)),
                      pl.BlockSpec((rows, kv_w), lambda i, s: (rb0 + i, kb)),
                      pl.BlockSpec((rows, kv_w), lambda i, s: (rb0 + i, vb)),
                      cache_spec, cache_spec, tab_spec, tab_spec, tab_spec],
            out_specs=[pl.BlockSpec((rows, attn_w), lambda i, s: (i, 0)), cache_spec, cache_spec]),
        out_shape=[jax.ShapeDtypeStruct((n_seq * t_new, attn_w), BF16),
                   jax.ShapeDtypeStruct(cache_k.shape, F32),
                   jax.ShapeDtypeStruct(cache_v.shape, F32)],
        compiler_params=_params(1, 32),
        name="attn_sample",
    )(sinks, qkv, qkv, qkv, cache_k, cache_v, *tables)


def _pool_kernel(z_ref, halo_ref, w_ref, sc_ref, o_ref, ext_ref, *, tm, pg, from_start):
    i = pl.program_id(0)
    if from_start:
        ext_ref[0:POOL_HALO, :] = jnp.where(i > 0, halo_ref[...], 0.0)
    else:
        ext_ref[0:POOL_HALO, :] = jnp.zeros((POOL_HALO, ext_ref.shape[1]), F32)
    ext_ref[POOL_HALO:, :] = z_ref[...]
    pos = i * tm + lax.broadcasted_iota(I32, (tm, 1), 0)
    for g, w in enumerate(POOL_WINDOWS):
        c0, c1 = g * pg, (g + 1) * pg
        cur = ext_ref[POOL_HALO:POOL_HALO + tm, c0:c1]
        acc = cur
        for d in range(1, w):
            acc = acc + ext_ref[POOL_HALO - d:POOL_HALO - d + tm, c0:c1]
        if from_start:
            cnt = jnp.minimum(pos + 1, w).astype(F32)
        else:
            cnt = jnp.full((tm, 1), w, F32)
        pooled = (acc / cnt - cur).astype(BF16)
        mixed = jnp.dot(pooled, w_ref[g].astype(BF16), preferred_element_type=F32) * sc_ref[:, c0:c1]
        o_ref[:, c0:c1] = mixed.astype(o_ref.dtype)


def _pool_mix(z, w_grp, scale, tm, from_start, name):
    n, pw = z.shape
    hb = tm // POOL_HALO
    return pl.pallas_call(
        functools.partial(_pool_kernel, tm=tm, pg=pw // len(POOL_WINDOWS), from_start=from_start),
        grid=(n // tm,),
        in_specs=[pl.BlockSpec((tm, pw), lambda i: (i, 0)),
                  pl.BlockSpec((POOL_HALO, pw), lambda i: (jnp.maximum(i * hb - 1, 0), 0)),
                  pl.BlockSpec(w_grp.shape, lambda i: (0, 0, 0)),
                  pl.BlockSpec((1, pw), lambda i: (0, 0))],
        out_specs=pl.BlockSpec((tm, pw), lambda i: (i, 0)),
        out_shape=jax.ShapeDtypeStruct((n, pw), BF16),
        scratch_shapes=[pltpu.VMEM((tm + POOL_HALO, pw), F32)],
        compiler_params=_params(1, 40),
        name=name,
    )(z, z, w_grp, scale.reshape(1, pw))


def _merge_kernel(a1_ref, a2_ref, w1_ref, w2_ref, ga_ref, gb_ref, ba_ref, bb_ref, o_ref, w1b_ref, w2b_ref):
    @pl.when(pl.program_id(1) == 0)
    def _():
        w1b_ref[...] = w1_ref[...].astype(BF16)
        w2b_ref[...] = w2_ref[...].astype(BF16)

    a_up = jnp.dot(a1_ref[...], w1b_ref[...], preferred_element_type=F32)
    b_up = jnp.dot(a2_ref[...], w2b_ref[...], preferred_element_type=F32)
    merged = (jax.nn.sigmoid(ga_ref[...] + ba_ref[...]) * a_up
              + jax.nn.sigmoid(gb_ref[...] + bb_ref[...]) * b_up)
    o_ref[...] = merged.astype(o_ref.dtype)


def _merge(attn, mixed, w_attn_up, w_pool_up, gates, b_gate):
    n, ka = attn.shape
    kp = mixed.shape[1]
    d = w_attn_up.shape[1]
    nj = d // TN
    return pl.pallas_call(
        _merge_kernel,
        grid=(nj, n // TM),
        in_specs=[pl.BlockSpec((TM, ka), lambda j, i: (i, 0)),
                  pl.BlockSpec((TM, kp), lambda j, i: (i, 0)),
                  pl.BlockSpec((ka, TN), lambda j, i: (0, j)),
                  pl.BlockSpec((kp, TN), lambda j, i: (0, j)),
                  pl.BlockSpec((TM, TN), lambda j, i: (i, j)),
                  pl.BlockSpec((TM, TN), lambda j, i: (i, nj + j)),
                  pl.BlockSpec((1, TN), lambda j, i: (0, j)),
                  pl.BlockSpec((1, TN), lambda j, i: (0, nj + j))],
        out_specs=pl.BlockSpec((TM, TN), lambda j, i: (i, j)),
        out_shape=jax.ShapeDtypeStruct((n, d), BF16),
        scratch_shapes=[pltpu.VMEM((ka, TN), BF16), pltpu.VMEM((kp, TN), BF16)],
        compiler_params=_params(2, 56),
        name="merge",
    )(attn, mixed, w_attn_up, w_pool_up, gates, gates, b_gate, b_gate)


def _out_proj_kernel(a_ref, w_ref, h_ref, o_ref, wb_ref):
    @pl.when(pl.program_id(1) == 0)
    def _():
        wb_ref[...] = w_ref[...].astype(BF16)

    o_ref[...] = h_ref[...] + jnp.dot(a_ref[...], wb_ref[...], preferred_element_type=F32)


def _out_proj(merged, w_out, h):
    n, d = merged.shape
    return pl.pallas_call(
        _out_proj_kernel,
        grid=(d // TN, n // TM),
        in_specs=[pl.BlockSpec((TM, d), lambda j, i: (i, 0)),
                  pl.BlockSpec((d, TN), lambda j, i: (0, j)),
                  pl.BlockSpec((TM, TN), lambda j, i: (i, j))],
        out_specs=pl.BlockSpec((TM, TN), lambda j, i: (i, j)),
        out_shape=jax.ShapeDtypeStruct((n, d), F32),
        scratch_shapes=[pltpu.VMEM((d, TN), BF16)],
        compiler_params=_params(2, 56),
        name="out_proj",
    )(merged, w_out, h)


def _router_kernel(h_ref, g_ref, wr_ref, br_ref, eid_ref, cw_ref):
    xf = _rms_rows(h_ref[...], g_ref[...]).astype(BF16)
    logits = jnp.dot(xf, wr_ref[...].astype(BF16), preferred_element_type=F32) + br_ref[...]
    lane = lax.broadcasted_iota(I32, logits.shape, 1)
    big = jnp.int32(1 << 20)
    ninf = jnp.float32(-jnp.inf)

    is_g = lane < N_GROUPS
    gl = jnp.where(is_g, logits, ninf)
    gmax = jnp.max(gl, axis=-1, keepdims=True)
    gden = jnp.sum(jnp.exp(gl - gmax), axis=-1, keepdims=True)
    g_p = 1.0 / gden
    g_idx = jnp.min(jnp.where(gl == gmax, lane, big), axis=-1, keepdims=True)

    e_lane = lane - N_GROUPS
    in_grp = (e_lane >= g_idx * EXPERTS_PER_GROUP) & (e_lane < (g_idx + 1) * EXPERTS_PER_GROUP)
    el = jnp.where(in_grp, logits, ninf)
    m1 = jnp.max(el, axis=-1, keepdims=True)
    i1 = jnp.min(jnp.where(in_grp & (el == m1), e_lane, big), axis=-1, keepdims=True)
    el2 = jnp.where(e_lane == i1, ninf, el)
    m2 = jnp.max(el2, axis=-1, keepdims=True)
    i2 = jnp.min(jnp.where(in_grp & (el2 == m2), e_lane, big), axis=-1, keepdims=True)
    t = jnp.exp(m2 - m1)
    den = 1.0 + t
    c1 = g_p * (1.0 / den)
    c2 = g_p * (t / den)
    eid_ref[...] = jnp.where(lane == 0, i1, jnp.where(lane == 1, i2, 0))
    cw_ref[...] = jnp.where(lane == 0, c1, jnp.where(lane == 1, c2, 0.0))


def _router(h, g_ffn, w_rg, b_rg, w_re, b_re):
    n, d = h.shape
    n_log = N_GROUPS + N_EXPERTS
    wr = jnp.concatenate([w_rg, w_re, jnp.zeros((d, LANES - n_log), F32)], axis=1)
    br = jnp.concatenate([b_rg, b_re, jnp.zeros((LANES - n_log,), F32)]).reshape(1, LANES)
    return pl.pallas_call(
        _router_kernel,
        grid=(n // TM_ROW,),
        in_specs=[pl.BlockSpec((TM_ROW, d), lambda i: (i, 0)),
                  pl.BlockSpec((1, d), lambda i: (0, 0)),
                  pl.BlockSpec((d, LANES), lambda i: (0, 0)),
                  pl.BlockSpec((1, LANES), lambda i: (0, 0))],
        out_specs=[pl.BlockSpec((TM_ROW, LANES), lambda i: (i, 0)),
                   pl.BlockSpec((TM_ROW, LANES), lambda i: (i, 0))],
        out_shape=[jax.ShapeDtypeStruct((n, LANES), I32), jax.ShapeDtypeStruct((n, LANES), F32)],
        compiler_params=_params(1, 32),
        name="router",
    )(h, g_ffn.reshape(1, d), wr, br)


def _dispatch_plan(eid, cw, n_tiles):
    n = eid.shape[0]
    ef = eid.reshape(-1)
    onehot = (ef[:, None] == jnp.arange(N_EXPERTS, dtype=I32)[None, :]).astype(I32)
    csum = jnp.cumsum(onehot, axis=0)
    rank = jnp.sum((csum - onehot) * onehot, axis=1)
    counts = csum[-1]
    tiles_per = (counts + TM_E - 1) // TM_E
    tiles_end = jnp.cumsum(tiles_per)
    row_start = (tiles_end - tiles_per) * TM_E
    pos = (jnp.sum(onehot * row_start[None, :], axis=1) + rank).astype(I32)
    tok_sorted = jnp.zeros((n_tiles * TM_E,), I32).at[pos].set(jnp.arange(n * TOP_K, dtype=I32) // TOP_K)
    cw_sorted = jnp.zeros((n_tiles * TM_E,), F32).at[pos].set(cw.reshape(-1))
    tile_ids = jnp.arange(n_tiles, dtype=I32)
    tile_expert = jnp.minimum(
        jnp.sum((tile_ids[:, None] >= tiles_end[None, :]).astype(I32), axis=1), N_EXPERTS - 1).astype(I32)
    n_used = tiles_end[-1:].astype(I32)
    return pos, tok_sorted, cw_sorted.reshape(-1, 1), tile_expert, n_used


def _gather_norm_kernel(tok_ref, h_hbm, g_ref, o_ref, buf_ref, sem):
    base = pl.program_id(0) * TM_E

    def row_copy(r, src_row):
        return pltpu.make_async_copy(h_hbm.at[pl.ds(src_row, 1), :], buf_ref.at[pl.ds(r, 1), :], sem)

    def issue(r, carry):
        row_copy(r, tok_ref[base + r]).start()
        return carry

    def drain(r, carry):
        row_copy(r, 0).wait()
        return carry

    lax.fori_loop(0, TM_E, issue, 0)
    lax.fori_loop(0, TM_E, drain, 0)
    o_ref[...] = _rms_rows(buf_ref[...], g_ref[...]).astype(o_ref.dtype)


def _gather_norm(tok_sorted, h, g_ffn, n_tiles):
    d = h.shape[1]
    return pl.pallas_call(
        _gather_norm_kernel,
        grid_spec=pltpu.PrefetchScalarGridSpec(
            num_scalar_prefetch=1,
            grid=(n_tiles,),
            in_specs=[pl.BlockSpec(memory_space=pl.ANY),
                      pl.BlockSpec((1, d), lambda t, tok: (0, 0))],
            out_specs=pl.BlockSpec((TM_E, d), lambda t, tok: (t, 0)),
            scratch_shapes=[pltpu.VMEM((TM_E, d), F32), pltpu.SemaphoreType.DMA(())]),
        out_shape=jax.ShapeDtypeStruct((n_tiles * TM_E, d), BF16),
        compiler_params=_params(1, 40),
        name="moe_gather",
    )(tok_sorted, h, g_ffn.reshape(1, d))


def _expert_changed(te_ref, t):
    return (t == 0) | (te_ref[t] != te_ref[jnp.maximum(t - 1, 0)])


def _moe_up_kernel(te_ref, nu_ref, xs_ref, wg_ref, wu_ref, cw_ref, o_ref, wgb_ref, wub_ref):
    t = pl.program_id(1)

    @pl.when(_expert_changed(te_ref, t))
    def _():
        wgb_ref[...] = wg_ref[...].astype(BF16)
        wub_ref[...] = wu_ref[...].astype(BF16)

    @pl.when(t < nu_ref[0])
    def _():
        x = xs_ref[...]
        gate = jnp.dot(x, wgb_ref[...], preferred_element_type=F32)
        up = jnp.dot(x, wub_ref[...], preferred_element_type=F32)
        o_ref[...] = (jax.nn.silu(gate) * up * cw_ref[...]).astype(o_ref.dtype)

    @pl.when(t >= nu_ref[0])
    def _():
        o_ref[...] = jnp.zeros_like(o_ref)


def _moe_up(tile_expert, n_used, xs, w_gate, w_up, cw_sorted, n_tiles):
    d, f = w_gate.shape[1], w_gate.shape[2]
    tf = TN
    w_spec = pl.BlockSpec((None, d, tf), lambda j, t, te, nu: (te[t], 0, j))
    return pl.pallas_call(
        _moe_up_kernel,
        grid_spec=pltpu.PrefetchScalarGridSpec(
            num_scalar_prefetch=2,
            grid=(f // tf, n_tiles),
            in_specs=[pl.BlockSpec((TM_E, d), lambda j, t, te, nu: (t, 0)),
                      w_spec, w_spec,
                      pl.BlockSpec((TM_E, 1), lambda j, t, te, nu: (t, 0))],
            out_specs=pl.BlockSpec((TM_E, tf), lambda j, t, te, nu: (t, j)),
            scratch_shapes=[pltpu.VMEM((d, tf), BF16), pltpu.VMEM((d, tf), BF16)]),
        out_shape=jax.ShapeDtypeStruct((n_tiles * TM_E, f), BF16),
        compiler_params=_params(2, 56),
        name="moe_up",
    )(tile_expert, n_used, xs, w_gate, w_up, cw_sorted)


def _moe_down_kernel(te_ref, nu_ref, hid_ref, wd_ref, o_ref, wdb_ref):
    t = pl.program_id(1)

    @pl.when(_expert_changed(te_ref, t))
    def _():
        wdb_ref[...] = wd_ref[...].astype(BF16)

    @pl.when(t < nu_ref[0])
    def _():
        o_ref[...] = jnp.dot(hid_ref[...], wdb_ref[...], preferred_element_type=F32)

    @pl.when(t >= nu_ref[0])
    def _():
        o_ref[...] = jnp.zeros_like(o_ref)


def _moe_down(tile_expert, n_used, hid, w_down, n_tiles):
    f, d = w_down.shape[1], w_down.shape[2]
    tn = 2 * TN
    return pl.pallas_call(
        _moe_down_kernel,
        grid_spec=pltpu.PrefetchScalarGridSpec(
            num_scalar_prefetch=2,
            grid=(d // tn, n_tiles),
            in_specs=[pl.BlockSpec((TM_E, f), lambda j, t, te, nu: (t, 0)),
                      pl.BlockSpec((None, f, tn), lambda j, t, te, nu: (te[t], 0, j))],
            out_specs=pl.BlockSpec((TM_E, tn), lambda j, t, te, nu: (t, j)),
            scratch_shapes=[pltpu.VMEM((f, tn), BF16)]),
        out_shape=jax.ShapeDtypeStruct((n_tiles * TM_E, d), F32),
        compiler_params=_params(2, 40),
        name="moe_down",
    )(tile_expert, n_used, hid, w_down)


def _combine_kernel(pos_ref, h_ref, y_hbm, g_ref, ho_ref, xo_ref, buf_ref, sem):
    base = pl.program_id(0) * TM_ROW

    def row_copy(k, r, src_row):
        return pltpu.make_async_copy(y_hbm.at[pl.ds(src_row, 1), :], buf_ref.at[k, pl.ds(r, 1), :], sem)

    def issue(r, carry):
        for k in range(TOP_K):
            row_copy(k, r, pos_ref[(base + r) * TOP_K + k]).start()
        return carry

    def drain(r, carry):
        for k in range(TOP_K):
            row_copy(k, r, 0).wait()
        return carry

    lax.fori_loop(0, TM_ROW, issue, 0)
    lax.fori_loop(0, TM_ROW, drain, 0)
    h2 = h_ref[...] + (buf_ref[0] + buf_ref[1])
    ho_ref[...] = h2
    xo_ref[...] = _rms_rows(h2, g_ref[...]).astype(xo_ref.dtype)


def _combine(pos, h, y, g_ple):
    n, d = h.shape
    row_spec = pl.BlockSpec((TM_ROW, d), lambda i, p: (i, 0))
    return pl.pallas_call(
        _combine_kernel,
        grid_spec=pltpu.PrefetchScalarGridSpec(
            num_scalar_prefetch=1,
            grid=(n // TM_ROW,),
            in_specs=[row_spec,
                      pl.BlockSpec(memory_space=pl.ANY),
                      pl.BlockSpec((1, d), lambda i, p: (0, 0))],
            out_specs=[row_spec, row_spec],
            scratch_shapes=[pltpu.VMEM((TOP_K, TM_ROW, d), F32), pltpu.SemaphoreType.DMA(())]),
        out_shape=[jax.ShapeDtypeStruct((n, d), F32), jax.ShapeDtypeStruct((n, d), BF16)],
        compiler_params=_params(1, 48),
        name="moe_combine",
    )(pos, h, y, g_ple.reshape(1, d))


def _ple_kernel(xn_ref, wg_ref, bg_ref, p_ref, wp_ref, h_ref, o_ref, wgb_ref):
    @pl.when(pl.program_id(1) == 0)
    def _():
        wgb_ref[...] = wg_ref[...].astype(BF16)

    gate = jax.nn.sigmoid(jnp.dot(xn_ref[...], wgb_ref[...], preferred_element_type=F32) + bg_ref[...])
    ple = jnp.dot(p_ref[...].astype(BF16), wp_ref[...].astype(BF16), preferred_element_type=F32)
    o_ref[...] = h_ref[...] + gate * ple


def _ple(xn, w_gate, b_gate, p, w_proj, h):
    n, d = h.shape
    pd = p.shape[1]
    return pl.pallas_call(
        _ple_kernel,
        grid=(d // TN, n // TM),
        in_specs=[pl.BlockSpec((TM, d), lambda j, i: (i, 0)),
                  pl.BlockSpec((d, TN), lambda j, i: (0, j)),
                  pl.BlockSpec((1, TN), lambda j, i: (0, j)),
                  pl.BlockSpec((TM, pd), lambda j, i: (i, 0)),
                  pl.BlockSpec((pd, TN), lambda j, i: (0, j)),
                  pl.BlockSpec((TM, TN), lambda j, i: (i, j))],
        out_specs=pl.BlockSpec((TM, TN), lambda j, i: (i, j)),
        out_shape=jax.ShapeDtypeStruct((n, d), F32),
        scratch_shapes=[pltpu.VMEM((d, TN), BF16)],
        compiler_params=_params(2, 56),
        name="ple",
    )(xn, w_gate, b_gate.reshape(1, d), p, w_proj, h)


def _layer(h, p_all, n_prompt, cache_k, cache_v, state_pool, lw):
    n, d = h.shape
    n_seq, n_cache, n_kv, _ = cache_k.shape
    t_new = (n - n_prompt) // n_seq
    kv_w = n_kv * HEAD_DIM
    attn_w = lw['w_attn_up'].shape[0]
    pool_w = lw['w_pool_up'].shape[0]
    n_pool = state_pool.shape[1]

    xn = _rmsnorm(h, lw['g_mix'], BF16, "norm_mix")
    qkv = _in_proj(xn, lw['w_in'], 0, attn_w + 2 * kv_w, "in_proj_qkv")
    z = _in_proj(xn, lw['w_in'], attn_w + 2 * kv_w, pool_w, "in_proj_pool")
    gates = _in_proj(xn, lw['w_in'], attn_w + 2 * kv_w + pool_w, 2 * d, "in_proj_gates")

    tab_p = _rope_tables(jnp.arange(n_prompt, dtype=F32))
    tab_s = tuple(jnp.tile(a, (SEQS_PER_STEP, 1))
                  for a in _rope_tables(PAST_LEN + jnp.arange(t_new, dtype=F32)))
    attn_p, k_rot_p = _attn_prompt(qkv, lw['attn_sinks'], tab_p, n_prompt, attn_w, kv_w)
    attn_s, new_k_s, new_v_s = _attn_sample(
        qkv, n_prompt, cache_k.reshape(n_seq, n_cache, kv_w), cache_v.reshape(n_seq, n_cache, kv_w),
        lw['attn_sinks'], tab_s, n_seq, t_new, attn_w, kv_w)
    attn = jnp.concatenate([attn_p, attn_s], axis=0)

    z_s = z[n_prompt:].reshape(n_seq, t_new, pool_w)
    seq_rows = 1 + n_pool + t_new
    z_s_all = jnp.concatenate([jnp.zeros((n_seq, 1, pool_w), F32), state_pool, z_s], axis=1)
    mixed_p = _pool_mix(z[:n_prompt], lw['w_pool_grp'], lw['pool_scale'], TM_ROW, True, "pool_prompt")
    mixed_s = _pool_mix(z_s_all.reshape(n_seq * seq_rows, pool_w), lw['w_pool_grp'], lw['pool_scale'],
                        16 * seq_rows, False, "pool_sample")
    mixed_s = mixed_s.reshape(n_seq, seq_rows, pool_w)[:, seq_rows - t_new:].reshape(n_seq * t_new, pool_w)
    mixed = jnp.concatenate([mixed_p, mixed_s], axis=0)

    merged = _merge(attn, mixed, lw['w_attn_up'], lw['w_pool_up'], gates, lw['b_gate'].reshape(1, 2 * d))
    h1 = _out_proj(merged, lw['w_out'], h)

    n_tiles = (n * TOP_K) // TM_E + N_EXPERTS
    eid, cw = _router(h1, lw['g_ffn'], lw['w_route_group'], lw['b_route_group'],
                      lw['w_route_expert'], lw['b_route_expert'])
    pos, tok_sorted, cw_sorted, tile_expert, n_used = _dispatch_plan(eid[:, :TOP_K], cw[:, :TOP_K], n_tiles)
    xs = _gather_norm(tok_sorted, h1, lw['g_ffn'], n_tiles)
    hid = _moe_up(tile_expert, n_used, xs, lw['w_exp_gate'], lw['w_exp_up'], cw_sorted, n_tiles)
    y = _moe_down(tile_expert, n_used, hid, lw['w_exp_down'], n_tiles)
    h2, xn2 = _combine(pos, h1, y, lw['g_ple'])

    h3 = _ple(xn2, lw['w_ple_gate'], lw['b_ple_gate'], p_all, lw['w_ple_proj'], h2)

    new_k_p = k_rot_p[n_prompt - n_cache:].reshape(1, n_cache, n_kv, HEAD_DIM)
    new_v_p = qkv[n_prompt - n_cache:n_prompt, attn_w + kv_w:].reshape(1, n_cache, n_kv, HEAD_DIM)
    new_z_p = z[n_prompt - n_pool:n_prompt].reshape(1, n_pool, pool_w)
    new_z_s = z_s_all[:, seq_rows - n_pool:]
    return (h3, new_k_p, new_v_p, new_z_p,
            new_k_s.reshape(cache_k.shape), new_v_s.reshape(cache_v.shape), new_z_s)


def kernel(x_prompt, x_sample, cache_k, cache_v, state_pool, p_prompt, p_sample, g_mix, w_in, b_gate, attn_sinks, w_pool_grp, pool_scale, w_attn_up, w_pool_up, w_out, g_ffn, w_route_group, b_route_group, w_route_expert, b_route_expert, w_exp_gate, w_exp_up, w_exp_down, g_ple, w_ple_gate, b_ple_gate, w_ple_proj, g_final):
    batch, seq, d = x_prompt.shape
    n_seq, t_new, _ = x_sample.shape
    depth = w_in.shape[0]
    assert batch == 1, "prompt rows are treated as one sequence"
    n_prompt = batch * seq
    n = n_prompt + n_seq * t_new
    assert n_prompt % TM == 0 and n % TM == 0 and (n * TOP_K) % TM_E == 0
    assert n_seq % SEQS_PER_STEP == 0 and t_new & (t_new - 1) == 0
    assert cache_k.shape[2] == WINDOW and state_pool.shape[2] == POOL_HALO - 1

    h = jnp.concatenate([x_prompt.reshape(n_prompt, d), x_sample.reshape(n_seq * t_new, d)], axis=0)
    weights = dict(g_mix=g_mix, w_in=w_in, b_gate=b_gate, attn_sinks=attn_sinks, w_pool_grp=w_pool_grp,
                   pool_scale=pool_scale, w_attn_up=w_attn_up, w_pool_up=w_pool_up, w_out=w_out, g_ffn=g_ffn,
                   w_route_group=w_route_group, b_route_group=b_route_group, w_route_expert=w_route_expert,
                   b_route_expert=b_route_expert, w_exp_gate=w_exp_gate, w_exp_up=w_exp_up,
                   w_exp_down=w_exp_down, g_ple=g_ple, w_ple_gate=w_ple_gate, b_ple_gate=b_ple_gate,
                   w_ple_proj=w_ple_proj)
    state = [[] for _ in range(6)]
    for l in range(depth):
        lw = {name: w[l] for name, w in weights.items()}
        p_all = jnp.concatenate([p_prompt[l].reshape(n_prompt, -1), p_sample[l].reshape(n_seq * t_new, -1)], axis=0)
        out = _layer(h, p_all, n_prompt, cache_k[l], cache_v[l], state_pool[l], lw)
        h = out[0]
        for acc, piece in zip(state, out[1:]):
            acc.append(piece)
    y = _rmsnorm(h, g_final, F32, "norm_final")
    y_prompt = y[:n_prompt].reshape(batch, seq, d)
    y_sample = y[n_prompt:].reshape(n_seq, t_new, d)
    new_k_p, new_v_p, new_z_p, new_k_s, new_v_s, new_z_s = (jnp.stack(s) for s in state)
    return (y_prompt, y_sample, new_k_p, new_v_p, new_z_p, new_k_s, new_v_s, new_z_s)
```

```python
import functools

import jax
import jax.numpy as jnp
from jax import lax
from jax.experimental import pallas as pl
from jax.experimental.pallas import tpu as pltpu

F32 = jnp.float32
BF16 = jnp.bfloat16
I32 = jnp.int32

HEAD_DIM = 64
Q_PER_KV = 4
WINDOW = 128
PAST_LEN = 8192
ROT_DIM = HEAD_DIM // 4
ROPE_THETA = 500000.0
POOL_WINDOWS = (2, 4, 8, 16)
POOL_HALO = 16
N_GROUPS = 4
EXPERTS_PER_GROUP = 4
N_EXPERTS = N_GROUPS * EXPERTS_PER_GROUP
TOP_K = 2
RMS_EPS = 1e-6
NEG_INF = -1e30
LANES = 128
SUBLANES = 8
MIB = 1024 * 1024

TM = 1024
TN = 512
TM_ROW = 256
TM_E = 512
TM_DISPATCH = 512
SEQS_PER_STEP = 4
POOL_SEQS = 16


def _params(n_axes, vmem_mib):
    return pltpu.CompilerParams(dimension_semantics=("arbitrary",) * n_axes,
                                vmem_limit_bytes=vmem_mib * MIB)


def _rms_rows(x, g):
    ms = jnp.mean(x * x, axis=-1, keepdims=True)
    return x * lax.rsqrt(ms + RMS_EPS) * g


def _first_last_specs(nb_first, block):
    zeros = (0,) * (len(block) - 1)
    first = pl.BlockSpec(block, lambda i, *_: (jnp.minimum(i, nb_first - 1),) + zeros)
    last = pl.BlockSpec(block, lambda i, *_: (jnp.maximum(i - nb_first, 0),) + zeros)
    return first, last


def _norm_in2_kernel(xa_ref, xb_ref, g_ref, o_ref, *, nb_first):
    i = pl.program_id(0)

    @pl.when(i < nb_first)
    def _():
        o_ref[...] = _rms_rows(xa_ref[...], g_ref[...]).astype(o_ref.dtype)

    @pl.when(i >= nb_first)
    def _():
        o_ref[...] = _rms_rows(xb_ref[...], g_ref[...]).astype(o_ref.dtype)


def _rmsnorm_in2(xa, xb, g, out_dtype, name):
    (na, d), nb = xa.shape, xb.shape[0]
    nb_first = na // TM_ROW
    spec_a, spec_b = _first_last_specs(nb_first, (TM_ROW, d))
    return pl.pallas_call(
        functools.partial(_norm_in2_kernel, nb_first=nb_first),
        grid=((na + nb) // TM_ROW,),
        in_specs=[spec_a, spec_b, pl.BlockSpec((1, d), lambda i: (0, 0))],
        out_specs=pl.BlockSpec((TM_ROW, d), lambda i: (i, 0)),
        out_shape=jax.ShapeDtypeStruct((na + nb, d), out_dtype),
        compiler_params=_params(1, 40),
        name=name,
    )(xa, xb, g.reshape(1, d))


def _norm_out2_kernel(x_ref, g_ref, oa_ref, ob_ref, *, nb_first):
    i = pl.program_id(0)

    @pl.when(i < nb_first)
    def _():
        oa_ref[...] = _rms_rows(x_ref[...], g_ref[...])

    @pl.when(i >= nb_first)
    def _():
        ob_ref[...] = _rms_rows(x_ref[...], g_ref[...])


def _rmsnorm_out2(x, g, na, name):
    n, d = x.shape
    nb_first = na // TM_ROW
    spec_a, spec_b = _first_last_specs(nb_first, (TM_ROW, d))
    return pl.pallas_call(
        functools.partial(_norm_out2_kernel, nb_first=nb_first),
        grid=(n // TM_ROW,),
        in_specs=[pl.BlockSpec((TM_ROW, d), lambda i: (i, 0)), pl.BlockSpec((1, d), lambda i: (0, 0))],
        out_specs=[spec_a, spec_b],
        out_shape=[jax.ShapeDtypeStruct((na, d), F32), jax.ShapeDtypeStruct((n - na, d), F32)],
        compiler_params=_params(1, 40),
        name=name,
    )(x, g.reshape(1, d))


def _proj_kernel(a_ref, w_ref, o_ref, wb_ref):
    @pl.when(pl.program_id(1) == 0)
    def _():
        wb_ref[...] = w_ref[...].astype(BF16)

    o_ref[...] = jnp.dot(a_ref[...], wb_ref[...], preferred_element_type=F32)


def _in_proj(xn, w, col0, width, name):
    n, d = xn.shape
    cb0 = col0 // TN
    return pl.pallas_call(
        _proj_kernel,
        grid=(width // TN, n // TM),
        in_specs=[pl.BlockSpec((TM, d), lambda j, i: (i, 0)),
                  pl.BlockSpec((d, TN), lambda j, i: (0, cb0 + j))],
        out_specs=pl.BlockSpec((TM, TN), lambda j, i: (i, j)),
        out_shape=jax.ShapeDtypeStruct((n, width), F32),
        scratch_shapes=[pltpu.VMEM((d, TN), BF16)],
        compiler_params=_params(2, 56),
        name=name,
    )(xn, w)


def _rope_tables(pos):
    half = ROT_DIM // 2
    inv_freq = ROPE_THETA ** (-jnp.arange(half, dtype=F32) * (2.0 / ROT_DIM))
    ang = pos[:, None] * inv_freq[None, :]
    cos, sin = jnp.cos(ang), jnp.sin(ang)
    t = pos.shape[0]
    pad = jnp.zeros((t, HEAD_DIM - ROT_DIM), F32)
    zeros = jnp.zeros((t, half), F32)
    cos_h = jnp.concatenate([cos, cos, pad + 1.0], axis=1)
    sa_h = jnp.concatenate([zeros, sin, pad], axis=1)
    sb_h = jnp.concatenate([-sin, zeros, pad], axis=1)
    reps = LANES // HEAD_DIM
    return tuple(jnp.tile(a, (1, reps)) for a in (cos_h, sa_h, sb_h))


def _rope(x, cos_t, sa_t, sb_t):
    pieces = []
    for c in range(x.shape[1] // LANES):
        xc = x[:, c * LANES:(c + 1) * LANES]
        pieces.append(xc * cos_t
                      + pltpu.roll(xc, ROT_DIM // 2, 1) * sa_t
                      + pltpu.roll(xc, LANES - ROT_DIM // 2, 1) * sb_t)
    return jnp.concatenate(pieces, axis=1)


def _softmax_with_sink(s, sink):
    m = jnp.maximum(jnp.max(s, axis=-1, keepdims=True), sink)
    p = jnp.exp(s - m)
    den = jnp.sum(p, axis=-1, keepdims=True) + jnp.exp(sink - m)
    return p / den


def _attn_prompt_kernel(sink_ref, q_ref, k_ref, v_ref, cos_ref, sa_ref, sb_ref,
                        o_ref, ko_ref, kprev_ref, vprev_ref, *, n_kv, nb):
    b = pl.program_id(0)

    @pl.when(b == 0)
    def _():
        kprev_ref[...] = jnp.zeros_like(kprev_ref)
        vprev_ref[...] = jnp.zeros_like(vprev_ref)

    @pl.when(b >= nb)
    def _():
        o_ref[...] = jnp.zeros_like(o_ref)

    @pl.when(b < nb)
    def _():
        _attn_prompt_block(sink_ref, q_ref, k_ref, v_ref, cos_ref, sa_ref, sb_ref,
                           o_ref, ko_ref, kprev_ref, vprev_ref, n_kv=n_kv)


def _attn_prompt_block(sink_ref, q_ref, k_ref, v_ref, cos_ref, sa_ref, sb_ref,
                       o_ref, ko_ref, kprev_ref, vprev_ref, *, n_kv):
    b = pl.program_id(0)
    cos_t, sa_t, sb_t = cos_ref[...], sa_ref[...], sb_ref[...]
    q = _rope(q_ref[...], cos_t, sa_t, sb_t)
    k = _rope(k_ref[...], cos_t, sa_t, sb_t)
    v = v_ref[...]
    ko_ref[...] = k
    kcat = jnp.concatenate([kprev_ref[...], k], axis=0).astype(BF16)
    vcat = jnp.concatenate([vprev_ref[...], v], axis=0).astype(BF16)
    kprev_ref[...] = k
    vprev_ref[...] = v

    rows = Q_PER_KV * WINDOW
    qi = lax.broadcasted_iota(I32, (rows, 2 * WINDOW), 0) & (WINDOW - 1)
    kj = lax.broadcasted_iota(I32, (rows, 2 * WINDOW), 1)
    first_key = jnp.where(b > 0, 0, WINDOW)
    mask = (kj > qi) & (kj <= qi + WINDOW) & (kj >= first_key)

    outs = []
    for g in range(n_kv):
        qg = jnp.concatenate(
            [q[:, (g * Q_PER_KV + r) * HEAD_DIM:(g * Q_PER_KV + r + 1) * HEAD_DIM] for r in range(Q_PER_KV)],
            axis=0).astype(BF16)
        kg = kcat[:, g * HEAD_DIM:(g + 1) * HEAD_DIM]
        vg = vcat[:, g * HEAD_DIM:(g + 1) * HEAD_DIM]
        s = lax.dot_general(qg, kg, (((1,), (1,)), ((), ())), preferred_element_type=F32) * (HEAD_DIM ** -0.5)
        s = jnp.where(mask, s, NEG_INF)
        sink = jnp.concatenate(
            [jnp.full((WINDOW, 1), sink_ref[g * Q_PER_KV + r], F32) for r in range(Q_PER_KV)], axis=0)
        probs = _softmax_with_sink(s, sink).astype(BF16)
        og = jnp.dot(probs, vg, preferred_element_type=F32)
        for r in range(Q_PER_KV):
            outs.append(og[r * WINDOW:(r + 1) * WINDOW])
    o_ref[...] = jnp.concatenate(outs, axis=1).astype(o_ref.dtype)


def _attn_prompt(qkv, sinks, tables, n_rows, attn_w, kv_w):
    nb = n_rows // WINDOW
    kb, vb = attn_w // kv_w, attn_w // kv_w + 1

    def blk(b, s):
        return jnp.minimum(b, nb - 1)

    tab_spec = pl.BlockSpec((WINDOW, LANES), lambda b, s: (blk(b, s), 0))
    return pl.pallas_call(
        functools.partial(_attn_prompt_kernel, n_kv=kv_w // HEAD_DIM, nb=nb),
        grid_spec=pltpu.PrefetchScalarGridSpec(
            num_scalar_prefetch=1,
            grid=(qkv.shape[0] // WINDOW,),
            in_specs=[pl.BlockSpec((WINDOW, attn_w), lambda b, s: (blk(b, s), 0)),
                      pl.BlockSpec((WINDOW, kv_w), lambda b, s: (blk(b, s), kb)),
                      pl.BlockSpec((WINDOW, kv_w), lambda b, s: (blk(b, s), vb)),
                      tab_spec, tab_spec, tab_spec],
            out_specs=[pl.BlockSpec((WINDOW, attn_w), lambda b, s: (b, 0)),
                       pl.BlockSpec((WINDOW, kv_w), lambda b, s: (blk(b, s), 0))],
            scratch_shapes=[pltpu.VMEM((WINDOW, kv_w), F32), pltpu.VMEM((WINDOW, kv_w), F32)]),
        out_shape=[jax.ShapeDtypeStruct((qkv.shape[0], attn_w), BF16),
                   jax.ShapeDtypeStruct((n_rows, kv_w), F32)],
        compiler_params=_params(1, 32),
        name="attn_prompt",
    )(sinks, qkv, qkv, qkv, *tables)


def _attn_sample_kernel(sink_ref, q_ref, k_ref, v_ref, ck_ref, cv_ref, cos_ref, sa_ref, sb_ref, attn_in_ref,
                        o_ref, nk_ref, nv_ref, *, n_kv, t_new):
    del attn_in_ref
    cos_t, sa_t, sb_t = cos_ref[...], sa_ref[...], sb_ref[...]
    q = _rope(q_ref[...], cos_t, sa_t, sb_t)
    k = _rope(k_ref[...], cos_t, sa_t, sb_t)
    v = v_ref[...]
    n_cache = ck_ref.shape[1]
    n_keys = 2 * WINDOW
    rows = Q_PER_KV * t_new
    qt = lax.broadcasted_iota(I32, (rows, n_keys), 0) & (t_new - 1)
    kj = lax.broadcasted_iota(I32, (rows, n_keys), 1)
    mask = (kj > qt + (n_cache - WINDOW)) & (kj <= qt + n_cache)
    zpad = jnp.zeros((n_keys - n_cache - t_new, k.shape[1]), F32)

    seq_outs = []
    for s_i in range(q.shape[0] // t_new):
        ck, cv = ck_ref[s_i], cv_ref[s_i]
        kn, vn = k[s_i * t_new:(s_i + 1) * t_new], v[s_i * t_new:(s_i + 1) * t_new]
        nk_ref[s_i] = jnp.concatenate([ck[t_new:], kn], axis=0)
        nv_ref[s_i] = jnp.concatenate([cv[t_new:], vn], axis=0)
        kall = jnp.concatenate([ck, kn, zpad], axis=0).astype(BF16)
        vall = jnp.concatenate([cv, vn, zpad], axis=0).astype(BF16)
        qs = q[s_i * t_new:(s_i + 1) * t_new]
        outs = []
        for g in range(n_kv):
            qg = jnp.concatenate(
                [qs[:, (g * Q_PER_KV + r) * HEAD_DIM:(g * Q_PER_KV + r + 1) * HEAD_DIM] for r in range(Q_PER_KV)],
                axis=0).astype(BF16)
            kg = kall[:, g * HEAD_DIM:(g + 1) * HEAD_DIM]
            vg = vall[:, g * HEAD_DIM:(g + 1) * HEAD_DIM]
            s = lax.dot_general(qg, kg, (((1,), (1,)), ((), ())), preferred_element_type=F32) * (HEAD_DIM ** -0.5)
            s = jnp.where(mask, s, NEG_INF)
            sink = jnp.concatenate(
                [jnp.full((t_new, 1), sink_ref[g * Q_PER_KV + r], F32) for r in range(Q_PER_KV)], axis=0)
            probs = _softmax_with_sink(s, sink).astype(BF16)
            og = jnp.dot(probs, vg, preferred_element_type=F32)
            for r in range(Q_PER_KV):
                outs.append(og[r * t_new:(r + 1) * t_new])
        seq_outs.append(jnp.concatenate(outs, axis=1))
    o_ref[...] = jnp.concatenate(seq_outs, axis=0).astype(o_ref.dtype)


def _attn_sample(qkv, attn, row0, cache_k, cache_v, sinks, tables, n_seq, t_new, attn_w, kv_w):
    rows = SEQS_PER_STEP * t_new
    rb0 = row0 // rows
    kb, vb = attn_w // kv_w, attn_w // kv_w + 1
    n_cache = cache_k.shape[1]
    tab_spec = pl.BlockSpec((rows, LANES), lambda i, s: (0, 0))
    cache_spec = pl.BlockSpec((SEQS_PER_STEP, n_cache, kv_w), lambda i, s: (i, 0, 0))
    return pl.pallas_call(
        functools.partial(_attn_sample_kernel, n_kv=kv_w // HEAD_DIM, t_new=t_new),
        grid_spec=pltpu.PrefetchScalarGridSpec(
            num_scalar_prefetch=1,
            grid=(n_seq // SEQS_PER_STEP,),
            in_specs=[pl.BlockSpec((rows, attn_w), lambda i, s: (rb0 + i, 0)),
                      pl.BlockSpec((rows, kv_w), lambda i, s: (rb0 + i, kb)),
                      pl.BlockSpec((rows, kv_w), lambda i, s: (rb0 + i, vb)),
                      cache_spec, cache_spec, tab_spec, tab_spec, tab_spec,
                      pl.BlockSpec(memory_space=pl.ANY)],
            out_specs=[pl.BlockSpec((rows, attn_w), lambda i, s: (rb0 + i, 0)), cache_spec, cache_spec]),
        out_shape=[jax.ShapeDtypeStruct(attn.shape, attn.dtype),
                   jax.ShapeDtypeStruct(cache_k.shape, F32),
                   jax.ShapeDtypeStruct(cache_v.shape, F32)],
        input_output_aliases={9: 0},
        compiler_params=_params(1, 32),
        name="attn_sample",
    )(sinks, qkv, qkv, qkv, cache_k, cache_v, *tables, attn)


def _pool_prompt_kernel(z_ref, halo_ref, w_ref, sc_ref, o_ref, ext_ref, *, tm, pg, nb):
    i = pl.program_id(0)

    @pl.when(i >= nb)
    def _():
        o_ref[...] = jnp.zeros_like(o_ref)

    @pl.when(i < nb)
    def _():
        _pool_prompt_block(z_ref, halo_ref, w_ref, sc_ref, o_ref, ext_ref, tm=tm, pg=pg)


def _pool_prompt_block(z_ref, halo_ref, w_ref, sc_ref, o_ref, ext_ref, *, tm, pg):
    i = pl.program_id(0)
    ext_ref[0:POOL_HALO, :] = jnp.where(i > 0, halo_ref[...], 0.0)
    ext_ref[POOL_HALO:, :] = z_ref[...]
    pos = i * tm + lax.broadcasted_iota(I32, (tm, 1), 0)
    for g, w in enumerate(POOL_WINDOWS):
        c0, c1 = g * pg, (g + 1) * pg
        cur = ext_ref[POOL_HALO:POOL_HALO + tm, c0:c1]
        acc = cur
        for d in range(1, w):
            acc = acc + ext_ref[POOL_HALO - d:POOL_HALO - d + tm, c0:c1]
        cnt = jnp.minimum(pos + 1, w).astype(F32)
        pooled = (acc / cnt - cur).astype(BF16)
        mixed = jnp.dot(pooled, w_ref[g].astype(BF16), preferred_element_type=F32) * sc_ref[:, c0:c1]
        o_ref[:, c0:c1] = mixed.astype(o_ref.dtype)


def _pool_prompt(z, n_rows, w_grp, scale):
    n, pw = z.shape
    tm = TM_ROW
    hb = tm // POOL_HALO
    return pl.pallas_call(
        functools.partial(_pool_prompt_kernel, tm=tm, pg=pw // len(POOL_WINDOWS), nb=n_rows // tm),
        grid=(n // tm,),
        in_specs=[pl.BlockSpec((tm, pw), lambda i: (i, 0)),
                  pl.BlockSpec((POOL_HALO, pw), lambda i: (jnp.maximum(i * hb - 1, 0), 0)),
                  pl.BlockSpec(w_grp.shape, lambda i: (0, 0, 0)),
                  pl.BlockSpec((1, pw), lambda i: (0, 0))],
        out_specs=pl.BlockSpec((tm, pw), lambda i: (i, 0)),
        out_shape=jax.ShapeDtypeStruct((n, pw), BF16),
        scratch_shapes=[pltpu.VMEM((tm + POOL_HALO, pw), F32)],
        compiler_params=_params(1, 40),
        name="pool_prompt",
    )(z, z, w_grp, scale.reshape(1, pw))


def _pool_sample_kernel(st_ref, z_ref, w_ref, sc_ref, mixed_in_ref, o_ref, ns_ref, ext_ref, *, pg, t_new):
    del mixed_in_ref
    n_st = st_ref.shape[1]
    seqs = st_ref.shape[0]
    hist = ext_ref.shape[1] - t_new
    ext_ref[:, hist - n_st:hist, :] = st_ref[...]
    ext_ref[:, hist:, :] = z_ref[...]
    ns_ref[...] = ext_ref[:, hist + t_new - n_st:, :]
    for g, w in enumerate(POOL_WINDOWS):
        c0, c1 = g * pg, (g + 1) * pg
        cur = ext_ref[:, hist:, c0:c1]
        acc = cur
        for d in range(1, w):
            acc = acc + ext_ref[:, hist - d:hist - d + t_new, c0:c1]
        pooled = (acc / float(w) - cur).reshape(seqs * t_new, pg).astype(BF16)
        mixed = jnp.dot(pooled, w_ref[g].astype(BF16), preferred_element_type=F32) * sc_ref[:, c0:c1]
        o_ref[:, c0:c1] = mixed.astype(o_ref.dtype)


def _pool_sample(state, z, mixed, row0, w_grp, scale, t_new):
    n_seq, n_st, pw = state.shape
    assert t_new == SUBLANES and n_st <= POOL_HALO - 1
    z3 = z.reshape(z.shape[0] // t_new, t_new, pw)
    sb0 = row0 // t_new // POOL_SEQS
    rows = POOL_SEQS * t_new
    st_spec = pl.BlockSpec((POOL_SEQS, n_st, pw), lambda i: (i, 0, 0))
    return pl.pallas_call(
        functools.partial(_pool_sample_kernel, pg=pw // len(POOL_WINDOWS), t_new=t_new),
        grid=(n_seq // POOL_SEQS,),
        in_specs=[st_spec,
                  pl.BlockSpec((POOL_SEQS, t_new, pw), lambda i: (sb0 + i, 0, 0)),
                  pl.BlockSpec(w_grp.shape, lambda i: (0, 0, 0)),
                  pl.BlockSpec((1, pw), lambda i: (0, 0)),
                  pl.BlockSpec(memory_space=pl.ANY)],
        out_specs=[pl.BlockSpec((rows, pw), lambda i: (row0 // rows + i, 0)), st_spec],
        out_shape=[jax.ShapeDtypeStruct(mixed.shape, mixed.dtype), jax.ShapeDtypeStruct(state.shape, F32)],
        scratch_shapes=[pltpu.VMEM((POOL_SEQS, POOL_HALO + t_new, pw), F32)],
        input_output_aliases={4: 0},
        compiler_params=_params(1, 40),
        name="pool_sample",
    )(state, z3, w_grp, scale.reshape(1, pw), mixed)


def _merge_kernel(a1_ref, a2_ref, w1_ref, w2_ref, ga_ref, gb_ref, ba_ref, bb_ref, o_ref, w1b_ref, w2b_ref):
    @pl.when(pl.program_id(1) == 0)
    def _():
        w1b_ref[...] = w1_ref[...].astype(BF16)
        w2b_ref[...] = w2_ref[...].astype(BF16)

    a_up = jnp.dot(a1_ref[...], w1b_ref[...], preferred_element_type=F32)
    b_up = jnp.dot(a2_ref[...], w2b_ref[...], preferred_element_type=F32)
    merged = (jax.nn.sigmoid(ga_ref[...] + ba_ref[...]) * a_up
              + jax.nn.sigmoid(gb_ref[...] + bb_ref[...]) * b_up)
    o_ref[...] = merged.astype(o_ref.dtype)


def _merge(attn, mixed, w_attn_up, w_pool_up, gates, b_gate):
    n, ka = attn.shape
    kp = mixed.shape[1]
    d = w_attn_up.shape[1]
    nj = d // TN
    return pl.pallas_call(
        _merge_kernel,
        grid=(nj, n // TM),
        in_specs=[pl.BlockSpec((TM, ka), lambda j, i: (i, 0)),
                  pl.BlockSpec((TM, kp), lambda j, i: (i, 0)),
                  pl.BlockSpec((ka, TN), lambda j, i: (0, j)),
                  pl.BlockSpec((kp, TN), lambda j, i: (0, j)),
                  pl.BlockSpec((TM, TN), lambda j, i: (i, j)),
                  pl.BlockSpec((TM, TN), lambda j, i: (i, nj + j)),
                  pl.BlockSpec((1, TN), lambda j, i: (0, j)),
                  pl.BlockSpec((1, TN), lambda j, i: (0, nj + j))],
        out_specs=pl.BlockSpec((TM, TN), lambda j, i: (i, j)),
        out_shape=jax.ShapeDtypeStruct((n, d), BF16),
        scratch_shapes=[pltpu.VMEM((ka, TN), BF16), pltpu.VMEM((kp, TN), BF16)],
        compiler_params=_params(2, 56),
        name="merge",
    )(attn, mixed, w_attn_up, w_pool_up, gates, gates, b_gate, b_gate)


def _out_proj_kernel(a_ref, w_ref, ha_ref, hb_ref, o_ref, wb_ref, *, nb_first):
    @pl.when(pl.program_id(1) == 0)
    def _():
        wb_ref[...] = w_ref[...].astype(BF16)

    h = jnp.where(pl.program_id(1) < nb_first, ha_ref[...], hb_ref[...])
    o_ref[...] = h + jnp.dot(a_ref[...], wb_ref[...], preferred_element_type=F32)


def _out_proj(merged, w_out, ha, hb):
    n, d = merged.shape
    nb_first = ha.shape[0] // TM
    h_first = pl.BlockSpec((TM, TN), lambda j, i: (jnp.minimum(i, nb_first - 1), j))
    h_last = pl.BlockSpec((TM, TN), lambda j, i: (jnp.maximum(i - nb_first, 0), j))
    return pl.pallas_call(
        functools.partial(_out_proj_kernel, nb_first=nb_first),
        grid=(d // TN, n // TM),
        in_specs=[pl.BlockSpec((TM, d), lambda j, i: (i, 0)),
                  pl.BlockSpec((d, TN), lambda j, i: (0, j)),
                  h_first, h_last],
        out_specs=pl.BlockSpec((TM, TN), lambda j, i: (i, j)),
        out_shape=jax.ShapeDtypeStruct((n, d), F32),
        scratch_shapes=[pltpu.VMEM((d, TN), BF16)],
        compiler_params=_params(2, 56),
        name="out_proj",
    )(merged, w_out, ha, hb)


def _router_kernel(h_ref, g_ref, wr_ref, br_ref, info_ref, cw_ref, xf_ref, cnt_ref, base_ref):
    tm = h_ref.shape[0]

    @pl.when(pl.program_id(0) == 0)
    def _():
        base_ref[...] = jnp.zeros_like(base_ref)

    xf = _rms_rows(h_ref[...], g_ref[...]).astype(BF16)
    xf_ref[...] = xf.reshape(xf_ref.shape)
    logits = jnp.dot(xf, wr_ref[...].astype(BF16), preferred_element_type=F32) + br_ref[...]
    lane = lax.broadcasted_iota(I32, logits.shape, 1)
    big = jnp.int32(1 << 20)
    ninf = jnp.float32(-jnp.inf)

    is_g = lane < N_GROUPS
    gl = jnp.where(is_g, logits, ninf)
    gmax = jnp.max(gl, axis=-1, keepdims=True)
    gden = jnp.sum(jnp.exp(gl - gmax), axis=-1, keepdims=True)
    g_p = 1.0 / gden
    g_idx = jnp.min(jnp.where(gl == gmax, lane, big), axis=-1, keepdims=True)

    e_lane = lane - N_GROUPS
    in_grp = (e_lane >= g_idx * EXPERTS_PER_GROUP) & (e_lane < (g_idx + 1) * EXPERTS_PER_GROUP)
    el = jnp.where(in_grp, logits, ninf)
    m1 = jnp.max(el, axis=-1, keepdims=True)
    i1 = jnp.min(jnp.where(in_grp & (el == m1), e_lane, big), axis=-1, keepdims=True)
    el2 = jnp.where(e_lane == i1, ninf, el)
    m2 = jnp.max(el2, axis=-1, keepdims=True)
    i2 = jnp.min(jnp.where(in_grp & (el2 == m2), e_lane, big), axis=-1, keepdims=True)
    t = jnp.exp(m2 - m1)
    den = 1.0 + t
    c1 = g_p * (1.0 / den)
    c2 = g_p * (t / den)
    cw_ref[...] = jnp.where(lane == 0, c1, jnp.where(lane == 1, c2, 0.0))

    hit1, hit2 = lane == i1, lane == i2
    onehot = (hit1 | hit2).astype(BF16)
    earlier = (lax.broadcasted_iota(I32, (tm, tm), 1) < lax.broadcasted_iota(I32, (tm, tm), 0)).astype(BF16)
    before = jnp.dot(earlier, onehot, preferred_element_type=F32) + base_ref[...]
    r1 = jnp.sum(jnp.where(hit1, before, 0.0), axis=-1, keepdims=True).astype(I32)
    r2 = jnp.sum(jnp.where(hit2, before, 0.0), axis=-1, keepdims=True).astype(I32)
    base_ref[...] += jnp.sum(onehot.astype(F32), axis=0, keepdims=True)
    cnt_ref[...] = base_ref[...].astype(I32)
    info_ref[...] = jnp.where(lane == 0, i1, jnp.where(lane == 1, i2,
                              jnp.where(lane == 2, r1, jnp.where(lane == 3, r2, 0))))


def _router(h, g_ffn, w_rg, b_rg, w_re, b_re):
    n, d = h.shape
    n_log = N_GROUPS + N_EXPERTS
    wr = jnp.concatenate([w_rg, w_re, jnp.zeros((d, LANES - n_log), F32)], axis=1)
    br = jnp.concatenate([b_rg, b_re, jnp.zeros((LANES - n_log,), F32)]).reshape(1, LANES)
    lane_spec = pl.BlockSpec((TM_ROW, LANES), lambda i: (i, 0))
    return pl.pallas_call(
        _router_kernel,
        grid=(n // TM_ROW,),
        in_specs=[pl.BlockSpec((TM_ROW, d), lambda i: (i, 0)),
                  pl.BlockSpec((1, d), lambda i: (0, 0)),
                  pl.BlockSpec((d, LANES), lambda i: (0, 0)),
                  pl.BlockSpec((1, LANES), lambda i: (0, 0))],
        out_specs=[lane_spec, lane_spec,
                   pl.BlockSpec((TM_ROW, d // LANES, LANES), lambda i: (i, 0, 0)),
                   pl.BlockSpec((1, LANES), lambda i: (0, 0))],
        out_shape=[jax.ShapeDtypeStruct((n, LANES), I32), jax.ShapeDtypeStruct((n, LANES), F32),
                   jax.ShapeDtypeStruct((n, d // LANES, LANES), BF16), jax.ShapeDtypeStruct((1, LANES), I32)],
        scratch_shapes=[pltpu.VMEM((1, LANES), F32)],
        compiler_params=_params(1, 40),
        name="router",
    )(h, g_ffn.reshape(1, d), wr, br)


def _dispatch_plan(info, counts, n_tiles):
    counts = counts[0, :N_EXPERTS]
    tiles_per = (counts + TM_E - 1) // TM_E
    tiles_end = jnp.cumsum(tiles_per)
    row_start = (tiles_end - tiles_per) * TM_E
    eid, rank = info[:, :TOP_K], info[:, TOP_K:2 * TOP_K]
    onehot = (eid[:, :, None] == jnp.arange(N_EXPERTS, dtype=I32)[None, None, :]).astype(I32)
    pos = (jnp.sum(onehot * row_start[None, None, :], axis=-1) + rank).reshape(-1).astype(I32)
    tile_ids = jnp.arange(n_tiles, dtype=I32)
    tile_expert = jnp.minimum(
        jnp.sum((tile_ids[:, None] >= tiles_end[None, :]).astype(I32), axis=1), N_EXPERTS - 1).astype(I32)
    n_used = tiles_end[-1:].astype(I32)
    last_tile_row = (jnp.clip(tiles_end - 1, 0, n_tiles - 1) * TM_E).astype(I32)
    return pos, tile_expert, n_used, last_tile_row


def _dispatch_kernel(pos_ref, ltr_ref, nu_ref, xf_hbm, xs_hbm, zero_ref, sem):
    i = pl.program_id(0)
    n_tiles = xs_hbm.shape[0] // TM_E

    def clear_tile(row):
        return pltpu.make_async_copy(zero_ref, xs_hbm.at[pl.ds(pl.multiple_of(row, TM_E), TM_E)], sem)

    @pl.when(i == 0)
    def _():
        zero_ref[...] = jnp.zeros_like(zero_ref)
        for e in range(N_EXPERTS):
            clear_tile(ltr_ref[e]).start()

        def start_unused(t, carry):
            clear_tile(t * TM_E).start()
            return carry

        def wait_one(t, carry):
            clear_tile(0).wait()
            return carry

        lax.fori_loop(nu_ref[0], n_tiles, start_unused, 0)
        lax.fori_loop(nu_ref[0] - N_EXPERTS, n_tiles, wait_one, 0)

    base = i * TM_DISPATCH

    def row_copy(src_row, dst_row):
        return pltpu.make_async_copy(xf_hbm.at[pl.ds(src_row, 1)], xs_hbm.at[pl.ds(dst_row, 1)], sem)

    def issue(r, carry):
        for k in range(TOP_K):
            row_copy(base + r, pos_ref[(base + r) * TOP_K + k]).start()
        return carry

    def drain(r, carry):
        for k in range(TOP_K):
            row_copy(0, 0).wait()
        return carry

    lax.fori_loop(0, TM_DISPATCH, issue, 0)
    lax.fori_loop(0, TM_DISPATCH, drain, 0)


def _dispatch(pos, last_tile_row, n_used, xf3, n_tiles):
    n, c, l = xf3.shape
    return pl.pallas_call(
        _dispatch_kernel,
        grid_spec=pltpu.PrefetchScalarGridSpec(
            num_scalar_prefetch=3,
            grid=(n // TM_DISPATCH,),
            in_specs=[pl.BlockSpec(memory_space=pl.ANY)],
            out_specs=pl.BlockSpec(memory_space=pl.ANY),
            scratch_shapes=[pltpu.VMEM((TM_E, c, l), xf3.dtype), pltpu.SemaphoreType.DMA(())]),
        out_shape=jax.ShapeDtypeStruct((n_tiles * TM_E, c, l), xf3.dtype),
        compiler_params=_params(1, 16),
        name="moe_dispatch",
    )(pos, last_tile_row, n_used, xf3)


def _expert_changed(te_ref, t):
    return (t == 0) | (te_ref[t] != te_ref[jnp.maximum(t - 1, 0)])


def _moe_up_kernel(te_ref, nu_ref, xs_ref, wg_ref, wu_ref, o_ref, wgb_ref, wub_ref):
    t = pl.program_id(1)

    @pl.when(_expert_changed(te_ref, t))
    def _():
        wgb_ref[...] = wg_ref[...].astype(BF16)
        wub_ref[...] = wu_ref[...].astype(BF16)

    @pl.when(t < nu_ref[0])
    def _():
        x = xs_ref[...].reshape(xs_ref.shape[0], -1)
        gate = jnp.dot(x, wgb_ref[...], preferred_element_type=F32)
        up = jnp.dot(x, wub_ref[...], preferred_element_type=F32)
        o_ref[...] = (jax.nn.silu(gate) * up).astype(o_ref.dtype)

    @pl.when(t >= nu_ref[0])
    def _():
        o_ref[...] = jnp.zeros_like(o_ref)


def _moe_up(tile_expert, n_used, xs3, w_gate, w_up, n_tiles):
    d, f = w_gate.shape[1], w_gate.shape[2]
    tf = TN
    w_spec = pl.BlockSpec((None, d, tf), lambda j, t, te, nu: (te[t], 0, j))
    return pl.pallas_call(
        _moe_up_kernel,
        grid_spec=pltpu.PrefetchScalarGridSpec(
            num_scalar_prefetch=2,
            grid=(f // tf, n_tiles),
            in_specs=[pl.BlockSpec((TM_E,) + xs3.shape[1:], lambda j, t, te, nu: (jnp.minimum(t, nu[0] - 1), 0, 0)),
                      w_spec, w_spec],
            out_specs=pl.BlockSpec((TM_E, tf), lambda j, t, te, nu: (t, j)),
            scratch_shapes=[pltpu.VMEM((d, tf), BF16), pltpu.VMEM((d, tf), BF16)]),
        out_shape=jax.ShapeDtypeStruct((n_tiles * TM_E, f), BF16),
        compiler_params=_params(2, 56),
        name="moe_up",
    )(tile_expert, n_used, xs3, w_gate, w_up)


def _moe_down_kernel(te_ref, nu_ref, hid_ref, wd_ref, o_ref, wdb_ref):
    t = pl.program_id(1)

    @pl.when(_expert_changed(te_ref, t))
    def _():
        wdb_ref[...] = wd_ref[...].astype(BF16)

    @pl.when(t < nu_ref[0])
    def _():
        y = jnp.dot(hid_ref[...], wdb_ref[...], preferred_element_type=F32)
        o_ref[...] = y.reshape(o_ref.shape)

    @pl.when(t >= nu_ref[0])
    def _():
        o_ref[...] = jnp.zeros_like(o_ref)


def _moe_down(tile_expert, n_used, hid, w_down, n_tiles):
    f, d = w_down.shape[1], w_down.shape[2]
    tn = SUBLANES * LANES
    return pl.pallas_call(
        _moe_down_kernel,
        grid_spec=pltpu.PrefetchScalarGridSpec(
            num_scalar_prefetch=2,
            grid=(d // tn, n_tiles),
            in_specs=[pl.BlockSpec((TM_E, f), lambda j, t, te, nu: (t, 0)),
                      pl.BlockSpec((None, f, tn), lambda j, t, te, nu: (te[t], 0, j))],
            out_specs=pl.BlockSpec((TM_E, SUBLANES, LANES), lambda j, t, te, nu: (t, j, 0)),
            scratch_shapes=[pltpu.VMEM((f, tn), BF16)]),
        out_shape=jax.ShapeDtypeStruct((n_tiles * TM_E, d // LANES, LANES), F32),
        compiler_params=_params(2, 40),
        name="moe_down",
    )(tile_expert, n_used, hid, w_down)


def _combine_kernel(pos_ref, h_ref, cw_ref, y_hbm, g_ref, ho_ref, xo_ref, buf_ref, sem):
    i = pl.program_id(0)
    tm, d = h_ref.shape

    def row_copy(slot, k, r, src_row):
        return pltpu.make_async_copy(y_hbm.at[pl.ds(src_row, 1)], buf_ref.at[slot, k, pl.ds(r, 1)], sem.at[slot])

    def issue_step(step, slot):
        def body(r, carry):
            for k in range(TOP_K):
                row_copy(slot, k, r, pos_ref[(step * tm + r) * TOP_K + k]).start()
            return carry
        lax.fori_loop(0, tm, body, 0)

    @pl.when(i == 0)
    def _():
        issue_step(0, 0)

    @pl.when(i + 1 < pl.num_programs(0))
    def _():
        issue_step(i + 1, (i + 1) % 2)

    slot = i % 2

    def drain(r, carry):
        for k in range(TOP_K):
            row_copy(slot, k, r, 0).wait()
        return carry

    lax.fori_loop(0, tm, drain, 0)
    cw = cw_ref[...]
    y1 = buf_ref[slot, 0].reshape(tm, d)
    y2 = buf_ref[slot, 1].reshape(tm, d)
    h2 = h_ref[...] + (cw[:, 0:1] * y1 + cw[:, 1:2] * y2)
    ho_ref[...] = h2
    xo_ref[...] = _rms_rows(h2, g_ref[...]).astype(xo_ref.dtype)


def _combine(pos, h, cw, y3, g_ple):
    n, d = h.shape
    row_spec = pl.BlockSpec((TM_ROW, d), lambda i, p: (i, 0))
    return pl.pallas_call(
        _combine_kernel,
        grid_spec=pltpu.PrefetchScalarGridSpec(
            num_scalar_prefetch=1,
            grid=(n // TM_ROW,),
            in_specs=[row_spec,
                      pl.BlockSpec((TM_ROW, LANES), lambda i, p: (i, 0)),
                      pl.BlockSpec(memory_space=pl.ANY),
                      pl.BlockSpec((1, d), lambda i, p: (0, 0))],
            out_specs=[row_spec, row_spec],
            scratch_shapes=[pltpu.VMEM((2, TOP_K, TM_ROW) + y3.shape[1:], F32), pltpu.SemaphoreType.DMA((2,))]),
        out_shape=[jax.ShapeDtypeStruct((n, d), F32), jax.ShapeDtypeStruct((n, d), BF16)],
        compiler_params=_params(1, 56),
        name="moe_combine",
    )(pos, h, cw, y3, g_ple.reshape(1, d))


def _ple_kernel(xn_ref, wg_ref, bg_ref, pa_ref, pb_ref, wp_ref, h_ref, o_ref, wgb_ref, *, nb_first):
    @pl.when(pl.program_id(1) == 0)
    def _():
        wgb_ref[...] = wg_ref[...].astype(BF16)

    gate = jax.nn.sigmoid(jnp.dot(xn_ref[...], wgb_ref[...], preferred_element_type=F32) + bg_ref[...])
    p = jnp.where(pl.program_id(1) < nb_first, pa_ref[...], pb_ref[...]).astype(BF16)
    ple = jnp.dot(p, wp_ref[...].astype(BF16), preferred_element_type=F32)
    o_ref[...] = h_ref[...] + gate * ple


def _ple(xn, w_gate, b_gate, pa, pb, w_proj, h):
    n, d = h.shape
    pd = pa.shape[1]
    nb_first = pa.shape[0] // TM
    p_first = pl.BlockSpec((TM, pd), lambda j, i: (jnp.minimum(i, nb_first - 1), 0))
    p_last = pl.BlockSpec((TM, pd), lambda j, i: (jnp.maximum(i - nb_first, 0), 0))
    return pl.pallas_call(
        functools.partial(_ple_kernel, nb_first=nb_first),
        grid=(d // TN, n // TM),
        in_specs=[pl.BlockSpec((TM, d), lambda j, i: (i, 0)),
                  pl.BlockSpec((d, TN), lambda j, i: (0, j)),
                  pl.BlockSpec((1, TN), lambda j, i: (0, j)),
                  p_first, p_last,
                  pl.BlockSpec((pd, TN), lambda j, i: (0, j)),
                  pl.BlockSpec((TM, TN), lambda j, i: (i, j))],
        out_specs=pl.BlockSpec((TM, TN), lambda j, i: (i, j)),
        out_shape=jax.ShapeDtypeStruct((n, d), F32),
        scratch_shapes=[pltpu.VMEM((d, TN), BF16)],
        compiler_params=_params(2, 56),
        name="ple",
    )(xn, w_gate, b_gate.reshape(1, d), pa, pb, w_proj, h)


def _layer(xn, h_p, h_s, p_p, p_s, cache_k, cache_v, state_pool, lw):
    n, d = xn.shape
    n_prompt = h_p.shape[0]
    n_seq, n_cache, n_kv, _ = cache_k.shape
    t_new = (n - n_prompt) // n_seq
    kv_w = n_kv * HEAD_DIM
    attn_w = lw['w_attn_up'].shape[0]
    pool_w = lw['w_pool_up'].shape[0]
    n_pool = state_pool.shape[1]

    qkv = _in_proj(xn, lw['w_in'], 0, attn_w + 2 * kv_w, "in_proj_qkv")
    z = _in_proj(xn, lw['w_in'], attn_w + 2 * kv_w, pool_w, "in_proj_pool")
    gates = _in_proj(xn, lw['w_in'], attn_w + 2 * kv_w + pool_w, 2 * d, "in_proj_gates")

    tab_p = _rope_tables(jnp.arange(n_prompt, dtype=F32))
    tab_s = tuple(jnp.tile(a, (SEQS_PER_STEP, 1))
                  for a in _rope_tables(PAST_LEN + jnp.arange(t_new, dtype=F32)))
    attn, k_rot_p = _attn_prompt(qkv, lw['attn_sinks'], tab_p, n_prompt, attn_w, kv_w)
    attn, new_k_s, new_v_s = _attn_sample(
        qkv, attn, n_prompt, cache_k.reshape(n_seq, n_cache, kv_w), cache_v.reshape(n_seq, n_cache, kv_w),
        lw['attn_sinks'], tab_s, n_seq, t_new, attn_w, kv_w)

    mixed = _pool_prompt(z, n_prompt, lw['w_pool_grp'], lw['pool_scale'])
    mixed, new_z_s = _pool_sample(state_pool, z, mixed, n_prompt, lw['w_pool_grp'], lw['pool_scale'], t_new)

    merged = _merge(attn, mixed, lw['w_attn_up'], lw['w_pool_up'], gates, lw['b_gate'].reshape(1, 2 * d))
    h1 = _out_proj(merged, lw['w_out'], h_p, h_s)

    n_tiles = (n * TOP_K) // TM_E + N_EXPERTS
    info, cw, xf3, counts = _router(h1, lw['g_ffn'], lw['w_route_group'], lw['b_route_group'],
                                    lw['w_route_expert'], lw['b_route_expert'])
    pos, tile_expert, n_used, last_tile_row = _dispatch_plan(info, counts, n_tiles)
    xs3 = _dispatch(pos, last_tile_row, n_used, xf3, n_tiles)
    hid = _moe_up(tile_expert, n_used, xs3, lw['w_exp_gate'], lw['w_exp_up'], n_tiles)
    y3 = _moe_down(tile_expert, n_used, hid, lw['w_exp_down'], n_tiles)
    h2, xn2 = _combine(pos, h1, cw, y3, lw['g_ple'])

    h3 = _ple(xn2, lw['w_ple_gate'], lw['b_ple_gate'], p_p, p_s, lw['w_ple_proj'], h2)

    new_k_p = k_rot_p[n_prompt - n_cache:].reshape(1, n_cache, n_kv, HEAD_DIM)
    new_v_p = qkv[n_prompt - n_cache:n_prompt, attn_w + kv_w:].reshape(1, n_cache, n_kv, HEAD_DIM)
    new_z_p = z[n_prompt - n_pool:n_prompt].reshape(1, n_pool, pool_w)
    return (h3, new_k_p, new_v_p, new_z_p,
            new_k_s.reshape(cache_k.shape), new_v_s.reshape(cache_v.shape), new_z_s)


def kernel(x_prompt, x_sample, cache_k, cache_v, state_pool, p_prompt, p_sample, g_mix, w_in, b_gate, attn_sinks, w_pool_grp, pool_scale, w_attn_up, w_pool_up, w_out, g_ffn, w_route_group, b_route_group, w_route_expert, b_route_expert, w_exp_gate, w_exp_up, w_exp_down, g_ple, w_ple_gate, b_ple_gate, w_ple_proj, g_final):
    batch, seq, d = x_prompt.shape
    n_seq, t_new, _ = x_sample.shape
    depth = w_in.shape[0]
    assert batch == 1, "prompt rows are treated as one sequence"
    n_prompt = batch * seq
    n_sample = n_seq * t_new
    n = n_prompt + n_sample
    assert n_prompt % TM == 0 and n_sample % TM == 0 and (n * TOP_K) % TM_E == 0 and n % TM_DISPATCH == 0
    assert n_seq % SEQS_PER_STEP == 0 and n_seq % POOL_SEQS == 0 and t_new & (t_new - 1) == 0
    assert cache_k.shape[2] == WINDOW and state_pool.shape[2] == POOL_HALO - 1
    assert depth == 1, "one layer per step"

    weights = dict(g_mix=g_mix, w_in=w_in, b_gate=b_gate, attn_sinks=attn_sinks, w_pool_grp=w_pool_grp,
                   pool_scale=pool_scale, w_attn_up=w_attn_up, w_pool_up=w_pool_up, w_out=w_out, g_ffn=g_ffn,
                   w_route_group=w_route_group, b_route_group=b_route_group, w_route_expert=w_route_expert,
                   b_route_expert=b_route_expert, w_exp_gate=w_exp_gate, w_exp_up=w_exp_up,
                   w_exp_down=w_exp_down, g_ple=g_ple, w_ple_gate=w_ple_gate, b_ple_gate=b_ple_gate,
                   w_ple_proj=w_ple_proj)
    lw = {name: w[0] for name, w in weights.items()}
    h_p = x_prompt.reshape(n_prompt, d)
    h_s = x_sample.reshape(n_sample, d)
    xn = _rmsnorm_in2(h_p, h_s, lw['g_mix'], BF16, "norm_mix")
    out = _layer(xn, h_p, h_s, p_prompt[0].reshape(n_prompt, -1), p_sample[0].reshape(n_sample, -1),
                 cache_k[0], cache_v[0], state_pool[0], lw)
    y_p, y_s = _rmsnorm_out2(out[0], g_final, n_prompt, "norm_final")
    return (y_p.reshape(batch, seq, d), y_s.reshape(n_seq, t_new, d)) + tuple(piece[None] for piece in out[1:])
```

```python
import functools

import jax
import jax.numpy as jnp
from jax import lax
from jax.experimental import pallas as pl
from jax.experimental.pallas import tpu as pltpu

F32 = jnp.float32
BF16 = jnp.bfloat16
I32 = jnp.int32

HEAD_DIM = 64
Q_PER_KV = 4
WINDOW = 128
PAST_LEN = 8192
ROT_DIM = HEAD_DIM // 4
ROPE_THETA = 500000.0
POOL_WINDOWS = (2, 4, 8, 16)
POOL_HALO = 16
N_GROUPS = 4
EXPERTS_PER_GROUP = 4
N_EXPERTS = N_GROUPS * EXPERTS_PER_GROUP
TOP_K = 2
RMS_EPS = 1e-6
NEG_INF = -1e30
LANES = 128
SUBLANES = 8
MIB = 1024 * 1024

TM = 1024
TN = 512
TM_ROW = 256
TM_E = 512
TM_DISPATCH = 512
SEQS_PER_STEP = 4
POOL_SEQS = 16


def _params(n_axes, vmem_mib):
    return pltpu.CompilerParams(dimension_semantics=("arbitrary",) * n_axes,
                                vmem_limit_bytes=vmem_mib * MIB)


def _rms_rows(x, g):
    ms = jnp.mean(x * x, axis=-1, keepdims=True)
    return x * lax.rsqrt(ms + RMS_EPS) * g


def _first_last_specs(nb_first, block):
    zeros = (0,) * (len(block) - 1)
    first = pl.BlockSpec(block, lambda i, *_: (jnp.minimum(i, nb_first - 1),) + zeros)
    last = pl.BlockSpec(block, lambda i, *_: (jnp.maximum(i - nb_first, 0),) + zeros)
    return first, last


def _norm_in2_kernel(xa_ref, xb_ref, g_ref, o_ref, *, nb_first):
    i = pl.program_id(0)

    @pl.when(i < nb_first)
    def _():
        o_ref[...] = _rms_rows(xa_ref[...], g_ref[...]).astype(o_ref.dtype)

    @pl.when(i >= nb_first)
    def _():
        o_ref[...] = _rms_rows(xb_ref[...], g_ref[...]).astype(o_ref.dtype)


def _rmsnorm_in2(xa, xb, g, out_dtype, name):
    (na, d), nb = xa.shape, xb.shape[0]
    nb_first = na // TM_ROW
    spec_a, spec_b = _first_last_specs(nb_first, (TM_ROW, d))
    return pl.pallas_call(
        functools.partial(_norm_in2_kernel, nb_first=nb_first),
        grid=((na + nb) // TM_ROW,),
        in_specs=[spec_a, spec_b, pl.BlockSpec((1, d), lambda i: (0, 0))],
        out_specs=pl.BlockSpec((TM_ROW, d), lambda i: (i, 0)),
        out_shape=jax.ShapeDtypeStruct((na + nb, d), out_dtype),
        compiler_params=_params(1, 40),
        name=name,
    )(xa, xb, g.reshape(1, d))


def _norm_out2_kernel(x_ref, g_ref, oa_ref, ob_ref, *, nb_first):
    i = pl.program_id(0)

    @pl.when(i < nb_first)
    def _():
        oa_ref[...] = _rms_rows(x_ref[...], g_ref[...])

    @pl.when(i >= nb_first)
    def _():
        ob_ref[...] = _rms_rows(x_ref[...], g_ref[...])


def _rmsnorm_out2(x, g, na, name):
    n, d = x.shape
    nb_first = na // TM_ROW
    spec_a, spec_b = _first_last_specs(nb_first, (TM_ROW, d))
    return pl.pallas_call(
        functools.partial(_norm_out2_kernel, nb_first=nb_first),
        grid=(n // TM_ROW,),
        in_specs=[pl.BlockSpec((TM_ROW, d), lambda i: (i, 0)), pl.BlockSpec((1, d), lambda i: (0, 0))],
        out_specs=[spec_a, spec_b],
        out_shape=[jax.ShapeDtypeStruct((na, d), F32), jax.ShapeDtypeStruct((n - na, d), F32)],
        compiler_params=_params(1, 40),
        name=name,
    )(x, g.reshape(1, d))


def _proj_kernel(a_ref, w_ref, o_ref, wb_ref):
    @pl.when(pl.program_id(1) == 0)
    def _():
        wb_ref[...] = w_ref[...].astype(BF16)

    o_ref[...] = jnp.dot(a_ref[...], wb_ref[...], preferred_element_type=F32)


def _in_proj(xn, w, col0, width, name):
    n, d = xn.shape
    cb0 = col0 // TN
    return pl.pallas_call(
        _proj_kernel,
        grid=(width // TN, n // TM),
        in_specs=[pl.BlockSpec((TM, d), lambda j, i: (i, 0)),
                  pl.BlockSpec((d, TN), lambda j, i: (0, cb0 + j))],
        out_specs=pl.BlockSpec((TM, TN), lambda j, i: (i, j)),
        out_shape=jax.ShapeDtypeStruct((n, width), F32),
        scratch_shapes=[pltpu.VMEM((d, TN), BF16)],
        compiler_params=_params(2, 56),
        name=name,
    )(xn, w)


def _rope_tables(pos):
    half = ROT_DIM // 2
    inv_freq = ROPE_THETA ** (-jnp.arange(half, dtype=F32) * (2.0 / ROT_DIM))
    ang = pos[:, None] * inv_freq[None, :]
    cos, sin = jnp.cos(ang), jnp.sin(ang)
    t = pos.shape[0]
    pad = jnp.zeros((t, HEAD_DIM - ROT_DIM), F32)
    zeros = jnp.zeros((t, half), F32)
    cos_h = jnp.concatenate([cos, cos, pad + 1.0], axis=1)
    sa_h = jnp.concatenate([zeros, sin, pad], axis=1)
    sb_h = jnp.concatenate([-sin, zeros, pad], axis=1)
    reps = LANES // HEAD_DIM
    return tuple(jnp.tile(a, (1, reps)) for a in (cos_h, sa_h, sb_h))


def _rope(x, cos_t, sa_t, sb_t):
    pieces = []
    for c in range(x.shape[1] // LANES):
        xc = x[:, c * LANES:(c + 1) * LANES]
        pieces.append(xc * cos_t
                      + pltpu.roll(xc, ROT_DIM // 2, 1) * sa_t
                      + pltpu.roll(xc, LANES - ROT_DIM // 2, 1) * sb_t)
    return jnp.concatenate(pieces, axis=1)


def _softmax_with_sink(s, sink):
    m = jnp.maximum(jnp.max(s, axis=-1, keepdims=True), sink)
    p = jnp.exp(s - m)
    den = jnp.sum(p, axis=-1, keepdims=True) + jnp.exp(sink - m)
    return p / den


def _attn_prompt_kernel(sink_ref, q_ref, k_ref, v_ref, cos_ref, sa_ref, sb_ref,
                        o_ref, ko_ref, kprev_ref, vprev_ref, *, n_kv, nb):
    b = pl.program_id(0)

    @pl.when(b == 0)
    def _():
        kprev_ref[...] = jnp.zeros_like(kprev_ref)
        vprev_ref[...] = jnp.zeros_like(vprev_ref)

    @pl.when(b >= nb)
    def _():
        o_ref[...] = jnp.zeros_like(o_ref)

    @pl.when(b < nb)
    def _():
        _attn_prompt_block(sink_ref, q_ref, k_ref, v_ref, cos_ref, sa_ref, sb_ref,
                           o_ref, ko_ref, kprev_ref, vprev_ref, n_kv=n_kv)


def _attn_prompt_block(sink_ref, q_ref, k_ref, v_ref, cos_ref, sa_ref, sb_ref,
                       o_ref, ko_ref, kprev_ref, vprev_ref, *, n_kv):
    b = pl.program_id(0)
    cos_t, sa_t, sb_t = cos_ref[...], sa_ref[...], sb_ref[...]
    q = _rope(q_ref[...], cos_t, sa_t, sb_t)
    k = _rope(k_ref[...], cos_t, sa_t, sb_t)
    v = v_ref[...]
    ko_ref[...] = k
    kcat = jnp.concatenate([kprev_ref[...], k], axis=0).astype(BF16)
    vcat = jnp.concatenate([vprev_ref[...], v], axis=0).astype(BF16)
    kprev_ref[...] = k
    vprev_ref[...] = v

    rows = Q_PER_KV * WINDOW
    qi = lax.broadcasted_iota(I32, (rows, 2 * WINDOW), 0) & (WINDOW - 1)
    kj = lax.broadcasted_iota(I32, (rows, 2 * WINDOW), 1)
    first_key = jnp.where(b > 0, 0, WINDOW)
    mask = (kj > qi) & (kj <= qi + WINDOW) & (kj >= first_key)

    outs = []
    for g in range(n_kv):
        qg = jnp.concatenate(
            [q[:, (g * Q_PER_KV + r) * HEAD_DIM:(g * Q_PER_KV + r + 1) * HEAD_DIM] for r in range(Q_PER_KV)],
            axis=0).astype(BF16)
        kg = kcat[:, g * HEAD_DIM:(g + 1) * HEAD_DIM]
        vg = vcat[:, g * HEAD_DIM:(g + 1) * HEAD_DIM]
        s = lax.dot_general(qg, kg, (((1,), (1,)), ((), ())), preferred_element_type=F32) * (HEAD_DIM ** -0.5)
        s = jnp.where(mask, s, NEG_INF)
        sink = jnp.concatenate(
            [jnp.full((WINDOW, 1), sink_ref[g * Q_PER_KV + r], F32) for r in range(Q_PER_KV)], axis=0)
        probs = _softmax_with_sink(s, sink).astype(BF16)
        og = jnp.dot(probs, vg, preferred_element_type=F32)
        for r in range(Q_PER_KV):
            outs.append(og[r * WINDOW:(r + 1) * WINDOW])
    o_ref[...] = jnp.concatenate(outs, axis=1).astype(o_ref.dtype)


def _attn_prompt(qkv, sinks, tables, n_rows, attn_w, kv_w):
    nb = n_rows // WINDOW
    kb, vb = attn_w // kv_w, attn_w // kv_w + 1

    def blk(b, s):
        return jnp.minimum(b, nb - 1)

    tab_spec = pl.BlockSpec((WINDOW, LANES), lambda b, s: (blk(b, s), 0))
    return pl.pallas_call(
        functools.partial(_attn_prompt_kernel, n_kv=kv_w // HEAD_DIM, nb=nb),
        grid_spec=pltpu.PrefetchScalarGridSpec(
            num_scalar_prefetch=1,
            grid=(qkv.shape[0] // WINDOW,),
            in_specs=[pl.BlockSpec((WINDOW, attn_w), lambda b, s: (blk(b, s), 0)),
                      pl.BlockSpec((WINDOW, kv_w), lambda b, s: (blk(b, s), kb)),
                      pl.BlockSpec((WINDOW, kv_w), lambda b, s: (blk(b, s), vb)),
                      tab_spec, tab_spec, tab_spec],
            out_specs=[pl.BlockSpec((WINDOW, attn_w), lambda b, s: (b, 0)),
                       pl.BlockSpec((WINDOW, kv_w), lambda b, s: (blk(b, s), 0))],
            scratch_shapes=[pltpu.VMEM((WINDOW, kv_w), F32), pltpu.VMEM((WINDOW, kv_w), F32)]),
        out_shape=[jax.ShapeDtypeStruct((qkv.shape[0], attn_w), BF16),
                   jax.ShapeDtypeStruct((n_rows, kv_w), F32)],
        compiler_params=_params(1, 32),
        name="attn_prompt",
    )(sinks, qkv, qkv, qkv, *tables)


def _attn_sample_kernel(sink_ref, q_ref, k_ref, v_ref, ck_ref, cv_ref, cos_ref, sa_ref, sb_ref, attn_in_ref,
                        o_ref, nk_ref, nv_ref, *, n_kv, t_new):
    del attn_in_ref
    cos_t, sa_t, sb_t = cos_ref[...], sa_ref[...], sb_ref[...]
    q = _rope(q_ref[...], cos_t, sa_t, sb_t)
    k = _rope(k_ref[...], cos_t, sa_t, sb_t)
    v = v_ref[...]
    n_cache = ck_ref.shape[1]
    n_keys = 2 * WINDOW
    rows = Q_PER_KV * t_new
    qt = lax.broadcasted_iota(I32, (rows, n_keys), 0) & (t_new - 1)
    kj = lax.broadcasted_iota(I32, (rows, n_keys), 1)
    mask = (kj > qt + (n_cache - WINDOW)) & (kj <= qt + n_cache)
    zpad = jnp.zeros((n_keys - n_cache - t_new, k.shape[1]), F32)

    seq_outs = []
    for s_i in range(q.shape[0] // t_new):
        ck, cv = ck_ref[s_i], cv_ref[s_i]
        kn, vn = k[s_i * t_new:(s_i + 1) * t_new], v[s_i * t_new:(s_i + 1) * t_new]
        nk_ref[s_i] = jnp.concatenate([ck[t_new:], kn], axis=0)
        nv_ref[s_i] = jnp.concatenate([cv[t_new:], vn], axis=0)
        kall = jnp.concatenate([ck, kn, zpad], axis=0).astype(BF16)
        vall = jnp.concatenate([cv, vn, zpad], axis=0).astype(BF16)
        qs = q[s_i * t_new:(s_i + 1) * t_new]
        outs = []
        for g in range(n_kv):
            qg = jnp.concatenate(
                [qs[:, (g * Q_PER_KV + r) * HEAD_DIM:(g * Q_PER_KV + r + 1) * HEAD_DIM] for r in range(Q_PER_KV)],
                axis=0).astype(BF16)
            kg = kall[:, g * HEAD_DIM:(g + 1) * HEAD_DIM]
            vg = vall[:, g * HEAD_DIM:(g + 1) * HEAD_DIM]
            s = lax.dot_general(qg, kg, (((1,), (1,)), ((), ())), preferred_element_type=F32) * (HEAD_DIM ** -0.5)
            s = jnp.where(mask, s, NEG_INF)
            sink = jnp.concatenate(
                [jnp.full((t_new, 1), sink_ref[g * Q_PER_KV + r], F32) for r in range(Q_PER_KV)], axis=0)
            probs = _softmax_with_sink(s, sink).astype(BF16)
            og = jnp.dot(probs, vg, preferred_element_type=F32)
            for r in range(Q_PER_KV):
                outs.append(og[r * t_new:(r + 1) * t_new])
        seq_outs.append(jnp.concatenate(outs, axis=1))
    o_ref[...] = jnp.concatenate(seq_outs, axis=0).astype(o_ref.dtype)


def _attn_sample(qkv, attn, row0, cache_k, cache_v, sinks, tables, n_seq, t_new, attn_w, kv_w):
    rows = SEQS_PER_STEP * t_new
    rb0 = row0 // rows
    kb, vb = attn_w // kv_w, attn_w // kv_w + 1
    n_cache = cache_k.shape[1]
    tab_spec = pl.BlockSpec((rows, LANES), lambda i, s: (0, 0))
    cache_spec = pl.BlockSpec((SEQS_PER_STEP, n_cache, kv_w), lambda i, s: (i, 0, 0))
    return pl.pallas_call(
        functools.partial(_attn_sample_kernel, n_kv=kv_w // HEAD_DIM, t_new=t_new),
        grid_spec=pltpu.PrefetchScalarGridSpec(
            num_scalar_prefetch=1,
            grid=(n_seq // SEQS_PER_STEP,),
            in_specs=[pl.BlockSpec((rows, attn_w), lambda i, s: (rb0 + i, 0)),
                      pl.BlockSpec((rows, kv_w), lambda i, s: (rb0 + i, kb)),
                      pl.BlockSpec((rows, kv_w), lambda i, s: (rb0 + i, vb)),
                      cache_spec, cache_spec, tab_spec, tab_spec, tab_spec,
                      pl.BlockSpec(memory_space=pl.ANY)],
            out_specs=[pl.BlockSpec((rows, attn_w), lambda i, s: (rb0 + i, 0)), cache_spec, cache_spec]),
        out_shape=[jax.ShapeDtypeStruct(attn.shape, attn.dtype),
                   jax.ShapeDtypeStruct(cache_k.shape, F32),
                   jax.ShapeDtypeStruct(cache_v.shape, F32)],
        input_output_aliases={9: 0},
        compiler_params=_params(1, 32),
        name="attn_sample",
    )(sinks, qkv, qkv, qkv, cache_k, cache_v, *tables, attn)


def _pool_prompt_kernel(z_ref, halo_ref, w_ref, sc_ref, o_ref, ext_ref, *, tm, pg, nb):
    i = pl.program_id(0)

    @pl.when(i >= nb)
    def _():
        o_ref[...] = jnp.zeros_like(o_ref)

    @pl.when(i < nb)
    def _():
        _pool_prompt_block(z_ref, halo_ref, w_ref, sc_ref, o_ref, ext_ref, tm=tm, pg=pg)


def _pool_prompt_block(z_ref, halo_ref, w_ref, sc_ref, o_ref, ext_ref, *, tm, pg):
    i = pl.program_id(0)
    ext_ref[0:POOL_HALO, :] = jnp.where(i > 0, halo_ref[...], 0.0)
    ext_ref[POOL_HALO:, :] = z_ref[...]
    pos = i * tm + lax.broadcasted_iota(I32, (tm, 1), 0)
    for g, w in enumerate(POOL_WINDOWS):
        c0, c1 = g * pg, (g + 1) * pg
        cur = ext_ref[POOL_HALO:POOL_HALO + tm, c0:c1]
        acc = cur
        for d in range(1, w):
            acc = acc + ext_ref[POOL_HALO - d:POOL_HALO - d + tm, c0:c1]
        cnt = jnp.minimum(pos + 1, w).astype(F32)
        pooled = (acc / cnt - cur).astype(BF16)
        mixed = jnp.dot(pooled, w_ref[g].astype(BF16), preferred_element_type=F32) * sc_ref[:, c0:c1]
        o_ref[:, c0:c1] = mixed.astype(o_ref.dtype)


def _pool_prompt(z, n_rows, w_grp, scale):
    n, pw = z.shape
    tm = TM_ROW
    hb = tm // POOL_HALO
    return pl.pallas_call(
        functools.partial(_pool_prompt_kernel, tm=tm, pg=pw // len(POOL_WINDOWS), nb=n_rows // tm),
        grid=(n // tm,),
        in_specs=[pl.BlockSpec((tm, pw), lambda i: (i, 0)),
                  pl.BlockSpec((POOL_HALO, pw), lambda i: (jnp.maximum(i * hb - 1, 0), 0)),
                  pl.BlockSpec(w_grp.shape, lambda i: (0, 0, 0)),
                  pl.BlockSpec((1, pw), lambda i: (0, 0))],
        out_specs=pl.BlockSpec((tm, pw), lambda i: (i, 0)),
        out_shape=jax.ShapeDtypeStruct((n, pw), BF16),
        scratch_shapes=[pltpu.VMEM((tm + POOL_HALO, pw), F32)],
        compiler_params=_params(1, 40),
        name="pool_prompt",
    )(z, z, w_grp, scale.reshape(1, pw))


def _pool_sample_kernel(st_ref, z_ref, w_ref, sc_ref, mixed_in_ref, o_ref, ns_ref, ext_ref, *, pg, t_new):
    del mixed_in_ref
    n_st = st_ref.shape[1]
    seqs = st_ref.shape[0]
    hist = ext_ref.shape[1] - t_new
    ext_ref[:, hist - n_st:hist, :] = st_ref[...]
    ext_ref[:, hist:, :] = z_ref[...]
    ns_ref[...] = ext_ref[:, hist + t_new - n_st:, :]
    for g, w in enumerate(POOL_WINDOWS):
        c0, c1 = g * pg, (g + 1) * pg
        cur = ext_ref[:, hist:, c0:c1]
        acc = cur
        for d in range(1, w):
            acc = acc + ext_ref[:, hist - d:hist - d + t_new, c0:c1]
        pooled = (acc / float(w) - cur).reshape(seqs * t_new, pg).astype(BF16)
        mixed = jnp.dot(pooled, w_ref[g].astype(BF16), preferred_element_type=F32) * sc_ref[:, c0:c1]
        o_ref[:, c0:c1] = mixed.astype(o_ref.dtype)


def _pool_sample(state, z, mixed, row0, w_grp, scale, t_new):
    n_seq, n_st, pw = state.shape
    assert t_new == SUBLANES and n_st <= POOL_HALO - 1
    z3 = z.reshape(z.shape[0] // t_new, t_new, pw)
    sb0 = row0 // t_new // POOL_SEQS
    rows = POOL_SEQS * t_new
    st_spec = pl.BlockSpec((POOL_SEQS, n_st, pw), lambda i: (i, 0, 0))
    return pl.pallas_call(
        functools.partial(_pool_sample_kernel, pg=pw // len(POOL_WINDOWS), t_new=t_new),
        grid=(n_seq // POOL_SEQS,),
        in_specs=[st_spec,
                  pl.BlockSpec((POOL_SEQS, t_new, pw), lambda i: (sb0 + i, 0, 0)),
                  pl.BlockSpec(w_grp.shape, lambda i: (0, 0, 0)),
                  pl.BlockSpec((1, pw), lambda i: (0, 0)),
                  pl.BlockSpec(memory_space=pl.ANY)],
        out_specs=[pl.BlockSpec((rows, pw), lambda i: (row0 // rows + i, 0)), st_spec],
        out_shape=[jax.ShapeDtypeStruct(mixed.shape, mixed.dtype), jax.ShapeDtypeStruct(state.shape, F32)],
        scratch_shapes=[pltpu.VMEM((POOL_SEQS, POOL_HALO + t_new, pw), F32)],
        input_output_aliases={4: 0},
        compiler_params=_params(1, 40),
        name="pool_sample",
    )(state, z3, w_grp, scale.reshape(1, pw), mixed)


def _merge_kernel(a1_ref, a2_ref, w1_ref, w2_ref, ga_ref, gb_ref, ba_ref, bb_ref, o_ref, w1b_ref, w2b_ref):
    @pl.when(pl.program_id(1) == 0)
    def _():
        w1b_ref[...] = w1_ref[...].astype(BF16)
        w2b_ref[...] = w2_ref[...].astype(BF16)

    a_up = jnp.dot(a1_ref[...], w1b_ref[...], preferred_element_type=F32)
    b_up = jnp.dot(a2_ref[...], w2b_ref[...], preferred_element_type=F32)
    merged = (jax.nn.sigmoid(ga_ref[...] + ba_ref[...]) * a_up
              + jax.nn.sigmoid(gb_ref[...] + bb_ref[...]) * b_up)
    o_ref[...] = merged.astype(o_ref.dtype)


def _merge(attn, mixed, w_attn_up, w_pool_up, gates, b_gate):
    n, ka = attn.shape
    kp = mixed.shape[1]
    d = w_attn_up.shape[1]
    nj = d // TN
    return pl.pallas_call(
        _merge_kernel,
        grid=(nj, n // TM),
        in_specs=[pl.BlockSpec((TM, ka), lambda j, i: (i, 0)),
                  pl.BlockSpec((TM, kp), lambda j, i: (i, 0)),
                  pl.BlockSpec((ka, TN), lambda j, i: (0, j)),
                  pl.BlockSpec((kp, TN), lambda j, i: (0, j)),
                  pl.BlockSpec((TM, TN), lambda j, i: (i, j)),
                  pl.BlockSpec((TM, TN), lambda j, i: (i, nj + j)),
                  pl.BlockSpec((1, TN), lambda j, i: (0, j)),
                  pl.BlockSpec((1, TN), lambda j, i: (0, nj + j))],
        out_specs=pl.BlockSpec((TM, TN), lambda j, i: (i, j)),
        out_shape=jax.ShapeDtypeStruct((n, d), BF16),
        scratch_shapes=[pltpu.VMEM((ka, TN), BF16), pltpu.VMEM((kp, TN), BF16)],
        compiler_params=_params(2, 56),
        name="merge",
    )(attn, mixed, w_attn_up, w_pool_up, gates, gates, b_gate, b_gate)


def _out_proj_kernel(a_ref, w_ref, ha_ref, hb_ref, o_ref, wb_ref, *, nb_first):
    @pl.when(pl.program_id(1) == 0)
    def _():
        wb_ref[...] = w_ref[...].astype(BF16)

    h = jnp.where(pl.program_id(1) < nb_first, ha_ref[...], hb_ref[...])
    o_ref[...] = h + jnp.dot(a_ref[...], wb_ref[...], preferred_element_type=F32)


def _out_proj(merged, w_out, ha, hb):
    n, d = merged.shape
    nb_first = ha.shape[0] // TM
    h_first = pl.BlockSpec((TM, TN), lambda j, i: (jnp.minimum(i, nb_first - 1), j))
    h_last = pl.BlockSpec((TM, TN), lambda j, i: (jnp.maximum(i - nb_first, 0), j))
    return pl.pallas_call(
        functools.partial(_out_proj_kernel, nb_first=nb_first),
        grid=(d // TN, n // TM),
        in_specs=[pl.BlockSpec((TM, d), lambda j, i: (i, 0)),
                  pl.BlockSpec((d, TN), lambda j, i: (0, j)),
                  h_first, h_last],
        out_specs=pl.BlockSpec((TM, TN), lambda j, i: (i, j)),
        out_shape=jax.ShapeDtypeStruct((n, d), F32),
        scratch_shapes=[pltpu.VMEM((d, TN), BF16)],
        compiler_params=_params(2, 56),
        name="out_proj",
    )(merged, w_out, ha, hb)


def _router_kernel(h_ref, g_ref, wr_ref, br_ref, info_ref, cw_ref, xf_ref, cnt_ref, base_ref):
    tm = h_ref.shape[0]

    @pl.when(pl.program_id(0) == 0)
    def _():
        base_ref[...] = jnp.zeros_like(base_ref)

    xf = _rms_rows(h_ref[...], g_ref[...]).astype(BF16)
    xf_ref[...] = xf.reshape(xf_ref.shape)
    logits = jnp.dot(xf, wr_ref[...].astype(BF16), preferred_element_type=F32) + br_ref[...]
    lane = lax.broadcasted_iota(I32, logits.shape, 1)
    big = jnp.int32(1 << 20)
    ninf = jnp.float32(-jnp.inf)

    is_g = lane < N_GROUPS
    gl = jnp.where(is_g, logits, ninf)
    gmax = jnp.max(gl, axis=-1, keepdims=True)
    gden = jnp.sum(jnp.exp(gl - gmax), axis=-1, keepdims=True)
    g_p = 1.0 / gden
    g_idx = jnp.min(jnp.where(gl == gmax, lane, big), axis=-1, keepdims=True)

    e_lane = lane - N_GROUPS
    in_grp = (e_lane >= g_idx * EXPERTS_PER_GROUP) & (e_lane < (g_idx + 1) * EXPERTS_PER_GROUP)
    el = jnp.where(in_grp, logits, ninf)
    m1 = jnp.max(el, axis=-1, keepdims=True)
    i1 = jnp.min(jnp.where(in_grp & (el == m1), e_lane, big), axis=-1, keepdims=True)
    el2 = jnp.where(e_lane == i1, ninf, el)
    m2 = jnp.max(el2, axis=-1, keepdims=True)
    i2 = jnp.min(jnp.where(in_grp & (el2 == m2), e_lane, big), axis=-1, keepdims=True)
    t = jnp.exp(m2 - m1)
    den = 1.0 + t
    c1 = g_p * (1.0 / den)
    c2 = g_p * (t / den)
    cw_ref[...] = jnp.where(lane == 0, c1, jnp.where(lane == 1, c2, 0.0))

    hit1, hit2 = lane == i1, lane == i2
    onehot = (hit1 | hit2).astype(BF16)
    earlier = (lax.broadcasted_iota(I32, (tm, tm), 1) < lax.broadcasted_iota(I32, (tm, tm), 0)).astype(BF16)
    before = jnp.dot(earlier, onehot, preferred_element_type=F32) + base_ref[...]
    r1 = jnp.sum(jnp.where(hit1, before, 0.0), axis=-1, keepdims=True).astype(I32)
    r2 = jnp.sum(jnp.where(hit2, before, 0.0), axis=-1, keepdims=True).astype(I32)
    base_ref[...] += jnp.sum(onehot.astype(F32), axis=0, keepdims=True)
    cnt_ref[...] = base_ref[...].astype(I32)
    info_ref[...] = jnp.where(lane == 0, i1, jnp.where(lane == 1, i2,
                              jnp.where(lane == 2, r1, jnp.where(lane == 3, r2, 0))))


def _router(h, g_ffn, w_rg, b_rg, w_re, b_re):
    n, d = h.shape
    n_log = N_GROUPS + N_EXPERTS
    wr = jnp.concatenate([w_rg, w_re, jnp.zeros((d, LANES - n_log), F32)], axis=1)
    br = jnp.concatenate([b_rg, b_re, jnp.zeros((LANES - n_log,), F32)]).reshape(1, LANES)
    lane_spec = pl.BlockSpec((TM_ROW, LANES), lambda i: (i, 0))
    return pl.pallas_call(
        _router_kernel,
        grid=(n // TM_ROW,),
        in_specs=[pl.BlockSpec((TM_ROW, d), lambda i: (i, 0)),
                  pl.BlockSpec((1, d), lambda i: (0, 0)),
                  pl.BlockSpec((d, LANES), lambda i: (0, 0)),
                  pl.BlockSpec((1, LANES), lambda i: (0, 0))],
        out_specs=[lane_spec, lane_spec,
                   pl.BlockSpec((TM_ROW, d // LANES, LANES), lambda i: (i, 0, 0)),
                   pl.BlockSpec((1, LANES), lambda i: (0, 0))],
        out_shape=[jax.ShapeDtypeStruct((n, LANES), I32), jax.ShapeDtypeStruct((n, LANES), F32),
                   jax.ShapeDtypeStruct((n, d // LANES, LANES), BF16), jax.ShapeDtypeStruct((1, LANES), I32)],
        scratch_shapes=[pltpu.VMEM((1, LANES), F32)],
        compiler_params=_params(1, 40),
        name="router",
    )(h, g_ffn.reshape(1, d), wr, br)


def _dispatch_plan(info, counts, n_tiles):
    counts = counts[0, :N_EXPERTS]
    tiles_per = (counts + TM_E - 1) // TM_E
    tiles_end = jnp.cumsum(tiles_per)
    row_start = (tiles_end - tiles_per) * TM_E
    eid, rank = info[:, :TOP_K], info[:, TOP_K:2 * TOP_K]
    onehot = (eid[:, :, None] == jnp.arange(N_EXPERTS, dtype=I32)[None, None, :]).astype(I32)
    pos = (jnp.sum(onehot * row_start[None, None, :], axis=-1) + rank).reshape(-1).astype(I32)
    tile_ids = jnp.arange(n_tiles, dtype=I32)
    tile_expert = jnp.minimum(
        jnp.sum((tile_ids[:, None] >= tiles_end[None, :]).astype(I32), axis=1), N_EXPERTS - 1).astype(I32)
    n_used = tiles_end[-1:].astype(I32)
    last_tile_row = (jnp.clip(tiles_end - 1, 0, n_tiles - 1) * TM_E).astype(I32)
    return pos, tile_expert, n_used, last_tile_row


def _dispatch_kernel(pos_ref, ltr_ref, nu_ref, xf_ref, xs_hbm, zero_ref, sem):
    i = pl.program_id(0)
    n_tiles = xs_hbm.shape[0] // TM_E

    def clear_tile(row):
        return pltpu.make_async_copy(zero_ref, xs_hbm.at[pl.ds(pl.multiple_of(row, TM_E), TM_E)], sem)

    @pl.when(i == 0)
    def _():
        zero_ref[...] = jnp.zeros_like(zero_ref)
        for e in range(N_EXPERTS):
            clear_tile(ltr_ref[e]).start()

        def start_unused(t, carry):
            clear_tile(t * TM_E).start()
            return carry

        def wait_one(t, carry):
            clear_tile(0).wait()
            return carry

        lax.fori_loop(nu_ref[0], n_tiles, start_unused, 0)
        lax.fori_loop(nu_ref[0] - N_EXPERTS, n_tiles, wait_one, 0)

    base = i * TM_DISPATCH

    def issue(r, carry):
        for k in range(TOP_K):
            dst_row = pos_ref[(base + r) * TOP_K + k]
            pltpu.make_async_copy(xf_ref.at[pl.ds(r, 1)], xs_hbm.at[pl.ds(dst_row, 1)], sem).start()
        return carry

    lax.fori_loop(0, TM_DISPATCH, issue, 0)
    for k in range(TOP_K):
        pltpu.make_async_copy(xf_ref, xs_hbm.at[pl.ds(0, TM_DISPATCH)], sem).wait()


def _dispatch(pos, last_tile_row, n_used, xf3, n_tiles):
    n, c, l = xf3.shape
    return pl.pallas_call(
        _dispatch_kernel,
        grid_spec=pltpu.PrefetchScalarGridSpec(
            num_scalar_prefetch=3,
            grid=(n // TM_DISPATCH,),
            in_specs=[pl.BlockSpec((TM_DISPATCH, c, l), lambda i, *_: (i, 0, 0))],
            out_specs=pl.BlockSpec(memory_space=pl.ANY),
            scratch_shapes=[pltpu.VMEM((TM_E, c, l), xf3.dtype), pltpu.SemaphoreType.DMA(())]),
        out_shape=jax.ShapeDtypeStruct((n_tiles * TM_E, c, l), xf3.dtype),
        compiler_params=_params(1, 24),
        name="moe_dispatch",
    )(pos, last_tile_row, n_used, xf3)


def _expert_changed(te_ref, t):
    return (t == 0) | (te_ref[t] != te_ref[jnp.maximum(t - 1, 0)])


def _moe_up_kernel(te_ref, nu_ref, xs_ref, wg_ref, wu_ref, o_ref, wgb_ref, wub_ref):
    t = pl.program_id(1)

    @pl.when(_expert_changed(te_ref, t))
    def _():
        wgb_ref[...] = wg_ref[...].astype(BF16)
        wub_ref[...] = wu_ref[...].astype(BF16)

    @pl.when(t < nu_ref[0])
    def _():
        x = xs_ref[...].reshape(xs_ref.shape[0], -1)
        gate = jnp.dot(x, wgb_ref[...], preferred_element_type=F32)
        up = jnp.dot(x, wub_ref[...], preferred_element_type=F32)
        o_ref[...] = (jax.nn.silu(gate) * up).astype(o_ref.dtype)

    @pl.when(t >= nu_ref[0])
    def _():
        o_ref[...] = jnp.zeros_like(o_ref)


def _moe_up(tile_expert, n_used, xs3, w_gate, w_up, n_tiles):
    d, f = w_gate.shape[1], w_gate.shape[2]
    tf = TN
    w_spec = pl.BlockSpec((None, d, tf), lambda j, t, te, nu: (te[t], 0, j))
    return pl.pallas_call(
        _moe_up_kernel,
        grid_spec=pltpu.PrefetchScalarGridSpec(
            num_scalar_prefetch=2,
            grid=(f // tf, n_tiles),
            in_specs=[pl.BlockSpec((TM_E,) + xs3.shape[1:], lambda j, t, te, nu: (jnp.minimum(t, nu[0] - 1), 0, 0)),
                      w_spec, w_spec],
            out_specs=pl.BlockSpec((TM_E, tf), lambda j, t, te, nu: (t, j)),
            scratch_shapes=[pltpu.VMEM((d, tf), BF16), pltpu.VMEM((d, tf), BF16)]),
        out_shape=jax.ShapeDtypeStruct((n_tiles * TM_E, f), BF16),
        compiler_params=_params(2, 56),
        name="moe_up",
    )(tile_expert, n_used, xs3, w_gate, w_up)


def _moe_down_kernel(te_ref, nu_ref, hid_ref, wd_ref, o_ref, wdb_ref):
    t = pl.program_id(1)

    @pl.when(_expert_changed(te_ref, t))
    def _():
        wdb_ref[...] = wd_ref[...].astype(BF16)

    @pl.when(t < nu_ref[0])
    def _():
        y = jnp.dot(hid_ref[...], wdb_ref[...], preferred_element_type=F32)
        o_ref[...] = y.reshape(o_ref.shape)

    @pl.when(t >= nu_ref[0])
    def _():
        o_ref[...] = jnp.zeros_like(o_ref)


def _moe_down(tile_expert, n_used, hid, w_down, n_tiles):
    f, d = w_down.shape[1], w_down.shape[2]
    tn = 4 * TN
    return pl.pallas_call(
        _moe_down_kernel,
        grid_spec=pltpu.PrefetchScalarGridSpec(
            num_scalar_prefetch=2,
            grid=(d // tn, n_tiles),
            in_specs=[pl.BlockSpec((TM_E, f), lambda j, t, te, nu: (t, 0)),
                      pl.BlockSpec((None, f, tn), lambda j, t, te, nu: (te[t], 0, j))],
            out_specs=pl.BlockSpec((TM_E, tn // LANES, LANES), lambda j, t, te, nu: (t, j, 0)),
            scratch_shapes=[pltpu.VMEM((f, tn), BF16)]),
        out_shape=jax.ShapeDtypeStruct((n_tiles * TM_E, d // LANES, LANES), F32),
        compiler_params=_params(2, 40),
        name="moe_down",
    )(tile_expert, n_used, hid, w_down)


def _combine_kernel(pos_ref, h_ref, cw_ref, y_hbm, g_ref, ho_ref, xo_ref, buf_ref, sem):
    i = pl.program_id(0)
    tm, d = h_ref.shape

    def row_copy(slot, k, r, src_row):
        return pltpu.make_async_copy(y_hbm.at[pl.ds(src_row, 1)], buf_ref.at[slot, k, pl.ds(r, 1)], sem.at[slot])

    def issue_step(step, slot):
        def body(r, carry):
            for k in range(TOP_K):
                row_copy(slot, k, r, pos_ref[(step * tm + r) * TOP_K + k]).start()
            return carry
        lax.fori_loop(0, tm, body, 0)

    @pl.when(i == 0)
    def _():
        issue_step(0, 0)

    @pl.when(i + 1 < pl.num_programs(0))
    def _():
        issue_step(i + 1, (i + 1) % 2)

    slot = i % 2
    for k in range(TOP_K):
        pltpu.make_async_copy(y_hbm.at[pl.ds(0, tm)], buf_ref.at[slot, k], sem.at[slot]).wait()
    cw = cw_ref[...]
    y1 = buf_ref[slot, 0].reshape(tm, d)
    y2 = buf_ref[slot, 1].reshape(tm, d)
    h2 = h_ref[...] + (cw[:, 0:1] * y1 + cw[:, 1:2] * y2)
    ho_ref[...] = h2
    xo_ref[...] = _rms_rows(h2, g_ref[...]).astype(xo_ref.dtype)


def _combine(pos, h, cw, y3, g_ple):
    n, d = h.shape
    row_spec = pl.BlockSpec((TM_ROW, d), lambda i, p: (i, 0))
    return pl.pallas_call(
        _combine_kernel,
        grid_spec=pltpu.PrefetchScalarGridSpec(
            num_scalar_prefetch=1,
            grid=(n // TM_ROW,),
            in_specs=[row_spec,
                      pl.BlockSpec((TM_ROW, LANES), lambda i, p: (i, 0)),
                      pl.BlockSpec(memory_space=pl.ANY),
                      pl.BlockSpec((1, d), lambda i, p: (0, 0))],
            out_specs=[row_spec, row_spec],
            scratch_shapes=[pltpu.VMEM((2, TOP_K, TM_ROW) + y3.shape[1:], F32), pltpu.SemaphoreType.DMA((2,))]),
        out_shape=[jax.ShapeDtypeStruct((n, d), F32), jax.ShapeDtypeStruct((n, d), BF16)],
        compiler_params=_params(1, 56),
        name="moe_combine",
    )(pos, h, cw, y3, g_ple.reshape(1, d))


def _ple_kernel(xn_ref, wg_ref, bg_ref, pa_ref, pb_ref, wp_ref, h_ref, o_ref, wgb_ref, *, nb_first):
    @pl.when(pl.program_id(1) == 0)
    def _():
        wgb_ref[...] = wg_ref[...].astype(BF16)

    gate = jax.nn.sigmoid(jnp.dot(xn_ref[...], wgb_ref[...], preferred_element_type=F32) + bg_ref[...])
    p = jnp.where(pl.program_id(1) < nb_first, pa_ref[...], pb_ref[...]).astype(BF16)
    ple = jnp.dot(p, wp_ref[...].astype(BF16), preferred_element_type=F32)
    o_ref[...] = h_ref[...] + gate * ple


def _ple(xn, w_gate, b_gate, pa, pb, w_proj, h):
    n, d = h.shape
    pd = pa.shape[1]
    nb_first = pa.shape[0] // TM
    p_first = pl.BlockSpec((TM, pd), lambda j, i: (jnp.minimum(i, nb_first - 1), 0))
    p_last = pl.BlockSpec((TM, pd), lambda j, i: (jnp.maximum(i - nb_first, 0), 0))
    return pl.pallas_call(
        functools.partial(_ple_kernel, nb_first=nb_first),
        grid=(d // TN, n // TM),
        in_specs=[pl.BlockSpec((TM, d), lambda j, i: (i, 0)),
                  pl.BlockSpec((d, TN), lambda j, i: (0, j)),
                  pl.BlockSpec((1, TN), lambda j, i: (0, j)),
                  p_first, p_last,
                  pl.BlockSpec((pd, TN), lambda j, i: (0, j)),
                  pl.BlockSpec((TM, TN), lambda j, i: (i, j))],
        out_specs=pl.BlockSpec((TM, TN), lambda j, i: (i, j)),
        out_shape=jax.ShapeDtypeStruct((n, d), F32),
        scratch_shapes=[pltpu.VMEM((d, TN), BF16)],
        compiler_params=_params(2, 56),
        name="ple",
    )(xn, w_gate, b_gate.reshape(1, d), pa, pb, w_proj, h)


def _layer(xn, h_p, h_s, p_p, p_s, cache_k, cache_v, state_pool, lw):
    n, d = xn.shape
    n_prompt = h_p.shape[0]
    n_seq, n_cache, n_kv, _ = cache_k.shape
    t_new = (n - n_prompt) // n_seq
    kv_w = n_kv * HEAD_DIM
    attn_w = lw['w_attn_up'].shape[0]
    pool_w = lw['w_pool_up'].shape[0]
    n_pool = state_pool.shape[1]

    qkv = _in_proj(xn, lw['w_in'], 0, attn_w + 2 * kv_w, "in_proj_qkv")
    z = _in_proj(xn, lw['w_in'], attn_w + 2 * kv_w, pool_w, "in_proj_pool")
    gates = _in_proj(xn, lw['w_in'], attn_w + 2 * kv_w + pool_w, 2 * d, "in_proj_gates")

    tab_p = _rope_tables(jnp.arange(n_prompt, dtype=F32))
    tab_s = tuple(jnp.tile(a, (SEQS_PER_STEP, 1))
                  for a in _rope_tables(PAST_LEN + jnp.arange(t_new, dtype=F32)))
    attn, k_rot_p = _attn_prompt(qkv, lw['attn_sinks'], tab_p, n_prompt, attn_w, kv_w)
    attn, new_k_s, new_v_s = _attn_sample(
        qkv, attn, n_prompt, cache_k.reshape(n_seq, n_cache, kv_w), cache_v.reshape(n_seq, n_cache, kv_w),
        lw['attn_sinks'], tab_s, n_seq, t_new, attn_w, kv_w)

    mixed = _pool_prompt(z, n_prompt, lw['w_pool_grp'], lw['pool_scale'])
    mixed, new_z_s = _pool_sample(state_pool, z, mixed, n_prompt, lw['w_pool_grp'], lw['pool_scale'], t_new)

    merged = _merge(attn, mixed, lw['w_attn_up'], lw['w_pool_up'], gates, lw['b_gate'].reshape(1, 2 * d))
    h1 = _out_proj(merged, lw['w_out'], h_p, h_s)

    n_tiles = (n * TOP_K) // TM_E + N_EXPERTS
    info, cw, xf3, counts = _router(h1, lw['g_ffn'], lw['w_route_group'], lw['b_route_group'],
                                    lw['w_route_expert'], lw['b_route_expert'])
    pos, tile_expert, n_used, last_tile_row = _dispatch_plan(info, counts, n_tiles)
    xs3 = _dispatch(pos, last_tile_row, n_used, xf3, n_tiles)
    hid = _moe_up(tile_expert, n_used, xs3, lw['w_exp_gate'], lw['w_exp_up'], n_tiles)
    y3 = _moe_down(tile_expert, n_used, hid, lw['w_exp_down'], n_tiles)
    h2, xn2 = _combine(pos, h1, cw, y3, lw['g_ple'])

    h3 = _ple(xn2, lw['w_ple_gate'], lw['b_ple_gate'], p_p, p_s, lw['w_ple_proj'], h2)

    new_k_p = k_rot_p[n_prompt - n_cache:].reshape(1, n_cache, n_kv, HEAD_DIM)
    new_v_p = qkv[n_prompt - n_cache:n_prompt, attn_w + kv_w:].reshape(1, n_cache, n_kv, HEAD_DIM)
    new_z_p = z[n_prompt - n_pool:n_prompt].reshape(1, n_pool, pool_w)
    return (h3, new_k_p, new_v_p, new_z_p,
            new_k_s.reshape(cache_k.shape), new_v_s.reshape(cache_v.shape), new_z_s)


def kernel(x_prompt, x_sample, cache_k, cache_v, state_pool, p_prompt, p_sample, g_mix, w_in, b_gate, attn_sinks, w_pool_grp, pool_scale, w_attn_up, w_pool_up, w_out, g_ffn, w_route_group, b_route_group, w_route_expert, b_route_expert, w_exp_gate, w_exp_up, w_exp_down, g_ple, w_ple_gate, b_ple_gate, w_ple_proj, g_final):
    batch, seq, d = x_prompt.shape
    n_seq, t_new, _ = x_sample.shape
    depth = w_in.shape[0]
    assert batch == 1, "prompt rows are treated as one sequence"
    n_prompt = batch * seq
    n_sample = n_seq * t_new
    n = n_prompt + n_sample
    assert n_prompt % TM == 0 and n_sample % TM == 0 and (n * TOP_K) % TM_E == 0 and n % TM_DISPATCH == 0
    assert n_seq % SEQS_PER_STEP == 0 and n_seq % POOL_SEQS == 0 and t_new & (t_new - 1) == 0
    assert cache_k.shape[2] == WINDOW and state_pool.shape[2] == POOL_HALO - 1
    assert depth == 1, "one layer per step"

    weights = dict(g_mix=g_mix, w_in=w_in, b_gate=b_gate, attn_sinks=attn_sinks, w_pool_grp=w_pool_grp,
                   pool_scale=pool_scale, w_attn_up=w_attn_up, w_pool_up=w_pool_up, w_out=w_out, g_ffn=g_ffn,
                   w_route_group=w_route_group, b_route_group=b_route_group, w_route_expert=w_route_expert,
                   b_route_expert=b_route_expert, w_exp_gate=w_exp_gate, w_exp_up=w_exp_up,
                   w_exp_down=w_exp_down, g_ple=g_ple, w_ple_gate=w_ple_gate, b_ple_gate=b_ple_gate,
                   w_ple_proj=w_ple_proj)
    lw = {name: w[0] for name, w in weights.items()}
    h_p = x_prompt.reshape(n_prompt, d)
    h_s = x_sample.reshape(n_sample, d)
    xn = _rmsnorm_in2(h_p, h_s, lw['g_mix'], BF16, "norm_mix")
    out = _layer(xn, h_p, h_s, p_prompt[0].reshape(n_prompt, -1), p_sample[0].reshape(n_sample, -1),
                 cache_k[0], cache_v[0], state_pool[0], lw)
    y_p, y_s = _rmsnorm_out2(out[0], g_final, n_prompt, "norm_final")
    return (y_p.reshape(batch, seq, d), y_s.reshape(n_seq, t_new, d)) + tuple(piece[None] for piece in out[1:])
```

```python
import functools

import jax
import jax.numpy as jnp
from jax import lax
from jax.experimental import pallas as pl
from jax.experimental.pallas import tpu as pltpu

F32 = jnp.float32
BF16 = jnp.bfloat16
I32 = jnp.int32

HEAD_DIM = 64
Q_PER_KV = 4
WINDOW = 128
PAST_LEN = 8192
ROT_DIM = HEAD_DIM // 4
ROPE_THETA = 500000.0
POOL_WINDOWS = (2, 4, 8, 16)
POOL_HALO = 16
N_GROUPS = 4
EXPERTS_PER_GROUP = 4
N_EXPERTS = N_GROUPS * EXPERTS_PER_GROUP
TOP_K = 2
RMS_EPS = 1e-6
NEG_INF = -1e30
LANES = 128
SUBLANES = 8
MIB = 1024 * 1024

TM = 1024
TN = 512
TM_ROW = 256
TM_E = 512
TM_DISPATCH = 512
SEQS_PER_STEP = 8
POOL_SEQS = 16


def _params(n_axes, vmem_mib):
    return pltpu.CompilerParams(dimension_semantics=("arbitrary",) * n_axes,
                                vmem_limit_bytes=vmem_mib * MIB)


def _rms_rows(x, g):
    ms = jnp.mean(x * x, axis=-1, keepdims=True)
    return x * lax.rsqrt(ms + RMS_EPS) * g


def _first_last_specs(nb_first, block):
    zeros = (0,) * (len(block) - 1)
    first = pl.BlockSpec(block, lambda i, *_: (jnp.minimum(i, nb_first - 1),) + zeros)
    last = pl.BlockSpec(block, lambda i, *_: (jnp.maximum(i - nb_first, 0),) + zeros)
    return first, last


def _norm_in2_kernel(xa_ref, xb_ref, g_ref, o_ref, *, nb_first):
    i = pl.program_id(0)

    @pl.when(i < nb_first)
    def _():
        o_ref[...] = _rms_rows(xa_ref[...], g_ref[...]).astype(o_ref.dtype)

    @pl.when(i >= nb_first)
    def _():
        o_ref[...] = _rms_rows(xb_ref[...], g_ref[...]).astype(o_ref.dtype)


def _rmsnorm_in2(xa, xb, g, out_dtype, name):
    (na, d), nb = xa.shape, xb.shape[0]
    nb_first = na // TM_ROW
    spec_a, spec_b = _first_last_specs(nb_first, (TM_ROW, d))
    return pl.pallas_call(
        functools.partial(_norm_in2_kernel, nb_first=nb_first),
        grid=((na + nb) // TM_ROW,),
        in_specs=[spec_a, spec_b, pl.BlockSpec((1, d), lambda i: (0, 0))],
        out_specs=pl.BlockSpec((TM_ROW, d), lambda i: (i, 0)),
        out_shape=jax.ShapeDtypeStruct((na + nb, d), out_dtype),
        compiler_params=_params(1, 40),
        name=name,
    )(xa, xb, g.reshape(1, d))


def _norm_out2_kernel(x_ref, g_ref, oa_ref, ob_ref, *, nb_first):
    i = pl.program_id(0)

    @pl.when(i < nb_first)
    def _():
        oa_ref[...] = _rms_rows(x_ref[...], g_ref[...])

    @pl.when(i >= nb_first)
    def _():
        ob_ref[...] = _rms_rows(x_ref[...], g_ref[...])


def _rmsnorm_out2(x, g, na, name):
    n, d = x.shape
    nb_first = na // TM_ROW
    spec_a, spec_b = _first_last_specs(nb_first, (TM_ROW, d))
    return pl.pallas_call(
        functools.partial(_norm_out2_kernel, nb_first=nb_first),
        grid=(n // TM_ROW,),
        in_specs=[pl.BlockSpec((TM_ROW, d), lambda i: (i, 0)), pl.BlockSpec((1, d), lambda i: (0, 0))],
        out_specs=[spec_a, spec_b],
        out_shape=[jax.ShapeDtypeStruct((na, d), F32), jax.ShapeDtypeStruct((n - na, d), F32)],
        compiler_params=_params(1, 40),
        name=name,
    )(x, g.reshape(1, d))


def _proj_kernel(a_ref, w_ref, o_ref, wb_ref):
    @pl.when(pl.program_id(1) == 0)
    def _():
        wb_ref[...] = w_ref[...].astype(BF16)

    o_ref[...] = jnp.dot(a_ref[...], wb_ref[...], preferred_element_type=F32)


def _in_proj(xn, w, col0, width, name):
    n, d = xn.shape
    cb0 = col0 // TN
    return pl.pallas_call(
        _proj_kernel,
        grid=(width // TN, n // TM),
        in_specs=[pl.BlockSpec((TM, d), lambda j, i: (i, 0)),
                  pl.BlockSpec((d, TN), lambda j, i: (0, cb0 + j))],
        out_specs=pl.BlockSpec((TM, TN), lambda j, i: (i, j)),
        out_shape=jax.ShapeDtypeStruct((n, width), F32),
        scratch_shapes=[pltpu.VMEM((d, TN), BF16)],
        compiler_params=_params(2, 56),
        name=name,
    )(xn, w)


def _rope_tables(pos):
    half = ROT_DIM // 2
    inv_freq = ROPE_THETA ** (-jnp.arange(half, dtype=F32) * (2.0 / ROT_DIM))
    ang = pos[:, None] * inv_freq[None, :]
    cos, sin = jnp.cos(ang), jnp.sin(ang)
    t = pos.shape[0]
    pad = jnp.zeros((t, HEAD_DIM - ROT_DIM), F32)
    zeros = jnp.zeros((t, half), F32)
    cos_h = jnp.concatenate([cos, cos, pad + 1.0], axis=1)
    sa_h = jnp.concatenate([zeros, sin, pad], axis=1)
    sb_h = jnp.concatenate([-sin, zeros, pad], axis=1)
    reps = LANES // HEAD_DIM
    return tuple(jnp.tile(a, (1, reps)) for a in (cos_h, sa_h, sb_h))


def _rope_tables_t(pos):
    half = ROT_DIM // 2
    inv_freq = ROPE_THETA ** (-jnp.arange(half, dtype=F32) * (2.0 / ROT_DIM))
    ang = pos[:, None] * inv_freq[None, :]
    return jnp.cos(ang).T, jnp.sin(ang).T


def _rope(x, cos_t, sa_t, sb_t):
    pieces = []
    for c in range(x.shape[1] // LANES):
        xc = x[:, c * LANES:(c + 1) * LANES]
        pieces.append(xc * cos_t
                      + pltpu.roll(xc, ROT_DIM // 2, 1) * sa_t
                      + pltpu.roll(xc, LANES - ROT_DIM // 2, 1) * sb_t)
    return jnp.concatenate(pieces, axis=1)


def _softmax_with_sink(s, sink):
    m = jnp.maximum(jnp.max(s, axis=-1, keepdims=True), sink)
    p = jnp.exp(s - m)
    den = jnp.sum(p, axis=-1, keepdims=True) + jnp.exp(sink - m)
    return p / den


def _attn_prompt_kernel(sink_ref, q_ref, k_ref, v_ref, cos_ref, sa_ref, sb_ref, cost_ref, sint_ref,
                        o_ref, ko_ref, kprev_ref, vprevt_ref, *, n_kv, nb):
    b = pl.program_id(0)

    @pl.when(b == 0)
    def _():
        kprev_ref[...] = jnp.zeros_like(kprev_ref)
        vprevt_ref[...] = jnp.zeros_like(vprevt_ref)

    @pl.when(b >= nb)
    def _():
        o_ref[...] = jnp.zeros_like(o_ref)

    @pl.when(b < nb)
    def _():
        _attn_prompt_block(sink_ref, q_ref, k_ref, v_ref, cos_ref, sa_ref, sb_ref, cost_ref, sint_ref,
                           o_ref, ko_ref, kprev_ref, vprevt_ref, n_kv=n_kv)


def _attn_prompt_block(sink_ref, q_ref, k_ref, v_ref, cos_ref, sa_ref, sb_ref, cost_ref, sint_ref,
                       o_ref, ko_ref, kprev_ref, vprevt_ref, *, n_kv):
    b = pl.program_id(0)
    half = ROT_DIM // 2
    k = _rope(k_ref[...], cos_ref[...], sa_ref[...], sb_ref[...])
    ko_ref[...] = k
    kcat = jnp.concatenate([kprev_ref[...], k], axis=0).astype(BF16)
    vt = v_ref[...].T
    vcat_t = jnp.concatenate([vprevt_ref[...], vt], axis=1).astype(BF16)
    kprev_ref[...] = k
    vprevt_ref[...] = vt

    qt = q_ref[...].T
    cos_t, sin_t = cost_ref[...], sint_ref[...]

    def roped_head_t(h):
        x = qt[h * HEAD_DIM:(h + 1) * HEAD_DIM]
        x1, x2 = x[0:half], x[half:ROT_DIM]
        rot = jnp.concatenate([x1 * cos_t - x2 * sin_t, x2 * cos_t + x1 * sin_t, x[ROT_DIM:]], axis=0)
        return (rot * (HEAD_DIM ** -0.5)).astype(BF16)

    n_heads = n_kv * Q_PER_KV
    lanes = n_heads * WINDOW
    zeros_head = jnp.zeros((HEAD_DIM, WINDOW), BF16)
    heads_per_tile = LANES // HEAD_DIM

    scores = []
    for g in range(n_kv):
        tile, slot = g // heads_per_tile, g % heads_per_tile
        k2 = kcat[:, tile * LANES:(tile + 1) * LANES]
        cols = []
        for r in range(Q_PER_KV):
            parts = [zeros_head] * heads_per_tile
            parts[slot] = roped_head_t(g * Q_PER_KV + r)
            cols.append(jnp.concatenate(parts, axis=0))
        rhs = jnp.concatenate(cols, axis=1)
        scores.append(jnp.dot(k2, rhs, preferred_element_type=F32))
    st = jnp.concatenate(scores, axis=1)

    key = lax.broadcasted_iota(I32, (2 * WINDOW, lanes), 0)
    qry = lax.broadcasted_iota(I32, (2 * WINDOW, lanes), 1) & (WINDOW - 1)
    first_key = jnp.where(b > 0, 0, WINDOW)
    mask = (key > qry) & (key <= qry + WINDOW) & (key >= first_key)
    st = jnp.where(mask, st, NEG_INF)
    sink = jnp.concatenate([jnp.full((1, WINDOW), sink_ref[h], F32) for h in range(n_heads)], axis=1)
    m = jnp.maximum(jnp.max(st, axis=0, keepdims=True), sink)
    p = jnp.exp(st - m)
    den = jnp.sum(p, axis=0, keepdims=True) + jnp.exp(sink - m)
    probs_t = (p * (1.0 / den)).astype(BF16)

    for g in range(n_kv):
        ot = jnp.dot(vcat_t[g * HEAD_DIM:(g + 1) * HEAD_DIM],
                     probs_t[:, g * Q_PER_KV * WINDOW:(g + 1) * Q_PER_KV * WINDOW],
                     preferred_element_type=F32)
        for pr in range(Q_PER_KV // heads_per_tile):
            pair = jnp.concatenate(
                [ot[:, (pr * heads_per_tile + u) * WINDOW:(pr * heads_per_tile + u + 1) * WINDOW]
                 for u in range(heads_per_tile)], axis=0)
            c = (g * Q_PER_KV) // heads_per_tile + pr
            o_ref[:, c * LANES:(c + 1) * LANES] = pair.T.astype(o_ref.dtype)


def _attn_prompt(qkv, sinks, tables, tables_t, n_rows, attn_w, kv_w):
    nb = n_rows // WINDOW
    kb, vb = attn_w // kv_w, attn_w // kv_w + 1

    def blk(b, s):
        return jnp.minimum(b, nb - 1)

    tab_spec = pl.BlockSpec((WINDOW, LANES), lambda b, s: (blk(b, s), 0))
    tab_t_spec = pl.BlockSpec((ROT_DIM // 2, WINDOW), lambda b, s: (0, blk(b, s)))
    return pl.pallas_call(
        functools.partial(_attn_prompt_kernel, n_kv=kv_w // HEAD_DIM, nb=nb),
        grid_spec=pltpu.PrefetchScalarGridSpec(
            num_scalar_prefetch=1,
            grid=(qkv.shape[0] // WINDOW,),
            in_specs=[pl.BlockSpec((WINDOW, attn_w), lambda b, s: (blk(b, s), 0)),
                      pl.BlockSpec((WINDOW, kv_w), lambda b, s: (blk(b, s), kb)),
                      pl.BlockSpec((WINDOW, kv_w), lambda b, s: (blk(b, s), vb)),
                      tab_spec, tab_spec, tab_spec, tab_t_spec, tab_t_spec],
            out_specs=[pl.BlockSpec((WINDOW, attn_w), lambda b, s: (b, 0)),
                       pl.BlockSpec((WINDOW, kv_w), lambda b, s: (blk(b, s), 0))],
            scratch_shapes=[pltpu.VMEM((WINDOW, kv_w), F32), pltpu.VMEM((kv_w, WINDOW), F32)]),
        out_shape=[jax.ShapeDtypeStruct((qkv.shape[0], attn_w), BF16),
                   jax.ShapeDtypeStruct((n_rows, kv_w), F32)],
        compiler_params=_params(1, 32),
        name="attn_prompt",
    )(sinks, qkv, qkv, qkv, *tables, *tables_t)


def _attn_sample_kernel(sink_ref, q_ref, k_ref, v_ref, ck_ref, cv_ref, cos_ref, sa_ref, sb_ref, attn_in_ref,
                        o_ref, nk_ref, nv_ref, *, n_kv, t_new):
    del attn_in_ref
    assert LANES // HEAD_DIM == 2 and Q_PER_KV % 2 == 0
    cos_t, sa_t, sb_t = cos_ref[...], sa_ref[...], sb_ref[...]
    q = _rope(q_ref[...], cos_t, sa_t, sb_t) * (HEAD_DIM ** -0.5)
    k = _rope(k_ref[...], cos_t, sa_t, sb_t)
    v = v_ref[...]
    n_cache = ck_ref.shape[1]
    n_keys = 2 * WINDOW
    n_seq = q.shape[0] // t_new
    n_heads = n_kv * Q_PER_KV
    zpad = jnp.zeros((n_keys - n_cache - t_new, k.shape[1]), F32)
    low = lax.broadcasted_iota(I32, (t_new, LANES), 1) < HEAD_DIM
    high = jnp.logical_not(low)

    def swap_halves(x):
        return pltpu.roll(x, HEAD_DIM, 1)

    scores, values = [], []
    for s_i in range(n_seq):
        ck, cv = ck_ref[s_i], cv_ref[s_i]
        kn, vn = k[s_i * t_new:(s_i + 1) * t_new], v[s_i * t_new:(s_i + 1) * t_new]
        nk_ref[s_i] = jnp.concatenate([ck[t_new:], kn], axis=0)
        nv_ref[s_i] = jnp.concatenate([cv[t_new:], vn], axis=0)
        kall = jnp.concatenate([ck, kn, zpad], axis=0).astype(BF16)
        values.append(jnp.concatenate([cv, vn, zpad], axis=0).astype(BF16))
        qs = q[s_i * t_new:(s_i + 1) * t_new]
        for g in range(n_kv):
            tile, odd = g // 2, g % 2
            parts = []
            for r in range(Q_PER_KV):
                h = g * Q_PER_KV + r
                x = qs[:, (h // 2) * LANES:(h // 2 + 1) * LANES]
                if h % 2 != odd:
                    x = swap_halves(x)
                parts.append(jnp.where(high if odd else low, x, 0.0))
            qg = jnp.concatenate(parts, axis=0).astype(BF16)
            k2 = kall[:, tile * LANES:(tile + 1) * LANES]
            scores.append(lax.dot_general(qg, k2, (((1,), (1,)), ((), ())), preferred_element_type=F32))
    s = jnp.concatenate(scores, axis=0)

    rows = n_seq * n_heads * t_new
    qt = lax.broadcasted_iota(I32, (rows, n_keys), 0) & (t_new - 1)
    kj = lax.broadcasted_iota(I32, (rows, n_keys), 1)
    mask = (kj > qt + (n_cache - WINDOW)) & (kj <= qt + n_cache)
    s = jnp.where(mask, s, NEG_INF)
    sink_seq = jnp.concatenate([jnp.full((t_new, 1), sink_ref[h], F32) for h in range(n_heads)], axis=0)
    sink = jnp.concatenate([sink_seq] * n_seq, axis=0)
    probs = _softmax_with_sink(s, sink).astype(BF16)

    seq_outs = []
    for s_i in range(n_seq):
        out_tiles = []
        for g in range(n_kv):
            tile, odd = g // 2, g % 2
            r0 = (s_i * n_kv + g) * Q_PER_KV * t_new
            v2 = values[s_i][:, tile * LANES:(tile + 1) * LANES]
            og = jnp.dot(probs[r0:r0 + Q_PER_KV * t_new], v2, preferred_element_type=F32)
            for pr in range(Q_PER_KV // 2):
                even = og[(2 * pr) * t_new:(2 * pr + 1) * t_new]
                oddh = og[(2 * pr + 1) * t_new:(2 * pr + 2) * t_new]
                if odd:
                    even = swap_halves(even)
                else:
                    oddh = swap_halves(oddh)
                out_tiles.append(jnp.where(low, even, oddh))
        seq_outs.append(jnp.concatenate(out_tiles, axis=1))
    o_ref[...] = jnp.concatenate(seq_outs, axis=0).astype(o_ref.dtype)


def _attn_sample(qkv, attn, row0, cache_k, cache_v, sinks, tables, n_seq, t_new, attn_w, kv_w):
    rows = SEQS_PER_STEP * t_new
    rb0 = row0 // rows
    kb, vb = attn_w // kv_w, attn_w // kv_w + 1
    n_cache = cache_k.shape[1]
    tab_spec = pl.BlockSpec((rows, LANES), lambda i, s: (0, 0))
    cache_spec = pl.BlockSpec((SEQS_PER_STEP, n_cache, kv_w), lambda i, s: (i, 0, 0))
    return pl.pallas_call(
        functools.partial(_attn_sample_kernel, n_kv=kv_w // HEAD_DIM, t_new=t_new),
        grid_spec=pltpu.PrefetchScalarGridSpec(
            num_scalar_prefetch=1,
            grid=(n_seq // SEQS_PER_STEP,),
            in_specs=[pl.BlockSpec((rows, attn_w), lambda i, s: (rb0 + i, 0)),
                      pl.BlockSpec((rows, kv_w), lambda i, s: (rb0 + i, kb)),
                      pl.BlockSpec((rows, kv_w), lambda i, s: (rb0 + i, vb)),
                      cache_spec, cache_spec, tab_spec, tab_spec, tab_spec,
                      pl.BlockSpec(memory_space=pl.ANY)],
            out_specs=[pl.BlockSpec((rows, attn_w), lambda i, s: (rb0 + i, 0)), cache_spec, cache_spec]),
        out_shape=[jax.ShapeDtypeStruct(attn.shape, attn.dtype),
                   jax.ShapeDtypeStruct(cache_k.shape, F32),
                   jax.ShapeDtypeStruct(cache_v.shape, F32)],
        input_output_aliases={9: 0},
        compiler_params=_params(1, 32),
        name="attn_sample",
    )(sinks, qkv, qkv, qkv, cache_k, cache_v, *tables, attn)


def _pool_prompt_kernel(z_ref, halo_ref, w_ref, sc_ref, o_ref, ext_ref, *, tm, pg, nb):
    i = pl.program_id(0)

    @pl.when(i >= nb)
    def _():
        o_ref[...] = jnp.zeros_like(o_ref)

    @pl.when(i < nb)
    def _():
        _pool_prompt_block(z_ref, halo_ref, w_ref, sc_ref, o_ref, ext_ref, tm=tm, pg=pg)


def _pool_prompt_block(z_ref, halo_ref, w_ref, sc_ref, o_ref, ext_ref, *, tm, pg):
    i = pl.program_id(0)
    ext_ref[0:POOL_HALO, :] = jnp.where(i > 0, halo_ref[...], 0.0)
    ext_ref[POOL_HALO:, :] = z_ref[...]
    pos = i * tm + lax.broadcasted_iota(I32, (tm, 1), 0)
    for g, w in enumerate(POOL_WINDOWS):
        c0, c1 = g * pg, (g + 1) * pg
        cur = ext_ref[POOL_HALO:POOL_HALO + tm, c0:c1]
        acc = cur
        for d in range(1, w):
            acc = acc + ext_ref[POOL_HALO - d:POOL_HALO - d + tm, c0:c1]
        cnt = jnp.minimum(pos + 1, w).astype(F32)
        pooled = (acc / cnt - cur).astype(BF16)
        mixed = jnp.dot(pooled, w_ref[g].astype(BF16), preferred_element_type=F32) * sc_ref[:, c0:c1]
        o_ref[:, c0:c1] = mixed.astype(o_ref.dtype)


def _pool_prompt(z, n_rows, w_grp, scale):
    n, pw = z.shape
    tm = TM_ROW
    hb = tm // POOL_HALO
    return pl.pallas_call(
        functools.partial(_pool_prompt_kernel, tm=tm, pg=pw // len(POOL_WINDOWS), nb=n_rows // tm),
        grid=(n // tm,),
        in_specs=[pl.BlockSpec((tm, pw), lambda i: (i, 0)),
                  pl.BlockSpec((POOL_HALO, pw), lambda i: (jnp.maximum(i * hb - 1, 0), 0)),
                  pl.BlockSpec(w_grp.shape, lambda i: (0, 0, 0)),
                  pl.BlockSpec((1, pw), lambda i: (0, 0))],
        out_specs=pl.BlockSpec((tm, pw), lambda i: (i, 0)),
        out_shape=jax.ShapeDtypeStruct((n, pw), BF16),
        scratch_shapes=[pltpu.VMEM((tm + POOL_HALO, pw), F32)],
        compiler_params=_params(1, 40),
        name="pool_prompt",
    )(z, z, w_grp, scale.reshape(1, pw))


def _pool_sample_kernel(st_ref, z_ref, w_ref, sc_ref, mixed_in_ref, o_ref, ns_ref, ext_ref, *, pg, t_new):
    del mixed_in_ref
    n_st = st_ref.shape[1]
    seqs = st_ref.shape[0]
    hist = ext_ref.shape[1] - t_new
    ext_ref[:, hist - n_st:hist, :] = st_ref[...]
    ext_ref[:, hist:, :] = z_ref[...]
    ns_ref[...] = ext_ref[:, hist + t_new - n_st:, :]
    for g, w in enumerate(POOL_WINDOWS):
        c0, c1 = g * pg, (g + 1) * pg
        cur = ext_ref[:, hist:, c0:c1]
        acc = cur
        for d in range(1, w):
            acc = acc + ext_ref[:, hist - d:hist - d + t_new, c0:c1]
        pooled = (acc / float(w) - cur).reshape(seqs * t_new, pg).astype(BF16)
        mixed = jnp.dot(pooled, w_ref[g].astype(BF16), preferred_element_type=F32) * sc_ref[:, c0:c1]
        o_ref[:, c0:c1] = mixed.astype(o_ref.dtype)


def _pool_sample(state, z, mixed, row0, w_grp, scale, t_new):
    n_seq, n_st, pw = state.shape
    assert t_new == SUBLANES and n_st <= POOL_HALO - 1
    z3 = z.reshape(z.shape[0] // t_new, t_new, pw)
    sb0 = row0 // t_new // POOL_SEQS
    rows = POOL_SEQS * t_new
    st_spec = pl.BlockSpec((POOL_SEQS, n_st, pw), lambda i: (i, 0, 0))
    return pl.pallas_call(
        functools.partial(_pool_sample_kernel, pg=pw // len(POOL_WINDOWS), t_new=t_new),
        grid=(n_seq // POOL_SEQS,),
        in_specs=[st_spec,
                  pl.BlockSpec((POOL_SEQS, t_new, pw), lambda i: (sb0 + i, 0, 0)),
                  pl.BlockSpec(w_grp.shape, lambda i: (0, 0, 0)),
                  pl.BlockSpec((1, pw), lambda i: (0, 0)),
                  pl.BlockSpec(memory_space=pl.ANY)],
        out_specs=[pl.BlockSpec((rows, pw), lambda i: (row0 // rows + i, 0)), st_spec],
        out_shape=[jax.ShapeDtypeStruct(mixed.shape, mixed.dtype), jax.ShapeDtypeStruct(state.shape, F32)],
        scratch_shapes=[pltpu.VMEM((POOL_SEQS, POOL_HALO + t_new, pw), F32)],
        input_output_aliases={4: 0},
        compiler_params=_params(1, 40),
        name="pool_sample",
    )(state, z3, w_grp, scale.reshape(1, pw), mixed)


def _merge_kernel(a1_ref, a2_ref, w1_ref, w2_ref, ga_ref, gb_ref, ba_ref, bb_ref, o_ref, w1b_ref, w2b_ref):
    @pl.when(pl.program_id(1) == 0)
    def _():
        w1b_ref[...] = w1_ref[...].astype(BF16)
        w2b_ref[...] = w2_ref[...].astype(BF16)

    a_up = jnp.dot(a1_ref[...], w1b_ref[...], preferred_element_type=F32)
    b_up = jnp.dot(a2_ref[...], w2b_ref[...], preferred_element_type=F32)
    merged = (jax.nn.sigmoid(ga_ref[...] + ba_ref[...]) * a_up
              + jax.nn.sigmoid(gb_ref[...] + bb_ref[...]) * b_up)
    o_ref[...] = merged.astype(o_ref.dtype)


def _merge(attn, mixed, w_attn_up, w_pool_up, gates, b_gate):
    n, ka = attn.shape
    kp = mixed.shape[1]
    d = w_attn_up.shape[1]
    nj = d // TN
    return pl.pallas_call(
        _merge_kernel,
        grid=(nj, n // TM),
        in_specs=[pl.BlockSpec((TM, ka), lambda j, i: (i, 0)),
                  pl.BlockSpec((TM, kp), lambda j, i: (i, 0)),
                  pl.BlockSpec((ka, TN), lambda j, i: (0, j)),
                  pl.BlockSpec((kp, TN), lambda j, i: (0, j)),
                  pl.BlockSpec((TM, TN), lambda j, i: (i, j)),
                  pl.BlockSpec((TM, TN), lambda j, i: (i, nj + j)),
                  pl.BlockSpec((1, TN), lambda j, i: (0, j)),
                  pl.BlockSpec((1, TN), lambda j, i: (0, nj + j))],
        out_specs=pl.BlockSpec((TM, TN), lambda j, i: (i, j)),
        out_shape=jax.ShapeDtypeStruct((n, d), BF16),
        scratch_shapes=[pltpu.VMEM((ka, TN), BF16), pltpu.VMEM((kp, TN), BF16)],
        compiler_params=_params(2, 56),
        name="merge",
    )(attn, mixed, w_attn_up, w_pool_up, gates, gates, b_gate, b_gate)


def _out_proj_kernel(a_ref, w_ref, ha_ref, hb_ref, o_ref, wb_ref, *, nb_first):
    @pl.when(pl.program_id(1) == 0)
    def _():
        wb_ref[...] = w_ref[...].astype(BF16)

    h = jnp.where(pl.program_id(1) < nb_first, ha_ref[...], hb_ref[...])
    o_ref[...] = h + jnp.dot(a_ref[...], wb_ref[...], preferred_element_type=F32)


def _out_proj(merged, w_out, ha, hb):
    n, d = merged.shape
    nb_first = ha.shape[0] // TM
    h_first = pl.BlockSpec((TM, TN), lambda j, i: (jnp.minimum(i, nb_first - 1), j))
    h_last = pl.BlockSpec((TM, TN), lambda j, i: (jnp.maximum(i - nb_first, 0), j))
    return pl.pallas_call(
        functools.partial(_out_proj_kernel, nb_first=nb_first),
        grid=(d // TN, n // TM),
        in_specs=[pl.BlockSpec((TM, d), lambda j, i: (i, 0)),
                  pl.BlockSpec((d, TN), lambda j, i: (0, j)),
                  h_first, h_last],
        out_specs=pl.BlockSpec((TM, TN), lambda j, i: (i, j)),
        out_shape=jax.ShapeDtypeStruct((n, d), F32),
        scratch_shapes=[pltpu.VMEM((d, TN), BF16)],
        compiler_params=_params(2, 56),
        name="out_proj",
    )(merged, w_out, ha, hb)


def _router_kernel(h_ref, g_ref, wr_ref, br_ref, info_ref, cw_ref, xf_ref, cnt_ref, base_ref):
    tm = h_ref.shape[0]

    @pl.when(pl.program_id(0) == 0)
    def _():
        base_ref[...] = jnp.zeros_like(base_ref)

    xf = _rms_rows(h_ref[...], g_ref[...]).astype(BF16)
    xf_ref[...] = xf.reshape(xf_ref.shape)
    logits = jnp.dot(xf, wr_ref[...].astype(BF16), preferred_element_type=F32) + br_ref[...]
    lane = lax.broadcasted_iota(I32, logits.shape, 1)
    big = jnp.int32(1 << 20)
    ninf = jnp.float32(-jnp.inf)

    is_g = lane < N_GROUPS
    gl = jnp.where(is_g, logits, ninf)
    gmax = jnp.max(gl, axis=-1, keepdims=True)
    gden = jnp.sum(jnp.exp(gl - gmax), axis=-1, keepdims=True)
    g_p = 1.0 / gden
    g_idx = jnp.min(jnp.where(gl == gmax, lane, big), axis=-1, keepdims=True)

    e_lane = lane - N_GROUPS
    in_grp = (e_lane >= g_idx * EXPERTS_PER_GROUP) & (e_lane < (g_idx + 1) * EXPERTS_PER_GROUP)
    el = jnp.where(in_grp, logits, ninf)
    m1 = jnp.max(el, axis=-1, keepdims=True)
    i1 = jnp.min(jnp.where(in_grp & (el == m1), e_lane, big), axis=-1, keepdims=True)
    el2 = jnp.where(e_lane == i1, ninf, el)
    m2 = jnp.max(el2, axis=-1, keepdims=True)
    i2 = jnp.min(jnp.where(in_grp & (el2 == m2), e_lane, big), axis=-1, keepdims=True)
    t = jnp.exp(m2 - m1)
    den = 1.0 + t
    c1 = g_p * (1.0 / den)
    c2 = g_p * (t / den)
    cw_ref[...] = jnp.where(lane == 0, c1, jnp.where(lane == 1, c2, 0.0))

    hit1, hit2 = lane == i1, lane == i2
    onehot = (hit1 | hit2).astype(BF16)
    earlier = (lax.broadcasted_iota(I32, (tm, tm), 1) < lax.broadcasted_iota(I32, (tm, tm), 0)).astype(BF16)
    before = jnp.dot(earlier, onehot, preferred_element_type=F32) + base_ref[...]
    r1 = jnp.sum(jnp.where(hit1, before, 0.0), axis=-1, keepdims=True).astype(I32)
    r2 = jnp.sum(jnp.where(hit2, before, 0.0), axis=-1, keepdims=True).astype(I32)
    base_ref[...] += jnp.sum(onehot.astype(F32), axis=0, keepdims=True)
    cnt_ref[...] = base_ref[...].astype(I32)
    info_ref[...] = jnp.where(lane == 0, i1, jnp.where(lane == 1, i2,
                              jnp.where(lane == 2, r1, jnp.where(lane == 3, r2, 0))))


def _router(h, g_ffn, w_rg, b_rg, w_re, b_re):
    n, d = h.shape
    n_log = N_GROUPS + N_EXPERTS
    wr = jnp.concatenate([w_rg, w_re, jnp.zeros((d, LANES - n_log), F32)], axis=1)
    br = jnp.concatenate([b_rg, b_re, jnp.zeros((LANES - n_log,), F32)]).reshape(1, LANES)
    lane_spec = pl.BlockSpec((TM_ROW, LANES), lambda i: (i, 0))
    return pl.pallas_call(
        _router_kernel,
        grid=(n // TM_ROW,),
        in_specs=[pl.BlockSpec((TM_ROW, d), lambda i: (i, 0)),
                  pl.BlockSpec((1, d), lambda i: (0, 0)),
                  pl.BlockSpec((d, LANES), lambda i: (0, 0)),
                  pl.BlockSpec((1, LANES), lambda i: (0, 0))],
        out_specs=[lane_spec, lane_spec,
                   pl.BlockSpec((TM_ROW, d // LANES, LANES), lambda i: (i, 0, 0)),
                   pl.BlockSpec((1, LANES), lambda i: (0, 0))],
        out_shape=[jax.ShapeDtypeStruct((n, LANES), I32), jax.ShapeDtypeStruct((n, LANES), F32),
                   jax.ShapeDtypeStruct((n, d // LANES, LANES), BF16), jax.ShapeDtypeStruct((1, LANES), I32)],
        scratch_shapes=[pltpu.VMEM((1, LANES), F32)],
        compiler_params=_params(1, 40),
        name="router",
    )(h, g_ffn.reshape(1, d), wr, br)


def _dispatch_plan(info, counts, n_tiles):
    counts = counts[0, :N_EXPERTS]
    tiles_per = (counts + TM_E - 1) // TM_E
    tiles_end = jnp.cumsum(tiles_per)
    row_start = (tiles_end - tiles_per) * TM_E
    eid, rank = info[:, :TOP_K], info[:, TOP_K:2 * TOP_K]
    onehot = (eid[:, :, None] == jnp.arange(N_EXPERTS, dtype=I32)[None, None, :]).astype(I32)
    pos = (jnp.sum(onehot * row_start[None, None, :], axis=-1) + rank).reshape(-1).astype(I32)
    tile_ids = jnp.arange(n_tiles, dtype=I32)
    tile_expert = jnp.minimum(
        jnp.sum((tile_ids[:, None] >= tiles_end[None, :]).astype(I32), axis=1), N_EXPERTS - 1).astype(I32)
    n_used = tiles_end[-1:].astype(I32)
    last_tile_row = (jnp.clip(tiles_end - 1, 0, n_tiles - 1) * TM_E).astype(I32)
    return pos, tile_expert, n_used, last_tile_row


def _dispatch_kernel(pos_ref, ltr_ref, nu_ref, xf_ref, xs_hbm, zero_ref, sem):
    i = pl.program_id(0)
    n_tiles = xs_hbm.shape[0] // TM_E

    def clear_tile(row):
        return pltpu.make_async_copy(zero_ref, xs_hbm.at[pl.ds(pl.multiple_of(row, TM_E), TM_E)], sem)

    @pl.when(i == 0)
    def _():
        zero_ref[...] = jnp.zeros_like(zero_ref)
        for e in range(N_EXPERTS):
            clear_tile(ltr_ref[e]).start()

        def start_unused(t, carry):
            clear_tile(t * TM_E).start()
            return carry

        def wait_one(t, carry):
            clear_tile(0).wait()
            return carry

        lax.fori_loop(nu_ref[0], n_tiles, start_unused, 0)
        lax.fori_loop(nu_ref[0] - N_EXPERTS, n_tiles, wait_one, 0)

    base = i * TM_DISPATCH

    def issue(r, carry):
        for k in range(TOP_K):
            dst_row = pos_ref[(base + r) * TOP_K + k]
            pltpu.make_async_copy(xf_ref.at[pl.ds(r, 1)], xs_hbm.at[pl.ds(dst_row, 1)], sem).start()
        return carry

    lax.fori_loop(0, TM_DISPATCH, issue, 0)
    for k in range(TOP_K):
        pltpu.make_async_copy(xf_ref, xs_hbm.at[pl.ds(0, TM_DISPATCH)], sem).wait()


def _dispatch(pos, last_tile_row, n_used, xf3, n_tiles):
    n, c, l = xf3.shape
    return pl.pallas_call(
        _dispatch_kernel,
        grid_spec=pltpu.PrefetchScalarGridSpec(
            num_scalar_prefetch=3,
            grid=(n // TM_DISPATCH,),
            in_specs=[pl.BlockSpec((TM_DISPATCH, c, l), lambda i, *_: (i, 0, 0))],
            out_specs=pl.BlockSpec(memory_space=pl.ANY),
            scratch_shapes=[pltpu.VMEM((TM_E, c, l), xf3.dtype), pltpu.SemaphoreType.DMA(())]),
        out_shape=jax.ShapeDtypeStruct((n_tiles * TM_E, c, l), xf3.dtype),
        compiler_params=_params(1, 24),
        name="moe_dispatch",
    )(pos, last_tile_row, n_used, xf3)


def _expert_changed(te_ref, t):
    return (t == 0) | (te_ref[t] != te_ref[jnp.maximum(t - 1, 0)])


def _moe_up_kernel(te_ref, nu_ref, xs_ref, wg_ref, wu_ref, o_ref, wgb_ref, wub_ref):
    t = pl.program_id(1)

    @pl.when(_expert_changed(te_ref, t))
    def _():
        wgb_ref[...] = wg_ref[...].astype(BF16)
        wub_ref[...] = wu_ref[...].astype(BF16)

    @pl.when(t < nu_ref[0])
    def _():
        x = xs_ref[...].reshape(xs_ref.shape[0], -1)
        gate = jnp.dot(x, wgb_ref[...], preferred_element_type=F32)
        up = jnp.dot(x, wub_ref[...], preferred_element_type=F32)
        o_ref[...] = (jax.nn.silu(gate) * up).astype(o_ref.dtype)

    @pl.when(t >= nu_ref[0])
    def _():
        o_ref[...] = jnp.zeros_like(o_ref)


def _moe_up(tile_expert, n_used, xs3, w_gate, w_up, n_tiles):
    d, f = w_gate.shape[1], w_gate.shape[2]
    tf = TN
    w_spec = pl.BlockSpec((None, d, tf), lambda j, t, te, nu: (te[t], 0, j))
    return pl.pallas_call(
        _moe_up_kernel,
        grid_spec=pltpu.PrefetchScalarGridSpec(
            num_scalar_prefetch=2,
            grid=(f // tf, n_tiles),
            in_specs=[pl.BlockSpec((TM_E,) + xs3.shape[1:], lambda j, t, te, nu: (jnp.minimum(t, nu[0] - 1), 0, 0)),
                      w_spec, w_spec],
            out_specs=pl.BlockSpec((TM_E, tf), lambda j, t, te, nu: (t, j)),
            scratch_shapes=[pltpu.VMEM((d, tf), BF16), pltpu.VMEM((d, tf), BF16)]),
        out_shape=jax.ShapeDtypeStruct((n_tiles * TM_E, f), BF16),
        compiler_params=_params(2, 56),
        name="moe_up",
    )(tile_expert, n_used, xs3, w_gate, w_up)


def _moe_down_kernel(te_ref, nu_ref, hid_ref, wd_ref, o_ref, wdb_ref):
    t = pl.program_id(1)

    @pl.when(_expert_changed(te_ref, t))
    def _():
        wdb_ref[...] = wd_ref[...].astype(BF16)

    @pl.when(t < nu_ref[0])
    def _():
        y = jnp.dot(hid_ref[...], wdb_ref[...], preferred_element_type=F32)
        o_ref[...] = y.reshape(o_ref.shape)

    @pl.when(t >= nu_ref[0])
    def _():
        o_ref[...] = jnp.zeros_like(o_ref)


def _moe_down(tile_expert, n_used, hid, w_down, n_tiles):
    f, d = w_down.shape[1], w_down.shape[2]
    tn = 4 * TN
    return pl.pallas_call(
        _moe_down_kernel,
        grid_spec=pltpu.PrefetchScalarGridSpec(
            num_scalar_prefetch=2,
            grid=(d // tn, n_tiles),
            in_specs=[pl.BlockSpec((TM_E, f), lambda j, t, te, nu: (t, 0)),
                      pl.BlockSpec((None, f, tn), lambda j, t, te, nu: (te[t], 0, j))],
            out_specs=pl.BlockSpec((TM_E, tn // LANES, LANES), lambda j, t, te, nu: (t, j, 0)),
            scratch_shapes=[pltpu.VMEM((f, tn), BF16)]),
        out_shape=jax.ShapeDtypeStruct((n_tiles * TM_E, d // LANES, LANES), F32),
        compiler_params=_params(2, 40),
        name="moe_down",
    )(tile_expert, n_used, hid, w_down)


def _combine_kernel(pos_ref, h_ref, cw_ref, y_hbm, g_ref, ho_ref, xo_ref, buf_ref, sem):
    i = pl.program_id(0)
    tm, d = h_ref.shape

    def row_copy(slot, k, r, src_row):
        return pltpu.make_async_copy(y_hbm.at[pl.ds(src_row, 1)], buf_ref.at[slot, k, pl.ds(r, 1)], sem.at[slot])

    def issue_step(step, slot):
        def body(r, carry):
            for k in range(TOP_K):
                row_copy(slot, k, r, pos_ref[(step * tm + r) * TOP_K + k]).start()
            return carry
        lax.fori_loop(0, tm, body, 0)

    @pl.when(i == 0)
    def _():
        issue_step(0, 0)

    @pl.when(i + 1 < pl.num_programs(0))
    def _():
        issue_step(i + 1, (i + 1) % 2)

    slot = i % 2
    for k in range(TOP_K):
        pltpu.make_async_copy(y_hbm.at[pl.ds(0, tm)], buf_ref.at[slot, k], sem.at[slot]).wait()
    cw = cw_ref[...]
    y1 = buf_ref[slot, 0].reshape(tm, d)
    y2 = buf_ref[slot, 1].reshape(tm, d)
    h2 = h_ref[...] + (cw[:, 0:1] * y1 + cw[:, 1:2] * y2)
    ho_ref[...] = h2
    xo_ref[...] = _rms_rows(h2, g_ref[...]).astype(xo_ref.dtype)


def _combine(pos, h, cw, y3, g_ple):
    n, d = h.shape
    row_spec = pl.BlockSpec((TM_ROW, d), lambda i, p: (i, 0))
    return pl.pallas_call(
        _combine_kernel,
        grid_spec=pltpu.PrefetchScalarGridSpec(
            num_scalar_prefetch=1,
            grid=(n // TM_ROW,),
            in_specs=[row_spec,
                      pl.BlockSpec((TM_ROW, LANES), lambda i, p: (i, 0)),
                      pl.BlockSpec(memory_space=pl.ANY),
                      pl.BlockSpec((1, d), lambda i, p: (0, 0))],
            out_specs=[row_spec, row_spec],
            scratch_shapes=[pltpu.VMEM((2, TOP_K, TM_ROW) + y3.shape[1:], F32), pltpu.SemaphoreType.DMA((2,))]),
        out_shape=[jax.ShapeDtypeStruct((n, d), F32), jax.ShapeDtypeStruct((n, d), BF16)],
        compiler_params=_params(1, 56),
        name="moe_combine",
    )(pos, h, cw, y3, g_ple.reshape(1, d))


def _ple_kernel(xn_ref, wg_ref, bg_ref, pa_ref, pb_ref, wp_ref, h_ref, o_ref, wgb_ref, *, nb_first):
    @pl.when(pl.program_id(1) == 0)
    def _():
        wgb_ref[...] = wg_ref[...].astype(BF16)

    gate = jax.nn.sigmoid(jnp.dot(xn_ref[...], wgb_ref[...], preferred_element_type=F32) + bg_ref[...])
    p = jnp.where(pl.program_id(1) < nb_first, pa_ref[...], pb_ref[...]).astype(BF16)
    ple = jnp.dot(p, wp_ref[...].astype(BF16), preferred_element_type=F32)
    o_ref[...] = h_ref[...] + gate * ple


def _ple(xn, w_gate, b_gate, pa, pb, w_proj, h):
    n, d = h.shape
    pd = pa.shape[1]
    nb_first = pa.shape[0] // TM
    p_first = pl.BlockSpec((TM, pd), lambda j, i: (jnp.minimum(i, nb_first - 1), 0))
    p_last = pl.BlockSpec((TM, pd), lambda j, i: (jnp.maximum(i - nb_first, 0), 0))
    return pl.pallas_call(
        functools.partial(_ple_kernel, nb_first=nb_first),
        grid=(d // TN, n // TM),
        in_specs=[pl.BlockSpec((TM, d), lambda j, i: (i, 0)),
                  pl.BlockSpec((d, TN), lambda j, i: (0, j)),
                  pl.BlockSpec((1, TN), lambda j, i: (0, j)),
                  p_first, p_last,
                  pl.BlockSpec((pd, TN), lambda j, i: (0, j)),
                  pl.BlockSpec((TM, TN), lambda j, i: (i, j))],
        out_specs=pl.BlockSpec((TM, TN), lambda j, i: (i, j)),
        out_shape=jax.ShapeDtypeStruct((n, d), F32),
        scratch_shapes=[pltpu.VMEM((d, TN), BF16)],
        compiler_params=_params(2, 56),
        name="ple",
    )(xn, w_gate, b_gate.reshape(1, d), pa, pb, w_proj, h)


def _layer(xn, h_p, h_s, p_p, p_s, cache_k, cache_v, state_pool, lw):
    n, d = xn.shape
    n_prompt = h_p.shape[0]
    n_seq, n_cache, n_kv, _ = cache_k.shape
    t_new = (n - n_prompt) // n_seq
    kv_w = n_kv * HEAD_DIM
    attn_w = lw['w_attn_up'].shape[0]
    pool_w = lw['w_pool_up'].shape[0]
    n_pool = state_pool.shape[1]

    qkv = _in_proj(xn, lw['w_in'], 0, attn_w + 2 * kv_w, "in_proj_qkv")
    z = _in_proj(xn, lw['w_in'], attn_w + 2 * kv_w, pool_w, "in_proj_pool")
    gates = _in_proj(xn, lw['w_in'], attn_w + 2 * kv_w + pool_w, 2 * d, "in_proj_gates")

    tab_p = _rope_tables(jnp.arange(n_prompt, dtype=F32))
    tab_s = tuple(jnp.tile(a, (SEQS_PER_STEP, 1))
                  for a in _rope_tables(PAST_LEN + jnp.arange(t_new, dtype=F32)))
    tab_p_t = _rope_tables_t(jnp.arange(n_prompt, dtype=F32))
    attn, k_rot_p = _attn_prompt(qkv, lw['attn_sinks'], tab_p, tab_p_t, n_prompt, attn_w, kv_w)
    attn, new_k_s, new_v_s = _attn_sample(
        qkv, attn, n_prompt, cache_k.reshape(n_seq, n_cache, kv_w), cache_v.reshape(n_seq, n_cache, kv_w),
        lw['attn_sinks'], tab_s, n_seq, t_new, attn_w, kv_w)

    mixed = _pool_prompt(z, n_prompt, lw['w_pool_grp'], lw['pool_scale'])
    mixed, new_z_s = _pool_sample(state_pool, z, mixed, n_prompt, lw['w_pool_grp'], lw['pool_scale'], t_new)

    merged = _merge(attn, mixed, lw['w_attn_up'], lw['w_pool_up'], gates, lw['b_gate'].reshape(1, 2 * d))
    h1 = _out_proj(merged, lw['w_out'], h_p, h_s)

    n_tiles = (n * TOP_K) // TM_E + N_EXPERTS
    info, cw, xf3, counts = _router(h1, lw['g_ffn'], lw['w_route_group'], lw['b_route_group'],
                                    lw['w_route_expert'], lw['b_route_expert'])
    pos, tile_expert, n_used, last_tile_row = _dispatch_plan(info, counts, n_tiles)
    xs3 = _dispatch(pos, last_tile_row, n_used, xf3, n_tiles)
    hid = _moe_up(tile_expert, n_used, xs3, lw['w_exp_gate'], lw['w_exp_up'], n_tiles)
    y3 = _moe_down(tile_expert, n_used, hid, lw['w_exp_down'], n_tiles)
    h2, xn2 = _combine(pos, h1, cw, y3, lw['g_ple'])

    h3 = _ple(xn2, lw['w_ple_gate'], lw['b_ple_gate'], p_p, p_s, lw['w_ple_proj'], h2)

    new_k_p = k_rot_p[n_prompt - n_cache:].reshape(1, n_cache, n_kv, HEAD_DIM)
    new_v_p = qkv[n_prompt - n_cache:n_prompt, attn_w + kv_w:].reshape(1, n_cache, n_kv, HEAD_DIM)
    new_z_p = z[n_prompt - n_pool:n_prompt].reshape(1, n_pool, pool_w)
    return (h3, new_k_p, new_v_p, new_z_p,
            new_k_s.reshape(cache_k.shape), new_v_s.reshape(cache_v.shape), new_z_s)


def kernel(x_prompt, x_sample, cache_k, cache_v, state_pool, p_prompt, p_sample, g_mix, w_in, b_gate, attn_sinks, w_pool_grp, pool_scale, w_attn_up, w_pool_up, w_out, g_ffn, w_route_group, b_route_group, w_route_expert, b_route_expert, w_exp_gate, w_exp_up, w_exp_down, g_ple, w_ple_gate, b_ple_gate, w_ple_proj, g_final):
    batch, seq, d = x_prompt.shape
    n_seq, t_new, _ = x_sample.shape
    depth = w_in.shape[0]
    assert batch == 1, "prompt rows are treated as one sequence"
    n_prompt = batch * seq
    n_sample = n_seq * t_new
    n = n_prompt + n_sample
    assert n_prompt % TM == 0 and n_sample % TM == 0 and (n * TOP_K) % TM_E == 0 and n % TM_DISPATCH == 0
    assert n_seq % SEQS_PER_STEP == 0 and n_seq % POOL_SEQS == 0 and t_new & (t_new - 1) == 0
    assert cache_k.shape[2] == WINDOW and state_pool.shape[2] == POOL_HALO - 1
    assert depth == 1, "one layer per step"

    weights = dict(g_mix=g_mix, w_in=w_in, b_gate=b_gate, attn_sinks=attn_sinks, w_pool_grp=w_pool_grp,
                   pool_scale=pool_scale, w_attn_up=w_attn_up, w_pool_up=w_pool_up, w_out=w_out, g_ffn=g_ffn,
                   w_route_group=w_route_group, b_route_group=b_route_group, w_route_expert=w_route_expert,
                   b_route_expert=b_route_expert, w_exp_gate=w_exp_gate, w_exp_up=w_exp_up,
                   w_exp_down=w_exp_down, g_ple=g_ple, w_ple_gate=w_ple_gate, b_ple_gate=b_ple_gate,
                   w_ple_proj=w_ple_proj)
    lw = {name: w[0] for name, w in weights.items()}
    h_p = x_prompt.reshape(n_prompt, d)
    h_s = x_sample.reshape(n_sample, d)
    xn = _rmsnorm_in2(h_p, h_s, lw['g_mix'], BF16, "norm_mix")
    out = _layer(xn, h_p, h_s, p_prompt[0].reshape(n_prompt, -1), p_sample[0].reshape(n_sample, -1),
                 cache_k[0], cache_v[0], state_pool[0], lw)
    y_p, y_s = _rmsnorm_out2(out[0], g_final, n_prompt, "norm_final")
    return (y_p.reshape(batch, seq, d), y_s.reshape(n_seq, t_new, d)) + tuple(piece[None] for piece in out[1:])
```

```python
import functools

import jax
import jax.numpy as jnp
from jax import lax
from jax.experimental import pallas as pl
from jax.experimental.pallas import tpu as pltpu

F32 = jnp.float32
BF16 = jnp.bfloat16
I32 = jnp.int32

HEAD_DIM = 64
Q_PER_KV = 4
WINDOW = 128
PAST_LEN = 8192
ROT_DIM = HEAD_DIM // 4
ROPE_THETA = 500000.0
POOL_WINDOWS = (2, 4, 8, 16)
POOL_HALO = 16
N_GROUPS = 4
EXPERTS_PER_GROUP = 4
N_EXPERTS = N_GROUPS * EXPERTS_PER_GROUP
TOP_K = 2
RMS_EPS = 1e-6
NEG_INF = -1e30
LANES = 128
SUBLANES = 8
MIB = 1024 * 1024

TM = 1024
TN = 512
TM_ROW = 256
TM_E = 512
TM_DISPATCH = 512
SEQS_PER_STEP = 8
POOL_SEQS = 16


def _params(n_axes, vmem_mib):
    return pltpu.CompilerParams(dimension_semantics=("arbitrary",) * n_axes,
                                vmem_limit_bytes=vmem_mib * MIB)


def _rms_rows(x, g):
    ms = jnp.mean(x * x, axis=-1, keepdims=True)
    return x * lax.rsqrt(ms + RMS_EPS) * g


def _first_last_specs(nb_first, block):
    zeros = (0,) * (len(block) - 1)
    first = pl.BlockSpec(block, lambda i, *_: (jnp.minimum(i, nb_first - 1),) + zeros)
    last = pl.BlockSpec(block, lambda i, *_: (jnp.maximum(i - nb_first, 0),) + zeros)
    return first, last


def _norm_in2_kernel(xa_ref, xb_ref, g_ref, o_ref, *, nb_first):
    i = pl.program_id(0)

    @pl.when(i < nb_first)
    def _():
        o_ref[...] = _rms_rows(xa_ref[...], g_ref[...]).astype(o_ref.dtype)

    @pl.when(i >= nb_first)
    def _():
        o_ref[...] = _rms_rows(xb_ref[...], g_ref[...]).astype(o_ref.dtype)


def _rmsnorm_in2(xa, xb, g, out_dtype, name):
    (na, d), nb = xa.shape, xb.shape[0]
    nb_first = na // TM_ROW
    spec_a, spec_b = _first_last_specs(nb_first, (TM_ROW, d))
    return pl.pallas_call(
        functools.partial(_norm_in2_kernel, nb_first=nb_first),
        grid=((na + nb) // TM_ROW,),
        in_specs=[spec_a, spec_b, pl.BlockSpec((1, d), lambda i: (0, 0))],
        out_specs=pl.BlockSpec((TM_ROW, d), lambda i: (i, 0)),
        out_shape=jax.ShapeDtypeStruct((na + nb, d), out_dtype),
        compiler_params=_params(1, 40),
        name=name,
    )(xa, xb, g.reshape(1, d))


def _norm_out2_kernel(x_ref, g_ref, oa_ref, ob_ref, *, nb_first):
    i = pl.program_id(0)

    @pl.when(i < nb_first)
    def _():
        oa_ref[...] = _rms_rows(x_ref[...], g_ref[...])

    @pl.when(i >= nb_first)
    def _():
        ob_ref[...] = _rms_rows(x_ref[...], g_ref[...])


def _rmsnorm_out2(x, g, na, name):
    n, d = x.shape
    nb_first = na // TM_ROW
    spec_a, spec_b = _first_last_specs(nb_first, (TM_ROW, d))
    return pl.pallas_call(
        functools.partial(_norm_out2_kernel, nb_first=nb_first),
        grid=(n // TM_ROW,),
        in_specs=[pl.BlockSpec((TM_ROW, d), lambda i: (i, 0)), pl.BlockSpec((1, d), lambda i: (0, 0))],
        out_specs=[spec_a, spec_b],
        out_shape=[jax.ShapeDtypeStruct((na, d), F32), jax.ShapeDtypeStruct((n - na, d), F32)],
        compiler_params=_params(1, 40),
        name=name,
    )(x, g.reshape(1, d))


def _proj_kernel(a_ref, w_ref, o_ref, wb_ref):
    @pl.when(pl.program_id(1) == 0)
    def _():
        wb_ref[...] = w_ref[...].astype(BF16)

    o_ref[...] = jnp.dot(a_ref[...], wb_ref[...], preferred_element_type=F32)


def _in_proj(xn, w, col0, width, name):
    n, d = xn.shape
    cb0 = col0 // TN
    return pl.pallas_call(
        _proj_kernel,
        grid=(width // TN, n // TM),
        in_specs=[pl.BlockSpec((TM, d), lambda j, i: (i, 0)),
                  pl.BlockSpec((d, TN), lambda j, i: (0, cb0 + j))],
        out_specs=pl.BlockSpec((TM, TN), lambda j, i: (i, j)),
        out_shape=jax.ShapeDtypeStruct((n, width), F32),
        scratch_shapes=[pltpu.VMEM((d, TN), BF16)],
        compiler_params=_params(2, 56),
        name=name,
    )(xn, w)


def _rope_tables(pos):
    half = ROT_DIM // 2
    inv_freq = ROPE_THETA ** (-jnp.arange(half, dtype=F32) * (2.0 / ROT_DIM))
    ang = pos[:, None] * inv_freq[None, :]
    cos, sin = jnp.cos(ang), jnp.sin(ang)
    t = pos.shape[0]
    pad = jnp.zeros((t, HEAD_DIM - ROT_DIM), F32)
    zeros = jnp.zeros((t, half), F32)
    cos_h = jnp.concatenate([cos, cos, pad + 1.0], axis=1)
    sa_h = jnp.concatenate([zeros, sin, pad], axis=1)
    sb_h = jnp.concatenate([-sin, zeros, pad], axis=1)
    reps = LANES // HEAD_DIM
    return tuple(jnp.tile(a, (1, reps)) for a in (cos_h, sa_h, sb_h))


def _rope_tables_t(pos):
    half = ROT_DIM // 2
    inv_freq = ROPE_THETA ** (-jnp.arange(half, dtype=F32) * (2.0 / ROT_DIM))
    ang = pos[:, None] * inv_freq[None, :]
    return jnp.cos(ang).T, jnp.sin(ang).T


def _rope(x, cos_t, sa_t, sb_t):
    pieces = []
    for c in range(x.shape[1] // LANES):
        xc = x[:, c * LANES:(c + 1) * LANES]
        pieces.append(xc * cos_t
                      + pltpu.roll(xc, ROT_DIM // 2, 1) * sa_t
                      + pltpu.roll(xc, LANES - ROT_DIM // 2, 1) * sb_t)
    return jnp.concatenate(pieces, axis=1)


def _softmax_with_sink(s, sink):
    m = jnp.maximum(jnp.max(s, axis=-1, keepdims=True), sink)
    p = jnp.exp(s - m)
    den = jnp.sum(p, axis=-1, keepdims=True) + jnp.exp(sink - m)
    return p / den


def _attn_prompt_kernel(sink_ref, q_ref, k_ref, v_ref, cos_ref, sa_ref, sb_ref, cost_ref, sint_ref,
                        o_ref, ko_ref, kprev_ref, vprevt_ref, *, n_kv, nb):
    b = pl.program_id(0)

    @pl.when(b == 0)
    def _():
        kprev_ref[...] = jnp.zeros_like(kprev_ref)
        vprevt_ref[...] = jnp.zeros_like(vprevt_ref)

    @pl.when(b >= nb)
    def _():
        o_ref[...] = jnp.zeros_like(o_ref)

    @pl.when(b < nb)
    def _():
        _attn_prompt_block(sink_ref, q_ref, k_ref, v_ref, cos_ref, sa_ref, sb_ref, cost_ref, sint_ref,
                           o_ref, ko_ref, kprev_ref, vprevt_ref, n_kv=n_kv)


def _attn_prompt_block(sink_ref, q_ref, k_ref, v_ref, cos_ref, sa_ref, sb_ref, cost_ref, sint_ref,
                       o_ref, ko_ref, kprev_ref, vprevt_ref, *, n_kv):
    b = pl.program_id(0)
    half = ROT_DIM // 2
    k = _rope(k_ref[...], cos_ref[...], sa_ref[...], sb_ref[...])
    ko_ref[...] = k
    kcat = jnp.concatenate([kprev_ref[...], k], axis=0).astype(BF16)
    vt = v_ref[...].T
    vcat_t = jnp.concatenate([vprevt_ref[...], vt], axis=1).astype(BF16)
    kprev_ref[...] = k
    vprevt_ref[...] = vt

    qt = q_ref[...].T
    cos_t, sin_t = cost_ref[...], sint_ref[...]

    def roped_head_t(h):
        x = qt[h * HEAD_DIM:(h + 1) * HEAD_DIM]
        x1, x2 = x[0:half], x[half:ROT_DIM]
        rot = jnp.concatenate([x1 * cos_t - x2 * sin_t, x2 * cos_t + x1 * sin_t, x[ROT_DIM:]], axis=0)
        return (rot * (HEAD_DIM ** -0.5)).astype(BF16)

    n_heads = n_kv * Q_PER_KV
    lanes = n_heads * WINDOW
    zeros_head = jnp.zeros((HEAD_DIM, WINDOW), BF16)
    heads_per_tile = LANES // HEAD_DIM

    scores = []
    for g in range(n_kv):
        tile, slot = g // heads_per_tile, g % heads_per_tile
        k2 = kcat[:, tile * LANES:(tile + 1) * LANES]
        cols = []
        for r in range(Q_PER_KV):
            parts = [zeros_head] * heads_per_tile
            parts[slot] = roped_head_t(g * Q_PER_KV + r)
            cols.append(jnp.concatenate(parts, axis=0))
        rhs = jnp.concatenate(cols, axis=1)
        scores.append(jnp.dot(k2, rhs, preferred_element_type=F32))
    st = jnp.concatenate(scores, axis=1)

    key = lax.broadcasted_iota(I32, (2 * WINDOW, lanes), 0)
    qry = lax.broadcasted_iota(I32, (2 * WINDOW, lanes), 1) & (WINDOW - 1)
    first_key = jnp.where(b > 0, 0, WINDOW)
    mask = (key > qry) & (key <= qry + WINDOW) & (key >= first_key)
    st = jnp.where(mask, st, NEG_INF)
    sink = jnp.concatenate([jnp.full((1, WINDOW), sink_ref[h], F32) for h in range(n_heads)], axis=1)
    m = jnp.maximum(jnp.max(st, axis=0, keepdims=True), sink)
    p = jnp.exp(st - m)
    den = jnp.sum(p, axis=0, keepdims=True) + jnp.exp(sink - m)
    probs_t = (p * (1.0 / den)).astype(BF16)

    for g in range(n_kv):
        ot = jnp.dot(vcat_t[g * HEAD_DIM:(g + 1) * HEAD_DIM],
                     probs_t[:, g * Q_PER_KV * WINDOW:(g + 1) * Q_PER_KV * WINDOW],
                     preferred_element_type=F32)
        for pr in range(Q_PER_KV // heads_per_tile):
            pair = jnp.concatenate(
                [ot[:, (pr * heads_per_tile + u) * WINDOW:(pr * heads_per_tile + u + 1) * WINDOW]
                 for u in range(heads_per_tile)], axis=0)
            c = (g * Q_PER_KV) // heads_per_tile + pr
            o_ref[:, c * LANES:(c + 1) * LANES] = pair.T.astype(o_ref.dtype)


def _attn_prompt(qkv, sinks, tables, tables_t, n_rows, attn_w, kv_w):
    nb = n_rows // WINDOW
    kb, vb = attn_w // kv_w, attn_w // kv_w + 1

    def blk(b, s):
        return jnp.minimum(b, nb - 1)

    tab_spec = pl.BlockSpec((WINDOW, LANES), lambda b, s: (blk(b, s), 0))
    tab_t_spec = pl.BlockSpec((ROT_DIM // 2, WINDOW), lambda b, s: (0, blk(b, s)))
    return pl.pallas_call(
        functools.partial(_attn_prompt_kernel, n_kv=kv_w // HEAD_DIM, nb=nb),
        grid_spec=pltpu.PrefetchScalarGridSpec(
            num_scalar_prefetch=1,
            grid=(qkv.shape[0] // WINDOW,),
            in_specs=[pl.BlockSpec((WINDOW, attn_w), lambda b, s: (blk(b, s), 0)),
                      pl.BlockSpec((WINDOW, kv_w), lambda b, s: (blk(b, s), kb)),
                      pl.BlockSpec((WINDOW, kv_w), lambda b, s: (blk(b, s), vb)),
                      tab_spec, tab_spec, tab_spec, tab_t_spec, tab_t_spec],
            out_specs=[pl.BlockSpec((WINDOW, attn_w), lambda b, s: (b, 0)),
                       pl.BlockSpec((WINDOW, kv_w), lambda b, s: (blk(b, s), 0))],
            scratch_shapes=[pltpu.VMEM((WINDOW, kv_w), F32), pltpu.VMEM((kv_w, WINDOW), F32)]),
        out_shape=[jax.ShapeDtypeStruct((qkv.shape[0], attn_w), BF16),
                   jax.ShapeDtypeStruct((n_rows, kv_w), F32)],
        compiler_params=_params(1, 32),
        name="attn_prompt",
    )(sinks, qkv, qkv, qkv, *tables, *tables_t)


def _attn_sample_kernel(sink_ref, q_ref, k_ref, v_ref, ck_ref, cv_ref, cos_ref, sa_ref, sb_ref, attn_in_ref,
                        o_ref, nk_ref, nv_ref, *, n_kv, t_new):
    del attn_in_ref
    assert LANES // HEAD_DIM == 2 and Q_PER_KV % 2 == 0
    cos_t, sa_t, sb_t = cos_ref[...], sa_ref[...], sb_ref[...]
    q = _rope(q_ref[...], cos_t, sa_t, sb_t) * (HEAD_DIM ** -0.5)
    k = _rope(k_ref[...], cos_t, sa_t, sb_t)
    v = v_ref[...]
    n_cache = ck_ref.shape[1]
    n_keys = 2 * WINDOW
    n_seq = q.shape[0] // t_new
    n_heads = n_kv * Q_PER_KV
    zpad = jnp.zeros((n_keys - n_cache - t_new, k.shape[1]), F32)
    low = lax.broadcasted_iota(I32, (t_new, LANES), 1) < HEAD_DIM
    high = jnp.logical_not(low)

    def swap_halves(x):
        return pltpu.roll(x, HEAD_DIM, 1)

    scores, values = [], []
    for s_i in range(n_seq):
        ck, cv = ck_ref[s_i], cv_ref[s_i]
        kn, vn = k[s_i * t_new:(s_i + 1) * t_new], v[s_i * t_new:(s_i + 1) * t_new]
        nk_ref[s_i] = jnp.concatenate([ck[t_new:], kn], axis=0)
        nv_ref[s_i] = jnp.concatenate([cv[t_new:], vn], axis=0)
        kall = jnp.concatenate([ck, kn, zpad], axis=0).astype(BF16)
        values.append(jnp.concatenate([cv, vn, zpad], axis=0).astype(BF16))
        qs = q[s_i * t_new:(s_i + 1) * t_new]
        for g in range(n_kv):
            tile, odd = g // 2, g % 2
            parts = []
            for r in range(Q_PER_KV):
                h = g * Q_PER_KV + r
                x = qs[:, (h // 2) * LANES:(h // 2 + 1) * LANES]
                if h % 2 != odd:
                    x = swap_halves(x)
                parts.append(jnp.where(high if odd else low, x, 0.0))
            qg = jnp.concatenate(parts, axis=0).astype(BF16)
            k2 = kall[:, tile * LANES:(tile + 1) * LANES]
            scores.append(lax.dot_general(qg, k2, (((1,), (1,)), ((), ())), preferred_element_type=F32))
    s = jnp.concatenate(scores, axis=0)

    rows = n_seq * n_heads * t_new
    qt = lax.broadcasted_iota(I32, (rows, n_keys), 0) & (t_new - 1)
    kj = lax.broadcasted_iota(I32, (rows, n_keys), 1)
    mask = (kj > qt + (n_cache - WINDOW)) & (kj <= qt + n_cache)
    s = jnp.where(mask, s, NEG_INF)
    sink_seq = jnp.concatenate([jnp.full((t_new, 1), sink_ref[h], F32) for h in range(n_heads)], axis=0)
    sink = jnp.concatenate([sink_seq] * n_seq, axis=0)
    probs = _softmax_with_sink(s, sink).astype(BF16)

    seq_outs = []
    for s_i in range(n_seq):
        out_tiles = []
        for g in range(n_kv):
            tile, odd = g // 2, g % 2
            r0 = (s_i * n_kv + g) * Q_PER_KV * t_new
            v2 = values[s_i][:, tile * LANES:(tile + 1) * LANES]
            og = jnp.dot(probs[r0:r0 + Q_PER_KV * t_new], v2, preferred_element_type=F32)
            for pr in range(Q_PER_KV // 2):
                even = og[(2 * pr) * t_new:(2 * pr + 1) * t_new]
                oddh = og[(2 * pr + 1) * t_new:(2 * pr + 2) * t_new]
                if odd:
                    even = swap_halves(even)
                else:
                    oddh = swap_halves(oddh)
                out_tiles.append(jnp.where(low, even, oddh))
        seq_outs.append(jnp.concatenate(out_tiles, axis=1))
    o_ref[...] = jnp.concatenate(seq_outs, axis=0).astype(o_ref.dtype)


def _attn_sample(qkv, attn, row0, cache_k, cache_v, sinks, tables, n_seq, t_new, attn_w, kv_w):
    rows = SEQS_PER_STEP * t_new
    rb0 = row0 // rows
    kb, vb = attn_w // kv_w, attn_w // kv_w + 1
    n_cache = cache_k.shape[1]
    tab_spec = pl.BlockSpec((rows, LANES), lambda i, s: (0, 0))
    cache_spec = pl.BlockSpec((SEQS_PER_STEP, n_cache, kv_w), lambda i, s: (i, 0, 0))
    return pl.pallas_call(
        functools.partial(_attn_sample_kernel, n_kv=kv_w // HEAD_DIM, t_new=t_new),
        grid_spec=pltpu.PrefetchScalarGridSpec(
            num_scalar_prefetch=1,
            grid=(n_seq // SEQS_PER_STEP,),
            in_specs=[pl.BlockSpec((rows, attn_w), lambda i, s: (rb0 + i, 0)),
                      pl.BlockSpec((rows, kv_w), lambda i, s: (rb0 + i, kb)),
                      pl.BlockSpec((rows, kv_w), lambda i, s: (rb0 + i, vb)),
                      cache_spec, cache_spec, tab_spec, tab_spec, tab_spec,
                      pl.BlockSpec(memory_space=pl.ANY)],
            out_specs=[pl.BlockSpec((rows, attn_w), lambda i, s: (rb0 + i, 0)), cache_spec, cache_spec]),
        out_shape=[jax.ShapeDtypeStruct(attn.shape, attn.dtype),
                   jax.ShapeDtypeStruct(cache_k.shape, F32),
                   jax.ShapeDtypeStruct(cache_v.shape, F32)],
        input_output_aliases={9: 0},
        compiler_params=_params(1, 32),
        name="attn_sample",
    )(sinks, qkv, qkv, qkv, cache_k, cache_v, *tables, attn)


def _pool_prompt_kernel(z_ref, halo_ref, w_ref, sc_ref, o_ref, ext_ref, *, tm, pg, nb):
    i = pl.program_id(0)

    @pl.when(i >= nb)
    def _():
        o_ref[...] = jnp.zeros_like(o_ref)

    @pl.when(i < nb)
    def _():
        _pool_prompt_block(z_ref, halo_ref, w_ref, sc_ref, o_ref, ext_ref, tm=tm, pg=pg)


def _pool_prompt_block(z_ref, halo_ref, w_ref, sc_ref, o_ref, ext_ref, *, tm, pg):
    i = pl.program_id(0)
    ext_ref[0:POOL_HALO, :] = jnp.where(i > 0, halo_ref[...], 0.0)
    ext_ref[POOL_HALO:, :] = z_ref[...]
    pos = i * tm + lax.broadcasted_iota(I32, (tm, 1), 0)
    for g, w in enumerate(POOL_WINDOWS):
        c0, c1 = g * pg, (g + 1) * pg
        cur = ext_ref[POOL_HALO:POOL_HALO + tm, c0:c1]
        acc = cur
        for d in range(1, w):
            acc = acc + ext_ref[POOL_HALO - d:POOL_HALO - d + tm, c0:c1]
        cnt = jnp.minimum(pos + 1, w).astype(F32)
        pooled = (acc / cnt - cur).astype(BF16)
        mixed = jnp.dot(pooled, w_ref[g].astype(BF16), preferred_element_type=F32) * sc_ref[:, c0:c1]
        o_ref[:, c0:c1] = mixed.astype(o_ref.dtype)


def _pool_prompt(z, n_rows, w_grp, scale):
    n, pw = z.shape
    tm = TM_ROW
    hb = tm // POOL_HALO
    return pl.pallas_call(
        functools.partial(_pool_prompt_kernel, tm=tm, pg=pw // len(POOL_WINDOWS), nb=n_rows // tm),
        grid=(n // tm,),
        in_specs=[pl.BlockSpec((tm, pw), lambda i: (i, 0)),
                  pl.BlockSpec((POOL_HALO, pw), lambda i: (jnp.maximum(i * hb - 1, 0), 0)),
                  pl.BlockSpec(w_grp.shape, lambda i: (0, 0, 0)),
                  pl.BlockSpec((1, pw), lambda i: (0, 0))],
        out_specs=pl.BlockSpec((tm, pw), lambda i: (i, 0)),
        out_shape=jax.ShapeDtypeStruct((n, pw), BF16),
        scratch_shapes=[pltpu.VMEM((tm + POOL_HALO, pw), F32)],
        compiler_params=_params(1, 40),
        name="pool_prompt",
    )(z, z, w_grp, scale.reshape(1, pw))


def _pool_sample_kernel(st_ref, z_ref, w_ref, sc_ref, mixed_in_ref, o_ref, ns_ref, ext_ref, *, pg, t_new):
    del mixed_in_ref
    n_st = st_ref.shape[1]
    seqs = st_ref.shape[0]
    hist = ext_ref.shape[1] - t_new
    ext_ref[:, hist - n_st:hist, :] = st_ref[...]
    ext_ref[:, hist:, :] = z_ref[...]
    ns_ref[...] = ext_ref[:, hist + t_new - n_st:, :]
    for g, w in enumerate(POOL_WINDOWS):
        c0, c1 = g * pg, (g + 1) * pg
        cur = ext_ref[:, hist:, c0:c1]
        acc = cur
        for d in range(1, w):
            acc = acc + ext_ref[:, hist - d:hist - d + t_new, c0:c1]
        pooled = (acc / float(w) - cur).reshape(seqs * t_new, pg).astype(BF16)
        mixed = jnp.dot(pooled, w_ref[g].astype(BF16), preferred_element_type=F32) * sc_ref[:, c0:c1]
        o_ref[:, c0:c1] = mixed.astype(o_ref.dtype)


def _pool_sample(state, z, mixed, row0, w_grp, scale, t_new):
    n_seq, n_st, pw = state.shape
    assert t_new == SUBLANES and n_st <= POOL_HALO - 1
    z3 = z.reshape(z.shape[0] // t_new, t_new, pw)
    sb0 = row0 // t_new // POOL_SEQS
    rows = POOL_SEQS * t_new
    st_spec = pl.BlockSpec((POOL_SEQS, n_st, pw), lambda i: (i, 0, 0))
    return pl.pallas_call(
        functools.partial(_pool_sample_kernel, pg=pw // len(POOL_WINDOWS), t_new=t_new),
        grid=(n_seq // POOL_SEQS,),
        in_specs=[st_spec,
                  pl.BlockSpec((POOL_SEQS, t_new, pw), lambda i: (sb0 + i, 0, 0)),
                  pl.BlockSpec(w_grp.shape, lambda i: (0, 0, 0)),
                  pl.BlockSpec((1, pw), lambda i: (0, 0)),
                  pl.BlockSpec(memory_space=pl.ANY)],
        out_specs=[pl.BlockSpec((rows, pw), lambda i: (row0 // rows + i, 0)), st_spec],
        out_shape=[jax.ShapeDtypeStruct(mixed.shape, mixed.dtype), jax.ShapeDtypeStruct(state.shape, F32)],
        scratch_shapes=[pltpu.VMEM((POOL_SEQS, POOL_HALO + t_new, pw), F32)],
        input_output_aliases={4: 0},
        compiler_params=_params(1, 40),
        name="pool_sample",
    )(state, z3, w_grp, scale.reshape(1, pw), mixed)


def _merge_kernel(a1_ref, a2_ref, w1_ref, w2_ref, ga_ref, gb_ref, ba_ref, bb_ref, o_ref, w1b_ref, w2b_ref):
    @pl.when(pl.program_id(1) == 0)
    def _():
        w1b_ref[...] = w1_ref[...].astype(BF16)
        w2b_ref[...] = w2_ref[...].astype(BF16)

    a_up = jnp.dot(a1_ref[...], w1b_ref[...], preferred_element_type=F32)
    b_up = jnp.dot(a2_ref[...], w2b_ref[...], preferred_element_type=F32)
    merged = (jax.nn.sigmoid(ga_ref[...] + ba_ref[...]) * a_up
              + jax.nn.sigmoid(gb_ref[...] + bb_ref[...]) * b_up)
    o_ref[...] = merged.astype(o_ref.dtype)


def _merge(attn, mixed, w_attn_up, w_pool_up, gates, b_gate):
    n, ka = attn.shape
    kp = mixed.shape[1]
    d = w_attn_up.shape[1]
    nj = d // TN
    return pl.pallas_call(
        _merge_kernel,
        grid=(nj, n // TM),
        in_specs=[pl.BlockSpec((TM, ka), lambda j, i: (i, 0)),
                  pl.BlockSpec((TM, kp), lambda j, i: (i, 0)),
                  pl.BlockSpec((ka, TN), lambda j, i: (0, j)),
                  pl.BlockSpec((kp, TN), lambda j, i: (0, j)),
                  pl.BlockSpec((TM, TN), lambda j, i: (i, j)),
                  pl.BlockSpec((TM, TN), lambda j, i: (i, nj + j)),
                  pl.BlockSpec((1, TN), lambda j, i: (0, j)),
                  pl.BlockSpec((1, TN), lambda j, i: (0, nj + j))],
        out_specs=pl.BlockSpec((TM, TN), lambda j, i: (i, j)),
        out_shape=jax.ShapeDtypeStruct((n, d), BF16),
        scratch_shapes=[pltpu.VMEM((ka, TN), BF16), pltpu.VMEM((kp, TN), BF16)],
        compiler_params=_params(2, 56),
        name="merge",
    )(attn, mixed, w_attn_up, w_pool_up, gates, gates, b_gate, b_gate)


def _out_proj_kernel(a_ref, w_ref, ha_ref, hb_ref, o_ref, wb_ref, *, nb_first):
    @pl.when(pl.program_id(1) == 0)
    def _():
        wb_ref[...] = w_ref[...].astype(BF16)

    h = jnp.where(pl.program_id(1) < nb_first, ha_ref[...], hb_ref[...])
    o_ref[...] = h + jnp.dot(a_ref[...], wb_ref[...], preferred_element_type=F32)


def _out_proj(merged, w_out, ha, hb):
    n, d = merged.shape
    nb_first = ha.shape[0] // TM
    h_first = pl.BlockSpec((TM, TN), lambda j, i: (jnp.minimum(i, nb_first - 1), j))
    h_last = pl.BlockSpec((TM, TN), lambda j, i: (jnp.maximum(i - nb_first, 0), j))
    return pl.pallas_call(
        functools.partial(_out_proj_kernel, nb_first=nb_first),
        grid=(d // TN, n // TM),
        in_specs=[pl.BlockSpec((TM, d), lambda j, i: (i, 0)),
                  pl.BlockSpec((d, TN), lambda j, i: (0, j)),
                  h_first, h_last],
        out_specs=pl.BlockSpec((TM, TN), lambda j, i: (i, j)),
        out_shape=jax.ShapeDtypeStruct((n, d), F32),
        scratch_shapes=[pltpu.VMEM((d, TN), BF16)],
        compiler_params=_params(2, 56),
        name="out_proj",
    )(merged, w_out, ha, hb)


def _router_kernel(h_ref, g_ref, wr_ref, br_ref, info_ref, cw_ref, xf_ref, cnt_ref, base_ref):
    tm = h_ref.shape[0]

    @pl.when(pl.program_id(0) == 0)
    def _():
        base_ref[...] = jnp.zeros_like(base_ref)

    xf = _rms_rows(h_ref[...], g_ref[...]).astype(BF16)
    xf_ref[...] = xf.reshape(xf_ref.shape)
    logits = jnp.dot(xf, wr_ref[...].astype(BF16), preferred_element_type=F32) + br_ref[...]
    lane = lax.broadcasted_iota(I32, logits.shape, 1)
    big = jnp.int32(1 << 20)
    ninf = jnp.float32(-jnp.inf)

    is_g = lane < N_GROUPS
    gl = jnp.where(is_g, logits, ninf)
    gmax = jnp.max(gl, axis=-1, keepdims=True)
    gden = jnp.sum(jnp.exp(gl - gmax), axis=-1, keepdims=True)
    g_p = 1.0 / gden
    g_idx = jnp.min(jnp.where(gl == gmax, lane, big), axis=-1, keepdims=True)

    e_lane = lane - N_GROUPS
    in_grp = (e_lane >= g_idx * EXPERTS_PER_GROUP) & (e_lane < (g_idx + 1) * EXPERTS_PER_GROUP)
    el = jnp.where(in_grp, logits, ninf)
    m1 = jnp.max(el, axis=-1, keepdims=True)
    i1 = jnp.min(jnp.where(in_grp & (el == m1), e_lane, big), axis=-1, keepdims=True)
    el2 = jnp.where(e_lane == i1, ninf, el)
    m2 = jnp.max(el2, axis=-1, keepdims=True)
    i2 = jnp.min(jnp.where(in_grp & (el2 == m2), e_lane, big), axis=-1, keepdims=True)
    t = jnp.exp(m2 - m1)
    den = 1.0 + t
    c1 = g_p * (1.0 / den)
    c2 = g_p * (t / den)
    cw_ref[...] = jnp.where(lane == 0, c1, jnp.where(lane == 1, c2, 0.0))

    hit1, hit2 = lane == i1, lane == i2
    onehot = (hit1 | hit2).astype(BF16)
    earlier = (lax.broadcasted_iota(I32, (tm, tm), 1) < lax.broadcasted_iota(I32, (tm, tm), 0)).astype(BF16)
    before = jnp.dot(earlier, onehot, preferred_element_type=F32) + base_ref[...]
    r1 = jnp.sum(jnp.where(hit1, before, 0.0), axis=-1, keepdims=True).astype(I32)
    r2 = jnp.sum(jnp.where(hit2, before, 0.0), axis=-1, keepdims=True).astype(I32)
    base_ref[...] += jnp.sum(onehot.astype(F32), axis=0, keepdims=True)
    cnt_ref[...] = base_ref[...].astype(I32)
    info_ref[...] = jnp.where(lane == 0, i1, jnp.where(lane == 1, i2,
                              jnp.where(lane == 2, r1, jnp.where(lane == 3, r2, 0))))


def _router(h, g_ffn, w_rg, b_rg, w_re, b_re):
    n, d = h.shape
    n_log = N_GROUPS + N_EXPERTS
    wr = jnp.concatenate([w_rg, w_re, jnp.zeros((d, LANES - n_log), F32)], axis=1)
    br = jnp.concatenate([b_rg, b_re, jnp.zeros((LANES - n_log,), F32)]).reshape(1, LANES)
    lane_spec = pl.BlockSpec((TM_ROW, LANES), lambda i: (i, 0))
    return pl.pallas_call(
        _router_kernel,
        grid=(n // TM_ROW,),
        in_specs=[pl.BlockSpec((TM_ROW, d), lambda i: (i, 0)),
                  pl.BlockSpec((1, d), lambda i: (0, 0)),
                  pl.BlockSpec((d, LANES), lambda i: (0, 0)),
                  pl.BlockSpec((1, LANES), lambda i: (0, 0))],
        out_specs=[lane_spec, lane_spec,
                   pl.BlockSpec((TM_ROW, d // LANES, LANES), lambda i: (i, 0, 0)),
                   pl.BlockSpec((1, LANES), lambda i: (0, 0))],
        out_shape=[jax.ShapeDtypeStruct((n, LANES), I32), jax.ShapeDtypeStruct((n, LANES), F32),
                   jax.ShapeDtypeStruct((n, d // LANES, LANES), BF16), jax.ShapeDtypeStruct((1, LANES), I32)],
        scratch_shapes=[pltpu.VMEM((1, LANES), F32)],
        compiler_params=_params(1, 40),
        name="router",
    )(h, g_ffn.reshape(1, d), wr, br)


def _dispatch_plan(info, counts, n_tiles):
    counts = counts[0, :N_EXPERTS]
    tiles_per = (counts + TM_E - 1) // TM_E
    tiles_end = jnp.cumsum(tiles_per)
    row_start = (tiles_end - tiles_per) * TM_E
    eid, rank = info[:, :TOP_K], info[:, TOP_K:2 * TOP_K]
    onehot = (eid[:, :, None] == jnp.arange(N_EXPERTS, dtype=I32)[None, None, :]).astype(I32)
    pos = (jnp.sum(onehot * row_start[None, None, :], axis=-1) + rank).reshape(-1).astype(I32)
    n_used = tiles_end[-1:].astype(I32)
    tile_ids = jnp.minimum(jnp.arange(n_tiles, dtype=I32), n_used - 1)
    tile_expert = jnp.sum((tile_ids[:, None] >= tiles_end[None, :]).astype(I32), axis=1).astype(I32)
    last_tile_row = (jnp.clip(tiles_end - 1, 0, n_tiles - 1) * TM_E).astype(I32)
    starts = jnp.concatenate([jnp.ones((1,), I32), (tile_expert[1:] != tile_expert[:-1]).astype(I32)])
    run_idx = jnp.cumsum(starts) - 1
    n_runs = run_idx[-1:] + 1
    run_expert = jnp.zeros((n_tiles,), I32).at[run_idx].set(tile_expert)
    next_expert = run_expert[(run_idx + 1) % n_runs]
    plan = (tile_expert, n_used, run_idx.astype(I32), next_expert.astype(I32), n_runs.astype(I32))
    return pos, plan, last_tile_row


def _dispatch_kernel(pos_ref, ltr_ref, nu_ref, xf_ref, xs_hbm, zero_ref, sem):
    i = pl.program_id(0)
    n_tiles = xs_hbm.shape[0] // TM_E

    def clear_tile(row):
        return pltpu.make_async_copy(zero_ref, xs_hbm.at[pl.ds(pl.multiple_of(row, TM_E), TM_E)], sem)

    @pl.when(i == 0)
    def _():
        zero_ref[...] = jnp.zeros_like(zero_ref)
        for e in range(N_EXPERTS):
            clear_tile(ltr_ref[e]).start()

        def start_unused(t, carry):
            clear_tile(t * TM_E).start()
            return carry

        def wait_one(t, carry):
            clear_tile(0).wait()
            return carry

        lax.fori_loop(nu_ref[0], n_tiles, start_unused, 0)
        lax.fori_loop(nu_ref[0] - N_EXPERTS, n_tiles, wait_one, 0)

    base = i * TM_DISPATCH

    def issue(r, carry):
        for k in range(TOP_K):
            dst_row = pos_ref[(base + r) * TOP_K + k]
            pltpu.make_async_copy(xf_ref.at[pl.ds(r, 1)], xs_hbm.at[pl.ds(dst_row, 1)], sem).start()
        return carry

    lax.fori_loop(0, TM_DISPATCH, issue, 0)
    for k in range(TOP_K):
        pltpu.make_async_copy(xf_ref, xs_hbm.at[pl.ds(0, TM_DISPATCH)], sem).wait()


def _dispatch(pos, last_tile_row, n_used, xf3, n_tiles):
    n, c, l = xf3.shape
    return pl.pallas_call(
        _dispatch_kernel,
        grid_spec=pltpu.PrefetchScalarGridSpec(
            num_scalar_prefetch=3,
            grid=(n // TM_DISPATCH,),
            in_specs=[pl.BlockSpec((TM_DISPATCH, c, l), lambda i, *_: (i, 0, 0))],
            out_specs=pl.BlockSpec(memory_space=pl.ANY),
            scratch_shapes=[pltpu.VMEM((TM_E, c, l), xf3.dtype), pltpu.SemaphoreType.DMA(())]),
        out_shape=jax.ShapeDtypeStruct((n_tiles * TM_E, c, l), xf3.dtype),
        compiler_params=_params(1, 24),
        name="moe_dispatch",
    )(pos, last_tile_row, n_used, xf3)


def _expert_weights_step(plan_refs, w_hbms, wbuf_ref, wb_refs, sem, tn):
    te_ref, _, run_ref, nxt_ref, nr_ref = plan_refs
    j, t = pl.program_id(0), pl.program_id(1)

    def copies(e, jj):
        col = pl.multiple_of(jj * tn, tn)
        return [pltpu.make_async_copy(w.at[e, :, pl.ds(col, tn)], wbuf_ref.at[m], sem)
                for m, w in enumerate(w_hbms)]

    @pl.when((t == 0) | (te_ref[t] != te_ref[jnp.maximum(t - 1, 0)]))
    def _():
        @pl.when((j == 0) & (t == 0))
        def _():
            for c in copies(te_ref[t], j):
                c.start()

        for c in copies(te_ref[t], j):
            c.wait()
        for m, wb_ref in enumerate(wb_refs):
            wb_ref[...] = wbuf_ref[m].astype(BF16)

        next_j = jnp.where(run_ref[t] + 1 == nr_ref[0], j + 1, j)

        @pl.when(next_j < pl.num_programs(0))
        def _():
            for c in copies(nxt_ref[t], next_j):
                c.start()


def _moe_up_kernel(te_ref, nu_ref, run_ref, nxt_ref, nr_ref, xs_ref, wg_hbm, wu_hbm, o_ref,
                   wbuf_ref, wgb_ref, wub_ref, sem):
    t = pl.program_id(1)
    _expert_weights_step((te_ref, nu_ref, run_ref, nxt_ref, nr_ref), (wg_hbm, wu_hbm), wbuf_ref,
                         (wgb_ref, wub_ref), sem, o_ref.shape[1])

    @pl.when(t < nu_ref[0])
    def _():
        x = xs_ref[...].reshape(xs_ref.shape[0], -1)
        gate = jnp.dot(x, wgb_ref[...], preferred_element_type=F32)
        up = jnp.dot(x, wub_ref[...], preferred_element_type=F32)
        o_ref[...] = (jax.nn.silu(gate) * up).astype(o_ref.dtype)

    @pl.when(t >= nu_ref[0])
    def _():
        o_ref[...] = jnp.zeros_like(o_ref)


def _moe_up(plan, xs3, w_gate, w_up, n_tiles):
    d, f = w_gate.shape[1], w_gate.shape[2]
    tf = TN
    any_spec = pl.BlockSpec(memory_space=pl.ANY)
    return pl.pallas_call(
        _moe_up_kernel,
        grid_spec=pltpu.PrefetchScalarGridSpec(
            num_scalar_prefetch=len(plan),
            grid=(f // tf, n_tiles),
            in_specs=[pl.BlockSpec((TM_E,) + xs3.shape[1:], lambda j, t, te, nu, *_: (jnp.minimum(t, nu[0] - 1), 0, 0)),
                      any_spec, any_spec],
            out_specs=pl.BlockSpec((TM_E, tf), lambda j, t, *_: (t, j)),
            scratch_shapes=[pltpu.VMEM((2, d, tf), F32), pltpu.VMEM((d, tf), BF16), pltpu.VMEM((d, tf), BF16),
                            pltpu.SemaphoreType.DMA(())]),
        out_shape=jax.ShapeDtypeStruct((n_tiles * TM_E, f), BF16),
        compiler_params=_params(2, 56),
        name="moe_up",
    )(*plan, xs3, w_gate, w_up)


def _moe_down_kernel(te_ref, nu_ref, run_ref, nxt_ref, nr_ref, hid_ref, wd_hbm, o_ref, wbuf_ref, wdb_ref, sem):
    t = pl.program_id(1)
    _expert_weights_step((te_ref, nu_ref, run_ref, nxt_ref, nr_ref), (wd_hbm,), wbuf_ref, (wdb_ref,), sem,
                         wdb_ref.shape[1])

    @pl.when(t < nu_ref[0])
    def _():
        y = jnp.dot(hid_ref[...], wdb_ref[...], preferred_element_type=F32)
        o_ref[...] = y.reshape(o_ref.shape)

    @pl.when(t >= nu_ref[0])
    def _():
        o_ref[...] = jnp.zeros_like(o_ref)


def _moe_down(plan, hid, w_down, n_tiles):
    f, d = w_down.shape[1], w_down.shape[2]
    tn = 4 * TN
    return pl.pallas_call(
        _moe_down_kernel,
        grid_spec=pltpu.PrefetchScalarGridSpec(
            num_scalar_prefetch=len(plan),
            grid=(d // tn, n_tiles),
            in_specs=[pl.BlockSpec((TM_E, f), lambda j, t, *_: (t, 0)),
                      pl.BlockSpec(memory_space=pl.ANY)],
            out_specs=pl.BlockSpec((TM_E, tn // LANES, LANES), lambda j, t, *_: (t, j, 0)),
            scratch_shapes=[pltpu.VMEM((1, f, tn), F32), pltpu.VMEM((f, tn), BF16),
                            pltpu.SemaphoreType.DMA(())]),
        out_shape=jax.ShapeDtypeStruct((n_tiles * TM_E, d // LANES, LANES), F32),
        compiler_params=_params(2, 40),
        name="moe_down",
    )(*plan, hid, w_down)


def _combine_kernel(pos_ref, h_ref, cw_ref, y_hbm, g_ref, ho_ref, xo_ref, buf_ref, sem):
    i = pl.program_id(0)
    tm, d = h_ref.shape

    def row_copy(slot, k, r, src_row):
        return pltpu.make_async_copy(y_hbm.at[pl.ds(src_row, 1)], buf_ref.at[slot, k, pl.ds(r, 1)], sem.at[slot])

    def issue_step(step, slot):
        def body(r, carry):
            for k in range(TOP_K):
                row_copy(slot, k, r, pos_ref[(step * tm + r) * TOP_K + k]).start()
            return carry
        lax.fori_loop(0, tm, body, 0)

    @pl.when(i == 0)
    def _():
        issue_step(0, 0)

    @pl.when(i + 1 < pl.num_programs(0))
    def _():
        issue_step(i + 1, (i + 1) % 2)

    slot = i % 2
    for k in range(TOP_K):
        pltpu.make_async_copy(y_hbm.at[pl.ds(0, tm)], buf_ref.at[slot, k], sem.at[slot]).wait()
    cw = cw_ref[...]
    y1 = buf_ref[slot, 0].reshape(tm, d)
    y2 = buf_ref[slot, 1].reshape(tm, d)
    h2 = h_ref[...] + (cw[:, 0:1] * y1 + cw[:, 1:2] * y2)
    ho_ref[...] = h2
    xo_ref[...] = _rms_rows(h2, g_ref[...]).astype(xo_ref.dtype)


def _combine(pos, h, cw, y3, g_ple):
    n, d = h.shape
    row_spec = pl.BlockSpec((TM_ROW, d), lambda i, p: (i, 0))
    return pl.pallas_call(
        _combine_kernel,
        grid_spec=pltpu.PrefetchScalarGridSpec(
            num_scalar_prefetch=1,
            grid=(n // TM_ROW,),
            in_specs=[row_spec,
                      pl.BlockSpec((TM_ROW, LANES), lambda i, p: (i, 0)),
                      pl.BlockSpec(memory_space=pl.ANY),
                      pl.BlockSpec((1, d), lambda i, p: (0, 0))],
            out_specs=[row_spec, row_spec],
            scratch_shapes=[pltpu.VMEM((2, TOP_K, TM_ROW) + y3.shape[1:], F32), pltpu.SemaphoreType.DMA((2,))]),
        out_shape=[jax.ShapeDtypeStruct((n, d), F32), jax.ShapeDtypeStruct((n, d), BF16)],
        compiler_params=_params(1, 56),
        name="moe_combine",
    )(pos, h, cw, y3, g_ple.reshape(1, d))


def _ple_kernel(xn_ref, wg_ref, bg_ref, pa_ref, pb_ref, wp_ref, h_ref, o_ref, wgb_ref, *, nb_first):
    @pl.when(pl.program_id(1) == 0)
    def _():
        wgb_ref[...] = wg_ref[...].astype(BF16)

    gate = jax.nn.sigmoid(jnp.dot(xn_ref[...], wgb_ref[...], preferred_element_type=F32) + bg_ref[...])
    p = jnp.where(pl.program_id(1) < nb_first, pa_ref[...], pb_ref[...]).astype(BF16)
    ple = jnp.dot(p, wp_ref[...].astype(BF16), preferred_element_type=F32)
    o_ref[...] = h_ref[...] + gate * ple


def _ple(xn, w_gate, b_gate, pa, pb, w_proj, h):
    n, d = h.shape
    pd = pa.shape[1]
    nb_first = pa.shape[0] // TM
    p_first = pl.BlockSpec((TM, pd), lambda j, i: (jnp.minimum(i, nb_first - 1), 0))
    p_last = pl.BlockSpec((TM, pd), lambda j, i: (jnp.maximum(i - nb_first, 0), 0))
    return pl.pallas_call(
        functools.partial(_ple_kernel, nb_first=nb_first),
        grid=(d // TN, n // TM),
        in_specs=[pl.BlockSpec((TM, d), lambda j, i: (i, 0)),
                  pl.BlockSpec((d, TN), lambda j, i: (0, j)),
                  pl.BlockSpec((1, TN), lambda j, i: (0, j)),
                  p_first, p_last,
                  pl.BlockSpec((pd, TN), lambda j, i: (0, j)),
                  pl.BlockSpec((TM, TN), lambda j, i: (i, j))],
        out_specs=pl.BlockSpec((TM, TN), lambda j, i: (i, j)),
        out_shape=jax.ShapeDtypeStruct((n, d), F32),
        scratch_shapes=[pltpu.VMEM((d, TN), BF16)],
        compiler_params=_params(2, 56),
        name="ple",
    )(xn, w_gate, b_gate.reshape(1, d), pa, pb, w_proj, h)


def _layer(xn, h_p, h_s, p_p, p_s, cache_k, cache_v, state_pool, lw):
    n, d = xn.shape
    n_prompt = h_p.shape[0]
    n_seq, n_cache, n_kv, _ = cache_k.shape
    t_new = (n - n_prompt) // n_seq
    kv_w = n_kv * HEAD_DIM
    attn_w = lw['w_attn_up'].shape[0]
    pool_w = lw['w_pool_up'].shape[0]
    n_pool = state_pool.shape[1]

    qkv = _in_proj(xn, lw['w_in'], 0, attn_w + 2 * kv_w, "in_proj_qkv")
    z = _in_proj(xn, lw['w_in'], attn_w + 2 * kv_w, pool_w, "in_proj_pool")
    gates = _in_proj(xn, lw['w_in'], attn_w + 2 * kv_w + pool_w, 2 * d, "in_proj_gates")

    tab_p = _rope_tables(jnp.arange(n_prompt, dtype=F32))
    tab_s = tuple(jnp.tile(a, (SEQS_PER_STEP, 1))
                  for a in _rope_tables(PAST_LEN + jnp.arange(t_new, dtype=F32)))
    tab_p_t = _rope_tables_t(jnp.arange(n_prompt, dtype=F32))
    attn, k_rot_p = _attn_prompt(qkv, lw['attn_sinks'], tab_p, tab_p_t, n_prompt, attn_w, kv_w)
    attn, new_k_s, new_v_s = _attn_sample(
        qkv, attn, n_prompt, cache_k.reshape(n_seq, n_cache, kv_w), cache_v.reshape(n_seq, n_cache, kv_w),
        lw['attn_sinks'], tab_s, n_seq, t_new, attn_w, kv_w)

    mixed = _pool_prompt(z, n_prompt, lw['w_pool_grp'], lw['pool_scale'])
    mixed, new_z_s = _pool_sample(state_pool, z, mixed, n_prompt, lw['w_pool_grp'], lw['pool_scale'], t_new)

    merged = _merge(attn, mixed, lw['w_attn_up'], lw['w_pool_up'], gates, lw['b_gate'].reshape(1, 2 * d))
    h1 = _out_proj(merged, lw['w_out'], h_p, h_s)

    n_tiles = (n * TOP_K) // TM_E + N_EXPERTS
    info, cw, xf3, counts = _router(h1, lw['g_ffn'], lw['w_route_group'], lw['b_route_group'],
                                    lw['w_route_expert'], lw['b_route_expert'])
    pos, plan, last_tile_row = _dispatch_plan(info, counts, n_tiles)
    xs3 = _dispatch(pos, last_tile_row, plan[1], xf3, n_tiles)
    hid = _moe_up(plan, xs3, lw['w_exp_gate'], lw['w_exp_up'], n_tiles)
    y3 = _moe_down(plan, hid, lw['w_exp_down'], n_tiles)
    h2, xn2 = _combine(pos, h1, cw, y3, lw['g_ple'])

    h3 = _ple(xn2, lw['w_ple_gate'], lw['b_ple_gate'], p_p, p_s, lw['w_ple_proj'], h2)

    new_k_p = k_rot_p[n_prompt - n_cache:].reshape(1, n_cache, n_kv, HEAD_DIM)
    new_v_p = qkv[n_prompt - n_cache:n_prompt, attn_w + kv_w:].reshape(1, n_cache, n_kv, HEAD_DIM)
    new_z_p = z[n_prompt - n_pool:n_prompt].reshape(1, n_pool, pool_w)
    return (h3, new_k_p, new_v_p, new_z_p,
            new_k_s.reshape(cache_k.shape), new_v_s.reshape(cache_v.shape), new_z_s)


def kernel(x_prompt, x_sample, cache_k, cache_v, state_pool, p_prompt, p_sample, g_mix, w_in, b_gate, attn_sinks, w_pool_grp, pool_scale, w_attn_up, w_pool_up, w_out, g_ffn, w_route_group, b_route_group, w_route_expert, b_route_expert, w_exp_gate, w_exp_up, w_exp_down, g_ple, w_ple_gate, b_ple_gate, w_ple_proj, g_final):
    batch, seq, d = x_prompt.shape
    n_seq, t_new, _ = x_sample.shape
    depth = w_in.shape[0]
    assert batch == 1, "prompt rows are treated as one sequence"
    n_prompt = batch * seq
    n_sample = n_seq * t_new
    n = n_prompt + n_sample
    assert n_prompt % TM == 0 and n_sample % TM == 0 and (n * TOP_K) % TM_E == 0 and n % TM_DISPATCH == 0
    assert n_seq % SEQS_PER_STEP == 0 and n_seq % POOL_SEQS == 0 and t_new & (t_new - 1) == 0
    assert cache_k.shape[2] == WINDOW and state_pool.shape[2] == POOL_HALO - 1
    assert depth == 1, "one layer per step"

    weights = dict(g_mix=g_mix, w_in=w_in, b_gate=b_gate, attn_sinks=attn_sinks, w_pool_grp=w_pool_grp,
                   pool_scale=pool_scale, w_attn_up=w_attn_up, w_pool_up=w_pool_up, w_out=w_out, g_ffn=g_ffn,
                   w_route_group=w_route_group, b_route_group=b_route_group, w_route_expert=w_route_expert,
                   b_route_expert=b_route_expert, w_exp_gate=w_exp_gate, w_exp_up=w_exp_up,
                   w_exp_down=w_exp_down, g_ple=g_ple, w_ple_gate=w_ple_gate, b_ple_gate=b_ple_gate,
                   w_ple_proj=w_ple_proj)
    lw = {name: w[0] for name, w in weights.items()}
    h_p = x_prompt.reshape(n_prompt, d)
    h_s = x_sample.reshape(n_sample, d)
    xn = _rmsnorm_in2(h_p, h_s, lw['g_mix'], BF16, "norm_mix")
    out = _layer(xn, h_p, h_s, p_prompt[0].reshape(n_prompt, -1), p_sample[0].reshape(n_sample, -1),
                 cache_k[0], cache_v[0], state_pool[0], lw)
    y_p, y_s = _rmsnorm_out2(out[0], g_final, n_prompt, "norm_final")
    return (y_p.reshape(batch, seq, d), y_s.reshape(n_seq, t_new, d)) + tuple(piece[None] for piece in out[1:])
```

```python
import functools

import jax
import jax.numpy as jnp
from jax import lax
from jax.experimental import pallas as pl
from jax.experimental.pallas import tpu as pltpu

F32 = jnp.float32
BF16 = jnp.bfloat16
I32 = jnp.int32

HEAD_DIM = 64
Q_PER_KV = 4
WINDOW = 128
PAST_LEN = 8192
ROT_DIM = HEAD_DIM // 4
ROPE_THETA = 500000.0
POOL_WINDOWS = (2, 4, 8, 16)
POOL_HALO = 16
N_GROUPS = 4
EXPERTS_PER_GROUP = 4
N_EXPERTS = N_GROUPS * EXPERTS_PER_GROUP
TOP_K = 2
RMS_EPS = 1e-6
NEG_INF = -1e30
LANES = 128
SUBLANES = 8
MIB = 1024 * 1024

TM = 1024
TN = 512
TN_PROJ = 1024
TM_ROW = 256
TM_E = 512
TM_DISPATCH = 512
SEQS_PER_STEP = 8
POOL_SEQS = 16


def _params(n_axes, vmem_mib):
    return pltpu.CompilerParams(dimension_semantics=("arbitrary",) * n_axes,
                                vmem_limit_bytes=vmem_mib * MIB)


def _rms_rows(x, g):
    ms = jnp.mean(x * x, axis=-1, keepdims=True)
    return x * lax.rsqrt(ms + RMS_EPS) * g


def _first_last_specs(nb_first, block):
    zeros = (0,) * (len(block) - 1)
    first = pl.BlockSpec(block, lambda i, *_: (jnp.minimum(i, nb_first - 1),) + zeros)
    last = pl.BlockSpec(block, lambda i, *_: (jnp.maximum(i - nb_first, 0),) + zeros)
    return first, last


def _norm_in2_kernel(xa_ref, xb_ref, g_ref, o_ref, *, nb_first):
    i = pl.program_id(0)

    @pl.when(i < nb_first)
    def _():
        o_ref[...] = _rms_rows(xa_ref[...], g_ref[...]).astype(o_ref.dtype)

    @pl.when(i >= nb_first)
    def _():
        o_ref[...] = _rms_rows(xb_ref[...], g_ref[...]).astype(o_ref.dtype)


def _rmsnorm_in2(xa, xb, g, out_dtype, name):
    (na, d), nb = xa.shape, xb.shape[0]
    nb_first = na // TM_ROW
    spec_a, spec_b = _first_last_specs(nb_first, (TM_ROW, d))
    return pl.pallas_call(
        functools.partial(_norm_in2_kernel, nb_first=nb_first),
        grid=((na + nb) // TM_ROW,),
        in_specs=[spec_a, spec_b, pl.BlockSpec((1, d), lambda i: (0, 0))],
        out_specs=pl.BlockSpec((TM_ROW, d), lambda i: (i, 0)),
        out_shape=jax.ShapeDtypeStruct((na + nb, d), out_dtype),
        compiler_params=_params(1, 40),
        name=name,
    )(xa, xb, g.reshape(1, d))


def _norm_out2_kernel(x_ref, g_ref, oa_ref, ob_ref, *, nb_first):
    i = pl.program_id(0)

    @pl.when(i < nb_first)
    def _():
        oa_ref[...] = _rms_rows(x_ref[...], g_ref[...])

    @pl.when(i >= nb_first)
    def _():
        ob_ref[...] = _rms_rows(x_ref[...], g_ref[...])


def _rmsnorm_out2(x, g, na, name):
    n, d = x.shape
    nb_first = na // TM_ROW
    spec_a, spec_b = _first_last_specs(nb_first, (TM_ROW, d))
    return pl.pallas_call(
        functools.partial(_norm_out2_kernel, nb_first=nb_first),
        grid=(n // TM_ROW,),
        in_specs=[pl.BlockSpec((TM_ROW, d), lambda i: (i, 0)), pl.BlockSpec((1, d), lambda i: (0, 0))],
        out_specs=[spec_a, spec_b],
        out_shape=[jax.ShapeDtypeStruct((na, d), F32), jax.ShapeDtypeStruct((n - na, d), F32)],
        compiler_params=_params(1, 40),
        name=name,
    )(x, g.reshape(1, d))


def _proj_kernel(a_ref, w_hbm, o_ref, wbuf_ref, wb_ref, sem, *, col0):
    j, i = pl.program_id(0), pl.program_id(1)
    tn = wb_ref.shape[1]

    def copy(jj):
        col = pl.multiple_of(col0 + jj * tn, tn)
        return pltpu.make_async_copy(w_hbm.at[:, pl.ds(col, tn)], wbuf_ref, sem)

    @pl.when(i == 0)
    def _():
        @pl.when(j == 0)
        def _():
            copy(j).start()

        copy(j).wait()
        wb_ref[...] = wbuf_ref[...].astype(BF16)

        @pl.when(j + 1 < pl.num_programs(0))
        def _():
            copy(j + 1).start()

    o_ref[...] = jnp.dot(a_ref[...], wb_ref[...], preferred_element_type=F32)


def _in_proj(xn, w, col0, width, name):
    n, d = xn.shape
    tn = TN_PROJ
    assert col0 % tn == 0 and width % tn == 0
    return pl.pallas_call(
        functools.partial(_proj_kernel, col0=col0),
        grid=(width // tn, n // TM),
        in_specs=[pl.BlockSpec((TM, d), lambda j, i: (i, 0)),
                  pl.BlockSpec(memory_space=pl.ANY)],
        out_specs=pl.BlockSpec((TM, tn), lambda j, i: (i, j)),
        out_shape=jax.ShapeDtypeStruct((n, width), F32),
        scratch_shapes=[pltpu.VMEM((d, tn), F32), pltpu.VMEM((d, tn), BF16), pltpu.SemaphoreType.DMA(())],
        compiler_params=_params(2, 56),
        name=name,
    )(xn, w)


def _rope_tables(pos):
    half = ROT_DIM // 2
    inv_freq = ROPE_THETA ** (-jnp.arange(half, dtype=F32) * (2.0 / ROT_DIM))
    ang = pos[:, None] * inv_freq[None, :]
    cos, sin = jnp.cos(ang), jnp.sin(ang)
    t = pos.shape[0]
    pad = jnp.zeros((t, HEAD_DIM - ROT_DIM), F32)
    zeros = jnp.zeros((t, half), F32)
    cos_h = jnp.concatenate([cos, cos, pad + 1.0], axis=1)
    sa_h = jnp.concatenate([zeros, sin, pad], axis=1)
    sb_h = jnp.concatenate([-sin, zeros, pad], axis=1)
    reps = LANES // HEAD_DIM
    return tuple(jnp.tile(a, (1, reps)) for a in (cos_h, sa_h, sb_h))


def _rope_tables_t(pos):
    half = ROT_DIM // 2
    inv_freq = ROPE_THETA ** (-jnp.arange(half, dtype=F32) * (2.0 / ROT_DIM))
    ang = pos[:, None] * inv_freq[None, :]
    return jnp.cos(ang).T, jnp.sin(ang).T


def _rope(x, cos_t, sa_t, sb_t):
    pieces = []
    for c in range(x.shape[1] // LANES):
        xc = x[:, c * LANES:(c + 1) * LANES]
        pieces.append(xc * cos_t
                      + pltpu.roll(xc, ROT_DIM // 2, 1) * sa_t
                      + pltpu.roll(xc, LANES - ROT_DIM // 2, 1) * sb_t)
    return jnp.concatenate(pieces, axis=1)


def _softmax_with_sink(s, sink):
    m = jnp.maximum(jnp.max(s, axis=-1, keepdims=True), sink)
    p = jnp.exp(s - m)
    den = jnp.sum(p, axis=-1, keepdims=True) + jnp.exp(sink - m)
    return p / den


def _attn_prompt_kernel(sink_ref, q_ref, k_ref, v_ref, cos_ref, sa_ref, sb_ref, cost_ref, sint_ref,
                        o_ref, ko_ref, kprev_ref, vprevt_ref, *, n_kv, nb):
    b = pl.program_id(0)

    @pl.when(b == 0)
    def _():
        kprev_ref[...] = jnp.zeros_like(kprev_ref)
        vprevt_ref[...] = jnp.zeros_like(vprevt_ref)

    @pl.when(b >= nb)
    def _():
        o_ref[...] = jnp.zeros_like(o_ref)

    @pl.when(b < nb)
    def _():
        _attn_prompt_block(sink_ref, q_ref, k_ref, v_ref, cos_ref, sa_ref, sb_ref, cost_ref, sint_ref,
                           o_ref, ko_ref, kprev_ref, vprevt_ref, n_kv=n_kv)


def _attn_prompt_block(sink_ref, q_ref, k_ref, v_ref, cos_ref, sa_ref, sb_ref, cost_ref, sint_ref,
                       o_ref, ko_ref, kprev_ref, vprevt_ref, *, n_kv):
    b = pl.program_id(0)
    half = ROT_DIM // 2
    k = _rope(k_ref[...], cos_ref[...], sa_ref[...], sb_ref[...])
    ko_ref[...] = k
    kcat = jnp.concatenate([kprev_ref[...], k], axis=0).astype(BF16)
    vt = v_ref[...].T
    vcat_t = jnp.concatenate([vprevt_ref[...], vt], axis=1).astype(BF16)
    kprev_ref[...] = k
    vprevt_ref[...] = vt

    qt = q_ref[...].T
    cos_t, sin_t = cost_ref[...], sint_ref[...]

    def roped_head_t(h):
        x = qt[h * HEAD_DIM:(h + 1) * HEAD_DIM]
        x1, x2 = x[0:half], x[half:ROT_DIM]
        rot = jnp.concatenate([x1 * cos_t - x2 * sin_t, x2 * cos_t + x1 * sin_t, x[ROT_DIM:]], axis=0)
        return (rot * (HEAD_DIM ** -0.5)).astype(BF16)

    n_heads = n_kv * Q_PER_KV
    lanes = n_heads * WINDOW
    zeros_head = jnp.zeros((HEAD_DIM, WINDOW), BF16)
    heads_per_tile = LANES // HEAD_DIM

    scores = []
    for g in range(n_kv):
        tile, slot = g // heads_per_tile, g % heads_per_tile
        k2 = kcat[:, tile * LANES:(tile + 1) * LANES]
        cols = []
        for r in range(Q_PER_KV):
            parts = [zeros_head] * heads_per_tile
            parts[slot] = roped_head_t(g * Q_PER_KV + r)
            cols.append(jnp.concatenate(parts, axis=0))
        rhs = jnp.concatenate(cols, axis=1)
        scores.append(jnp.dot(k2, rhs, preferred_element_type=F32))
    st = jnp.concatenate(scores, axis=1)

    key = lax.broadcasted_iota(I32, (2 * WINDOW, lanes), 0)
    qry = lax.broadcasted_iota(I32, (2 * WINDOW, lanes), 1) & (WINDOW - 1)
    first_key = jnp.where(b > 0, 0, WINDOW)
    mask = (key > qry) & (key <= qry + WINDOW) & (key >= first_key)
    st = jnp.where(mask, st, NEG_INF)
    sink = jnp.concatenate([jnp.full((1, WINDOW), sink_ref[h], F32) for h in range(n_heads)], axis=1)
    m = jnp.maximum(jnp.max(st, axis=0, keepdims=True), sink)
    p = jnp.exp(st - m)
    den = jnp.sum(p, axis=0, keepdims=True) + jnp.exp(sink - m)
    probs_t = (p * (1.0 / den)).astype(BF16)

    for g in range(n_kv):
        ot = jnp.dot(vcat_t[g * HEAD_DIM:(g + 1) * HEAD_DIM],
                     probs_t[:, g * Q_PER_KV * WINDOW:(g + 1) * Q_PER_KV * WINDOW],
                     preferred_element_type=F32)
        for pr in range(Q_PER_KV // heads_per_tile):
            pair = jnp.concatenate(
                [ot[:, (pr * heads_per_tile + u) * WINDOW:(pr * heads_per_tile + u + 1) * WINDOW]
                 for u in range(heads_per_tile)], axis=0)
            c = (g * Q_PER_KV) // heads_per_tile + pr
            o_ref[:, c * LANES:(c + 1) * LANES] = pair.T.astype(o_ref.dtype)


def _attn_prompt(qkv, sinks, tables, tables_t, n_rows, attn_w, kv_w):
    nb = n_rows // WINDOW
    kb, vb = attn_w // kv_w, attn_w // kv_w + 1

    def blk(b, s):
        return jnp.minimum(b, nb - 1)

    tab_spec = pl.BlockSpec((WINDOW, LANES), lambda b, s: (blk(b, s), 0))
    tab_t_spec = pl.BlockSpec((ROT_DIM // 2, WINDOW), lambda b, s: (0, blk(b, s)))
    return pl.pallas_call(
        functools.partial(_attn_prompt_kernel, n_kv=kv_w // HEAD_DIM, nb=nb),
        grid_spec=pltpu.PrefetchScalarGridSpec(
            num_scalar_prefetch=1,
            grid=(qkv.shape[0] // WINDOW,),
            in_specs=[pl.BlockSpec((WINDOW, attn_w), lambda b, s: (blk(b, s), 0)),
                      pl.BlockSpec((WINDOW, kv_w), lambda b, s: (blk(b, s), kb)),
                      pl.BlockSpec((WINDOW, kv_w), lambda b, s: (blk(b, s), vb)),
                      tab_spec, tab_spec, tab_spec, tab_t_spec, tab_t_spec],
            out_specs=[pl.BlockSpec((WINDOW, attn_w), lambda b, s: (b, 0)),
                       pl.BlockSpec((WINDOW, kv_w), lambda b, s: (blk(b, s), 0))],
            scratch_shapes=[pltpu.VMEM((WINDOW, kv_w), F32), pltpu.VMEM((kv_w, WINDOW), F32)]),
        out_shape=[jax.ShapeDtypeStruct((qkv.shape[0], attn_w), BF16),
                   jax.ShapeDtypeStruct((n_rows, kv_w), F32)],
        compiler_params=_params(1, 32),
        name="attn_prompt",
    )(sinks, qkv, qkv, qkv, *tables, *tables_t)


def _attn_sample_kernel(sink_ref, q_ref, k_ref, v_ref, ck_ref, cv_ref, cos_ref, sa_ref, sb_ref, attn_in_ref,
                        o_ref, nk_ref, nv_ref, *, n_kv, t_new):
    del attn_in_ref
    assert LANES // HEAD_DIM == 2 and Q_PER_KV % 2 == 0
    cos_t, sa_t, sb_t = cos_ref[...], sa_ref[...], sb_ref[...]
    q = _rope(q_ref[...], cos_t, sa_t, sb_t) * (HEAD_DIM ** -0.5)
    k = _rope(k_ref[...], cos_t, sa_t, sb_t)
    v = v_ref[...]
    n_cache = ck_ref.shape[1]
    n_keys = 2 * WINDOW
    n_seq = q.shape[0] // t_new
    n_heads = n_kv * Q_PER_KV
    zpad = jnp.zeros((n_keys - n_cache - t_new, k.shape[1]), F32)
    low = lax.broadcasted_iota(I32, (t_new, LANES), 1) < HEAD_DIM
    high = jnp.logical_not(low)

    def swap_halves(x):
        return pltpu.roll(x, HEAD_DIM, 1)

    scores, values = [], []
    for s_i in range(n_seq):
        ck, cv = ck_ref[s_i], cv_ref[s_i]
        kn, vn = k[s_i * t_new:(s_i + 1) * t_new], v[s_i * t_new:(s_i + 1) * t_new]
        nk_ref[s_i] = jnp.concatenate([ck[t_new:], kn], axis=0)
        nv_ref[s_i] = jnp.concatenate([cv[t_new:], vn], axis=0)
        kall = jnp.concatenate([ck, kn, zpad], axis=0).astype(BF16)
        values.append(jnp.concatenate([cv, vn, zpad], axis=0).astype(BF16))
        qs = q[s_i * t_new:(s_i + 1) * t_new]
        for g in range(n_kv):
            tile, odd = g // 2, g % 2
            parts = []
            for r in range(Q_PER_KV):
                h = g * Q_PER_KV + r
                x = qs[:, (h // 2) * LANES:(h // 2 + 1) * LANES]
                if h % 2 != odd:
                    x = swap_halves(x)
                parts.append(jnp.where(high if odd else low, x, 0.0))
            qg = jnp.concatenate(parts, axis=0).astype(BF16)
            k2 = kall[:, tile * LANES:(tile + 1) * LANES]
            scores.append(lax.dot_general(qg, k2, (((1,), (1,)), ((), ())), preferred_element_type=F32))
    s = jnp.concatenate(scores, axis=0)

    rows = n_seq * n_heads * t_new
    qt = lax.broadcasted_iota(I32, (rows, n_keys), 0) & (t_new - 1)
    kj = lax.broadcasted_iota(I32, (rows, n_keys), 1)
    mask = (kj > qt + (n_cache - WINDOW)) & (kj <= qt + n_cache)
    s = jnp.where(mask, s, NEG_INF)
    sink_seq = jnp.concatenate([jnp.full((t_new, 1), sink_ref[h], F32) for h in range(n_heads)], axis=0)
    sink = jnp.concatenate([sink_seq] * n_seq, axis=0)
    probs = _softmax_with_sink(s, sink).astype(BF16)

    seq_outs = []
    for s_i in range(n_seq):
        out_tiles = []
        for g in range(n_kv):
            tile, odd = g // 2, g % 2
            r0 = (s_i * n_kv + g) * Q_PER_KV * t_new
            v2 = values[s_i][:, tile * LANES:(tile + 1) * LANES]
            og = jnp.dot(probs[r0:r0 + Q_PER_KV * t_new], v2, preferred_element_type=F32)
            for pr in range(Q_PER_KV // 2):
                even = og[(2 * pr) * t_new:(2 * pr + 1) * t_new]
                oddh = og[(2 * pr + 1) * t_new:(2 * pr + 2) * t_new]
                if odd:
                    even = swap_halves(even)
                else:
                    oddh = swap_halves(oddh)
                out_tiles.append(jnp.where(low, even, oddh))
        seq_outs.append(jnp.concatenate(out_tiles, axis=1))
    o_ref[...] = jnp.concatenate(seq_outs, axis=0).astype(o_ref.dtype)


def _attn_sample(qkv, attn, row0, cache_k, cache_v, sinks, tables, n_seq, t_new, attn_w, kv_w):
    rows = SEQS_PER_STEP * t_new
    rb0 = row0 // rows
    kb, vb = attn_w // kv_w, attn_w // kv_w + 1
    n_cache = cache_k.shape[1]
    tab_spec = pl.BlockSpec((rows, LANES), lambda i, s: (0, 0))
    cache_spec = pl.BlockSpec((SEQS_PER_STEP, n_cache, kv_w), lambda i, s: (i, 0, 0))
    return pl.pallas_call(
        functools.partial(_attn_sample_kernel, n_kv=kv_w // HEAD_DIM, t_new=t_new),
        grid_spec=pltpu.PrefetchScalarGridSpec(
            num_scalar_prefetch=1,
            grid=(n_seq // SEQS_PER_STEP,),
            in_specs=[pl.BlockSpec((rows, attn_w), lambda i, s: (rb0 + i, 0)),
                      pl.BlockSpec((rows, kv_w), lambda i, s: (rb0 + i, kb)),
                      pl.BlockSpec((rows, kv_w), lambda i, s: (rb0 + i, vb)),
                      cache_spec, cache_spec, tab_spec, tab_spec, tab_spec,
                      pl.BlockSpec(memory_space=pl.ANY)],
            out_specs=[pl.BlockSpec((rows, attn_w), lambda i, s: (rb0 + i, 0)), cache_spec, cache_spec]),
        out_shape=[jax.ShapeDtypeStruct(attn.shape, attn.dtype),
                   jax.ShapeDtypeStruct(cache_k.shape, F32),
                   jax.ShapeDtypeStruct(cache_v.shape, F32)],
        input_output_aliases={9: 0},
        compiler_params=_params(1, 32),
        name="attn_sample",
    )(sinks, qkv, qkv, qkv, cache_k, cache_v, *tables, attn)


def _pool_prompt_kernel(z_ref, halo_ref, w_ref, sc_ref, o_ref, ext_ref, *, tm, pg, nb):
    i = pl.program_id(0)

    @pl.when(i >= nb)
    def _():
        o_ref[...] = jnp.zeros_like(o_ref)

    @pl.when(i < nb)
    def _():
        _pool_prompt_block(z_ref, halo_ref, w_ref, sc_ref, o_ref, ext_ref, tm=tm, pg=pg)


def _pool_prompt_block(z_ref, halo_ref, w_ref, sc_ref, o_ref, ext_ref, *, tm, pg):
    i = pl.program_id(0)
    ext_ref[0:POOL_HALO, :] = jnp.where(i > 0, halo_ref[...], 0.0)
    ext_ref[POOL_HALO:, :] = z_ref[...]
    pos = i * tm + lax.broadcasted_iota(I32, (tm, 1), 0)
    for g, w in enumerate(POOL_WINDOWS):
        c0, c1 = g * pg, (g + 1) * pg
        cur = ext_ref[POOL_HALO:POOL_HALO + tm, c0:c1]
        acc = cur
        for d in range(1, w):
            acc = acc + ext_ref[POOL_HALO - d:POOL_HALO - d + tm, c0:c1]
        cnt = jnp.minimum(pos + 1, w).astype(F32)
        pooled = (acc / cnt - cur).astype(BF16)
        mixed = jnp.dot(pooled, w_ref[g].astype(BF16), preferred_element_type=F32) * sc_ref[:, c0:c1]
        o_ref[:, c0:c1] = mixed.astype(o_ref.dtype)


def _pool_prompt(z, n_rows, w_grp, scale):
    n, pw = z.shape
    tm = TM_ROW
    hb = tm // POOL_HALO
    return pl.pallas_call(
        functools.partial(_pool_prompt_kernel, tm=tm, pg=pw // len(POOL_WINDOWS), nb=n_rows // tm),
        grid=(n // tm,),
        in_specs=[pl.BlockSpec((tm, pw), lambda i: (i, 0)),
                  pl.BlockSpec((POOL_HALO, pw), lambda i: (jnp.maximum(i * hb - 1, 0), 0)),
                  pl.BlockSpec(w_grp.shape, lambda i: (0, 0, 0)),
                  pl.BlockSpec((1, pw), lambda i: (0, 0))],
        out_specs=pl.BlockSpec((tm, pw), lambda i: (i, 0)),
        out_shape=jax.ShapeDtypeStruct((n, pw), BF16),
        scratch_shapes=[pltpu.VMEM((tm + POOL_HALO, pw), F32)],
        compiler_params=_params(1, 40),
        name="pool_prompt",
    )(z, z, w_grp, scale.reshape(1, pw))


def _pool_sample_kernel(st_ref, z_ref, w_ref, sc_ref, mixed_in_ref, o_ref, ns_ref, ext_ref, *, pg, t_new):
    del mixed_in_ref
    n_st = st_ref.shape[1]
    seqs = st_ref.shape[0]
    hist = ext_ref.shape[1] - t_new
    ext_ref[:, hist - n_st:hist, :] = st_ref[...]
    ext_ref[:, hist:, :] = z_ref[...]
    ns_ref[...] = ext_ref[:, hist + t_new - n_st:, :]
    for g, w in enumerate(POOL_WINDOWS):
        c0, c1 = g * pg, (g + 1) * pg
        cur = ext_ref[:, hist:, c0:c1]
        acc = cur
        for d in range(1, w):
            acc = acc + ext_ref[:, hist - d:hist - d + t_new, c0:c1]
        pooled = (acc / float(w) - cur).reshape(seqs * t_new, pg).astype(BF16)
        mixed = jnp.dot(pooled, w_ref[g].astype(BF16), preferred_element_type=F32) * sc_ref[:, c0:c1]
        o_ref[:, c0:c1] = mixed.astype(o_ref.dtype)


def _pool_sample(state, z, mixed, row0, w_grp, scale, t_new):
    n_seq, n_st, pw = state.shape
    assert t_new == SUBLANES and n_st <= POOL_HALO - 1
    z3 = z.reshape(z.shape[0] // t_new, t_new, pw)
    sb0 = row0 // t_new // POOL_SEQS
    rows = POOL_SEQS * t_new
    st_spec = pl.BlockSpec((POOL_SEQS, n_st, pw), lambda i: (i, 0, 0))
    return pl.pallas_call(
        functools.partial(_pool_sample_kernel, pg=pw // len(POOL_WINDOWS), t_new=t_new),
        grid=(n_seq // POOL_SEQS,),
        in_specs=[st_spec,
                  pl.BlockSpec((POOL_SEQS, t_new, pw), lambda i: (sb0 + i, 0, 0)),
                  pl.BlockSpec(w_grp.shape, lambda i: (0, 0, 0)),
                  pl.BlockSpec((1, pw), lambda i: (0, 0)),
                  pl.BlockSpec(memory_space=pl.ANY)],
        out_specs=[pl.BlockSpec((rows, pw), lambda i: (row0 // rows + i, 0)), st_spec],
        out_shape=[jax.ShapeDtypeStruct(mixed.shape, mixed.dtype), jax.ShapeDtypeStruct(state.shape, F32)],
        scratch_shapes=[pltpu.VMEM((POOL_SEQS, POOL_HALO + t_new, pw), F32)],
        input_output_aliases={4: 0},
        compiler_params=_params(1, 40),
        name="pool_sample",
    )(state, z3, w_grp, scale.reshape(1, pw), mixed)


def _merge_kernel(a1_ref, a2_ref, w1_ref, w2_ref, ga_ref, gb_ref, ba_ref, bb_ref, o_ref, w1b_ref, w2b_ref):
    @pl.when(pl.program_id(1) == 0)
    def _():
        w1b_ref[...] = w1_ref[...].astype(BF16)
        w2b_ref[...] = w2_ref[...].astype(BF16)

    a_up = jnp.dot(a1_ref[...], w1b_ref[...], preferred_element_type=F32)
    b_up = jnp.dot(a2_ref[...], w2b_ref[...], preferred_element_type=F32)
    merged = (jax.nn.sigmoid(ga_ref[...] + ba_ref[...]) * a_up
              + jax.nn.sigmoid(gb_ref[...] + bb_ref[...]) * b_up)
    o_ref[...] = merged.astype(o_ref.dtype)


def _merge(attn, mixed, w_attn_up, w_pool_up, gates, b_gate):
    n, ka = attn.shape
    kp = mixed.shape[1]
    d = w_attn_up.shape[1]
    nj = d // TN
    return pl.pallas_call(
        _merge_kernel,
        grid=(nj, n // TM),
        in_specs=[pl.BlockSpec((TM, ka), lambda j, i: (i, 0)),
                  pl.BlockSpec((TM, kp), lambda j, i: (i, 0)),
                  pl.BlockSpec((ka, TN), lambda j, i: (0, j)),
                  pl.BlockSpec((kp, TN), lambda j, i: (0, j)),
                  pl.BlockSpec((TM, TN), lambda j, i: (i, j)),
                  pl.BlockSpec((TM, TN), lambda j, i: (i, nj + j)),
                  pl.BlockSpec((1, TN), lambda j, i: (0, j)),
                  pl.BlockSpec((1, TN), lambda j, i: (0, nj + j))],
        out_specs=pl.BlockSpec((TM, TN), lambda j, i: (i, j)),
        out_shape=jax.ShapeDtypeStruct((n, d), BF16),
        scratch_shapes=[pltpu.VMEM((ka, TN), BF16), pltpu.VMEM((kp, TN), BF16)],
        compiler_params=_params(2, 56),
        name="merge",
    )(attn, mixed, w_attn_up, w_pool_up, gates, gates, b_gate, b_gate)


def _out_proj_kernel(a_ref, w_ref, ha_ref, hb_ref, o_ref, wb_ref, *, nb_first):
    @pl.when(pl.program_id(1) == 0)
    def _():
        wb_ref[...] = w_ref[...].astype(BF16)

    h = jnp.where(pl.program_id(1) < nb_first, ha_ref[...], hb_ref[...])
    o_ref[...] = h + jnp.dot(a_ref[...], wb_ref[...], preferred_element_type=F32)


def _out_proj(merged, w_out, ha, hb):
    n, d = merged.shape
    nb_first = ha.shape[0] // TM
    h_first = pl.BlockSpec((TM, TN), lambda j, i: (jnp.minimum(i, nb_first - 1), j))
    h_last = pl.BlockSpec((TM, TN), lambda j, i: (jnp.maximum(i - nb_first, 0), j))
    return pl.pallas_call(
        functools.partial(_out_proj_kernel, nb_first=nb_first),
        grid=(d // TN, n // TM),
        in_specs=[pl.BlockSpec((TM, d), lambda j, i: (i, 0)),
                  pl.BlockSpec((d, TN), lambda j, i: (0, j)),
                  h_first, h_last],
        out_specs=pl.BlockSpec((TM, TN), lambda j, i: (i, j)),
        out_shape=jax.ShapeDtypeStruct((n, d), F32),
        scratch_shapes=[pltpu.VMEM((d, TN), BF16)],
        compiler_params=_params(2, 56),
        name="out_proj",
    )(merged, w_out, ha, hb)


def _router_kernel(h_ref, g_ref, wr_ref, br_ref, info_ref, cw_ref, xf_ref, cnt_ref, base_ref):
    tm = h_ref.shape[0]

    @pl.when(pl.program_id(0) == 0)
    def _():
        base_ref[...] = jnp.zeros_like(base_ref)

    xf = _rms_rows(h_ref[...], g_ref[...]).astype(BF16)
    xf_ref[...] = xf.reshape(xf_ref.shape)
    logits = jnp.dot(xf, wr_ref[...].astype(BF16), preferred_element_type=F32) + br_ref[...]
    lane = lax.broadcasted_iota(I32, logits.shape, 1)
    big = jnp.int32(1 << 20)
    ninf = jnp.float32(-jnp.inf)

    is_g = lane < N_GROUPS
    gl = jnp.where(is_g, logits, ninf)
    gmax = jnp.max(gl, axis=-1, keepdims=True)
    gden = jnp.sum(jnp.exp(gl - gmax), axis=-1, keepdims=True)
    g_p = 1.0 / gden
    g_idx = jnp.min(jnp.where(gl == gmax, lane, big), axis=-1, keepdims=True)

    e_lane = lane - N_GROUPS
    in_grp = (e_lane >= g_idx * EXPERTS_PER_GROUP) & (e_lane < (g_idx + 1) * EXPERTS_PER_GROUP)
    el = jnp.where(in_grp, logits, ninf)
    m1 = jnp.max(el, axis=-1, keepdims=True)
    i1 = jnp.min(jnp.where(in_grp & (el == m1), e_lane, big), axis=-1, keepdims=True)
    el2 = jnp.where(e_lane == i1, ninf, el)
    m2 = jnp.max(el2, axis=-1, keepdims=True)
    i2 = jnp.min(jnp.where(in_grp & (el2 == m2), e_lane, big), axis=-1, keepdims=True)
    t = jnp.exp(m2 - m1)
    den = 1.0 + t
    c1 = g_p * (1.0 / den)
    c2 = g_p * (t / den)
    cw_ref[...] = jnp.where(lane == 0, c1, jnp.where(lane == 1, c2, 0.0))

    hit1, hit2 = lane == i1, lane == i2
    onehot = (hit1 | hit2).astype(BF16)
    earlier = (lax.broadcasted_iota(I32, (tm, tm), 1) < lax.broadcasted_iota(I32, (tm, tm), 0)).astype(BF16)
    before = jnp.dot(earlier, onehot, preferred_element_type=F32) + base_ref[...]
    r1 = jnp.sum(jnp.where(hit1, before, 0.0), axis=-1, keepdims=True).astype(I32)
    r2 = jnp.sum(jnp.where(hit2, before, 0.0), axis=-1, keepdims=True).astype(I32)
    base_ref[...] += jnp.sum(onehot.astype(F32), axis=0, keepdims=True)
    cnt_ref[...] = base_ref[...].astype(I32)
    info_ref[...] = jnp.where(lane == 0, i1, jnp.where(lane == 1, i2,
                              jnp.where(lane == 2, r1, jnp.where(lane == 3, r2, 0))))


def _router(h, g_ffn, w_rg, b_rg, w_re, b_re):
    n, d = h.shape
    n_log = N_GROUPS + N_EXPERTS
    wr = jnp.concatenate([w_rg, w_re, jnp.zeros((d, LANES - n_log), F32)], axis=1)
    br = jnp.concatenate([b_rg, b_re, jnp.zeros((LANES - n_log,), F32)]).reshape(1, LANES)
    lane_spec = pl.BlockSpec((TM_ROW, LANES), lambda i: (i, 0))
    return pl.pallas_call(
        _router_kernel,
        grid=(n // TM_ROW,),
        in_specs=[pl.BlockSpec((TM_ROW, d), lambda i: (i, 0)),
                  pl.BlockSpec((1, d), lambda i: (0, 0)),
                  pl.BlockSpec((d, LANES), lambda i: (0, 0)),
                  pl.BlockSpec((1, LANES), lambda i: (0, 0))],
        out_specs=[lane_spec, lane_spec,
                   pl.BlockSpec((TM_ROW, d // LANES, LANES), lambda i: (i, 0, 0)),
                   pl.BlockSpec((1, LANES), lambda i: (0, 0))],
        out_shape=[jax.ShapeDtypeStruct((n, LANES), I32), jax.ShapeDtypeStruct((n, LANES), F32),
                   jax.ShapeDtypeStruct((n, d // LANES, LANES), BF16), jax.ShapeDtypeStruct((1, LANES), I32)],
        scratch_shapes=[pltpu.VMEM((1, LANES), F32)],
        compiler_params=_params(1, 40),
        name="router",
    )(h, g_ffn.reshape(1, d), wr, br)


def _dispatch_plan(info, counts, n_tiles):
    counts = counts[0, :N_EXPERTS]
    tiles_per = (counts + TM_E - 1) // TM_E
    tiles_end = jnp.cumsum(tiles_per)
    row_start = (tiles_end - tiles_per) * TM_E
    eid, rank = info[:, :TOP_K], info[:, TOP_K:2 * TOP_K]
    onehot = (eid[:, :, None] == jnp.arange(N_EXPERTS, dtype=I32)[None, None, :]).astype(I32)
    pos = (jnp.sum(onehot * row_start[None, None, :], axis=-1) + rank).reshape(-1).astype(I32)
    n_used = tiles_end[-1:].astype(I32)
    tile_ids = jnp.minimum(jnp.arange(n_tiles, dtype=I32), n_used - 1)
    tile_expert = jnp.sum((tile_ids[:, None] >= tiles_end[None, :]).astype(I32), axis=1).astype(I32)
    last_tile_row = (jnp.clip(tiles_end - 1, 0, n_tiles - 1) * TM_E).astype(I32)
    starts = jnp.concatenate([jnp.ones((1,), I32), (tile_expert[1:] != tile_expert[:-1]).astype(I32)])
    run_idx = jnp.cumsum(starts) - 1
    n_runs = run_idx[-1:] + 1
    run_expert = jnp.zeros((n_tiles,), I32).at[run_idx].set(tile_expert)
    next_expert = run_expert[(run_idx + 1) % n_runs]
    rows_end = row_start + counts
    valid = jnp.clip(rows_end[tile_expert] - jnp.arange(n_tiles, dtype=I32) * TM_E, 0, TM_E)
    plan = (tile_expert, n_used, run_idx.astype(I32), next_expert.astype(I32), n_runs.astype(I32),
            valid.astype(I32))
    return pos, plan, last_tile_row


def _dispatch_kernel(pos_ref, ltr_ref, nu_ref, xf_ref, xs_hbm, zero_ref, sem):
    i = pl.program_id(0)
    n_tiles = xs_hbm.shape[0] // TM_E

    def clear_tile(row):
        return pltpu.make_async_copy(zero_ref, xs_hbm.at[pl.ds(pl.multiple_of(row, TM_E), TM_E)], sem)

    @pl.when(i == 0)
    def _():
        zero_ref[...] = jnp.zeros_like(zero_ref)
        for e in range(N_EXPERTS):
            clear_tile(ltr_ref[e]).start()

        def start_unused(t, carry):
            clear_tile(t * TM_E).start()
            return carry

        def wait_one(t, carry):
            clear_tile(0).wait()
            return carry

        lax.fori_loop(nu_ref[0], n_tiles, start_unused, 0)
        lax.fori_loop(nu_ref[0] - N_EXPERTS, n_tiles, wait_one, 0)

    base = i * TM_DISPATCH

    def issue(r, carry):
        for k in range(TOP_K):
            dst_row = pos_ref[(base + r) * TOP_K + k]
            pltpu.make_async_copy(xf_ref.at[pl.ds(r, 1)], xs_hbm.at[pl.ds(dst_row, 1)], sem).start()
        return carry

    lax.fori_loop(0, TM_DISPATCH, issue, 0, unroll=4)
    for k in range(TOP_K):
        pltpu.make_async_copy(xf_ref, xs_hbm.at[pl.ds(0, TM_DISPATCH)], sem).wait()


def _dispatch(pos, last_tile_row, n_used, xf3, n_tiles):
    n, c, l = xf3.shape
    return pl.pallas_call(
        _dispatch_kernel,
        grid_spec=pltpu.PrefetchScalarGridSpec(
            num_scalar_prefetch=3,
            grid=(n // TM_DISPATCH,),
            in_specs=[pl.BlockSpec((TM_DISPATCH, c, l), lambda i, *_: (i, 0, 0))],
            out_specs=pl.BlockSpec(memory_space=pl.ANY),
            scratch_shapes=[pltpu.VMEM((TM_E, c, l), xf3.dtype), pltpu.SemaphoreType.DMA(())]),
        out_shape=jax.ShapeDtypeStruct((n_tiles * TM_E, c, l), xf3.dtype),
        compiler_params=_params(1, 24),
        name="moe_dispatch",
    )(pos, last_tile_row, n_used, xf3)


def _expert_weights_step(plan_refs, w_hbms, wbuf_ref, wb_refs, sem, tn):
    te_ref, _, run_ref, nxt_ref, nr_ref = plan_refs
    j, t = pl.program_id(0), pl.program_id(1)

    def copies(e, jj):
        col = pl.multiple_of(jj * tn, tn)
        return [pltpu.make_async_copy(w.at[e, :, pl.ds(col, tn)], wbuf_ref.at[m], sem)
                for m, w in enumerate(w_hbms)]

    @pl.when((t == 0) | (te_ref[t] != te_ref[jnp.maximum(t - 1, 0)]))
    def _():
        @pl.when((j == 0) & (t == 0))
        def _():
            for c in copies(te_ref[t], j):
                c.start()

        for c in copies(te_ref[t], j):
            c.wait()
        for m, wb_ref in enumerate(wb_refs):
            wb_ref[...] = wbuf_ref[m].astype(BF16)

        next_j = jnp.where(run_ref[t] + 1 == nr_ref[0], j + 1, j)

        @pl.when(next_j < pl.num_programs(0))
        def _():
            for c in copies(nxt_ref[t], next_j):
                c.start()


def _by_valid_rows(valid, full_fn, zero_fn):
    half = TM_E // 2

    @pl.when(valid > half)
    def _():
        full_fn(TM_E)

    @pl.when((valid > 0) & (valid <= half))
    def _():
        full_fn(half)
        zero_fn(half)

    @pl.when(valid == 0)
    def _():
        zero_fn(0)


def _moe_up_kernel(te_ref, nu_ref, run_ref, nxt_ref, nr_ref, valid_ref, xs_ref, wg_hbm, wu_hbm, o_ref,
                   wbuf_ref, wgb_ref, wub_ref, sem):
    t = pl.program_id(1)
    _expert_weights_step((te_ref, nu_ref, run_ref, nxt_ref, nr_ref), (wg_hbm, wu_hbm), wbuf_ref,
                         (wgb_ref, wub_ref), sem, o_ref.shape[1])

    def compute(rows):
        x = xs_ref[0:rows].reshape(rows, -1)
        gate = jnp.dot(x, wgb_ref[...], preferred_element_type=F32)
        up = jnp.dot(x, wub_ref[...], preferred_element_type=F32)
        o_ref[0:rows] = (jax.nn.silu(gate) * up).astype(o_ref.dtype)

    def clear(row0):
        o_ref[row0:] = jnp.zeros((TM_E - row0,) + o_ref.shape[1:], o_ref.dtype)

    _by_valid_rows(valid_ref[t], compute, clear)


def _moe_up(plan, xs3, w_gate, w_up, n_tiles):
    d, f = w_gate.shape[1], w_gate.shape[2]
    tf = TN
    any_spec = pl.BlockSpec(memory_space=pl.ANY)
    return pl.pallas_call(
        _moe_up_kernel,
        grid_spec=pltpu.PrefetchScalarGridSpec(
            num_scalar_prefetch=len(plan),
            grid=(f // tf, n_tiles),
            in_specs=[pl.BlockSpec((TM_E,) + xs3.shape[1:], lambda j, t, te, nu, *_: (jnp.minimum(t, nu[0] - 1), 0, 0)),
                      any_spec, any_spec],
            out_specs=pl.BlockSpec((TM_E, tf), lambda j, t, *_: (t, j)),
            scratch_shapes=[pltpu.VMEM((2, d, tf), F32), pltpu.VMEM((d, tf), BF16), pltpu.VMEM((d, tf), BF16),
                            pltpu.SemaphoreType.DMA(())]),
        out_shape=jax.ShapeDtypeStruct((n_tiles * TM_E, f), BF16),
        compiler_params=_params(2, 56),
        name="moe_up",
    )(*plan, xs3, w_gate, w_up)


def _moe_down_kernel(te_ref, nu_ref, run_ref, nxt_ref, nr_ref, valid_ref, hid_ref, wd_hbm, o_ref,
                     wbuf_ref, wdb_ref, sem):
    t = pl.program_id(1)
    _expert_weights_step((te_ref, nu_ref, run_ref, nxt_ref, nr_ref), (wd_hbm,), wbuf_ref, (wdb_ref,), sem,
                         wdb_ref.shape[1])

    def compute(rows):
        y = jnp.dot(hid_ref[0:rows], wdb_ref[...], preferred_element_type=F32)
        o_ref[0:rows] = y.reshape((rows,) + o_ref.shape[1:])

    def clear(row0):
        o_ref[row0:] = jnp.zeros((TM_E - row0,) + o_ref.shape[1:], o_ref.dtype)

    _by_valid_rows(valid_ref[t], compute, clear)


def _moe_down(plan, hid, w_down, n_tiles):
    f, d = w_down.shape[1], w_down.shape[2]
    tn = 4 * TN
    return pl.pallas_call(
        _moe_down_kernel,
        grid_spec=pltpu.PrefetchScalarGridSpec(
            num_scalar_prefetch=len(plan),
            grid=(d // tn, n_tiles),
            in_specs=[pl.BlockSpec((TM_E, f), lambda j, t, *_: (t, 0)),
                      pl.BlockSpec(memory_space=pl.ANY)],
            out_specs=pl.BlockSpec((TM_E, tn // LANES, LANES), lambda j, t, *_: (t, j, 0)),
            scratch_shapes=[pltpu.VMEM((1, f, tn), F32), pltpu.VMEM((f, tn), BF16),
                            pltpu.SemaphoreType.DMA(())]),
        out_shape=jax.ShapeDtypeStruct((n_tiles * TM_E, d // LANES, LANES), F32),
        compiler_params=_params(2, 40),
        name="moe_down",
    )(*plan, hid, w_down)


def _combine_kernel(pos_ref, h_ref, cw_ref, y_hbm, g_ref, ho_ref, xo_ref, buf_ref, sem):
    i = pl.program_id(0)
    tm, d = h_ref.shape

    def row_copy(slot, k, r, src_row):
        return pltpu.make_async_copy(y_hbm.at[pl.ds(src_row, 1)], buf_ref.at[slot, k, pl.ds(r, 1)], sem.at[slot])

    def issue_step(step, slot):
        def body(r, carry):
            for k in range(TOP_K):
                row_copy(slot, k, r, pos_ref[(step * tm + r) * TOP_K + k]).start()
            return carry
        lax.fori_loop(0, tm, body, 0, unroll=4)

    @pl.when(i == 0)
    def _():
        issue_step(0, 0)

    @pl.when(i + 1 < pl.num_programs(0))
    def _():
        issue_step(i + 1, (i + 1) % 2)

    slot = i % 2
    for k in range(TOP_K):
        pltpu.make_async_copy(y_hbm.at[pl.ds(0, tm)], buf_ref.at[slot, k], sem.at[slot]).wait()
    cw = cw_ref[...]
    y1 = buf_ref[slot, 0].reshape(tm, d)
    y2 = buf_ref[slot, 1].reshape(tm, d)
    h2 = h_ref[...] + (cw[:, 0:1] * y1 + cw[:, 1:2] * y2)
    ho_ref[...] = h2
    xo_ref[...] = _rms_rows(h2, g_ref[...]).astype(xo_ref.dtype)


def _combine(pos, h, cw, y3, g_ple):
    n, d = h.shape
    row_spec = pl.BlockSpec((TM_ROW, d), lambda i, p: (i, 0))
    return pl.pallas_call(
        _combine_kernel,
        grid_spec=pltpu.PrefetchScalarGridSpec(
            num_scalar_prefetch=1,
            grid=(n // TM_ROW,),
            in_specs=[row_spec,
                      pl.BlockSpec((TM_ROW, LANES), lambda i, p: (i, 0)),
                      pl.BlockSpec(memory_space=pl.ANY),
                      pl.BlockSpec((1, d), lambda i, p: (0, 0))],
            out_specs=[row_spec, row_spec],
            scratch_shapes=[pltpu.VMEM((2, TOP_K, TM_ROW) + y3.shape[1:], F32), pltpu.SemaphoreType.DMA((2,))]),
        out_shape=[jax.ShapeDtypeStruct((n, d), F32), jax.ShapeDtypeStruct((n, d), BF16)],
        compiler_params=_params(1, 56),
        name="moe_combine",
    )(pos, h, cw, y3, g_ple.reshape(1, d))


def _ple_kernel(xn_ref, wg_ref, bg_ref, pa_ref, pb_ref, wp_ref, h_ref, o_ref, wgb_ref, *, nb_first):
    @pl.when(pl.program_id(1) == 0)
    def _():
        wgb_ref[...] = wg_ref[...].astype(BF16)

    gate = jax.nn.sigmoid(jnp.dot(xn_ref[...], wgb_ref[...], preferred_element_type=F32) + bg_ref[...])
    p = jnp.where(pl.program_id(1) < nb_first, pa_ref[...], pb_ref[...]).astype(BF16)
    ple = jnp.dot(p, wp_ref[...].astype(BF16), preferred_element_type=F32)
    o_ref[...] = h_ref[...] + gate * ple


def _ple(xn, w_gate, b_gate, pa, pb, w_proj, h):
    n, d = h.shape
    pd = pa.shape[1]
    nb_first = pa.shape[0] // TM
    p_first = pl.BlockSpec((TM, pd), lambda j, i: (jnp.minimum(i, nb_first - 1), 0))
    p_last = pl.BlockSpec((TM, pd), lambda j, i: (jnp.maximum(i - nb_first, 0), 0))
    return pl.pallas_call(
        functools.partial(_ple_kernel, nb_first=nb_first),
        grid=(d // TN, n // TM),
        in_specs=[pl.BlockSpec((TM, d), lambda j, i: (i, 0)),
                  pl.BlockSpec((d, TN), lambda j, i: (0, j)),
                  pl.BlockSpec((1, TN), lambda j, i: (0, j)),
                  p_first, p_last,
                  pl.BlockSpec((pd, TN), lambda j, i: (0, j)),
                  pl.BlockSpec((TM, TN), lambda j, i: (i, j))],
        out_specs=pl.BlockSpec((TM, TN), lambda j, i: (i, j)),
        out_shape=jax.ShapeDtypeStruct((n, d), F32),
        scratch_shapes=[pltpu.VMEM((d, TN), BF16)],
        compiler_params=_params(2, 56),
        name="ple",
    )(xn, w_gate, b_gate.reshape(1, d), pa, pb, w_proj, h)


def _layer(xn, h_p, h_s, p_p, p_s, cache_k, cache_v, state_pool, lw):
    n, d = xn.shape
    n_prompt = h_p.shape[0]
    n_seq, n_cache, n_kv, _ = cache_k.shape
    t_new = (n - n_prompt) // n_seq
    kv_w = n_kv * HEAD_DIM
    attn_w = lw['w_attn_up'].shape[0]
    pool_w = lw['w_pool_up'].shape[0]
    n_pool = state_pool.shape[1]

    qkv = _in_proj(xn, lw['w_in'], 0, attn_w + 2 * kv_w, "in_proj_qkv")
    z = _in_proj(xn, lw['w_in'], attn_w + 2 * kv_w, pool_w, "in_proj_pool")
    gates = _in_proj(xn, lw['w_in'], attn_w + 2 * kv_w + pool_w, 2 * d, "in_proj_gates")

    tab_p = _rope_tables(jnp.arange(n_prompt, dtype=F32))
    tab_s = tuple(jnp.tile(a, (SEQS_PER_STEP, 1))
                  for a in _rope_tables(PAST_LEN + jnp.arange(t_new, dtype=F32)))
    tab_p_t = _rope_tables_t(jnp.arange(n_prompt, dtype=F32))
    attn, k_rot_p = _attn_prompt(qkv, lw['attn_sinks'], tab_p, tab_p_t, n_prompt, attn_w, kv_w)
    attn, new_k_s, new_v_s = _attn_sample(
        qkv, attn, n_prompt, cache_k.reshape(n_seq, n_cache, kv_w), cache_v.reshape(n_seq, n_cache, kv_w),
        lw['attn_sinks'], tab_s, n_seq, t_new, attn_w, kv_w)

    mixed = _pool_prompt(z, n_prompt, lw['w_pool_grp'], lw['pool_scale'])
    mixed, new_z_s = _pool_sample(state_pool, z, mixed, n_prompt, lw['w_pool_grp'], lw['pool_scale'], t_new)

    merged = _merge(attn, mixed, lw['w_attn_up'], lw['w_pool_up'], gates, lw['b_gate'].reshape(1, 2 * d))
    h1 = _out_proj(merged, lw['w_out'], h_p, h_s)

    n_tiles = (n * TOP_K) // TM_E + N_EXPERTS
    info, cw, xf3, counts = _router(h1, lw['g_ffn'], lw['w_route_group'], lw['b_route_group'],
                                    lw['w_route_expert'], lw['b_route_expert'])
    pos, plan, last_tile_row = _dispatch_plan(info, counts, n_tiles)
    xs3 = _dispatch(pos, last_tile_row, plan[1], xf3, n_tiles)
    hid = _moe_up(plan, xs3, lw['w_exp_gate'], lw['w_exp_up'], n_tiles)
    y3 = _moe_down(plan, hid, lw['w_exp_down'], n_tiles)
    h2, xn2 = _combine(pos, h1, cw, y3, lw['g_ple'])

    h3 = _ple(xn2, lw['w_ple_gate'], lw['b_ple_gate'], p_p, p_s, lw['w_ple_proj'], h2)

    new_k_p = k_rot_p[n_prompt - n_cache:].reshape(1, n_cache, n_kv, HEAD_DIM)
    new_v_p = qkv[n_prompt - n_cache:n_prompt, attn_w + kv_w:].reshape(1, n_cache, n_kv, HEAD_DIM)
    new_z_p = z[n_prompt - n_pool:n_prompt].reshape(1, n_pool, pool_w)
    return (h3, new_k_p, new_v_p, new_z_p,
            new_k_s.reshape(cache_k.shape), new_v_s.reshape(cache_v.shape), new_z_s)


def kernel(x_prompt, x_sample, cache_k, cache_v, state_pool, p_prompt, p_sample, g_mix, w_in, b_gate, attn_sinks, w_pool_grp, pool_scale, w_attn_up, w_pool_up, w_out, g_ffn, w_route_group, b_route_group, w_route_expert, b_route_expert, w_exp_gate, w_exp_up, w_exp_down, g_ple, w_ple_gate, b_ple_gate, w_ple_proj, g_final):
    batch, seq, d = x_prompt.shape
    n_seq, t_new, _ = x_sample.shape
    depth = w_in.shape[0]
    assert batch == 1, "prompt rows are treated as one sequence"
    n_prompt = batch * seq
    n_sample = n_seq * t_new
    n = n_prompt + n_sample
    assert n_prompt % TM == 0 and n_sample % TM == 0 and (n * TOP_K) % TM_E == 0 and n % TM_DISPATCH == 0
    assert n_seq % SEQS_PER_STEP == 0 and n_seq % POOL_SEQS == 0 and t_new & (t_new - 1) == 0
    assert cache_k.shape[2] == WINDOW and state_pool.shape[2] == POOL_HALO - 1
    assert depth == 1, "one layer per step"

    weights = dict(g_mix=g_mix, w_in=w_in, b_gate=b_gate, attn_sinks=attn_sinks, w_pool_grp=w_pool_grp,
                   pool_scale=pool_scale, w_attn_up=w_attn_up, w_pool_up=w_pool_up, w_out=w_out, g_ffn=g_ffn,
                   w_route_group=w_route_group, b_route_group=b_route_group, w_route_expert=w_route_expert,
                   b_route_expert=b_route_expert, w_exp_gate=w_exp_gate, w_exp_up=w_exp_up,
                   w_exp_down=w_exp_down, g_ple=g_ple, w_ple_gate=w_ple_gate, b_ple_gate=b_ple_gate,
                   w_ple_proj=w_ple_proj)
    lw = {name: w[0] for name, w in weights.items()}
    h_p = x_prompt.reshape(n_prompt, d)
    h_s = x_sample.reshape(n_sample, d)
    xn = _rmsnorm_in2(h_p, h_s, lw['g_mix'], BF16, "norm_mix")
    out = _layer(xn, h_p, h_s, p_prompt[0].reshape(n_prompt, -1), p_sample[0].reshape(n_sample, -1),
                 cache_k[0], cache_v[0], state_pool[0], lw)
    y_p, y_s = _rmsnorm_out2(out[0], g_final, n_prompt, "norm_final")
    return (y_p.reshape(batch, seq, d), y_s.reshape(n_seq, t_new, d)) + tuple(piece[None] for piece in out[1:])
```

```python
import functools

import jax
import jax.numpy as jnp
from jax import lax
from jax.experimental import pallas as pl
from jax.experimental.pallas import tpu as pltpu

F32 = jnp.float32
BF16 = jnp.bfloat16
I32 = jnp.int32

HEAD_DIM = 64
Q_PER_KV = 4
WINDOW = 128
PAST_LEN = 8192
ROT_DIM = HEAD_DIM // 4
ROPE_THETA = 500000.0
POOL_WINDOWS = (2, 4, 8, 16)
POOL_HALO = 16
N_GROUPS = 4
EXPERTS_PER_GROUP = 4
N_EXPERTS = N_GROUPS * EXPERTS_PER_GROUP
TOP_K = 2
RMS_EPS = 1e-6
NEG_INF = -1e30
LANES = 128
SUBLANES = 8
MIB = 1024 * 1024

TM = 1024
TN = 512
TN_PROJ = 1024
CAST_ROWS = 512
TM_WIDE = 512
TN_WIDE = 1024
TM_ROW = 256
TM_E = 512
TM_DISPATCH = 512
SEQS_PER_STEP = 8
POOL_SEQS = 16


def _params(n_axes, vmem_mib):
    return pltpu.CompilerParams(dimension_semantics=("arbitrary",) * n_axes,
                                vmem_limit_bytes=vmem_mib * MIB)


def _rms_rows(x, g):
    ms = jnp.mean(x * x, axis=-1, keepdims=True)
    return x * lax.rsqrt(ms + RMS_EPS) * g


def _first_last_specs(nb_first, block):
    zeros = (0,) * (len(block) - 1)
    first = pl.BlockSpec(block, lambda i, *_: (jnp.minimum(i, nb_first - 1),) + zeros)
    last = pl.BlockSpec(block, lambda i, *_: (jnp.maximum(i - nb_first, 0),) + zeros)
    return first, last


def _norm_in2_kernel(xa_ref, xb_ref, g_ref, o_ref, *, nb_first):
    i = pl.program_id(0)

    @pl.when(i < nb_first)
    def _():
        o_ref[...] = _rms_rows(xa_ref[...], g_ref[...]).astype(o_ref.dtype)

    @pl.when(i >= nb_first)
    def _():
        o_ref[...] = _rms_rows(xb_ref[...], g_ref[...]).astype(o_ref.dtype)


def _rmsnorm_in2(xa, xb, g, out_dtype, name):
    (na, d), nb = xa.shape, xb.shape[0]
    nb_first = na // TM_ROW
    spec_a, spec_b = _first_last_specs(nb_first, (TM_ROW, d))
    return pl.pallas_call(
        functools.partial(_norm_in2_kernel, nb_first=nb_first),
        grid=((na + nb) // TM_ROW,),
        in_specs=[spec_a, spec_b, pl.BlockSpec((1, d), lambda i: (0, 0))],
        out_specs=pl.BlockSpec((TM_ROW, d), lambda i: (i, 0)),
        out_shape=jax.ShapeDtypeStruct((na + nb, d), out_dtype),
        compiler_params=_params(1, 40),
        name=name,
    )(xa, xb, g.reshape(1, d))


def _norm_out2_kernel(x_ref, g_ref, oa_ref, ob_ref, *, nb_first):
    i = pl.program_id(0)

    @pl.when(i < nb_first)
    def _():
        oa_ref[...] = _rms_rows(x_ref[...], g_ref[...])

    @pl.when(i >= nb_first)
    def _():
        ob_ref[...] = _rms_rows(x_ref[...], g_ref[...])


def _rmsnorm_out2(x, g, na, name):
    n, d = x.shape
    nb_first = na // TM_ROW
    spec_a, spec_b = _first_last_specs(nb_first, (TM_ROW, d))
    return pl.pallas_call(
        functools.partial(_norm_out2_kernel, nb_first=nb_first),
        grid=(n // TM_ROW,),
        in_specs=[pl.BlockSpec((TM_ROW, d), lambda i: (i, 0)), pl.BlockSpec((1, d), lambda i: (0, 0))],
        out_specs=[spec_a, spec_b],
        out_shape=[jax.ShapeDtypeStruct((na, d), F32), jax.ShapeDtypeStruct((n - na, d), F32)],
        compiler_params=_params(1, 40),
        name=name,
    )(x, g.reshape(1, d))


def _cast_rows(src_ref, dst_ref):
    rows = src_ref.shape[0]
    step = min(rows, CAST_ROWS)
    for r in range(0, rows, step):
        dst_ref[r:r + step] = src_ref[r:r + step].astype(dst_ref.dtype)


def _column_weights_step(w_hbms, col0s, wbuf_refs, wb_refs, sem):
    j, i = pl.program_id(0), pl.program_id(1)

    def copies(jj):
        out = []
        for w, col0, buf in zip(w_hbms, col0s, wbuf_refs):
            tn = buf.shape[1]
            out.append(pltpu.make_async_copy(w.at[:, pl.ds(pl.multiple_of(col0 + jj * tn, tn), tn)], buf, sem))
        return out

    @pl.when(i == 0)
    def _():
        @pl.when(j == 0)
        def _():
            for c in copies(j):
                c.start()

        for c in copies(j):
            c.wait()
        for buf, wb_ref in zip(wbuf_refs, wb_refs):
            _cast_rows(buf, wb_ref)

        @pl.when(j + 1 < pl.num_programs(0))
        def _():
            for c in copies(j + 1):
                c.start()


def _weight_scratch(shapes):
    return ([pltpu.VMEM(s, F32) for s in shapes] + [pltpu.VMEM(s, BF16) for s in shapes]
            + [pltpu.SemaphoreType.DMA(())])


def _proj_kernel(a_ref, w_hbm, o_ref, wbuf_ref, wb_ref, sem, *, col0):
    _column_weights_step((w_hbm,), (col0,), (wbuf_ref,), (wb_ref,), sem)
    o_ref[...] = jnp.dot(a_ref[...], wb_ref[...], preferred_element_type=F32)


def _in_proj(xn, w, col0, width, name):
    n, d = xn.shape
    tn = TN_PROJ
    assert col0 % tn == 0 and width % tn == 0
    return pl.pallas_call(
        functools.partial(_proj_kernel, col0=col0),
        grid=(width // tn, n // TM),
        in_specs=[pl.BlockSpec((TM, d), lambda j, i: (i, 0)),
                  pl.BlockSpec(memory_space=pl.ANY)],
        out_specs=pl.BlockSpec((TM, tn), lambda j, i: (i, j)),
        out_shape=jax.ShapeDtypeStruct((n, width), F32),
        scratch_shapes=_weight_scratch([(d, tn)]),
        compiler_params=_params(2, 56),
        name=name,
    )(xn, w)


def _rope_tables(pos):
    half = ROT_DIM // 2
    inv_freq = ROPE_THETA ** (-jnp.arange(half, dtype=F32) * (2.0 / ROT_DIM))
    ang = pos[:, None] * inv_freq[None, :]
    cos, sin = jnp.cos(ang), jnp.sin(ang)
    t = pos.shape[0]
    pad = jnp.zeros((t, HEAD_DIM - ROT_DIM), F32)
    zeros = jnp.zeros((t, half), F32)
    cos_h = jnp.concatenate([cos, cos, pad + 1.0], axis=1)
    sa_h = jnp.concatenate([zeros, sin, pad], axis=1)
    sb_h = jnp.concatenate([-sin, zeros, pad], axis=1)
    reps = LANES // HEAD_DIM
    return tuple(jnp.tile(a, (1, reps)) for a in (cos_h, sa_h, sb_h))


def _rope_tables_t(pos):
    half = ROT_DIM // 2
    inv_freq = ROPE_THETA ** (-jnp.arange(half, dtype=F32) * (2.0 / ROT_DIM))
    ang = pos[:, None] * inv_freq[None, :]
    return jnp.cos(ang).T, jnp.sin(ang).T


def _rope(x, cos_t, sa_t, sb_t):
    pieces = []
    for c in range(x.shape[1] // LANES):
        xc = x[:, c * LANES:(c + 1) * LANES]
        pieces.append(xc * cos_t
                      + pltpu.roll(xc, ROT_DIM // 2, 1) * sa_t
                      + pltpu.roll(xc, LANES - ROT_DIM // 2, 1) * sb_t)
    return jnp.concatenate(pieces, axis=1)


def _softmax_with_sink(s, sink):
    m = jnp.maximum(jnp.max(s, axis=-1, keepdims=True), sink)
    p = jnp.exp(s - m)
    den = jnp.sum(p, axis=-1, keepdims=True) + jnp.exp(sink - m)
    return p / den


def _attn_prompt_kernel(sink_ref, q_ref, k_ref, v_ref, cos_ref, sa_ref, sb_ref, cost_ref, sint_ref,
                        o_ref, ko_ref, kprev_ref, vprevt_ref, *, n_kv, nb):
    b = pl.program_id(0)

    @pl.when(b == 0)
    def _():
        kprev_ref[...] = jnp.zeros_like(kprev_ref)
        vprevt_ref[...] = jnp.zeros_like(vprevt_ref)

    @pl.when(b >= nb)
    def _():
        o_ref[...] = jnp.zeros_like(o_ref)

    @pl.when(b < nb)
    def _():
        _attn_prompt_block(sink_ref, q_ref, k_ref, v_ref, cos_ref, sa_ref, sb_ref, cost_ref, sint_ref,
                           o_ref, ko_ref, kprev_ref, vprevt_ref, n_kv=n_kv)


def _attn_prompt_block(sink_ref, q_ref, k_ref, v_ref, cos_ref, sa_ref, sb_ref, cost_ref, sint_ref,
                       o_ref, ko_ref, kprev_ref, vprevt_ref, *, n_kv):
    b = pl.program_id(0)
    half = ROT_DIM // 2
    k = _rope(k_ref[...], cos_ref[...], sa_ref[...], sb_ref[...])
    ko_ref[...] = k
    kcat = jnp.concatenate([kprev_ref[...], k], axis=0).astype(BF16)
    vt = v_ref[...].T
    vcat_t = jnp.concatenate([vprevt_ref[...], vt], axis=1).astype(BF16)
    kprev_ref[...] = k
    vprevt_ref[...] = vt

    qt = q_ref[...].T
    cos_t, sin_t = cost_ref[...], sint_ref[...]

    def roped_head_t(h):
        x = qt[h * HEAD_DIM:(h + 1) * HEAD_DIM]
        x1, x2 = x[0:half], x[half:ROT_DIM]
        rot = jnp.concatenate([x1 * cos_t - x2 * sin_t, x2 * cos_t + x1 * sin_t, x[ROT_DIM:]], axis=0)
        return (rot * (HEAD_DIM ** -0.5)).astype(BF16)

    n_heads = n_kv * Q_PER_KV
    lanes = n_heads * WINDOW
    zeros_head = jnp.zeros((HEAD_DIM, WINDOW), BF16)
    heads_per_tile = LANES // HEAD_DIM

    scores = []
    for g in range(n_kv):
        tile, slot = g // heads_per_tile, g % heads_per_tile
        k2 = kcat[:, tile * LANES:(tile + 1) * LANES]
        cols = []
        for r in range(Q_PER_KV):
            parts = [zeros_head] * heads_per_tile
            parts[slot] = roped_head_t(g * Q_PER_KV + r)
            cols.append(jnp.concatenate(parts, axis=0))
        rhs = jnp.concatenate(cols, axis=1)
        scores.append(jnp.dot(k2, rhs, preferred_element_type=F32))
    st = jnp.concatenate(scores, axis=1)

    key = lax.broadcasted_iota(I32, (2 * WINDOW, lanes), 0)
    qry = lax.broadcasted_iota(I32, (2 * WINDOW, lanes), 1) & (WINDOW - 1)
    first_key = jnp.where(b > 0, 0, WINDOW)
    mask = (key > qry) & (key <= qry + WINDOW) & (key >= first_key)
    st = jnp.where(mask, st, NEG_INF)
    sink = jnp.concatenate([jnp.full((1, WINDOW), sink_ref[h], F32) for h in range(n_heads)], axis=1)
    m = jnp.maximum(jnp.max(st, axis=0, keepdims=True), sink)
    p = jnp.exp(st - m)
    den = jnp.sum(p, axis=0, keepdims=True) + jnp.exp(sink - m)
    probs_t = (p * (1.0 / den)).astype(BF16)

    for g in range(n_kv):
        ot = jnp.dot(vcat_t[g * HEAD_DIM:(g + 1) * HEAD_DIM],
                     probs_t[:, g * Q_PER_KV * WINDOW:(g + 1) * Q_PER_KV * WINDOW],
                     preferred_element_type=F32)
        for pr in range(Q_PER_KV // heads_per_tile):
            pair = jnp.concatenate(
                [ot[:, (pr * heads_per_tile + u) * WINDOW:(pr * heads_per_tile + u + 1) * WINDOW]
                 for u in range(heads_per_tile)], axis=0)
            c = (g * Q_PER_KV) // heads_per_tile + pr
            o_ref[:, c * LANES:(c + 1) * LANES] = pair.T.astype(o_ref.dtype)


def _attn_prompt(qkv, sinks, tables, tables_t, n_rows, attn_w, kv_w):
    nb = n_rows // WINDOW
    kb, vb = attn_w // kv_w, attn_w // kv_w + 1

    def blk(b, s):
        return jnp.minimum(b, nb - 1)

    tab_spec = pl.BlockSpec((WINDOW, LANES), lambda b, s: (blk(b, s), 0))
    tab_t_spec = pl.BlockSpec((ROT_DIM // 2, WINDOW), lambda b, s: (0, blk(b, s)))
    return pl.pallas_call(
        functools.partial(_attn_prompt_kernel, n_kv=kv_w // HEAD_DIM, nb=nb),
        grid_spec=pltpu.PrefetchScalarGridSpec(
            num_scalar_prefetch=1,
            grid=(qkv.shape[0] // WINDOW,),
            in_specs=[pl.BlockSpec((WINDOW, attn_w), lambda b, s: (blk(b, s), 0)),
                      pl.BlockSpec((WINDOW, kv_w), lambda b, s: (blk(b, s), kb)),
                      pl.BlockSpec((WINDOW, kv_w), lambda b, s: (blk(b, s), vb)),
                      tab_spec, tab_spec, tab_spec, tab_t_spec, tab_t_spec],
            out_specs=[pl.BlockSpec((WINDOW, attn_w), lambda b, s: (b, 0)),
                       pl.BlockSpec((WINDOW, kv_w), lambda b, s: (blk(b, s), 0))],
            scratch_shapes=[pltpu.VMEM((WINDOW, kv_w), F32), pltpu.VMEM((kv_w, WINDOW), F32)]),
        out_shape=[jax.ShapeDtypeStruct((qkv.shape[0], attn_w), BF16),
                   jax.ShapeDtypeStruct((n_rows, kv_w), F32)],
        compiler_params=_params(1, 32),
        name="attn_prompt",
    )(sinks, qkv, qkv, qkv, *tables, *tables_t)


def _attn_sample_kernel(sink_ref, q_ref, k_ref, v_ref, ck_ref, cv_ref, cos_ref, sa_ref, sb_ref, attn_in_ref,
                        o_ref, nk_ref, nv_ref, *, n_kv, t_new):
    del attn_in_ref
    assert LANES // HEAD_DIM == 2 and Q_PER_KV % 2 == 0
    cos_t, sa_t, sb_t = cos_ref[...], sa_ref[...], sb_ref[...]
    q = _rope(q_ref[...], cos_t, sa_t, sb_t) * (HEAD_DIM ** -0.5)
    k = _rope(k_ref[...], cos_t, sa_t, sb_t)
    v = v_ref[...]
    n_cache = ck_ref.shape[1]
    n_keys = 2 * WINDOW
    n_seq = q.shape[0] // t_new
    n_heads = n_kv * Q_PER_KV
    zpad = jnp.zeros((n_keys - n_cache - t_new, k.shape[1]), F32)
    low = lax.broadcasted_iota(I32, (t_new, LANES), 1) < HEAD_DIM
    high = jnp.logical_not(low)

    def swap_halves(x):
        return pltpu.roll(x, HEAD_DIM, 1)

    scores, values = [], []
    for s_i in range(n_seq):
        ck, cv = ck_ref[s_i], cv_ref[s_i]
        kn, vn = k[s_i * t_new:(s_i + 1) * t_new], v[s_i * t_new:(s_i + 1) * t_new]
        nk_ref[s_i] = jnp.concatenate([ck[t_new:], kn], axis=0)
        nv_ref[s_i] = jnp.concatenate([cv[t_new:], vn], axis=0)
        kall = jnp.concatenate([ck, kn, zpad], axis=0).astype(BF16)
        values.append(jnp.concatenate([cv, vn, zpad], axis=0).astype(BF16))
        qs = q[s_i * t_new:(s_i + 1) * t_new]
        for g in range(n_kv):
            tile, odd = g // 2, g % 2
            parts = []
            for r in range(Q_PER_KV):
                h = g * Q_PER_KV + r
                x = qs[:, (h // 2) * LANES:(h // 2 + 1) * LANES]
                if h % 2 != odd:
                    x = swap_halves(x)
                parts.append(jnp.where(high if odd else low, x, 0.0))
            qg = jnp.concatenate(parts, axis=0).astype(BF16)
            k2 = kall[:, tile * LANES:(tile + 1) * LANES]
            scores.append(lax.dot_general(qg, k2, (((1,), (1,)), ((), ())), preferred_element_type=F32))
    s = jnp.concatenate(scores, axis=0)

    rows = n_seq * n_heads * t_new
    qt = lax.broadcasted_iota(I32, (rows, n_keys), 0) & (t_new - 1)
    kj = lax.broadcasted_iota(I32, (rows, n_keys), 1)
    mask = (kj > qt + (n_cache - WINDOW)) & (kj <= qt + n_cache)
    s = jnp.where(mask, s, NEG_INF)
    sink_seq = jnp.concatenate([jnp.full((t_new, 1), sink_ref[h], F32) for h in range(n_heads)], axis=0)
    sink = jnp.concatenate([sink_seq] * n_seq, axis=0)
    probs = _softmax_with_sink(s, sink).astype(BF16)

    seq_outs = []
    for s_i in range(n_seq):
        out_tiles = []
        for g in range(n_kv):
            tile, odd = g // 2, g % 2
            r0 = (s_i * n_kv + g) * Q_PER_KV * t_new
            v2 = values[s_i][:, tile * LANES:(tile + 1) * LANES]
            og = jnp.dot(probs[r0:r0 + Q_PER_KV * t_new], v2, preferred_element_type=F32)
            for pr in range(Q_PER_KV // 2):
                even = og[(2 * pr) * t_new:(2 * pr + 1) * t_new]
                oddh = og[(2 * pr + 1) * t_new:(2 * pr + 2) * t_new]
                if odd:
                    even = swap_halves(even)
                else:
                    oddh = swap_halves(oddh)
                out_tiles.append(jnp.where(low, even, oddh))
        seq_outs.append(jnp.concatenate(out_tiles, axis=1))
    o_ref[...] = jnp.concatenate(seq_outs, axis=0).astype(o_ref.dtype)


def _attn_sample(qkv, attn, row0, cache_k, cache_v, sinks, tables, n_seq, t_new, attn_w, kv_w):
    rows = SEQS_PER_STEP * t_new
    rb0 = row0 // rows
    kb, vb = attn_w // kv_w, attn_w // kv_w + 1
    n_cache = cache_k.shape[1]
    tab_spec = pl.BlockSpec((rows, LANES), lambda i, s: (0, 0))
    cache_spec = pl.BlockSpec((SEQS_PER_STEP, n_cache, kv_w), lambda i, s: (i, 0, 0))
    return pl.pallas_call(
        functools.partial(_attn_sample_kernel, n_kv=kv_w // HEAD_DIM, t_new=t_new),
        grid_spec=pltpu.PrefetchScalarGridSpec(
            num_scalar_prefetch=1,
            grid=(n_seq // SEQS_PER_STEP,),
            in_specs=[pl.BlockSpec((rows, attn_w), lambda i, s: (rb0 + i, 0)),
                      pl.BlockSpec((rows, kv_w), lambda i, s: (rb0 + i, kb)),
                      pl.BlockSpec((rows, kv_w), lambda i, s: (rb0 + i, vb)),
                      cache_spec, cache_spec, tab_spec, tab_spec, tab_spec,
                      pl.BlockSpec(memory_space=pl.ANY)],
            out_specs=[pl.BlockSpec((rows, attn_w), lambda i, s: (rb0 + i, 0)), cache_spec, cache_spec]),
        out_shape=[jax.ShapeDtypeStruct(attn.shape, attn.dtype),
                   jax.ShapeDtypeStruct(cache_k.shape, F32),
                   jax.ShapeDtypeStruct(cache_v.shape, F32)],
        input_output_aliases={9: 0},
        compiler_params=_params(1, 32),
        name="attn_sample",
    )(sinks, qkv, qkv, qkv, cache_k, cache_v, *tables, attn)


def _pool_prompt_kernel(z_ref, halo_ref, w_ref, sc_ref, o_ref, ext_ref, *, tm, pg, nb):
    i = pl.program_id(0)

    @pl.when(i >= nb)
    def _():
        o_ref[...] = jnp.zeros_like(o_ref)

    @pl.when(i < nb)
    def _():
        _pool_prompt_block(z_ref, halo_ref, w_ref, sc_ref, o_ref, ext_ref, tm=tm, pg=pg)


def _pool_prompt_block(z_ref, halo_ref, w_ref, sc_ref, o_ref, ext_ref, *, tm, pg):
    i = pl.program_id(0)
    ext_ref[0:POOL_HALO, :] = jnp.where(i > 0, halo_ref[...], 0.0)
    ext_ref[POOL_HALO:, :] = z_ref[...]
    pos = i * tm + lax.broadcasted_iota(I32, (tm, 1), 0)
    for g, w in enumerate(POOL_WINDOWS):
        c0, c1 = g * pg, (g + 1) * pg
        cur = ext_ref[POOL_HALO:POOL_HALO + tm, c0:c1]
        acc = cur
        for d in range(1, w):
            acc = acc + ext_ref[POOL_HALO - d:POOL_HALO - d + tm, c0:c1]
        cnt = jnp.minimum(pos + 1, w).astype(F32)
        pooled = (acc / cnt - cur).astype(BF16)
        mixed = jnp.dot(pooled, w_ref[g].astype(BF16), preferred_element_type=F32) * sc_ref[:, c0:c1]
        o_ref[:, c0:c1] = mixed.astype(o_ref.dtype)


def _pool_prompt(z, n_rows, w_grp, scale):
    n, pw = z.shape
    tm = TM_ROW
    hb = tm // POOL_HALO
    return pl.pallas_call(
        functools.partial(_pool_prompt_kernel, tm=tm, pg=pw // len(POOL_WINDOWS), nb=n_rows // tm),
        grid=(n // tm,),
        in_specs=[pl.BlockSpec((tm, pw), lambda i: (i, 0)),
                  pl.BlockSpec((POOL_HALO, pw), lambda i: (jnp.maximum(i * hb - 1, 0), 0)),
                  pl.BlockSpec(w_grp.shape, lambda i: (0, 0, 0)),
                  pl.BlockSpec((1, pw), lambda i: (0, 0))],
        out_specs=pl.BlockSpec((tm, pw), lambda i: (i, 0)),
        out_shape=jax.ShapeDtypeStruct((n, pw), BF16),
        scratch_shapes=[pltpu.VMEM((tm + POOL_HALO, pw), F32)],
        compiler_params=_params(1, 40),
        name="pool_prompt",
    )(z, z, w_grp, scale.reshape(1, pw))


def _pool_sample_kernel(st_ref, z_ref, w_ref, sc_ref, mixed_in_ref, o_ref, ns_ref, ext_ref, *, pg, t_new):
    del mixed_in_ref
    n_st = st_ref.shape[1]
    seqs = st_ref.shape[0]
    hist = ext_ref.shape[1] - t_new
    ext_ref[:, hist - n_st:hist, :] = st_ref[...]
    ext_ref[:, hist:, :] = z_ref[...]
    ns_ref[...] = ext_ref[:, hist + t_new - n_st:, :]
    for g, w in enumerate(POOL_WINDOWS):
        c0, c1 = g * pg, (g + 1) * pg
        cur = ext_ref[:, hist:, c0:c1]
        acc = cur
        for d in range(1, w):
            acc = acc + ext_ref[:, hist - d:hist - d + t_new, c0:c1]
        pooled = (acc / float(w) - cur).reshape(seqs * t_new, pg).astype(BF16)
        mixed = jnp.dot(pooled, w_ref[g].astype(BF16), preferred_element_type=F32) * sc_ref[:, c0:c1]
        o_ref[:, c0:c1] = mixed.astype(o_ref.dtype)


def _pool_sample(state, z, mixed, row0, w_grp, scale, t_new):
    n_seq, n_st, pw = state.shape
    assert t_new == SUBLANES and n_st <= POOL_HALO - 1
    z3 = z.reshape(z.shape[0] // t_new, t_new, pw)
    sb0 = row0 // t_new // POOL_SEQS
    rows = POOL_SEQS * t_new
    st_spec = pl.BlockSpec((POOL_SEQS, n_st, pw), lambda i: (i, 0, 0))
    return pl.pallas_call(
        functools.partial(_pool_sample_kernel, pg=pw // len(POOL_WINDOWS), t_new=t_new),
        grid=(n_seq // POOL_SEQS,),
        in_specs=[st_spec,
                  pl.BlockSpec((POOL_SEQS, t_new, pw), lambda i: (sb0 + i, 0, 0)),
                  pl.BlockSpec(w_grp.shape, lambda i: (0, 0, 0)),
                  pl.BlockSpec((1, pw), lambda i: (0, 0)),
                  pl.BlockSpec(memory_space=pl.ANY)],
        out_specs=[pl.BlockSpec((rows, pw), lambda i: (row0 // rows + i, 0)), st_spec],
        out_shape=[jax.ShapeDtypeStruct(mixed.shape, mixed.dtype), jax.ShapeDtypeStruct(state.shape, F32)],
        scratch_shapes=[pltpu.VMEM((POOL_SEQS, POOL_HALO + t_new, pw), F32)],
        input_output_aliases={4: 0},
        compiler_params=_params(1, 40),
        name="pool_sample",
    )(state, z3, w_grp, scale.reshape(1, pw), mixed)


def _merge_kernel(a1_ref, a2_ref, w1_hbm, w2_hbm, ga_ref, gb_ref, ba_ref, bb_ref, o_ref,
                  w1f_ref, w2f_ref, w1b_ref, w2b_ref, sem):
    _column_weights_step((w1_hbm, w2_hbm), (0, 0), (w1f_ref, w2f_ref), (w1b_ref, w2b_ref), sem)
    a_up = jnp.dot(a1_ref[...], w1b_ref[...], preferred_element_type=F32)
    b_up = jnp.dot(a2_ref[...], w2b_ref[...], preferred_element_type=F32)
    merged = (jax.nn.sigmoid(ga_ref[...] + ba_ref[...]) * a_up
              + jax.nn.sigmoid(gb_ref[...] + bb_ref[...]) * b_up)
    o_ref[...] = merged.astype(o_ref.dtype)


def _merge(attn, mixed, w_attn_up, w_pool_up, gates, b_gate):
    n, ka = attn.shape
    kp = mixed.shape[1]
    d = w_attn_up.shape[1]
    tm, tn = TM_WIDE, TN_WIDE
    nj = d // tn
    any_spec = pl.BlockSpec(memory_space=pl.ANY)
    return pl.pallas_call(
        _merge_kernel,
        grid=(nj, n // tm),
        in_specs=[pl.BlockSpec((tm, ka), lambda j, i: (i, 0)),
                  pl.BlockSpec((tm, kp), lambda j, i: (i, 0)),
                  any_spec, any_spec,
                  pl.BlockSpec((tm, tn), lambda j, i: (i, j)),
                  pl.BlockSpec((tm, tn), lambda j, i: (i, nj + j)),
                  pl.BlockSpec((1, tn), lambda j, i: (0, j)),
                  pl.BlockSpec((1, tn), lambda j, i: (0, nj + j))],
        out_specs=pl.BlockSpec((tm, tn), lambda j, i: (i, j)),
        out_shape=jax.ShapeDtypeStruct((n, d), BF16),
        scratch_shapes=_weight_scratch([(ka, tn), (kp, tn)]),
        compiler_params=_params(2, 56),
        name="merge",
    )(attn, mixed, w_attn_up, w_pool_up, gates, gates, b_gate, b_gate)


def _out_proj_kernel(a_ref, w_hbm, ha_ref, hb_ref, o_ref, wf_ref, wb_ref, sem, *, nb_first):
    _column_weights_step((w_hbm,), (0,), (wf_ref,), (wb_ref,), sem)
    h = jnp.where(pl.program_id(1) < nb_first, ha_ref[...], hb_ref[...])
    o_ref[...] = h + jnp.dot(a_ref[...], wb_ref[...], preferred_element_type=F32)


def _out_proj(merged, w_out, ha, hb):
    n, d = merged.shape
    tm, tn = TM_WIDE, TN_WIDE
    nb_first = ha.shape[0] // tm
    h_first = pl.BlockSpec((tm, tn), lambda j, i: (jnp.minimum(i, nb_first - 1), j))
    h_last = pl.BlockSpec((tm, tn), lambda j, i: (jnp.maximum(i - nb_first, 0), j))
    return pl.pallas_call(
        functools.partial(_out_proj_kernel, nb_first=nb_first),
        grid=(d // tn, n // tm),
        in_specs=[pl.BlockSpec((tm, d), lambda j, i: (i, 0)),
                  pl.BlockSpec(memory_space=pl.ANY),
                  h_first, h_last],
        out_specs=pl.BlockSpec((tm, tn), lambda j, i: (i, j)),
        out_shape=jax.ShapeDtypeStruct((n, d), F32),
        scratch_shapes=_weight_scratch([(d, tn)]),
        compiler_params=_params(2, 56),
        name="out_proj",
    )(merged, w_out, ha, hb)


def _router_kernel(h_ref, g_ref, wr_ref, br_ref, info_ref, cw_ref, xf_ref, cnt_ref, base_ref):
    tm = h_ref.shape[0]

    @pl.when(pl.program_id(0) == 0)
    def _():
        base_ref[...] = jnp.zeros_like(base_ref)

    xf = _rms_rows(h_ref[...], g_ref[...]).astype(BF16)
    xf_ref[...] = xf.reshape(xf_ref.shape)
    logits = jnp.dot(xf, wr_ref[...].astype(BF16), preferred_element_type=F32) + br_ref[...]
    lane = lax.broadcasted_iota(I32, logits.shape, 1)
    big = jnp.int32(1 << 20)
    ninf = jnp.float32(-jnp.inf)

    is_g = lane < N_GROUPS
    gl = jnp.where(is_g, logits, ninf)
    gmax = jnp.max(gl, axis=-1, keepdims=True)
    gden = jnp.sum(jnp.exp(gl - gmax), axis=-1, keepdims=True)
    g_p = 1.0 / gden
    g_idx = jnp.min(jnp.where(gl == gmax, lane, big), axis=-1, keepdims=True)

    e_lane = lane - N_GROUPS
    in_grp = (e_lane >= g_idx * EXPERTS_PER_GROUP) & (e_lane < (g_idx + 1) * EXPERTS_PER_GROUP)
    el = jnp.where(in_grp, logits, ninf)
    m1 = jnp.max(el, axis=-1, keepdims=True)
    i1 = jnp.min(jnp.where(in_grp & (el == m1), e_lane, big), axis=-1, keepdims=True)
    el2 = jnp.where(e_lane == i1, ninf, el)
    m2 = jnp.max(el2, axis=-1, keepdims=True)
    i2 = jnp.min(jnp.where(in_grp & (el2 == m2), e_lane, big), axis=-1, keepdims=True)
    t = jnp.exp(m2 - m1)
    den = 1.0 + t
    c1 = g_p * (1.0 / den)
    c2 = g_p * (t / den)
    cw_ref[...] = jnp.where(lane == 0, c1, jnp.where(lane == 1, c2, 0.0))

    hit1, hit2 = lane == i1, lane == i2
    onehot = (hit1 | hit2).astype(BF16)
    earlier = (lax.broadcasted_iota(I32, (tm, tm), 1) < lax.broadcasted_iota(I32, (tm, tm), 0)).astype(BF16)
    before = jnp.dot(earlier, onehot, preferred_element_type=F32) + base_ref[...]
    r1 = jnp.sum(jnp.where(hit1, before, 0.0), axis=-1, keepdims=True).astype(I32)
    r2 = jnp.sum(jnp.where(hit2, before, 0.0), axis=-1, keepdims=True).astype(I32)
    base_ref[...] += jnp.sum(onehot.astype(F32), axis=0, keepdims=True)
    cnt_ref[...] = base_ref[...].astype(I32)
    info_ref[...] = jnp.where(lane == 0, i1, jnp.where(lane == 1, i2,
                              jnp.where(lane == 2, r1, jnp.where(lane == 3, r2, 0))))


def _router(h, g_ffn, w_rg, b_rg, w_re, b_re):
    n, d = h.shape
    n_log = N_GROUPS + N_EXPERTS
    wr = jnp.concatenate([w_rg, w_re, jnp.zeros((d, LANES - n_log), F32)], axis=1)
    br = jnp.concatenate([b_rg, b_re, jnp.zeros((LANES - n_log,), F32)]).reshape(1, LANES)
    lane_spec = pl.BlockSpec((TM_ROW, LANES), lambda i: (i, 0))
    return pl.pallas_call(
        _router_kernel,
        grid=(n // TM_ROW,),
        in_specs=[pl.BlockSpec((TM_ROW, d), lambda i: (i, 0)),
                  pl.BlockSpec((1, d), lambda i: (0, 0)),
                  pl.BlockSpec((d, LANES), lambda i: (0, 0)),
                  pl.BlockSpec((1, LANES), lambda i: (0, 0))],
        out_specs=[lane_spec, lane_spec,
                   pl.BlockSpec((TM_ROW, d // LANES, LANES), lambda i: (i, 0, 0)),
                   pl.BlockSpec((1, LANES), lambda i: (0, 0))],
        out_shape=[jax.ShapeDtypeStruct((n, LANES), I32), jax.ShapeDtypeStruct((n, LANES), F32),
                   jax.ShapeDtypeStruct((n, d // LANES, LANES), BF16), jax.ShapeDtypeStruct((1, LANES), I32)],
        scratch_shapes=[pltpu.VMEM((1, LANES), F32)],
        compiler_params=_params(1, 40),
        name="router",
    )(h, g_ffn.reshape(1, d), wr, br)


def _dispatch_plan(info, counts, n_tiles):
    counts = counts[0, :N_EXPERTS]
    tiles_per = (counts + TM_E - 1) // TM_E
    tiles_end = jnp.cumsum(tiles_per)
    row_start = (tiles_end - tiles_per) * TM_E
    eid, rank = info[:, :TOP_K], info[:, TOP_K:2 * TOP_K]
    onehot = (eid[:, :, None] == jnp.arange(N_EXPERTS, dtype=I32)[None, None, :]).astype(I32)
    pos = (jnp.sum(onehot * row_start[None, None, :], axis=-1) + rank).reshape(-1).astype(I32)
    n_used = tiles_end[-1:].astype(I32)
    tile_ids = jnp.minimum(jnp.arange(n_tiles, dtype=I32), n_used - 1)
    tile_expert = jnp.sum((tile_ids[:, None] >= tiles_end[None, :]).astype(I32), axis=1).astype(I32)
    last_tile_row = (jnp.clip(tiles_end - 1, 0, n_tiles - 1) * TM_E).astype(I32)
    starts = jnp.concatenate([jnp.ones((1,), I32), (tile_expert[1:] != tile_expert[:-1]).astype(I32)])
    run_idx = jnp.cumsum(starts) - 1
    n_runs = run_idx[-1:] + 1
    run_expert = jnp.zeros((n_tiles,), I32).at[run_idx].set(tile_expert)
    next_expert = run_expert[(run_idx + 1) % n_runs]
    rows_end = row_start + counts
    valid = jnp.clip(rows_end[tile_expert] - jnp.arange(n_tiles, dtype=I32) * TM_E, 0, TM_E)
    plan = (tile_expert, n_used, run_idx.astype(I32), next_expert.astype(I32), n_runs.astype(I32),
            valid.astype(I32))
    return pos, plan, last_tile_row


def _dispatch_kernel(pos_ref, ltr_ref, nu_ref, xf_ref, xs_hbm, zero_ref, sem):
    i = pl.program_id(0)
    n_tiles = xs_hbm.shape[0] // TM_E

    def clear_tile(row):
        return pltpu.make_async_copy(zero_ref, xs_hbm.at[pl.ds(pl.multiple_of(row, TM_E), TM_E)], sem)

    @pl.when(i == 0)
    def _():
        zero_ref[...] = jnp.zeros_like(zero_ref)
        for e in range(N_EXPERTS):
            clear_tile(ltr_ref[e]).start()

        def start_unused(t, carry):
            clear_tile(t * TM_E).start()
            return carry

        def wait_one(t, carry):
            clear_tile(0).wait()
            return carry

        lax.fori_loop(nu_ref[0], n_tiles, start_unused, 0)
        lax.fori_loop(nu_ref[0] - N_EXPERTS, n_tiles, wait_one, 0)

    base = i * TM_DISPATCH

    def issue(r, carry):
        for k in range(TOP_K):
            dst_row = pos_ref[(base + r) * TOP_K + k]
            pltpu.make_async_copy(xf_ref.at[pl.ds(r, 1)], xs_hbm.at[pl.ds(dst_row, 1)], sem).start(priority=k % 2)
        return carry

    lax.fori_loop(0, TM_DISPATCH, issue, 0, unroll=4)
    for k in range(TOP_K):
        pltpu.make_async_copy(xf_ref, xs_hbm.at[pl.ds(0, TM_DISPATCH)], sem).wait()


def _dispatch(pos, last_tile_row, n_used, xf3, n_tiles):
    n, c, l = xf3.shape
    return pl.pallas_call(
        _dispatch_kernel,
        grid_spec=pltpu.PrefetchScalarGridSpec(
            num_scalar_prefetch=3,
            grid=(n // TM_DISPATCH,),
            in_specs=[pl.BlockSpec((TM_DISPATCH, c, l), lambda i, *_: (i, 0, 0))],
            out_specs=pl.BlockSpec(memory_space=pl.ANY),
            scratch_shapes=[pltpu.VMEM((TM_E, c, l), xf3.dtype), pltpu.SemaphoreType.DMA(())]),
        out_shape=jax.ShapeDtypeStruct((n_tiles * TM_E, c, l), xf3.dtype),
        compiler_params=_params(1, 24),
        name="moe_dispatch",
    )(pos, last_tile_row, n_used, xf3)


def _expert_weights_step(plan_refs, w_hbms, wbuf_ref, wb_refs, sem, tn):
    te_ref, _, run_ref, nxt_ref, nr_ref = plan_refs
    j, t = pl.program_id(0), pl.program_id(1)

    def copies(e, jj):
        col = pl.multiple_of(jj * tn, tn)
        return [pltpu.make_async_copy(w.at[e, :, pl.ds(col, tn)], wbuf_ref.at[m], sem)
                for m, w in enumerate(w_hbms)]

    @pl.when((t == 0) | (te_ref[t] != te_ref[jnp.maximum(t - 1, 0)]))
    def _():
        @pl.when((j == 0) & (t == 0))
        def _():
            for c in copies(te_ref[t], j):
                c.start()

        for c in copies(te_ref[t], j):
            c.wait()
        for m, wb_ref in enumerate(wb_refs):
            _cast_rows(wbuf_ref.at[m], wb_ref)

        next_j = jnp.where(run_ref[t] + 1 == nr_ref[0], j + 1, j)

        @pl.when(next_j < pl.num_programs(0))
        def _():
            for c in copies(nxt_ref[t], next_j):
                c.start()


def _by_valid_rows(valid, full_fn, zero_fn):
    half = TM_E // 2

    @pl.when(valid > half)
    def _():
        full_fn(TM_E)

    @pl.when((valid > 0) & (valid <= half))
    def _():
        full_fn(half)
        zero_fn(half)

    @pl.when(valid == 0)
    def _():
        zero_fn(0)


def _moe_up_kernel(te_ref, nu_ref, run_ref, nxt_ref, nr_ref, valid_ref, xs_ref, wg_hbm, wu_hbm, o_ref,
                   wbuf_ref, wgb_ref, wub_ref, sem):
    t = pl.program_id(1)
    _expert_weights_step((te_ref, nu_ref, run_ref, nxt_ref, nr_ref), (wg_hbm, wu_hbm), wbuf_ref,
                         (wgb_ref, wub_ref), sem, o_ref.shape[1])

    def compute(rows):
        x = xs_ref[0:rows].reshape(rows, -1)
        gate = jnp.dot(x, wgb_ref[...], preferred_element_type=F32)
        up = jnp.dot(x, wub_ref[...], preferred_element_type=F32)
        o_ref[0:rows] = (jax.nn.silu(gate) * up).astype(o_ref.dtype)

    def clear(row0):
        o_ref[row0:] = jnp.zeros((TM_E - row0,) + o_ref.shape[1:], o_ref.dtype)

    _by_valid_rows(valid_ref[t], compute, clear)


def _moe_up(plan, xs3, w_gate, w_up, n_tiles):
    d, f = w_gate.shape[1], w_gate.shape[2]
    tf = TN
    any_spec = pl.BlockSpec(memory_space=pl.ANY)
    return pl.pallas_call(
        _moe_up_kernel,
        grid_spec=pltpu.PrefetchScalarGridSpec(
            num_scalar_prefetch=len(plan),
            grid=(f // tf, n_tiles),
            in_specs=[pl.BlockSpec((TM_E,) + xs3.shape[1:], lambda j, t, te, nu, *_: (jnp.minimum(t, nu[0] - 1), 0, 0)),
                      any_spec, any_spec],
            out_specs=pl.BlockSpec((TM_E, tf), lambda j, t, *_: (t, j)),
            scratch_shapes=[pltpu.VMEM((2, d, tf), F32), pltpu.VMEM((d, tf), BF16), pltpu.VMEM((d, tf), BF16),
                            pltpu.SemaphoreType.DMA(())]),
        out_shape=jax.ShapeDtypeStruct((n_tiles * TM_E, f), BF16),
        compiler_params=_params(2, 56),
        name="moe_up",
    )(*plan, xs3, w_gate, w_up)


def _moe_down_kernel(te_ref, nu_ref, run_ref, nxt_ref, nr_ref, valid_ref, hid_ref, wd_hbm, o_ref,
                     wbuf_ref, wdb_ref, sem):
    t = pl.program_id(1)
    _expert_weights_step((te_ref, nu_ref, run_ref, nxt_ref, nr_ref), (wd_hbm,), wbuf_ref, (wdb_ref,), sem,
                         wdb_ref.shape[1])

    def compute(rows):
        y = jnp.dot(hid_ref[0:rows], wdb_ref[...], preferred_element_type=F32)
        o_ref[0:rows] = y.reshape((rows,) + o_ref.shape[1:])

    def clear(row0):
        o_ref[row0:] = jnp.zeros((TM_E - row0,) + o_ref.shape[1:], o_ref.dtype)

    _by_valid_rows(valid_ref[t], compute, clear)


def _moe_down(plan, hid, w_down, n_tiles):
    f, d = w_down.shape[1], w_down.shape[2]
    tn = 4 * TN
    return pl.pallas_call(
        _moe_down_kernel,
        grid_spec=pltpu.PrefetchScalarGridSpec(
            num_scalar_prefetch=len(plan),
            grid=(d // tn, n_tiles),
            in_specs=[pl.BlockSpec((TM_E, f), lambda j, t, *_: (t, 0)),
                      pl.BlockSpec(memory_space=pl.ANY)],
            out_specs=pl.BlockSpec((TM_E, tn // LANES, LANES), lambda j, t, *_: (t, j, 0)),
            scratch_shapes=[pltpu.VMEM((1, f, tn), F32), pltpu.VMEM((f, tn), BF16),
                            pltpu.SemaphoreType.DMA(())]),
        out_shape=jax.ShapeDtypeStruct((n_tiles * TM_E, d // LANES, LANES), F32),
        compiler_params=_params(2, 40),
        name="moe_down",
    )(*plan, hid, w_down)


def _combine_kernel(pos_ref, h_ref, cw_ref, y_hbm, g_ref, ho_ref, xo_ref, buf_ref, sem):
    i = pl.program_id(0)
    tm, d = h_ref.shape

    def row_copy(slot, k, r, src_row):
        return pltpu.make_async_copy(y_hbm.at[pl.ds(src_row, 1)], buf_ref.at[slot, k, pl.ds(r, 1)], sem.at[slot])

    def issue_step(step, slot):
        def body(r, carry):
            for k in range(TOP_K):
                row_copy(slot, k, r, pos_ref[(step * tm + r) * TOP_K + k]).start(priority=k % 2)
            return carry
        lax.fori_loop(0, tm, body, 0, unroll=4)

    @pl.when(i == 0)
    def _():
        issue_step(0, 0)

    @pl.when(i + 1 < pl.num_programs(0))
    def _():
        issue_step(i + 1, (i + 1) % 2)

    slot = i % 2
    for k in range(TOP_K):
        pltpu.make_async_copy(y_hbm.at[pl.ds(0, tm)], buf_ref.at[slot, k], sem.at[slot]).wait()
    cw = cw_ref[...]
    y1 = buf_ref[slot, 0].reshape(tm, d)
    y2 = buf_ref[slot, 1].reshape(tm, d)
    h2 = h_ref[...] + (cw[:, 0:1] * y1 + cw[:, 1:2] * y2)
    ho_ref[...] = h2
    xo_ref[...] = _rms_rows(h2, g_ref[...]).astype(xo_ref.dtype)


def _combine(pos, h, cw, y3, g_ple):
    n, d = h.shape
    row_spec = pl.BlockSpec((TM_ROW, d), lambda i, p: (i, 0))
    return pl.pallas_call(
        _combine_kernel,
        grid_spec=pltpu.PrefetchScalarGridSpec(
            num_scalar_prefetch=1,
            grid=(n // TM_ROW,),
            in_specs=[row_spec,
                      pl.BlockSpec((TM_ROW, LANES), lambda i, p: (i, 0)),
                      pl.BlockSpec(memory_space=pl.ANY),
                      pl.BlockSpec((1, d), lambda i, p: (0, 0))],
            out_specs=[row_spec, row_spec],
            scratch_shapes=[pltpu.VMEM((2, TOP_K, TM_ROW) + y3.shape[1:], F32), pltpu.SemaphoreType.DMA((2,))]),
        out_shape=[jax.ShapeDtypeStruct((n, d), F32), jax.ShapeDtypeStruct((n, d), BF16)],
        compiler_params=_params(1, 56),
        name="moe_combine",
    )(pos, h, cw, y3, g_ple.reshape(1, d))


def _ple_kernel(xn_ref, wg_hbm, bg_ref, pa_ref, pb_ref, wp_hbm, h_ref, o_ref,
                wgf_ref, wpf_ref, wgb_ref, wpb_ref, sem, *, nb_first):
    _column_weights_step((wg_hbm, wp_hbm), (0, 0), (wgf_ref, wpf_ref), (wgb_ref, wpb_ref), sem)
    gate = jax.nn.sigmoid(jnp.dot(xn_ref[...], wgb_ref[...], preferred_element_type=F32) + bg_ref[...])
    p = jnp.where(pl.program_id(1) < nb_first, pa_ref[...], pb_ref[...]).astype(BF16)
    ple = jnp.dot(p, wpb_ref[...], preferred_element_type=F32)
    o_ref[...] = h_ref[...] + gate * ple


def _ple(xn, w_gate, b_gate, pa, pb, w_proj, h):
    n, d = h.shape
    pd = pa.shape[1]
    tm, tn = TM, TN
    nb_first = pa.shape[0] // tm
    p_first = pl.BlockSpec((tm, pd), lambda j, i: (jnp.minimum(i, nb_first - 1), 0))
    p_last = pl.BlockSpec((tm, pd), lambda j, i: (jnp.maximum(i - nb_first, 0), 0))
    any_spec = pl.BlockSpec(memory_space=pl.ANY)
    return pl.pallas_call(
        functools.partial(_ple_kernel, nb_first=nb_first),
        grid=(d // tn, n // tm),
        in_specs=[pl.BlockSpec((tm, d), lambda j, i: (i, 0)),
                  any_spec,
                  pl.BlockSpec((1, tn), lambda j, i: (0, j)),
                  p_first, p_last,
                  any_spec,
                  pl.BlockSpec((tm, tn), lambda j, i: (i, j))],
        out_specs=pl.BlockSpec((tm, tn), lambda j, i: (i, j)),
        out_shape=jax.ShapeDtypeStruct((n, d), F32),
        scratch_shapes=_weight_scratch([(d, tn), (pd, tn)]),
        compiler_params=_params(2, 56),
        name="ple",
    )(xn, w_gate, b_gate.reshape(1, d), pa, pb, w_proj, h)


def _layer(xn, h_p, h_s, p_p, p_s, cache_k, cache_v, state_pool, lw):
    n, d = xn.shape
    n_prompt = h_p.shape[0]
    n_seq, n_cache, n_kv, _ = cache_k.shape
    t_new = (n - n_prompt) // n_seq
    kv_w = n_kv * HEAD_DIM
    attn_w = lw['w_attn_up'].shape[0]
    pool_w = lw['w_pool_up'].shape[0]
    n_pool = state_pool.shape[1]

    qkv = _in_proj(xn, lw['w_in'], 0, attn_w + 2 * kv_w, "in_proj_qkv")
    z = _in_proj(xn, lw['w_in'], attn_w + 2 * kv_w, pool_w, "in_proj_pool")
    gates = _in_proj(xn, lw['w_in'], attn_w + 2 * kv_w + pool_w, 2 * d, "in_proj_gates")

    tab_p = _rope_tables(jnp.arange(n_prompt, dtype=F32))
    tab_s = tuple(jnp.tile(a, (SEQS_PER_STEP, 1))
                  for a in _rope_tables(PAST_LEN + jnp.arange(t_new, dtype=F32)))
    tab_p_t = _rope_tables_t(jnp.arange(n_prompt, dtype=F32))
    attn, k_rot_p = _attn_prompt(qkv, lw['attn_sinks'], tab_p, tab_p_t, n_prompt, attn_w, kv_w)
    attn, new_k_s, new_v_s = _attn_sample(
        qkv, attn, n_prompt, cache_k.reshape(n_seq, n_cache, kv_w), cache_v.reshape(n_seq, n_cache, kv_w),
        lw['attn_sinks'], tab_s, n_seq, t_new, attn_w, kv_w)

    mixed = _pool_prompt(z, n_prompt, lw['w_pool_grp'], lw['pool_scale'])
    mixed, new_z_s = _pool_sample(state_pool, z, mixed, n_prompt, lw['w_pool_grp'], lw['pool_scale'], t_new)

    merged = _merge(attn, mixed, lw['w_attn_up'], lw['w_pool_up'], gates, lw['b_gate'].reshape(1, 2 * d))
    h1 = _out_proj(merged, lw['w_out'], h_p, h_s)

    n_tiles = (n * TOP_K) // TM_E + N_EXPERTS
    info, cw, xf3, counts = _router(h1, lw['g_ffn'], lw['w_route_group'], lw['b_route_group'],
                                    lw['w_route_expert'], lw['b_route_expert'])
    pos, plan, last_tile_row = _dispatch_plan(info, counts, n_tiles)
    xs3 = _dispatch(pos, last_tile_row, plan[1], xf3, n_tiles)
    hid = _moe_up(plan, xs3, lw['w_exp_gate'], lw['w_exp_up'], n_tiles)
    y3 = _moe_down(plan, hid, lw['w_exp_down'], n_tiles)
    h2, xn2 = _combine(pos, h1, cw, y3, lw['g_ple'])

    h3 = _ple(xn2, lw['w_ple_gate'], lw['b_ple_gate'], p_p, p_s, lw['w_ple_proj'], h2)

    new_k_p = k_rot_p[n_prompt - n_cache:].reshape(1, n_cache, n_kv, HEAD_DIM)
    new_v_p = qkv[n_prompt - n_cache:n_prompt, attn_w + kv_w:].reshape(1, n_cache, n_kv, HEAD_DIM)
    new_z_p = z[n_prompt - n_pool:n_prompt].reshape(1, n_pool, pool_w)
    return (h3, new_k_p, new_v_p, new_z_p,
            new_k_s.reshape(cache_k.shape), new_v_s.reshape(cache_v.shape), new_z_s)


def kernel(x_prompt, x_sample, cache_k, cache_v, state_pool, p_prompt, p_sample, g_mix, w_in, b_gate, attn_sinks, w_pool_grp, pool_scale, w_attn_up, w_pool_up, w_out, g_ffn, w_route_group, b_route_group, w_route_expert, b_route_expert, w_exp_gate, w_exp_up, w_exp_down, g_ple, w_ple_gate, b_ple_gate, w_ple_proj, g_final):
    batch, seq, d = x_prompt.shape
    n_seq, t_new, _ = x_sample.shape
    depth = w_in.shape[0]
    assert batch == 1, "prompt rows are treated as one sequence"
    n_prompt = batch * seq
    n_sample = n_seq * t_new
    n = n_prompt + n_sample
    assert n_prompt % TM == 0 and n_sample % TM == 0 and (n * TOP_K) % TM_E == 0 and n % TM_DISPATCH == 0
    assert n_seq % SEQS_PER_STEP == 0 and n_seq % POOL_SEQS == 0 and t_new & (t_new - 1) == 0
    assert cache_k.shape[2] == WINDOW and state_pool.shape[2] == POOL_HALO - 1
    assert depth == 1, "one layer per step"

    weights = dict(g_mix=g_mix, w_in=w_in, b_gate=b_gate, attn_sinks=attn_sinks, w_pool_grp=w_pool_grp,
                   pool_scale=pool_scale, w_attn_up=w_attn_up, w_pool_up=w_pool_up, w_out=w_out, g_ffn=g_ffn,
                   w_route_group=w_route_group, b_route_group=b_route_group, w_route_expert=w_route_expert,
                   b_route_expert=b_route_expert, w_exp_gate=w_exp_gate, w_exp_up=w_exp_up,
                   w_exp_down=w_exp_down, g_ple=g_ple, w_ple_gate=w_ple_gate, b_ple_gate=b_ple_gate,
                   w_ple_proj=w_ple_proj)
    lw = {name: w[0] for name, w in weights.items()}
    h_p = x_prompt.reshape(n_prompt, d)
    h_s = x_sample.reshape(n_sample, d)
    xn = _rmsnorm_in2(h_p, h_s, lw['g_mix'], BF16, "norm_mix")
    out = _layer(xn, h_p, h_s, p_prompt[0].reshape(n_prompt, -1), p_sample[0].reshape(n_sample, -1),
                 cache_k[0], cache_v[0], state_pool[0], lw)
    y_p, y_s = _rmsnorm_out2(out[0], g_final, n_prompt, "norm_final")
    return (y_p.reshape(batch, seq, d), y_s.reshape(n_seq, t_new, d)) + tuple(piece[None] for piece in out[1:])
```

```python
import functools

import jax
import jax.numpy as jnp
from jax import lax
from jax.experimental import pallas as pl
from jax.experimental.pallas import tpu as pltpu

F32 = jnp.float32
BF16 = jnp.bfloat16
I32 = jnp.int32

HEAD_DIM = 64
Q_PER_KV = 4
WINDOW = 128
PAST_LEN = 8192
ROT_DIM = HEAD_DIM // 4
ROPE_THETA = 500000.0
POOL_WINDOWS = (2, 4, 8, 16)
POOL_HALO = 16
N_GROUPS = 4
EXPERTS_PER_GROUP = 4
N_EXPERTS = N_GROUPS * EXPERTS_PER_GROUP
TOP_K = 2
RMS_EPS = 1e-6
NEG_INF = -1e30
LANES = 128
SUBLANES = 8
MIB = 1024 * 1024

TM = 1024
TN = 512
TN_PROJ = 1024
CAST_ROWS = 512
TM_WIDE = 512
TN_WIDE = 1024
TM_ROW = 256
TM_E = 512
TM_DISPATCH = 512
SEQS_PER_STEP = 8
POOL_SEQS = 16


def _params(n_axes, vmem_mib):
    return pltpu.CompilerParams(dimension_semantics=("arbitrary",) * n_axes,
                                vmem_limit_bytes=vmem_mib * MIB)


def _rms_rows(x, g):
    ms = jnp.mean(x * x, axis=-1, keepdims=True)
    return x * lax.rsqrt(ms + RMS_EPS) * g


def _first_last_specs(nb_first, block):
    zeros = (0,) * (len(block) - 1)
    first = pl.BlockSpec(block, lambda i, *_: (jnp.minimum(i, nb_first - 1),) + zeros)
    last = pl.BlockSpec(block, lambda i, *_: (jnp.maximum(i - nb_first, 0),) + zeros)
    return first, last


def _norm_in2_kernel(xa_ref, xb_ref, g_ref, o_ref, *, nb_first):
    i = pl.program_id(0)

    @pl.when(i < nb_first)
    def _():
        o_ref[...] = _rms_rows(xa_ref[...], g_ref[...]).astype(o_ref.dtype)

    @pl.when(i >= nb_first)
    def _():
        o_ref[...] = _rms_rows(xb_ref[...], g_ref[...]).astype(o_ref.dtype)


def _rmsnorm_in2(xa, xb, g, out_dtype, name):
    (na, d), nb = xa.shape, xb.shape[0]
    nb_first = na // TM_ROW
    spec_a, spec_b = _first_last_specs(nb_first, (TM_ROW, d))
    return pl.pallas_call(
        functools.partial(_norm_in2_kernel, nb_first=nb_first),
        grid=((na + nb) // TM_ROW,),
        in_specs=[spec_a, spec_b, pl.BlockSpec((1, d), lambda i: (0, 0))],
        out_specs=pl.BlockSpec((TM_ROW, d), lambda i: (i, 0)),
        out_shape=jax.ShapeDtypeStruct((na + nb, d), out_dtype),
        compiler_params=_params(1, 40),
        name=name,
    )(xa, xb, g.reshape(1, d))


def _norm_out2_kernel(x_ref, g_ref, oa_ref, ob_ref, *, nb_first):
    i = pl.program_id(0)

    @pl.when(i < nb_first)
    def _():
        oa_ref[...] = _rms_rows(x_ref[...], g_ref[...])

    @pl.when(i >= nb_first)
    def _():
        ob_ref[...] = _rms_rows(x_ref[...], g_ref[...])


def _rmsnorm_out2(x, g, na, name):
    n, d = x.shape
    nb_first = na // TM_ROW
    spec_a, spec_b = _first_last_specs(nb_first, (TM_ROW, d))
    return pl.pallas_call(
        functools.partial(_norm_out2_kernel, nb_first=nb_first),
        grid=(n // TM_ROW,),
        in_specs=[pl.BlockSpec((TM_ROW, d), lambda i: (i, 0)), pl.BlockSpec((1, d), lambda i: (0, 0))],
        out_specs=[spec_a, spec_b],
        out_shape=[jax.ShapeDtypeStruct((na, d), F32), jax.ShapeDtypeStruct((n - na, d), F32)],
        compiler_params=_params(1, 40),
        name=name,
    )(x, g.reshape(1, d))


def _cast_rows(src_ref, dst_ref):
    rows = src_ref.shape[0]
    step = min(rows, CAST_ROWS)
    for r in range(0, rows, step):
        dst_ref[r:r + step] = src_ref[r:r + step].astype(dst_ref.dtype)


def _column_weights_step(w_hbms, col0s, wbuf_refs, wb_refs, sem):
    j, i = pl.program_id(0), pl.program_id(1)

    def copies(jj):
        out = []
        for w, col0, buf in zip(w_hbms, col0s, wbuf_refs):
            tn = buf.shape[1]
            out.append(pltpu.make_async_copy(w.at[:, pl.ds(pl.multiple_of(col0 + jj * tn, tn), tn)], buf, sem))
        return out

    @pl.when(i == 0)
    def _():
        @pl.when(j == 0)
        def _():
            for c in copies(j):
                c.start()

        for c in copies(j):
            c.wait()
        for buf, wb_ref in zip(wbuf_refs, wb_refs):
            _cast_rows(buf, wb_ref)

        @pl.when(j + 1 < pl.num_programs(0))
        def _():
            for c in copies(j + 1):
                c.start()


def _weight_scratch(shapes):
    return ([pltpu.VMEM(s, F32) for s in shapes] + [pltpu.VMEM(s, BF16) for s in shapes]
            + [pltpu.SemaphoreType.DMA(())])


def _proj_kernel(a_ref, w_hbm, o_ref, wbuf_ref, wb_ref, sem, *, col0):
    _column_weights_step((w_hbm,), (col0,), (wbuf_ref,), (wb_ref,), sem)
    o_ref[...] = jnp.dot(a_ref[...], wb_ref[...], preferred_element_type=F32)


def _in_proj(xn, w, col0, width, name):
    n, d = xn.shape
    tn = TN_PROJ
    assert col0 % tn == 0 and width % tn == 0
    return pl.pallas_call(
        functools.partial(_proj_kernel, col0=col0),
        grid=(width // tn, n // TM),
        in_specs=[pl.BlockSpec((TM, d), lambda j, i: (i, 0)),
                  pl.BlockSpec(memory_space=pl.ANY)],
        out_specs=pl.BlockSpec((TM, tn), lambda j, i: (i, j)),
        out_shape=jax.ShapeDtypeStruct((n, width), F32),
        scratch_shapes=_weight_scratch([(d, tn)]),
        compiler_params=_params(2, 56),
        name=name,
    )(xn, w)


def _rope_tables(pos):
    half = ROT_DIM // 2
    inv_freq = ROPE_THETA ** (-jnp.arange(half, dtype=F32) * (2.0 / ROT_DIM))
    ang = pos[:, None] * inv_freq[None, :]
    cos, sin = jnp.cos(ang), jnp.sin(ang)
    t = pos.shape[0]
    pad = jnp.zeros((t, HEAD_DIM - ROT_DIM), F32)
    zeros = jnp.zeros((t, half), F32)
    cos_h = jnp.concatenate([cos, cos, pad + 1.0], axis=1)
    sa_h = jnp.concatenate([zeros, sin, pad], axis=1)
    sb_h = jnp.concatenate([-sin, zeros, pad], axis=1)
    reps = LANES // HEAD_DIM
    return tuple(jnp.tile(a, (1, reps)) for a in (cos_h, sa_h, sb_h))


def _rope_tables_t(pos):
    half = ROT_DIM // 2
    inv_freq = ROPE_THETA ** (-jnp.arange(half, dtype=F32) * (2.0 / ROT_DIM))
    ang = pos[:, None] * inv_freq[None, :]
    return jnp.cos(ang).T, jnp.sin(ang).T


def _rope(x, cos_t, sa_t, sb_t):
    pieces = []
    for c in range(x.shape[1] // LANES):
        xc = x[:, c * LANES:(c + 1) * LANES]
        pieces.append(xc * cos_t
                      + pltpu.roll(xc, ROT_DIM // 2, 1) * sa_t
                      + pltpu.roll(xc, LANES - ROT_DIM // 2, 1) * sb_t)
    return jnp.concatenate(pieces, axis=1)


def _softmax_with_sink(s, sink):
    m = jnp.maximum(jnp.max(s, axis=-1, keepdims=True), sink)
    p = jnp.exp(s - m)
    den = jnp.sum(p, axis=-1, keepdims=True) + jnp.exp(sink - m)
    return p / den


def _attn_prompt_kernel(sink_ref, q_ref, k_ref, v_ref, cos_ref, sa_ref, sb_ref, cost_ref, sint_ref,
                        o_ref, ko_ref, kprev_ref, vprevt_ref, *, n_kv, nb):
    b = pl.program_id(0)

    @pl.when(b == 0)
    def _():
        kprev_ref[...] = jnp.zeros_like(kprev_ref)
        vprevt_ref[...] = jnp.zeros_like(vprevt_ref)

    @pl.when(b >= nb)
    def _():
        o_ref[...] = jnp.zeros_like(o_ref)

    @pl.when(b < nb)
    def _():
        _attn_prompt_block(sink_ref, q_ref, k_ref, v_ref, cos_ref, sa_ref, sb_ref, cost_ref, sint_ref,
                           o_ref, ko_ref, kprev_ref, vprevt_ref, n_kv=n_kv)


def _attn_prompt_block(sink_ref, q_ref, k_ref, v_ref, cos_ref, sa_ref, sb_ref, cost_ref, sint_ref,
                       o_ref, ko_ref, kprev_ref, vprevt_ref, *, n_kv):
    b = pl.program_id(0)
    half = ROT_DIM // 2
    k = _rope(k_ref[...], cos_ref[...], sa_ref[...], sb_ref[...])
    ko_ref[...] = k
    kcat = jnp.concatenate([kprev_ref[...], k], axis=0).astype(BF16)
    vt = v_ref[...].T
    vcat_t = jnp.concatenate([vprevt_ref[...], vt], axis=1).astype(BF16)
    kprev_ref[...] = k
    vprevt_ref[...] = vt

    qt = q_ref[...].T
    cos_t, sin_t = cost_ref[...], sint_ref[...]

    def roped_head_t(h):
        x = qt[h * HEAD_DIM:(h + 1) * HEAD_DIM]
        x1, x2 = x[0:half], x[half:ROT_DIM]
        rot = jnp.concatenate([x1 * cos_t - x2 * sin_t, x2 * cos_t + x1 * sin_t, x[ROT_DIM:]], axis=0)
        return (rot * (HEAD_DIM ** -0.5)).astype(BF16)

    n_heads = n_kv * Q_PER_KV
    zeros_head = jnp.zeros((HEAD_DIM, WINDOW), BF16)
    heads_per_tile = LANES // HEAD_DIM

    key = lax.broadcasted_iota(I32, (2 * WINDOW, Q_PER_KV * WINDOW), 0)
    qry = lax.broadcasted_iota(I32, (2 * WINDOW, Q_PER_KV * WINDOW), 1) & (WINDOW - 1)
    first_key = jnp.where(b > 0, 0, WINDOW)
    mask = (key > qry) & (key <= qry + WINDOW) & (key >= first_key)

    scores = []
    for g in range(n_kv):
        tile, slot = g // heads_per_tile, g % heads_per_tile
        k2 = kcat[:, tile * LANES:(tile + 1) * LANES]
        cols = []
        for r in range(Q_PER_KV):
            parts = [zeros_head] * heads_per_tile
            parts[slot] = roped_head_t(g * Q_PER_KV + r)
            cols.append(jnp.concatenate(parts, axis=0))
        rhs = jnp.concatenate(cols, axis=1)
        scores.append(jnp.where(mask, jnp.dot(k2, rhs, preferred_element_type=F32), NEG_INF))
    st = jnp.concatenate(scores, axis=1)
    sink = jnp.concatenate([jnp.full((1, WINDOW), sink_ref[h], F32) for h in range(n_heads)], axis=1)
    m = jnp.maximum(jnp.max(st, axis=0, keepdims=True), sink)
    p = jnp.exp(st - m)
    den = jnp.sum(p, axis=0, keepdims=True) + jnp.exp(sink - m)
    probs_t = (p * (1.0 / den)).astype(BF16)

    for g in range(n_kv):
        ot = jnp.dot(vcat_t[g * HEAD_DIM:(g + 1) * HEAD_DIM],
                     probs_t[:, g * Q_PER_KV * WINDOW:(g + 1) * Q_PER_KV * WINDOW],
                     preferred_element_type=F32)
        for pr in range(Q_PER_KV // heads_per_tile):
            pair = jnp.concatenate(
                [ot[:, (pr * heads_per_tile + u) * WINDOW:(pr * heads_per_tile + u + 1) * WINDOW]
                 for u in range(heads_per_tile)], axis=0)
            c = (g * Q_PER_KV) // heads_per_tile + pr
            o_ref[:, c * LANES:(c + 1) * LANES] = pair.T.astype(o_ref.dtype)


def _attn_prompt(qkv, sinks, tables, tables_t, n_rows, attn_w, kv_w):
    nb = n_rows // WINDOW
    kb, vb = attn_w // kv_w, attn_w // kv_w + 1

    def blk(b, s):
        return jnp.minimum(b, nb - 1)

    tab_spec = pl.BlockSpec((WINDOW, LANES), lambda b, s: (blk(b, s), 0))
    tab_t_spec = pl.BlockSpec((ROT_DIM // 2, WINDOW), lambda b, s: (0, blk(b, s)))
    return pl.pallas_call(
        functools.partial(_attn_prompt_kernel, n_kv=kv_w // HEAD_DIM, nb=nb),
        grid_spec=pltpu.PrefetchScalarGridSpec(
            num_scalar_prefetch=1,
            grid=(qkv.shape[0] // WINDOW,),
            in_specs=[pl.BlockSpec((WINDOW, attn_w), lambda b, s: (blk(b, s), 0)),
                      pl.BlockSpec((WINDOW, kv_w), lambda b, s: (blk(b, s), kb)),
                      pl.BlockSpec((WINDOW, kv_w), lambda b, s: (blk(b, s), vb)),
                      tab_spec, tab_spec, tab_spec, tab_t_spec, tab_t_spec],
            out_specs=[pl.BlockSpec((WINDOW, attn_w), lambda b, s: (b, 0)),
                       pl.BlockSpec((WINDOW, kv_w), lambda b, s: (blk(b, s), 0))],
            scratch_shapes=[pltpu.VMEM((WINDOW, kv_w), F32), pltpu.VMEM((kv_w, WINDOW), F32)]),
        out_shape=[jax.ShapeDtypeStruct((qkv.shape[0], attn_w), BF16),
                   jax.ShapeDtypeStruct((n_rows, kv_w), F32)],
        compiler_params=_params(1, 32),
        name="attn_prompt",
    )(sinks, qkv, qkv, qkv, *tables, *tables_t)


def _attn_sample_kernel(sink_ref, q_ref, k_ref, v_ref, ck_ref, cv_ref, cos_ref, sa_ref, sb_ref, attn_in_ref,
                        o_ref, nk_ref, nv_ref, *, n_kv, t_new):
    del attn_in_ref
    assert LANES // HEAD_DIM == 2 and Q_PER_KV % 2 == 0
    cos_t, sa_t, sb_t = cos_ref[...], sa_ref[...], sb_ref[...]
    q = _rope(q_ref[...], cos_t, sa_t, sb_t) * (HEAD_DIM ** -0.5)
    k = _rope(k_ref[...], cos_t, sa_t, sb_t)
    v = v_ref[...]
    n_cache = ck_ref.shape[1]
    n_keys = 2 * WINDOW
    n_seq = q.shape[0] // t_new
    n_heads = n_kv * Q_PER_KV
    zpad = jnp.zeros((n_keys - n_cache - t_new, k.shape[1]), F32)
    low = lax.broadcasted_iota(I32, (t_new, LANES), 1) < HEAD_DIM
    high = jnp.logical_not(low)

    qt = lax.broadcasted_iota(I32, (Q_PER_KV * t_new, n_keys), 0) & (t_new - 1)
    kj = lax.broadcasted_iota(I32, (Q_PER_KV * t_new, n_keys), 1)
    mask = (kj > qt + (n_cache - WINDOW)) & (kj <= qt + n_cache)

    def swap_halves(x):
        return pltpu.roll(x, HEAD_DIM, 1)

    scores, values = [], []
    for s_i in range(n_seq):
        ck, cv = ck_ref[s_i], cv_ref[s_i]
        kn, vn = k[s_i * t_new:(s_i + 1) * t_new], v[s_i * t_new:(s_i + 1) * t_new]
        nk_ref[s_i] = jnp.concatenate([ck[t_new:], kn], axis=0)
        nv_ref[s_i] = jnp.concatenate([cv[t_new:], vn], axis=0)
        kall = jnp.concatenate([ck, kn, zpad], axis=0).astype(BF16)
        values.append(jnp.concatenate([cv, vn, zpad], axis=0).astype(BF16))
        qs = q[s_i * t_new:(s_i + 1) * t_new]
        for g in range(n_kv):
            tile, odd = g // 2, g % 2
            parts = []
            for r in range(Q_PER_KV):
                h = g * Q_PER_KV + r
                x = qs[:, (h // 2) * LANES:(h // 2 + 1) * LANES]
                if h % 2 != odd:
                    x = swap_halves(x)
                parts.append(jnp.where(high if odd else low, x, 0.0))
            qg = jnp.concatenate(parts, axis=0).astype(BF16)
            k2 = kall[:, tile * LANES:(tile + 1) * LANES]
            sg = lax.dot_general(qg, k2, (((1,), (1,)), ((), ())), preferred_element_type=F32)
            scores.append(jnp.where(mask, sg, NEG_INF))
    s = jnp.concatenate(scores, axis=0)
    sink_seq = jnp.concatenate([jnp.full((t_new, 1), sink_ref[h], F32) for h in range(n_heads)], axis=0)
    sink = jnp.concatenate([sink_seq] * n_seq, axis=0)
    probs = _softmax_with_sink(s, sink).astype(BF16)

    seq_outs = []
    for s_i in range(n_seq):
        out_tiles = []
        for g in range(n_kv):
            tile, odd = g // 2, g % 2
            r0 = (s_i * n_kv + g) * Q_PER_KV * t_new
            v2 = values[s_i][:, tile * LANES:(tile + 1) * LANES]
            og = jnp.dot(probs[r0:r0 + Q_PER_KV * t_new], v2, preferred_element_type=F32)
            for pr in range(Q_PER_KV // 2):
                even = og[(2 * pr) * t_new:(2 * pr + 1) * t_new]
                oddh = og[(2 * pr + 1) * t_new:(2 * pr + 2) * t_new]
                if odd:
                    even = swap_halves(even)
                else:
                    oddh = swap_halves(oddh)
                out_tiles.append(jnp.where(low, even, oddh))
        seq_outs.append(jnp.concatenate(out_tiles, axis=1))
    o_ref[...] = jnp.concatenate(seq_outs, axis=0).astype(o_ref.dtype)


def _attn_sample(qkv, attn, row0, cache_k, cache_v, sinks, tables, n_seq, t_new, attn_w, kv_w):
    rows = SEQS_PER_STEP * t_new
    rb0 = row0 // rows
    kb, vb = attn_w // kv_w, attn_w // kv_w + 1
    n_cache = cache_k.shape[1]
    tab_spec = pl.BlockSpec((rows, LANES), lambda i, s: (0, 0))
    cache_spec = pl.BlockSpec((SEQS_PER_STEP, n_cache, kv_w), lambda i, s: (i, 0, 0))
    return pl.pallas_call(
        functools.partial(_attn_sample_kernel, n_kv=kv_w // HEAD_DIM, t_new=t_new),
        grid_spec=pltpu.PrefetchScalarGridSpec(
            num_scalar_prefetch=1,
            grid=(n_seq // SEQS_PER_STEP,),
            in_specs=[pl.BlockSpec((rows, attn_w), lambda i, s: (rb0 + i, 0)),
                      pl.BlockSpec((rows, kv_w), lambda i, s: (rb0 + i, kb)),
                      pl.BlockSpec((rows, kv_w), lambda i, s: (rb0 + i, vb)),
                      cache_spec, cache_spec, tab_spec, tab_spec, tab_spec,
                      pl.BlockSpec(memory_space=pl.ANY)],
            out_specs=[pl.BlockSpec((rows, attn_w), lambda i, s: (rb0 + i, 0)), cache_spec, cache_spec]),
        out_shape=[jax.ShapeDtypeStruct(attn.shape, attn.dtype),
                   jax.ShapeDtypeStruct(cache_k.shape, F32),
                   jax.ShapeDtypeStruct(cache_v.shape, F32)],
        input_output_aliases={9: 0},
        compiler_params=_params(1, 32),
        name="attn_sample",
    )(sinks, qkv, qkv, qkv, cache_k, cache_v, *tables, attn)


def _pool_prompt_kernel(z_ref, halo_ref, w_ref, sc_ref, o_ref, ext_ref, *, tm, pg, nb):
    i = pl.program_id(0)

    @pl.when(i >= nb)
    def _():
        o_ref[...] = jnp.zeros_like(o_ref)

    @pl.when(i < nb)
    def _():
        _pool_prompt_block(z_ref, halo_ref, w_ref, sc_ref, o_ref, ext_ref, tm=tm, pg=pg)


def _pool_prompt_block(z_ref, halo_ref, w_ref, sc_ref, o_ref, ext_ref, *, tm, pg):
    i = pl.program_id(0)
    ext_ref[0:POOL_HALO, :] = jnp.where(i > 0, halo_ref[...], 0.0)
    ext_ref[POOL_HALO:, :] = z_ref[...]
    pos = i * tm + lax.broadcasted_iota(I32, (tm, 1), 0)
    for g, w in enumerate(POOL_WINDOWS):
        c0, c1 = g * pg, (g + 1) * pg
        cur = ext_ref[POOL_HALO:POOL_HALO + tm, c0:c1]
        acc = cur
        for d in range(1, w):
            acc = acc + ext_ref[POOL_HALO - d:POOL_HALO - d + tm, c0:c1]
        cnt = jnp.minimum(pos + 1, w).astype(F32)
        pooled = (acc / cnt - cur).astype(BF16)
        mixed = jnp.dot(pooled, w_ref[g].astype(BF16), preferred_element_type=F32) * sc_ref[:, c0:c1]
        o_ref[:, c0:c1] = mixed.astype(o_ref.dtype)


def _pool_prompt(z, n_rows, w_grp, scale):
    n, pw = z.shape
    tm = TM_ROW
    hb = tm // POOL_HALO
    return pl.pallas_call(
        functools.partial(_pool_prompt_kernel, tm=tm, pg=pw // len(POOL_WINDOWS), nb=n_rows // tm),
        grid=(n // tm,),
        in_specs=[pl.BlockSpec((tm, pw), lambda i: (i, 0)),
                  pl.BlockSpec((POOL_HALO, pw), lambda i: (jnp.maximum(i * hb - 1, 0), 0)),
                  pl.BlockSpec(w_grp.shape, lambda i: (0, 0, 0)),
                  pl.BlockSpec((1, pw), lambda i: (0, 0))],
        out_specs=pl.BlockSpec((tm, pw), lambda i: (i, 0)),
        out_shape=jax.ShapeDtypeStruct((n, pw), BF16),
        scratch_shapes=[pltpu.VMEM((tm + POOL_HALO, pw), F32)],
        compiler_params=_params(1, 40),
        name="pool_prompt",
    )(z, z, w_grp, scale.reshape(1, pw))


def _pool_sample_kernel(st_ref, z_ref, w_ref, sc_ref, mixed_in_ref, o_ref, ns_ref, ext_ref, *, pg, t_new):
    del mixed_in_ref
    n_st = st_ref.shape[1]
    seqs = st_ref.shape[0]
    hist = ext_ref.shape[1] - t_new
    ext_ref[:, hist - n_st:hist, :] = st_ref[...]
    ext_ref[:, hist:, :] = z_ref[...]
    ns_ref[...] = ext_ref[:, hist + t_new - n_st:, :]
    for g, w in enumerate(POOL_WINDOWS):
        c0, c1 = g * pg, (g + 1) * pg
        cur = ext_ref[:, hist:, c0:c1]
        acc = cur
        for d in range(1, w):
            acc = acc + ext_ref[:, hist - d:hist - d + t_new, c0:c1]
        pooled = (acc / float(w) - cur).reshape(seqs * t_new, pg).astype(BF16)
        mixed = jnp.dot(pooled, w_ref[g].astype(BF16), preferred_element_type=F32) * sc_ref[:, c0:c1]
        o_ref[:, c0:c1] = mixed.astype(o_ref.dtype)


def _pool_sample(state, z, mixed, row0, w_grp, scale, t_new):
    n_seq, n_st, pw = state.shape
    assert t_new == SUBLANES and n_st <= POOL_HALO - 1
    z3 = z.reshape(z.shape[0] // t_new, t_new, pw)
    sb0 = row0 // t_new // POOL_SEQS
    rows = POOL_SEQS * t_new
    st_spec = pl.BlockSpec((POOL_SEQS, n_st, pw), lambda i: (i, 0, 0))
    return pl.pallas_call(
        functools.partial(_pool_sample_kernel, pg=pw // len(POOL_WINDOWS), t_new=t_new),
        grid=(n_seq // POOL_SEQS,),
        in_specs=[st_spec,
                  pl.BlockSpec((POOL_SEQS, t_new, pw), lambda i: (sb0 + i, 0, 0)),
                  pl.BlockSpec(w_grp.shape, lambda i: (0, 0, 0)),
                  pl.BlockSpec((1, pw), lambda i: (0, 0)),
                  pl.BlockSpec(memory_space=pl.ANY)],
        out_specs=[pl.BlockSpec((rows, pw), lambda i: (row0 // rows + i, 0)), st_spec],
        out_shape=[jax.ShapeDtypeStruct(mixed.shape, mixed.dtype), jax.ShapeDtypeStruct(state.shape, F32)],
        scratch_shapes=[pltpu.VMEM((POOL_SEQS, POOL_HALO + t_new, pw), F32)],
        input_output_aliases={4: 0},
        compiler_params=_params(1, 40),
        name="pool_sample",
    )(state, z3, w_grp, scale.reshape(1, pw), mixed)


def _merge_kernel(a1_ref, a2_ref, w1_hbm, w2_hbm, ga_ref, gb_ref, ba_ref, bb_ref, o_ref,
                  w1f_ref, w2f_ref, w1b_ref, w2b_ref, sem):
    _column_weights_step((w1_hbm, w2_hbm), (0, 0), (w1f_ref, w2f_ref), (w1b_ref, w2b_ref), sem)
    a_up = jnp.dot(a1_ref[...], w1b_ref[...], preferred_element_type=F32)
    b_up = jnp.dot(a2_ref[...], w2b_ref[...], preferred_element_type=F32)
    merged = (jax.nn.sigmoid(ga_ref[...] + ba_ref[...]) * a_up
              + jax.nn.sigmoid(gb_ref[...] + bb_ref[...]) * b_up)
    o_ref[...] = merged.astype(o_ref.dtype)


def _merge(attn, mixed, w_attn_up, w_pool_up, gates, b_gate):
    n, ka = attn.shape
    kp = mixed.shape[1]
    d = w_attn_up.shape[1]
    tm, tn = TM_WIDE, TN_WIDE
    nj = d // tn
    any_spec = pl.BlockSpec(memory_space=pl.ANY)
    return pl.pallas_call(
        _merge_kernel,
        grid=(nj, n // tm),
        in_specs=[pl.BlockSpec((tm, ka), lambda j, i: (i, 0)),
                  pl.BlockSpec((tm, kp), lambda j, i: (i, 0)),
                  any_spec, any_spec,
                  pl.BlockSpec((tm, tn), lambda j, i: (i, j)),
                  pl.BlockSpec((tm, tn), lambda j, i: (i, nj + j)),
                  pl.BlockSpec((1, tn), lambda j, i: (0, j)),
                  pl.BlockSpec((1, tn), lambda j, i: (0, nj + j))],
        out_specs=pl.BlockSpec((tm, tn), lambda j, i: (i, j)),
        out_shape=jax.ShapeDtypeStruct((n, d), BF16),
        scratch_shapes=_weight_scratch([(ka, tn), (kp, tn)]),
        compiler_params=_params(2, 56),
        name="merge",
    )(attn, mixed, w_attn_up, w_pool_up, gates, gates, b_gate, b_gate)


def _out_proj_kernel(a_ref, w_hbm, ha_ref, hb_ref, o_ref, wf_ref, wb_ref, sem, *, nb_first):
    _column_weights_step((w_hbm,), (0,), (wf_ref,), (wb_ref,), sem)
    h = jnp.where(pl.program_id(1) < nb_first, ha_ref[...], hb_ref[...])
    o_ref[...] = h + jnp.dot(a_ref[...], wb_ref[...], preferred_element_type=F32)


def _out_proj(merged, w_out, ha, hb):
    n, d = merged.shape
    tm, tn = TM_WIDE, TN_WIDE
    nb_first = ha.shape[0] // tm
    h_first = pl.BlockSpec((tm, tn), lambda j, i: (jnp.minimum(i, nb_first - 1), j))
    h_last = pl.BlockSpec((tm, tn), lambda j, i: (jnp.maximum(i - nb_first, 0), j))
    return pl.pallas_call(
        functools.partial(_out_proj_kernel, nb_first=nb_first),
        grid=(d // tn, n // tm),
        in_specs=[pl.BlockSpec((tm, d), lambda j, i: (i, 0)),
                  pl.BlockSpec(memory_space=pl.ANY),
                  h_first, h_last],
        out_specs=pl.BlockSpec((tm, tn), lambda j, i: (i, j)),
        out_shape=jax.ShapeDtypeStruct((n, d), F32),
        scratch_shapes=_weight_scratch([(d, tn)]),
        compiler_params=_params(2, 56),
        name="out_proj",
    )(merged, w_out, ha, hb)


def _router_kernel(h_ref, g_ref, wr_ref, br_ref, info_ref, cw_ref, xf_ref, cnt_ref, base_ref):
    tm = h_ref.shape[0]

    @pl.when(pl.program_id(0) == 0)
    def _():
        base_ref[...] = jnp.zeros_like(base_ref)

    xf = _rms_rows(h_ref[...], g_ref[...]).astype(BF16)
    xf_ref[...] = xf
    logits = jnp.dot(xf, wr_ref[...].astype(BF16), preferred_element_type=F32) + br_ref[...]
    lane = lax.broadcasted_iota(I32, logits.shape, 1)
    big = jnp.int32(1 << 20)
    ninf = jnp.float32(-jnp.inf)

    is_g = lane < N_GROUPS
    gl = jnp.where(is_g, logits, ninf)
    gmax = jnp.max(gl, axis=-1, keepdims=True)
    gden = jnp.sum(jnp.exp(gl - gmax), axis=-1, keepdims=True)
    g_p = 1.0 / gden
    g_idx = jnp.min(jnp.where(gl == gmax, lane, big), axis=-1, keepdims=True)

    e_lane = lane - N_GROUPS
    in_grp = (e_lane >= g_idx * EXPERTS_PER_GROUP) & (e_lane < (g_idx + 1) * EXPERTS_PER_GROUP)
    el = jnp.where(in_grp, logits, ninf)
    m1 = jnp.max(el, axis=-1, keepdims=True)
    i1 = jnp.min(jnp.where(in_grp & (el == m1), e_lane, big), axis=-1, keepdims=True)
    el2 = jnp.where(e_lane == i1, ninf, el)
    m2 = jnp.max(el2, axis=-1, keepdims=True)
    i2 = jnp.min(jnp.where(in_grp & (el2 == m2), e_lane, big), axis=-1, keepdims=True)
    t = jnp.exp(m2 - m1)
    den = 1.0 + t
    c1 = g_p * (1.0 / den)
    c2 = g_p * (t / den)
    cw_ref[...] = jnp.where(lane == 0, c1, jnp.where(lane == 1, c2, 0.0))

    hit1, hit2 = lane == i1, lane == i2
    onehot = (hit1 | hit2).astype(BF16)
    earlier = (lax.broadcasted_iota(I32, (tm, tm), 1) < lax.broadcasted_iota(I32, (tm, tm), 0)).astype(BF16)
    before = jnp.dot(earlier, onehot, preferred_element_type=F32) + base_ref[...]
    r1 = jnp.sum(jnp.where(hit1, before, 0.0), axis=-1, keepdims=True).astype(I32)
    r2 = jnp.sum(jnp.where(hit2, before, 0.0), axis=-1, keepdims=True).astype(I32)
    base_ref[...] += jnp.sum(onehot.astype(F32), axis=0, keepdims=True)
    cnt_ref[...] = base_ref[...].astype(I32)
    info_ref[...] = jnp.where(lane == 0, i1, jnp.where(lane == 1, i2,
                              jnp.where(lane == 2, r1, jnp.where(lane == 3, r2, 0))))


def _router(h, g_ffn, w_rg, b_rg, w_re, b_re):
    n, d = h.shape
    n_log = N_GROUPS + N_EXPERTS
    wr = jnp.concatenate([w_rg, w_re, jnp.zeros((d, LANES - n_log), F32)], axis=1)
    br = jnp.concatenate([b_rg, b_re, jnp.zeros((LANES - n_log,), F32)]).reshape(1, LANES)
    lane_spec = pl.BlockSpec((TM_ROW, LANES), lambda i: (i, 0))
    return pl.pallas_call(
        _router_kernel,
        grid=(n // TM_ROW,),
        in_specs=[pl.BlockSpec((TM_ROW, d), lambda i: (i, 0)),
                  pl.BlockSpec((1, d), lambda i: (0, 0)),
                  pl.BlockSpec((d, LANES), lambda i: (0, 0)),
                  pl.BlockSpec((1, LANES), lambda i: (0, 0))],
        out_specs=[lane_spec, lane_spec,
                   pl.BlockSpec((TM_ROW, d), lambda i: (i, 0)),
                   pl.BlockSpec((1, LANES), lambda i: (0, 0))],
        out_shape=[jax.ShapeDtypeStruct((n, LANES), I32), jax.ShapeDtypeStruct((n, LANES), F32),
                   jax.ShapeDtypeStruct((n, d), BF16), jax.ShapeDtypeStruct((1, LANES), I32)],
        scratch_shapes=[pltpu.VMEM((1, LANES), F32)],
        compiler_params=_params(1, 40),
        name="router",
    )(h, g_ffn.reshape(1, d), wr, br)


def _dispatch_plan(info, counts, n_tiles):
    counts = counts[0, :N_EXPERTS]
    tiles_per = (counts + TM_E - 1) // TM_E
    tiles_end = jnp.cumsum(tiles_per)
    row_start = (tiles_end - tiles_per) * TM_E
    eid, rank = info[:, :TOP_K], info[:, TOP_K:2 * TOP_K]
    onehot = (eid[:, :, None] == jnp.arange(N_EXPERTS, dtype=I32)[None, None, :]).astype(I32)
    pos = (jnp.sum(onehot * row_start[None, None, :], axis=-1) + rank).reshape(-1).astype(I32)
    n_used = tiles_end[-1:].astype(I32)
    tile_ids = jnp.minimum(jnp.arange(n_tiles, dtype=I32), n_used - 1)
    tile_expert = jnp.sum((tile_ids[:, None] >= tiles_end[None, :]).astype(I32), axis=1).astype(I32)
    last_tile_row = (jnp.clip(tiles_end - 1, 0, n_tiles - 1) * TM_E).astype(I32)
    starts = jnp.concatenate([jnp.ones((1,), I32), (tile_expert[1:] != tile_expert[:-1]).astype(I32)])
    run_idx = jnp.cumsum(starts) - 1
    n_runs = run_idx[-1:] + 1
    run_expert = jnp.zeros((n_tiles,), I32).at[run_idx].set(tile_expert)
    next_expert = run_expert[(run_idx + 1) % n_runs]
    rows_end = row_start + counts
    valid = jnp.clip(rows_end[tile_expert] - jnp.arange(n_tiles, dtype=I32) * TM_E, 0, TM_E)
    plan = (tile_expert, n_used, run_idx.astype(I32), next_expert.astype(I32), n_runs.astype(I32),
            valid.astype(I32))
    return pos, plan, last_tile_row


def _dispatch_kernel(pos_ref, ltr_ref, nu_ref, xf_ref, xs_hbm, zero_ref, rows_ref, sem, row_sem):
    i = pl.program_id(0)
    n_tiles = xs_hbm.shape[0] // TM_E

    def clear_tile(row):
        return pltpu.make_async_copy(zero_ref, xs_hbm.at[pl.ds(pl.multiple_of(row, TM_E), TM_E)], sem)

    @pl.when(i == 0)
    def _():
        zero_ref[...] = jnp.zeros_like(zero_ref)
        for e in range(N_EXPERTS):
            clear_tile(ltr_ref[e]).start()

        def start_unused(t, carry):
            clear_tile(t * TM_E).start()
            return carry

        def wait_one(t, carry):
            clear_tile(0).wait()
            return carry

        lax.fori_loop(nu_ref[0], n_tiles, start_unused, 0)
        lax.fori_loop(nu_ref[0] - N_EXPERTS, n_tiles, wait_one, 0)

    base = i * TM_DISPATCH
    slot = i % 2
    rows_ref[slot] = xf_ref[...].reshape(rows_ref.shape[1:])

    def issue(r, carry):
        for k in range(TOP_K):
            dst_row = pos_ref[(base + r) * TOP_K + k]
            pltpu.make_async_copy(rows_ref.at[slot, pl.ds(r, 1)], xs_hbm.at[pl.ds(dst_row, 1)],
                                  row_sem.at[slot]).start(priority=k % 2)
        return carry

    lax.fori_loop(0, TM_DISPATCH, issue, 0, unroll=4)

    def wait_rows(s):
        for k in range(TOP_K):
            pltpu.make_async_copy(rows_ref.at[s], xs_hbm.at[pl.ds(0, TM_DISPATCH)], row_sem.at[s]).wait()

    @pl.when(i > 0)
    def _():
        wait_rows(1 - slot)

    @pl.when(i == pl.num_programs(0) - 1)
    def _():
        wait_rows(slot)


def _dispatch(pos, last_tile_row, n_used, xf, n_tiles):
    n, d = xf.shape
    c = d // LANES
    return pl.pallas_call(
        _dispatch_kernel,
        grid_spec=pltpu.PrefetchScalarGridSpec(
            num_scalar_prefetch=3,
            grid=(n // TM_DISPATCH,),
            in_specs=[pl.BlockSpec((TM_DISPATCH, d), lambda i, *_: (i, 0))],
            out_specs=pl.BlockSpec(memory_space=pl.ANY),
            scratch_shapes=[pltpu.VMEM((TM_E, c, LANES), xf.dtype),
                            pltpu.VMEM((2, TM_DISPATCH, c, LANES), xf.dtype),
                            pltpu.SemaphoreType.DMA(()), pltpu.SemaphoreType.DMA((2,))]),
        out_shape=jax.ShapeDtypeStruct((n_tiles * TM_E, c, LANES), xf.dtype),
        compiler_params=_params(1, 32),
        name="moe_dispatch",
    )(pos, last_tile_row, n_used, xf)


def _expert_weights_step(plan_refs, w_hbms, wbuf_ref, wb_refs, sem, tn):
    te_ref, _, run_ref, nxt_ref, nr_ref = plan_refs
    j, t = pl.program_id(0), pl.program_id(1)

    def copies(e, jj):
        col = pl.multiple_of(jj * tn, tn)
        return [pltpu.make_async_copy(w.at[e, :, pl.ds(col, tn)], wbuf_ref.at[m], sem)
                for m, w in enumerate(w_hbms)]

    @pl.when((t == 0) | (te_ref[t] != te_ref[jnp.maximum(t - 1, 0)]))
    def _():
        @pl.when((j == 0) & (t == 0))
        def _():
            for c in copies(te_ref[t], j):
                c.start()

        for c in copies(te_ref[t], j):
            c.wait()
        for m, wb_ref in enumerate(wb_refs):
            _cast_rows(wbuf_ref.at[m], wb_ref)

        next_j = jnp.where(run_ref[t] + 1 == nr_ref[0], j + 1, j)

        @pl.when(next_j < pl.num_programs(0))
        def _():
            for c in copies(nxt_ref[t], next_j):
                c.start()


def _by_valid_rows(valid, full_fn, zero_fn):
    half = TM_E // 2

    @pl.when(valid > half)
    def _():
        full_fn(TM_E)

    @pl.when((valid > 0) & (valid <= half))
    def _():
        full_fn(half)
        zero_fn(half)

    @pl.when(valid == 0)
    def _():
        zero_fn(0)


def _moe_up_kernel(te_ref, nu_ref, run_ref, nxt_ref, nr_ref, valid_ref, xs_ref, wg_hbm, wu_hbm, o_ref,
                   wbuf_ref, wgb_ref, wub_ref, sem):
    t = pl.program_id(1)
    _expert_weights_step((te_ref, nu_ref, run_ref, nxt_ref, nr_ref), (wg_hbm, wu_hbm), wbuf_ref,
                         (wgb_ref, wub_ref), sem, o_ref.shape[1])

    def compute(rows):
        x = xs_ref[0:rows].reshape(rows, -1)
        gate = jnp.dot(x, wgb_ref[...], preferred_element_type=F32)
        up = jnp.dot(x, wub_ref[...], preferred_element_type=F32)
        o_ref[0:rows] = (jax.nn.silu(gate) * up).astype(o_ref.dtype)

    def clear(row0):
        o_ref[row0:] = jnp.zeros((TM_E - row0,) + o_ref.shape[1:], o_ref.dtype)

    _by_valid_rows(valid_ref[t], compute, clear)


def _moe_up(plan, xs3, w_gate, w_up, n_tiles):
    d, f = w_gate.shape[1], w_gate.shape[2]
    tf = TN
    any_spec = pl.BlockSpec(memory_space=pl.ANY)
    return pl.pallas_call(
        _moe_up_kernel,
        grid_spec=pltpu.PrefetchScalarGridSpec(
            num_scalar_prefetch=len(plan),
            grid=(f // tf, n_tiles),
            in_specs=[pl.BlockSpec((TM_E,) + xs3.shape[1:], lambda j, t, te, nu, *_: (jnp.minimum(t, nu[0] - 1), 0, 0)),
                      any_spec, any_spec],
            out_specs=pl.BlockSpec((TM_E, tf), lambda j, t, *_: (t, j)),
            scratch_shapes=[pltpu.VMEM((2, d, tf), F32), pltpu.VMEM((d, tf), BF16), pltpu.VMEM((d, tf), BF16),
                            pltpu.SemaphoreType.DMA(())]),
        out_shape=jax.ShapeDtypeStruct((n_tiles * TM_E, f), BF16),
        compiler_params=_params(2, 56),
        name="moe_up",
    )(*plan, xs3, w_gate, w_up)


def _moe_down_kernel(te_ref, nu_ref, run_ref, nxt_ref, nr_ref, valid_ref, hid_ref, wd_hbm, o_ref,
                     wbuf_ref, wdb_ref, sem):
    t = pl.program_id(1)
    _expert_weights_step((te_ref, nu_ref, run_ref, nxt_ref, nr_ref), (wd_hbm,), wbuf_ref, (wdb_ref,), sem,
                         wdb_ref.shape[1])

    def compute(rows):
        y = jnp.dot(hid_ref[0:rows], wdb_ref[...], preferred_element_type=F32)
        o_ref[0:rows] = y.reshape((rows,) + o_ref.shape[1:])

    def clear(row0):
        o_ref[row0:] = jnp.zeros((TM_E - row0,) + o_ref.shape[1:], o_ref.dtype)

    _by_valid_rows(valid_ref[t], compute, clear)


def _moe_down(plan, hid, w_down, n_tiles):
    f, d = w_down.shape[1], w_down.shape[2]
    tn = 4 * TN
    return pl.pallas_call(
        _moe_down_kernel,
        grid_spec=pltpu.PrefetchScalarGridSpec(
            num_scalar_prefetch=len(plan),
            grid=(d // tn, n_tiles),
            in_specs=[pl.BlockSpec((TM_E, f), lambda j, t, *_: (t, 0)),
                      pl.BlockSpec(memory_space=pl.ANY)],
            out_specs=pl.BlockSpec((TM_E, tn // LANES, LANES), lambda j, t, *_: (t, j, 0)),
            scratch_shapes=[pltpu.VMEM((1, f, tn), F32), pltpu.VMEM((f, tn), BF16),
                            pltpu.SemaphoreType.DMA(())]),
        out_shape=jax.ShapeDtypeStruct((n_tiles * TM_E, d // LANES, LANES), F32),
        compiler_params=_params(2, 40),
        name="moe_down",
    )(*plan, hid, w_down)


def _combine_kernel(pos_ref, h_ref, cw_ref, y_hbm, g_ref, ho_ref, xo_ref, buf_ref, sem):
    i = pl.program_id(0)
    tm, d = h_ref.shape

    def row_copy(slot, k, r, src_row):
        return pltpu.make_async_copy(y_hbm.at[pl.ds(src_row, 1)], buf_ref.at[slot, k, pl.ds(r, 1)], sem.at[slot])

    def issue_step(step, slot):
        def body(r, carry):
            for k in range(TOP_K):
                row_copy(slot, k, r, pos_ref[(step * tm + r) * TOP_K + k]).start(priority=k % 2)
            return carry
        lax.fori_loop(0, tm, body, 0, unroll=4)

    @pl.when(i == 0)
    def _():
        issue_step(0, 0)

    @pl.when(i + 1 < pl.num_programs(0))
    def _():
        issue_step(i + 1, (i + 1) % 2)

    slot = i % 2
    for k in range(TOP_K):
        pltpu.make_async_copy(y_hbm.at[pl.ds(0, tm)], buf_ref.at[slot, k], sem.at[slot]).wait()
    cw = cw_ref[...]
    y1 = buf_ref[slot, 0].reshape(tm, d)
    y2 = buf_ref[slot, 1].reshape(tm, d)
    h2 = h_ref[...] + (cw[:, 0:1] * y1 + cw[:, 1:2] * y2)
    ho_ref[...] = h2
    xo_ref[...] = _rms_rows(h2, g_ref[...]).astype(xo_ref.dtype)


def _combine(pos, h, cw, y3, g_ple):
    n, d = h.shape
    row_spec = pl.BlockSpec((TM_ROW, d), lambda i, p: (i, 0))
    return pl.pallas_call(
        _combine_kernel,
        grid_spec=pltpu.PrefetchScalarGridSpec(
            num_scalar_prefetch=1,
            grid=(n // TM_ROW,),
            in_specs=[row_spec,
                      pl.BlockSpec((TM_ROW, LANES), lambda i, p: (i, 0)),
                      pl.BlockSpec(memory_space=pl.ANY),
                      pl.BlockSpec((1, d), lambda i, p: (0, 0))],
            out_specs=[row_spec, row_spec],
            scratch_shapes=[pltpu.VMEM((2, TOP_K, TM_ROW) + y3.shape[1:], F32), pltpu.SemaphoreType.DMA((2,))]),
        out_shape=[jax.ShapeDtypeStruct((n, d), F32), jax.ShapeDtypeStruct((n, d), BF16)],
        compiler_params=_params(1, 56),
        name="moe_combine",
    )(pos, h, cw, y3, g_ple.reshape(1, d))


def _ple_kernel(xn_ref, wg_hbm, bg_ref, pa_ref, pb_ref, wp_hbm, h_ref, o_ref,
                wgf_ref, wpf_ref, wgb_ref, wpb_ref, sem, *, nb_first):
    _column_weights_step((wg_hbm, wp_hbm), (0, 0), (wgf_ref, wpf_ref), (wgb_ref, wpb_ref), sem)
    gate = jax.nn.sigmoid(jnp.dot(xn_ref[...], wgb_ref[...], preferred_element_type=F32) + bg_ref[...])
    p = jnp.where(pl.program_id(1) < nb_first, pa_ref[...], pb_ref[...]).astype(BF16)
    ple = jnp.dot(p, wpb_ref[...], preferred_element_type=F32)
    o_ref[...] = h_ref[...] + gate * ple


def _ple(xn, w_gate, b_gate, pa, pb, w_proj, h):
    n, d = h.shape
    pd = pa.shape[1]
    tm, tn = TM, TN
    nb_first = pa.shape[0] // tm
    p_first = pl.BlockSpec((tm, pd), lambda j, i: (jnp.minimum(i, nb_first - 1), 0))
    p_last = pl.BlockSpec((tm, pd), lambda j, i: (jnp.maximum(i - nb_first, 0), 0))
    any_spec = pl.BlockSpec(memory_space=pl.ANY)
    return pl.pallas_call(
        functools.partial(_ple_kernel, nb_first=nb_first),
        grid=(d // tn, n // tm),
        in_specs=[pl.BlockSpec((tm, d), lambda j, i: (i, 0)),
                  any_spec,
                  pl.BlockSpec((1, tn), lambda j, i: (0, j)),
                  p_first, p_last,
                  any_spec,
                  pl.BlockSpec((tm, tn), lambda j, i: (i, j))],
        out_specs=pl.BlockSpec((tm, tn), lambda j, i: (i, j)),
        out_shape=jax.ShapeDtypeStruct((n, d), F32),
        scratch_shapes=_weight_scratch([(d, tn), (pd, tn)]),
        compiler_params=_params(2, 56),
        name="ple",
    )(xn, w_gate, b_gate.reshape(1, d), pa, pb, w_proj, h)


def _layer(xn, h_p, h_s, p_p, p_s, cache_k, cache_v, state_pool, lw):
    n, d = xn.shape
    n_prompt = h_p.shape[0]
    n_seq, n_cache, n_kv, _ = cache_k.shape
    t_new = (n - n_prompt) // n_seq
    kv_w = n_kv * HEAD_DIM
    attn_w = lw['w_attn_up'].shape[0]
    pool_w = lw['w_pool_up'].shape[0]
    n_pool = state_pool.shape[1]

    qkv = _in_proj(xn, lw['w_in'], 0, attn_w + 2 * kv_w, "in_proj_qkv")
    z = _in_proj(xn, lw['w_in'], attn_w + 2 * kv_w, pool_w, "in_proj_pool")
    gates = _in_proj(xn, lw['w_in'], attn_w + 2 * kv_w + pool_w, 2 * d, "in_proj_gates")

    tab_p = _rope_tables(jnp.arange(n_prompt, dtype=F32))
    tab_s = tuple(jnp.tile(a, (SEQS_PER_STEP, 1))
                  for a in _rope_tables(PAST_LEN + jnp.arange(t_new, dtype=F32)))
    tab_p_t = _rope_tables_t(jnp.arange(n_prompt, dtype=F32))
    attn, k_rot_p = _attn_prompt(qkv, lw['attn_sinks'], tab_p, tab_p_t, n_prompt, attn_w, kv_w)
    attn, new_k_s, new_v_s = _attn_sample(
        qkv, attn, n_prompt, cache_k.reshape(n_seq, n_cache, kv_w), cache_v.reshape(n_seq, n_cache, kv_w),
        lw['attn_sinks'], tab_s, n_seq, t_new, attn_w, kv_w)

    mixed = _pool_prompt(z, n_prompt, lw['w_pool_grp'], lw['pool_scale'])
    mixed, new_z_s = _pool_sample(state_pool, z, mixed, n_prompt, lw['w_pool_grp'], lw['pool_scale'], t_new)

    merged = _merge(attn, mixed, lw['w_attn_up'], lw['w_pool_up'], gates, lw['b_gate'].reshape(1, 2 * d))
    h1 = _out_proj(merged, lw['w_out'], h_p, h_s)

    n_tiles = (n * TOP_K) // TM_E + N_EXPERTS
    info, cw, xf3, counts = _router(h1, lw['g_ffn'], lw['w_route_group'], lw['b_route_group'],
                                    lw['w_route_expert'], lw['b_route_expert'])
    pos, plan, last_tile_row = _dispatch_plan(info, counts, n_tiles)
    xs3 = _dispatch(pos, last_tile_row, plan[1], xf3, n_tiles)
    hid = _moe_up(plan, xs3, lw['w_exp_gate'], lw['w_exp_up'], n_tiles)
    y3 = _moe_down(plan, hid, lw['w_exp_down'], n_tiles)
    h2, xn2 = _combine(pos, h1, cw, y3, lw['g_ple'])

    h3 = _ple(xn2, lw['w_ple_gate'], lw['b_ple_gate'], p_p, p_s, lw['w_ple_proj'], h2)

    new_k_p = k_rot_p[n_prompt - n_cache:].reshape(1, n_cache, n_kv, HEAD_DIM)
    new_v_p = qkv[n_prompt - n_cache:n_prompt, attn_w + kv_w:].reshape(1, n_cache, n_kv, HEAD_DIM)
    new_z_p = z[n_prompt - n_pool:n_prompt].reshape(1, n_pool, pool_w)
    return (h3, new_k_p, new_v_p, new_z_p,
            new_k_s.reshape(cache_k.shape), new_v_s.reshape(cache_v.shape), new_z_s)


def kernel(x_prompt, x_sample, cache_k, cache_v, state_pool, p_prompt, p_sample, g_mix, w_in, b_gate, attn_sinks, w_pool_grp, pool_scale, w_attn_up, w_pool_up, w_out, g_ffn, w_route_group, b_route_group, w_route_expert, b_route_expert, w_exp_gate, w_exp_up, w_exp_down, g_ple, w_ple_gate, b_ple_gate, w_ple_proj, g_final):
    batch, seq, d = x_prompt.shape
    n_seq, t_new, _ = x_sample.shape
    depth = w_in.shape[0]
    assert batch == 1, "prompt rows are treated as one sequence"
    n_prompt = batch * seq
    n_sample = n_seq * t_new
    n = n_prompt + n_sample
    assert n_prompt % TM == 0 and n_sample % TM == 0 and (n * TOP_K) % TM_E == 0 and n % TM_DISPATCH == 0
    assert n_seq % SEQS_PER_STEP == 0 and n_seq % POOL_SEQS == 0 and t_new & (t_new - 1) == 0
    assert cache_k.shape[2] == WINDOW and state_pool.shape[2] == POOL_HALO - 1
    assert depth == 1, "one layer per step"

    weights = dict(g_mix=g_mix, w_in=w_in, b_gate=b_gate, attn_sinks=attn_sinks, w_pool_grp=w_pool_grp,
                   pool_scale=pool_scale, w_attn_up=w_attn_up, w_pool_up=w_pool_up, w_out=w_out, g_ffn=g_ffn,
                   w_route_group=w_route_group, b_route_group=b_route_group, w_route_expert=w_route_expert,
                   b_route_expert=b_route_expert, w_exp_gate=w_exp_gate, w_exp_up=w_exp_up,
                   w_exp_down=w_exp_down, g_ple=g_ple, w_ple_gate=w_ple_gate, b_ple_gate=b_ple_gate,
                   w_ple_proj=w_ple_proj)
    lw = {name: w[0] for name, w in weights.items()}
    h_p = x_prompt.reshape(n_prompt, d)
    h_s = x_sample.reshape(n_sample, d)
    xn = _rmsnorm_in2(h_p, h_s, lw['g_mix'], BF16, "norm_mix")
    out = _layer(xn, h_p, h_s, p_prompt[0].reshape(n_prompt, -1), p_sample[0].reshape(n_sample, -1),
                 cache_k[0], cache_v[0], state_pool[0], lw)
    y_p, y_s = _rmsnorm_out2(out[0], g_final, n_prompt, "norm_final")
    return (y_p.reshape(batch, seq, d), y_s.reshape(n_seq, t_new, d)) + tuple(piece[None] for piece in out[1:])
```

```python
import functools

import jax
import jax.numpy as jnp
from jax import lax
from jax.experimental import pallas as pl
from jax.experimental.pallas import tpu as pltpu

F32 = jnp.float32
BF16 = jnp.bfloat16
I32 = jnp.int32

HEAD_DIM = 64
Q_PER_KV = 4
WINDOW = 128
PAST_LEN = 8192
ROT_DIM = HEAD_DIM // 4
ROPE_THETA = 500000.0
POOL_WINDOWS = (2, 4, 8, 16)
POOL_HALO = 16
N_GROUPS = 4
EXPERTS_PER_GROUP = 4
N_EXPERTS = N_GROUPS * EXPERTS_PER_GROUP
TOP_K = 2
RMS_EPS = 1e-6
NEG_INF = -1e30
LANES = 128
SUBLANES = 8
MIB = 1024 * 1024

TM = 1024
TN = 512
TN_PROJ = 1024
CAST_ROWS = 512
TM_WIDE = 512
TN_WIDE = 1024
TM_ROW = 256
TM_E = 512
TM_DISPATCH = 512
SEQS_PER_STEP = 8
POOL_SEQS = 16


def _params(n_axes, vmem_mib):
    return pltpu.CompilerParams(dimension_semantics=("arbitrary",) * n_axes,
                                vmem_limit_bytes=vmem_mib * MIB)


def _rms_rows(x, g):
    ms = jnp.mean(x * x, axis=-1, keepdims=True)
    return x * lax.rsqrt(ms + RMS_EPS) * g


def _first_last_specs(nb_first, block):
    zeros = (0,) * (len(block) - 1)
    first = pl.BlockSpec(block, lambda i, *_: (jnp.minimum(i, nb_first - 1),) + zeros)
    last = pl.BlockSpec(block, lambda i, *_: (jnp.maximum(i - nb_first, 0),) + zeros)
    return first, last


def _norm_in2_kernel(xa_ref, xb_ref, g_ref, o_ref, *, nb_first):
    i = pl.program_id(0)

    @pl.when(i < nb_first)
    def _():
        o_ref[...] = _rms_rows(xa_ref[...], g_ref[...]).astype(o_ref.dtype)

    @pl.when(i >= nb_first)
    def _():
        o_ref[...] = _rms_rows(xb_ref[...], g_ref[...]).astype(o_ref.dtype)


def _rmsnorm_in2(xa, xb, g, out_dtype, name):
    (na, d), nb = xa.shape, xb.shape[0]
    nb_first = na // TM_ROW
    spec_a, spec_b = _first_last_specs(nb_first, (TM_ROW, d))
    return pl.pallas_call(
        functools.partial(_norm_in2_kernel, nb_first=nb_first),
        grid=((na + nb) // TM_ROW,),
        in_specs=[spec_a, spec_b, pl.BlockSpec((1, d), lambda i: (0, 0))],
        out_specs=pl.BlockSpec((TM_ROW, d), lambda i: (i, 0)),
        out_shape=jax.ShapeDtypeStruct((na + nb, d), out_dtype),
        compiler_params=_params(1, 40),
        name=name,
    )(xa, xb, g.reshape(1, d))


def _norm_out2_kernel(x_ref, g_ref, oa_ref, ob_ref, *, nb_first):
    i = pl.program_id(0)

    @pl.when(i < nb_first)
    def _():
        oa_ref[...] = _rms_rows(x_ref[...], g_ref[...])

    @pl.when(i >= nb_first)
    def _():
        ob_ref[...] = _rms_rows(x_ref[...], g_ref[...])


def _rmsnorm_out2(x, g, na, name):
    n, d = x.shape
    nb_first = na // TM_ROW
    spec_a, spec_b = _first_last_specs(nb_first, (TM_ROW, d))
    return pl.pallas_call(
        functools.partial(_norm_out2_kernel, nb_first=nb_first),
        grid=(n // TM_ROW,),
        in_specs=[pl.BlockSpec((TM_ROW, d), lambda i: (i, 0)), pl.BlockSpec((1, d), lambda i: (0, 0))],
        out_specs=[spec_a, spec_b],
        out_shape=[jax.ShapeDtypeStruct((na, d), F32), jax.ShapeDtypeStruct((n - na, d), F32)],
        compiler_params=_params(1, 40),
        name=name,
    )(x, g.reshape(1, d))


def _cast_rows(src_ref, dst_ref):
    rows = src_ref.shape[0]
    step = min(rows, CAST_ROWS)
    for r in range(0, rows, step):
        dst_ref[r:r + step] = src_ref[r:r + step].astype(dst_ref.dtype)


def _column_weights_step(w_hbms, col0s, wbuf_refs, wb_refs, sem):
    j, i = pl.program_id(0), pl.program_id(1)

    def copies(jj):
        out = []
        for w, col0, buf in zip(w_hbms, col0s, wbuf_refs):
            tn = buf.shape[1]
            out.append(pltpu.make_async_copy(w.at[:, pl.ds(pl.multiple_of(col0 + jj * tn, tn), tn)], buf, sem))
        return out

    @pl.when(i == 0)
    def _():
        @pl.when(j == 0)
        def _():
            for c in copies(j):
                c.start()

        for c in copies(j):
            c.wait()
        for buf, wb_ref in zip(wbuf_refs, wb_refs):
            _cast_rows(buf, wb_ref)

        @pl.when(j + 1 < pl.num_programs(0))
        def _():
            for c in copies(j + 1):
                c.start()


def _weight_scratch(shapes):
    return ([pltpu.VMEM(s, F32) for s in shapes] + [pltpu.VMEM(s, BF16) for s in shapes]
            + [pltpu.SemaphoreType.DMA(())])


def _proj_kernel(a_ref, w_hbm, o_ref, wbuf_ref, wb_ref, sem, *, col0):
    _column_weights_step((w_hbm,), (col0,), (wbuf_ref,), (wb_ref,), sem)
    o_ref[...] = jnp.dot(a_ref[...], wb_ref[...], preferred_element_type=F32)


def _in_proj(xn, w, col0, width, name):
    n, d = xn.shape
    tn = TN_PROJ
    assert col0 % tn == 0 and width % tn == 0
    return pl.pallas_call(
        functools.partial(_proj_kernel, col0=col0),
        grid=(width // tn, n // TM),
        in_specs=[pl.BlockSpec((TM, d), lambda j, i: (i, 0)),
                  pl.BlockSpec(memory_space=pl.ANY)],
        out_specs=pl.BlockSpec((TM, tn), lambda j, i: (i, j)),
        out_shape=jax.ShapeDtypeStruct((n, width), F32),
        scratch_shapes=_weight_scratch([(d, tn)]),
        compiler_params=_params(2, 56),
        name=name,
    )(xn, w)


def _rope_tables(pos):
    half = ROT_DIM // 2
    inv_freq = ROPE_THETA ** (-jnp.arange(half, dtype=F32) * (2.0 / ROT_DIM))
    ang = pos[:, None] * inv_freq[None, :]
    cos, sin = jnp.cos(ang), jnp.sin(ang)
    t = pos.shape[0]
    pad = jnp.zeros((t, HEAD_DIM - ROT_DIM), F32)
    zeros = jnp.zeros((t, half), F32)
    cos_h = jnp.concatenate([cos, cos, pad + 1.0], axis=1)
    sa_h = jnp.concatenate([zeros, sin, pad], axis=1)
    sb_h = jnp.concatenate([-sin, zeros, pad], axis=1)
    reps = LANES // HEAD_DIM
    return tuple(jnp.tile(a, (1, reps)) for a in (cos_h, sa_h, sb_h))


def _rope_tables_t(pos):
    half = ROT_DIM // 2
    inv_freq = ROPE_THETA ** (-jnp.arange(half, dtype=F32) * (2.0 / ROT_DIM))
    ang = pos[:, None] * inv_freq[None, :]
    return jnp.cos(ang).T, jnp.sin(ang).T


def _rope(x, cos_t, sa_t, sb_t):
    pieces = []
    for c in range(x.shape[1] // LANES):
        xc = x[:, c * LANES:(c + 1) * LANES]
        pieces.append(xc * cos_t
                      + pltpu.roll(xc, ROT_DIM // 2, 1) * sa_t
                      + pltpu.roll(xc, LANES - ROT_DIM // 2, 1) * sb_t)
    return jnp.concatenate(pieces, axis=1)


def _softmax_with_sink(s, sink):
    m = jnp.maximum(jnp.max(s, axis=-1, keepdims=True), sink)
    p = jnp.exp(s - m)
    den = jnp.sum(p, axis=-1, keepdims=True) + jnp.exp(sink - m)
    return p / den


def _attn_prompt_kernel(sink_ref, q_ref, k_ref, v_ref, cos_ref, sa_ref, sb_ref, cost_ref, sint_ref,
                        o_ref, ko_ref, kprev_ref, vprevt_ref, *, n_kv, nb):
    b = pl.program_id(0)

    @pl.when(b == 0)
    def _():
        kprev_ref[...] = jnp.zeros_like(kprev_ref)
        vprevt_ref[...] = jnp.zeros_like(vprevt_ref)

    @pl.when(b >= nb)
    def _():
        o_ref[...] = jnp.zeros_like(o_ref)

    @pl.when(b < nb)
    def _():
        _attn_prompt_block(sink_ref, q_ref, k_ref, v_ref, cos_ref, sa_ref, sb_ref, cost_ref, sint_ref,
                           o_ref, ko_ref, kprev_ref, vprevt_ref, n_kv=n_kv)


def _attn_prompt_block(sink_ref, q_ref, k_ref, v_ref, cos_ref, sa_ref, sb_ref, cost_ref, sint_ref,
                       o_ref, ko_ref, kprev_ref, vprevt_ref, *, n_kv):
    b = pl.program_id(0)
    half = ROT_DIM // 2
    k = _rope(k_ref[...], cos_ref[...], sa_ref[...], sb_ref[...])
    ko_ref[...] = k
    kcat = jnp.concatenate([kprev_ref[...], k], axis=0).astype(BF16)
    vt = v_ref[...].T
    vcat_t = jnp.concatenate([vprevt_ref[...], vt], axis=1).astype(BF16)
    kprev_ref[...] = k
    vprevt_ref[...] = vt

    qt = q_ref[...].T
    cos_t, sin_t = cost_ref[...], sint_ref[...]

    def roped_head_t(h):
        x = qt[h * HEAD_DIM:(h + 1) * HEAD_DIM]
        x1, x2 = x[0:half], x[half:ROT_DIM]
        rot = jnp.concatenate([x1 * cos_t - x2 * sin_t, x2 * cos_t + x1 * sin_t, x[ROT_DIM:]], axis=0)
        return (rot * (HEAD_DIM ** -0.5)).astype(BF16)

    n_heads = n_kv * Q_PER_KV
    zeros_head = jnp.zeros((HEAD_DIM, WINDOW), BF16)
    heads_per_tile = LANES // HEAD_DIM

    key = lax.broadcasted_iota(I32, (2 * WINDOW, Q_PER_KV * WINDOW), 0)
    qry = lax.broadcasted_iota(I32, (2 * WINDOW, Q_PER_KV * WINDOW), 1) & (WINDOW - 1)
    first_key = jnp.where(b > 0, 0, WINDOW)
    mask = (key > qry) & (key <= qry + WINDOW) & (key >= first_key)

    scores = []
    for g in range(n_kv):
        tile, slot = g // heads_per_tile, g % heads_per_tile
        k2 = kcat[:, tile * LANES:(tile + 1) * LANES]
        cols = []
        for r in range(Q_PER_KV):
            parts = [zeros_head] * heads_per_tile
            parts[slot] = roped_head_t(g * Q_PER_KV + r)
            cols.append(jnp.concatenate(parts, axis=0))
        rhs = jnp.concatenate(cols, axis=1)
        scores.append(jnp.where(mask, jnp.dot(k2, rhs, preferred_element_type=F32), NEG_INF))
    st = jnp.concatenate(scores, axis=1)
    sink = jnp.concatenate([jnp.full((1, WINDOW), sink_ref[h], F32) for h in range(n_heads)], axis=1)
    m = jnp.maximum(jnp.max(st, axis=0, keepdims=True), sink)
    p = jnp.exp(st - m)
    den = jnp.sum(p, axis=0, keepdims=True) + jnp.exp(sink - m)
    probs_t = (p * (1.0 / den)).astype(BF16)

    for g in range(n_kv):
        ot = jnp.dot(vcat_t[g * HEAD_DIM:(g + 1) * HEAD_DIM],
                     probs_t[:, g * Q_PER_KV * WINDOW:(g + 1) * Q_PER_KV * WINDOW],
                     preferred_element_type=F32)
        for pr in range(Q_PER_KV // heads_per_tile):
            pair = jnp.concatenate(
                [ot[:, (pr * heads_per_tile + u) * WINDOW:(pr * heads_per_tile + u + 1) * WINDOW]
                 for u in range(heads_per_tile)], axis=0)
            c = (g * Q_PER_KV) // heads_per_tile + pr
            o_ref[:, c * LANES:(c + 1) * LANES] = pair.T.astype(o_ref.dtype)


def _attn_prompt(qkv, sinks, tables, tables_t, n_rows, attn_w, kv_w):
    nb = n_rows // WINDOW
    kb, vb = attn_w // kv_w, attn_w // kv_w + 1

    def blk(b, s):
        return jnp.minimum(b, nb - 1)

    tab_spec = pl.BlockSpec((WINDOW, LANES), lambda b, s: (blk(b, s), 0))
    tab_t_spec = pl.BlockSpec((ROT_DIM // 2, WINDOW), lambda b, s: (0, blk(b, s)))
    return pl.pallas_call(
        functools.partial(_attn_prompt_kernel, n_kv=kv_w // HEAD_DIM, nb=nb),
        grid_spec=pltpu.PrefetchScalarGridSpec(
            num_scalar_prefetch=1,
            grid=(qkv.shape[0] // WINDOW,),
            in_specs=[pl.BlockSpec((WINDOW, attn_w), lambda b, s: (blk(b, s), 0)),
                      pl.BlockSpec((WINDOW, kv_w), lambda b, s: (blk(b, s), kb)),
                      pl.BlockSpec((WINDOW, kv_w), lambda b, s: (blk(b, s), vb)),
                      tab_spec, tab_spec, tab_spec, tab_t_spec, tab_t_spec],
            out_specs=[pl.BlockSpec((WINDOW, attn_w), lambda b, s: (b, 0)),
                       pl.BlockSpec((WINDOW, kv_w), lambda b, s: (blk(b, s), 0))],
            scratch_shapes=[pltpu.VMEM((WINDOW, kv_w), F32), pltpu.VMEM((kv_w, WINDOW), F32)]),
        out_shape=[jax.ShapeDtypeStruct((qkv.shape[0], attn_w), BF16),
                   jax.ShapeDtypeStruct((n_rows, kv_w), F32)],
        compiler_params=_params(1, 32),
        name="attn_prompt",
    )(sinks, qkv, qkv, qkv, *tables, *tables_t)


def _attn_sample_kernel(sink_ref, q_ref, k_ref, v_ref, ck_ref, cv_ref, cos_ref, sa_ref, sb_ref, attn_in_ref,
                        o_ref, nk_ref, nv_ref, *, n_kv, t_new):
    del attn_in_ref
    assert LANES // HEAD_DIM == 2 and Q_PER_KV % 2 == 0
    cos_t, sa_t, sb_t = cos_ref[...], sa_ref[...], sb_ref[...]
    q = _rope(q_ref[...], cos_t, sa_t, sb_t) * (HEAD_DIM ** -0.5)
    k = _rope(k_ref[...], cos_t, sa_t, sb_t)
    v = v_ref[...]
    n_cache = ck_ref.shape[1]
    n_keys = 2 * WINDOW
    n_seq = q.shape[0] // t_new
    n_heads = n_kv * Q_PER_KV
    zpad = jnp.zeros((n_keys - n_cache - t_new, k.shape[1]), F32)
    low = lax.broadcasted_iota(I32, (t_new, LANES), 1) < HEAD_DIM
    high = jnp.logical_not(low)

    qt = lax.broadcasted_iota(I32, (Q_PER_KV * t_new, n_keys), 0) & (t_new - 1)
    kj = lax.broadcasted_iota(I32, (Q_PER_KV * t_new, n_keys), 1)
    mask = (kj > qt + (n_cache - WINDOW)) & (kj <= qt + n_cache)

    def swap_halves(x):
        return pltpu.roll(x, HEAD_DIM, 1)

    scores, values = [], []
    for s_i in range(n_seq):
        ck, cv = ck_ref[s_i], cv_ref[s_i]
        kn, vn = k[s_i * t_new:(s_i + 1) * t_new], v[s_i * t_new:(s_i + 1) * t_new]
        nk_ref[s_i] = jnp.concatenate([ck[t_new:], kn], axis=0)
        nv_ref[s_i] = jnp.concatenate([cv[t_new:], vn], axis=0)
        kall = jnp.concatenate([ck, kn, zpad], axis=0).astype(BF16)
        values.append(jnp.concatenate([cv, vn, zpad], axis=0).astype(BF16))
        qs = q[s_i * t_new:(s_i + 1) * t_new]
        for g in range(n_kv):
            tile, odd = g // 2, g % 2
            parts = []
            for r in range(Q_PER_KV):
                h = g * Q_PER_KV + r
                x = qs[:, (h // 2) * LANES:(h // 2 + 1) * LANES]
                if h % 2 != odd:
                    x = swap_halves(x)
                parts.append(jnp.where(high if odd else low, x, 0.0))
            qg = jnp.concatenate(parts, axis=0).astype(BF16)
            k2 = kall[:, tile * LANES:(tile + 1) * LANES]
            sg = lax.dot_general(qg, k2, (((1,), (1,)), ((), ())), preferred_element_type=F32)
            scores.append(jnp.where(mask, sg, NEG_INF))
    s = jnp.concatenate(scores, axis=0)
    sink_seq = jnp.concatenate([jnp.full((t_new, 1), sink_ref[h], F32) for h in range(n_heads)], axis=0)
    sink = jnp.concatenate([sink_seq] * n_seq, axis=0)
    probs = _softmax_with_sink(s, sink).astype(BF16)

    seq_outs = []
    for s_i in range(n_seq):
        out_tiles = []
        for g in range(n_kv):
            tile, odd = g // 2, g % 2
            r0 = (s_i * n_kv + g) * Q_PER_KV * t_new
            v2 = values[s_i][:, tile * LANES:(tile + 1) * LANES]
            og = jnp.dot(probs[r0:r0 + Q_PER_KV * t_new], v2, preferred_element_type=F32)
            for pr in range(Q_PER_KV // 2):
                even = og[(2 * pr) * t_new:(2 * pr + 1) * t_new]
                oddh = og[(2 * pr + 1) * t_new:(2 * pr + 2) * t_new]
                if odd:
                    even = swap_halves(even)
                else:
                    oddh = swap_halves(oddh)
                out_tiles.append(jnp.where(low, even, oddh))
        seq_outs.append(jnp.concatenate(out_tiles, axis=1))
    o_ref[...] = jnp.concatenate(seq_outs, axis=0).astype(o_ref.dtype)


def _attn_sample(qkv, attn, row0, cache_k, cache_v, sinks, tables, n_seq, t_new, attn_w, kv_w):
    rows = SEQS_PER_STEP * t_new
    rb0 = row0 // rows
    kb, vb = attn_w // kv_w, attn_w // kv_w + 1
    n_cache = cache_k.shape[1]
    tab_spec = pl.BlockSpec((rows, LANES), lambda i, s: (0, 0))
    cache_spec = pl.BlockSpec((SEQS_PER_STEP, n_cache, kv_w), lambda i, s: (i, 0, 0))
    return pl.pallas_call(
        functools.partial(_attn_sample_kernel, n_kv=kv_w // HEAD_DIM, t_new=t_new),
        grid_spec=pltpu.PrefetchScalarGridSpec(
            num_scalar_prefetch=1,
            grid=(n_seq // SEQS_PER_STEP,),
            in_specs=[pl.BlockSpec((rows, attn_w), lambda i, s: (rb0 + i, 0)),
                      pl.BlockSpec((rows, kv_w), lambda i, s: (rb0 + i, kb)),
                      pl.BlockSpec((rows, kv_w), lambda i, s: (rb0 + i, vb)),
                      cache_spec, cache_spec, tab_spec, tab_spec, tab_spec,
                      pl.BlockSpec(memory_space=pl.ANY)],
            out_specs=[pl.BlockSpec((rows, attn_w), lambda i, s: (rb0 + i, 0)), cache_spec, cache_spec]),
        out_shape=[jax.ShapeDtypeStruct(attn.shape, attn.dtype),
                   jax.ShapeDtypeStruct(cache_k.shape, F32),
                   jax.ShapeDtypeStruct(cache_v.shape, F32)],
        input_output_aliases={9: 0},
        compiler_params=_params(1, 32),
        name="attn_sample",
    )(sinks, qkv, qkv, qkv, cache_k, cache_v, *tables, attn)


def _pool_prompt_kernel(z_ref, halo_ref, w_ref, sc_ref, o_ref, ext_ref, *, tm, pg, nb):
    i = pl.program_id(0)

    @pl.when(i >= nb)
    def _():
        o_ref[...] = jnp.zeros_like(o_ref)

    @pl.when(i < nb)
    def _():
        _pool_prompt_block(z_ref, halo_ref, w_ref, sc_ref, o_ref, ext_ref, tm=tm, pg=pg)


def _pool_prompt_block(z_ref, halo_ref, w_ref, sc_ref, o_ref, ext_ref, *, tm, pg):
    i = pl.program_id(0)
    ext_ref[0:POOL_HALO, :] = jnp.where(i > 0, halo_ref[...], 0.0)
    ext_ref[POOL_HALO:, :] = z_ref[...]
    pos = i * tm + lax.broadcasted_iota(I32, (tm, 1), 0)
    for g, w in enumerate(POOL_WINDOWS):
        c0, c1 = g * pg, (g + 1) * pg
        cur = ext_ref[POOL_HALO:POOL_HALO + tm, c0:c1]
        acc = cur
        for d in range(1, w):
            acc = acc + ext_ref[POOL_HALO - d:POOL_HALO - d + tm, c0:c1]
        cnt = jnp.minimum(pos + 1, w).astype(F32)
        pooled = (acc / cnt - cur).astype(BF16)
        mixed = jnp.dot(pooled, w_ref[g].astype(BF16), preferred_element_type=F32) * sc_ref[:, c0:c1]
        o_ref[:, c0:c1] = mixed.astype(o_ref.dtype)


def _pool_prompt(z, n_rows, w_grp, scale):
    n, pw = z.shape
    tm = TM_ROW
    hb = tm // POOL_HALO
    return pl.pallas_call(
        functools.partial(_pool_prompt_kernel, tm=tm, pg=pw // len(POOL_WINDOWS), nb=n_rows // tm),
        grid=(n // tm,),
        in_specs=[pl.BlockSpec((tm, pw), lambda i: (i, 0)),
                  pl.BlockSpec((POOL_HALO, pw), lambda i: (jnp.maximum(i * hb - 1, 0), 0)),
                  pl.BlockSpec(w_grp.shape, lambda i: (0, 0, 0)),
                  pl.BlockSpec((1, pw), lambda i: (0, 0))],
        out_specs=pl.BlockSpec((tm, pw), lambda i: (i, 0)),
        out_shape=jax.ShapeDtypeStruct((n, pw), BF16),
        scratch_shapes=[pltpu.VMEM((tm + POOL_HALO, pw), F32)],
        compiler_params=_params(1, 40),
        name="pool_prompt",
    )(z, z, w_grp, scale.reshape(1, pw))


def _pool_sample_kernel(st_ref, z_ref, w_ref, sc_ref, mixed_in_ref, o_ref, ns_ref, ext_ref, *, pg, t_new):
    del mixed_in_ref
    n_st = st_ref.shape[1]
    seqs = st_ref.shape[0]
    hist = ext_ref.shape[1] - t_new
    ext_ref[:, hist - n_st:hist, :] = st_ref[...]
    ext_ref[:, hist:, :] = z_ref[...]
    ns_ref[...] = ext_ref[:, hist + t_new - n_st:, :]
    for g, w in enumerate(POOL_WINDOWS):
        c0, c1 = g * pg, (g + 1) * pg
        cur = ext_ref[:, hist:, c0:c1]
        acc = cur
        for d in range(1, w):
            acc = acc + ext_ref[:, hist - d:hist - d + t_new, c0:c1]
        pooled = (acc / float(w) - cur).reshape(seqs * t_new, pg).astype(BF16)
        mixed = jnp.dot(pooled, w_ref[g].astype(BF16), preferred_element_type=F32) * sc_ref[:, c0:c1]
        o_ref[:, c0:c1] = mixed.astype(o_ref.dtype)


def _pool_sample(state, z, mixed, row0, w_grp, scale, t_new):
    n_seq, n_st, pw = state.shape
    assert t_new == SUBLANES and n_st <= POOL_HALO - 1
    z3 = z.reshape(z.shape[0] // t_new, t_new, pw)
    sb0 = row0 // t_new // POOL_SEQS
    rows = POOL_SEQS * t_new
    st_spec = pl.BlockSpec((POOL_SEQS, n_st, pw), lambda i: (i, 0, 0))
    return pl.pallas_call(
        functools.partial(_pool_sample_kernel, pg=pw // len(POOL_WINDOWS), t_new=t_new),
        grid=(n_seq // POOL_SEQS,),
        in_specs=[st_spec,
                  pl.BlockSpec((POOL_SEQS, t_new, pw), lambda i: (sb0 + i, 0, 0)),
                  pl.BlockSpec(w_grp.shape, lambda i: (0, 0, 0)),
                  pl.BlockSpec((1, pw), lambda i: (0, 0)),
                  pl.BlockSpec(memory_space=pl.ANY)],
        out_specs=[pl.BlockSpec((rows, pw), lambda i: (row0 // rows + i, 0)), st_spec],
        out_shape=[jax.ShapeDtypeStruct(mixed.shape, mixed.dtype), jax.ShapeDtypeStruct(state.shape, F32)],
        scratch_shapes=[pltpu.VMEM((POOL_SEQS, POOL_HALO + t_new, pw), F32)],
        input_output_aliases={4: 0},
        compiler_params=_params(1, 40),
        name="pool_sample",
    )(state, z3, w_grp, scale.reshape(1, pw), mixed)


def _merge_kernel(a1_ref, a2_ref, w1_hbm, w2_hbm, ga_ref, gb_ref, ba_ref, bb_ref, o_ref,
                  w1f_ref, w2f_ref, w1b_ref, w2b_ref, sem):
    _column_weights_step((w1_hbm, w2_hbm), (0, 0), (w1f_ref, w2f_ref), (w1b_ref, w2b_ref), sem)
    a_up = jnp.dot(a1_ref[...], w1b_ref[...], preferred_element_type=F32)
    b_up = jnp.dot(a2_ref[...], w2b_ref[...], preferred_element_type=F32)
    merged = (jax.nn.sigmoid(ga_ref[...] + ba_ref[...]) * a_up
              + jax.nn.sigmoid(gb_ref[...] + bb_ref[...]) * b_up)
    o_ref[...] = merged.astype(o_ref.dtype)


def _merge(attn, mixed, w_attn_up, w_pool_up, gates, b_gate):
    n, ka = attn.shape
    kp = mixed.shape[1]
    d = w_attn_up.shape[1]
    tm, tn = TM_WIDE, TN_WIDE
    nj = d // tn
    any_spec = pl.BlockSpec(memory_space=pl.ANY)
    return pl.pallas_call(
        _merge_kernel,
        grid=(nj, n // tm),
        in_specs=[pl.BlockSpec((tm, ka), lambda j, i: (i, 0)),
                  pl.BlockSpec((tm, kp), lambda j, i: (i, 0)),
                  any_spec, any_spec,
                  pl.BlockSpec((tm, tn), lambda j, i: (i, j)),
                  pl.BlockSpec((tm, tn), lambda j, i: (i, nj + j)),
                  pl.BlockSpec((1, tn), lambda j, i: (0, j)),
                  pl.BlockSpec((1, tn), lambda j, i: (0, nj + j))],
        out_specs=pl.BlockSpec((tm, tn), lambda j, i: (i, j)),
        out_shape=jax.ShapeDtypeStruct((n, d), BF16),
        scratch_shapes=_weight_scratch([(ka, tn), (kp, tn)]),
        compiler_params=_params(2, 56),
        name="merge",
    )(attn, mixed, w_attn_up, w_pool_up, gates, gates, b_gate, b_gate)


def _out_proj_kernel(a_ref, w_hbm, ha_ref, hb_ref, o_ref, wf_ref, wb_ref, sem, *, nb_first):
    _column_weights_step((w_hbm,), (0,), (wf_ref,), (wb_ref,), sem)
    h = jnp.where(pl.program_id(1) < nb_first, ha_ref[...], hb_ref[...])
    o_ref[...] = h + jnp.dot(a_ref[...], wb_ref[...], preferred_element_type=F32)


def _out_proj(merged, w_out, ha, hb):
    n, d = merged.shape
    tm, tn = TM_WIDE, TN_WIDE
    nb_first = ha.shape[0] // tm
    h_first = pl.BlockSpec((tm, tn), lambda j, i: (jnp.minimum(i, nb_first - 1), j))
    h_last = pl.BlockSpec((tm, tn), lambda j, i: (jnp.maximum(i - nb_first, 0), j))
    return pl.pallas_call(
        functools.partial(_out_proj_kernel, nb_first=nb_first),
        grid=(d // tn, n // tm),
        in_specs=[pl.BlockSpec((tm, d), lambda j, i: (i, 0)),
                  pl.BlockSpec(memory_space=pl.ANY),
                  h_first, h_last],
        out_specs=pl.BlockSpec((tm, tn), lambda j, i: (i, j)),
        out_shape=jax.ShapeDtypeStruct((n, d), F32),
        scratch_shapes=_weight_scratch([(d, tn)]),
        compiler_params=_params(2, 56),
        name="out_proj",
    )(merged, w_out, ha, hb)


def _router_kernel(h_ref, g_ref, wr_ref, br_ref, info_ref, cw_ref, xf_ref, cnt_ref, base_ref):
    tm = h_ref.shape[0]

    @pl.when(pl.program_id(0) == 0)
    def _():
        base_ref[...] = jnp.zeros_like(base_ref)

    xf = _rms_rows(h_ref[...], g_ref[...]).astype(BF16)
    xf_ref[...] = xf.reshape(xf_ref.shape)
    logits = jnp.dot(xf, wr_ref[...].astype(BF16), preferred_element_type=F32) + br_ref[...]
    lane = lax.broadcasted_iota(I32, logits.shape, 1)
    big = jnp.int32(1 << 20)
    ninf = jnp.float32(-jnp.inf)

    is_g = lane < N_GROUPS
    gl = jnp.where(is_g, logits, ninf)
    gmax = jnp.max(gl, axis=-1, keepdims=True)
    gden = jnp.sum(jnp.exp(gl - gmax), axis=-1, keepdims=True)
    g_p = 1.0 / gden
    g_idx = jnp.min(jnp.where(gl == gmax, lane, big), axis=-1, keepdims=True)

    e_lane = lane - N_GROUPS
    in_grp = (e_lane >= g_idx * EXPERTS_PER_GROUP) & (e_lane < (g_idx + 1) * EXPERTS_PER_GROUP)
    el = jnp.where(in_grp, logits, ninf)
    m1 = jnp.max(el, axis=-1, keepdims=True)
    i1 = jnp.min(jnp.where(in_grp & (el == m1), e_lane, big), axis=-1, keepdims=True)
    el2 = jnp.where(e_lane == i1, ninf, el)
    m2 = jnp.max(el2, axis=-1, keepdims=True)
    i2 = jnp.min(jnp.where(in_grp & (el2 == m2), e_lane, big), axis=-1, keepdims=True)
    t = jnp.exp(m2 - m1)
    den = 1.0 + t
    c1 = g_p * (1.0 / den)
    c2 = g_p * (t / den)
    cw_ref[...] = jnp.where(lane == 0, c1, jnp.where(lane == 1, c2, 0.0))

    hit1, hit2 = lane == i1, lane == i2
    onehot = (hit1 | hit2).astype(BF16)
    earlier = (lax.broadcasted_iota(I32, (tm, tm), 1) < lax.broadcasted_iota(I32, (tm, tm), 0)).astype(BF16)
    before = jnp.dot(earlier, onehot, preferred_element_type=F32) + base_ref[...]
    r1 = jnp.sum(jnp.where(hit1, before, 0.0), axis=-1, keepdims=True).astype(I32)
    r2 = jnp.sum(jnp.where(hit2, before, 0.0), axis=-1, keepdims=True).astype(I32)
    base_ref[...] += jnp.sum(onehot.astype(F32), axis=0, keepdims=True)
    cnt_ref[...] = base_ref[...].astype(I32)
    info_ref[...] = jnp.where(lane == 0, i1, jnp.where(lane == 1, i2,
                              jnp.where(lane == 2, r1, jnp.where(lane == 3, r2, 0))))


def _router(h, g_ffn, w_rg, b_rg, w_re, b_re):
    n, d = h.shape
    n_log = N_GROUPS + N_EXPERTS
    wr = jnp.concatenate([w_rg, w_re, jnp.zeros((d, LANES - n_log), F32)], axis=1)
    br = jnp.concatenate([b_rg, b_re, jnp.zeros((LANES - n_log,), F32)]).reshape(1, LANES)
    lane_spec = pl.BlockSpec((TM_ROW, LANES), lambda i: (i, 0))
    return pl.pallas_call(
        _router_kernel,
        grid=(n // TM_ROW,),
        in_specs=[pl.BlockSpec((TM_ROW, d), lambda i: (i, 0)),
                  pl.BlockSpec((1, d), lambda i: (0, 0)),
                  pl.BlockSpec((d, LANES), lambda i: (0, 0)),
                  pl.BlockSpec((1, LANES), lambda i: (0, 0))],
        out_specs=[lane_spec, lane_spec,
                   pl.BlockSpec((TM_ROW, d // LANES, LANES), lambda i: (i, 0, 0)),
                   pl.BlockSpec((1, LANES), lambda i: (0, 0))],
        out_shape=[jax.ShapeDtypeStruct((n, LANES), I32), jax.ShapeDtypeStruct((n, LANES), F32),
                   jax.ShapeDtypeStruct((n, d // LANES, LANES), BF16), jax.ShapeDtypeStruct((1, LANES), I32)],
        scratch_shapes=[pltpu.VMEM((1, LANES), F32)],
        compiler_params=_params(1, 40),
        name="router",
    )(h, g_ffn.reshape(1, d), wr, br)


def _dispatch_plan(info, counts, n_tiles):
    counts = counts[0, :N_EXPERTS]
    tiles_per = (counts + TM_E - 1) // TM_E
    tiles_end = jnp.cumsum(tiles_per)
    row_start = (tiles_end - tiles_per) * TM_E
    eid, rank = info[:, :TOP_K], info[:, TOP_K:2 * TOP_K]
    onehot = (eid[:, :, None] == jnp.arange(N_EXPERTS, dtype=I32)[None, None, :]).astype(I32)
    pos = (jnp.sum(onehot * row_start[None, None, :], axis=-1) + rank).reshape(-1).astype(I32)
    n_used = tiles_end[-1:].astype(I32)
    tile_ids = jnp.minimum(jnp.arange(n_tiles, dtype=I32), n_used - 1)
    tile_expert = jnp.sum((tile_ids[:, None] >= tiles_end[None, :]).astype(I32), axis=1).astype(I32)
    last_tile_row = (jnp.clip(tiles_end - 1, 0, n_tiles - 1) * TM_E).astype(I32)
    starts = jnp.concatenate([jnp.ones((1,), I32), (tile_expert[1:] != tile_expert[:-1]).astype(I32)])
    run_idx = jnp.cumsum(starts) - 1
    n_runs = run_idx[-1:] + 1
    run_expert = jnp.zeros((n_tiles,), I32).at[run_idx].set(tile_expert)
    next_expert = run_expert[(run_idx + 1) % n_runs]
    rows_end = row_start + counts
    valid = jnp.clip(rows_end[tile_expert] - jnp.arange(n_tiles, dtype=I32) * TM_E, 0, TM_E)
    plan = (tile_expert, n_used, run_idx.astype(I32), next_expert.astype(I32), n_runs.astype(I32),
            valid.astype(I32))
    return pos, plan, last_tile_row


def _dispatch_kernel(pos_ref, ltr_ref, nu_ref, xf_ref, xs_hbm, zero_ref, sem):
    i = pl.program_id(0)
    n_tiles = xs_hbm.shape[0] // TM_E

    def clear_tile(row):
        return pltpu.make_async_copy(zero_ref, xs_hbm.at[pl.ds(pl.multiple_of(row, TM_E), TM_E)], sem)

    @pl.when(i == 0)
    def _():
        zero_ref[...] = jnp.zeros_like(zero_ref)
        for e in range(N_EXPERTS):
            clear_tile(ltr_ref[e]).start()

        def start_unused(t, carry):
            clear_tile(t * TM_E).start()
            return carry

        def wait_one(t, carry):
            clear_tile(0).wait()
            return carry

        lax.fori_loop(nu_ref[0], n_tiles, start_unused, 0)
        lax.fori_loop(nu_ref[0] - N_EXPERTS, n_tiles, wait_one, 0)

    base = i * TM_DISPATCH

    def issue(r, carry):
        for k in range(TOP_K):
            dst_row = pos_ref[(base + r) * TOP_K + k]
            pltpu.make_async_copy(xf_ref.at[pl.ds(r, 1)], xs_hbm.at[pl.ds(dst_row, 1)], sem).start(priority=k % 2)
        return carry

    lax.fori_loop(0, TM_DISPATCH, issue, 0, unroll=4)
    for k in range(TOP_K):
        pltpu.make_async_copy(xf_ref, xs_hbm.at[pl.ds(0, TM_DISPATCH)], sem).wait()


def _dispatch(pos, last_tile_row, n_used, xf3, n_tiles):
    n, c, l = xf3.shape
    return pl.pallas_call(
        _dispatch_kernel,
        grid_spec=pltpu.PrefetchScalarGridSpec(
            num_scalar_prefetch=3,
            grid=(n // TM_DISPATCH,),
            in_specs=[pl.BlockSpec((TM_DISPATCH, c, l), lambda i, *_: (i, 0, 0))],
            out_specs=pl.BlockSpec(memory_space=pl.ANY),
            scratch_shapes=[pltpu.VMEM((TM_E, c, l), xf3.dtype), pltpu.SemaphoreType.DMA(())]),
        out_shape=jax.ShapeDtypeStruct((n_tiles * TM_E, c, l), xf3.dtype),
        compiler_params=_params(1, 24),
        name="moe_dispatch",
    )(pos, last_tile_row, n_used, xf3)


def _expert_weights_step(plan_refs, w_hbms, wbuf_ref, wb_refs, sem, tn):
    te_ref, _, run_ref, nxt_ref, nr_ref = plan_refs
    j, t = pl.program_id(0), pl.program_id(1)

    def copies(e, jj):
        col = pl.multiple_of(jj * tn, tn)
        return [pltpu.make_async_copy(w.at[e, :, pl.ds(col, tn)], wbuf_ref.at[m], sem)
                for m, w in enumerate(w_hbms)]

    @pl.when((t == 0) | (te_ref[t] != te_ref[jnp.maximum(t - 1, 0)]))
    def _():
        @pl.when((j == 0) & (t == 0))
        def _():
            for c in copies(te_ref[t], j):
                c.start()

        for c in copies(te_ref[t], j):
            c.wait()
        for m, wb_ref in enumerate(wb_refs):
            _cast_rows(wbuf_ref.at[m], wb_ref)

        next_j = jnp.where(run_ref[t] + 1 == nr_ref[0], j + 1, j)

        @pl.when(next_j < pl.num_programs(0))
        def _():
            for c in copies(nxt_ref[t], next_j):
                c.start()


def _by_valid_rows(valid, full_fn, zero_fn):
    half = TM_E // 2

    @pl.when(valid > half)
    def _():
        full_fn(TM_E)

    @pl.when((valid > 0) & (valid <= half))
    def _():
        full_fn(half)
        zero_fn(half)

    @pl.when(valid == 0)
    def _():
        zero_fn(0)


def _moe_up_kernel(te_ref, nu_ref, run_ref, nxt_ref, nr_ref, valid_ref, xs_ref, wg_hbm, wu_hbm, o_ref,
                   wbuf_ref, wgb_ref, wub_ref, sem):
    t = pl.program_id(1)
    _expert_weights_step((te_ref, nu_ref, run_ref, nxt_ref, nr_ref), (wg_hbm, wu_hbm), wbuf_ref,
                         (wgb_ref, wub_ref), sem, o_ref.shape[1])

    def compute(rows):
        x = xs_ref[0:rows].reshape(rows, -1)
        gate = jnp.dot(x, wgb_ref[...], preferred_element_type=F32)
        up = jnp.dot(x, wub_ref[...], preferred_element_type=F32)
        o_ref[0:rows] = (jax.nn.silu(gate) * up).astype(o_ref.dtype)

    def clear(row0):
        o_ref[row0:] = jnp.zeros((TM_E - row0,) + o_ref.shape[1:], o_ref.dtype)

    _by_valid_rows(valid_ref[t], compute, clear)


def _moe_up(plan, xs3, w_gate, w_up, n_tiles):
    d, f = w_gate.shape[1], w_gate.shape[2]
    tf = TN
    any_spec = pl.BlockSpec(memory_space=pl.ANY)
    return pl.pallas_call(
        _moe_up_kernel,
        grid_spec=pltpu.PrefetchScalarGridSpec(
            num_scalar_prefetch=len(plan),
            grid=(f // tf, n_tiles),
            in_specs=[pl.BlockSpec((TM_E,) + xs3.shape[1:], lambda j, t, te, nu, *_: (jnp.minimum(t, nu[0] - 1), 0, 0)),
                      any_spec, any_spec],
            out_specs=pl.BlockSpec((TM_E, tf), lambda j, t, *_: (t, j)),
            scratch_shapes=[pltpu.VMEM((2, d, tf), F32), pltpu.VMEM((d, tf), BF16), pltpu.VMEM((d, tf), BF16),
                            pltpu.SemaphoreType.DMA(())]),
        out_shape=jax.ShapeDtypeStruct((n_tiles * TM_E, f), BF16),
        compiler_params=_params(2, 56),
        name="moe_up",
    )(*plan, xs3, w_gate, w_up)


def _moe_down_kernel(te_ref, nu_ref, run_ref, nxt_ref, nr_ref, valid_ref, hid_ref, wd_hbm, o_ref,
                     wbuf_ref, wdb_ref, sem):
    t = pl.program_id(1)
    _expert_weights_step((te_ref, nu_ref, run_ref, nxt_ref, nr_ref), (wd_hbm,), wbuf_ref, (wdb_ref,), sem,
                         wdb_ref.shape[1])

    def compute(rows):
        y = jnp.dot(hid_ref[0:rows], wdb_ref[...], preferred_element_type=F32)
        o_ref[0:rows] = y.reshape((rows,) + o_ref.shape[1:])

    def clear(row0):
        o_ref[row0:] = jnp.zeros((TM_E - row0,) + o_ref.shape[1:], o_ref.dtype)

    _by_valid_rows(valid_ref[t], compute, clear)


def _moe_down(plan, hid, w_down, n_tiles):
    f, d = w_down.shape[1], w_down.shape[2]
    tn = 4 * TN
    return pl.pallas_call(
        _moe_down_kernel,
        grid_spec=pltpu.PrefetchScalarGridSpec(
            num_scalar_prefetch=len(plan),
            grid=(d // tn, n_tiles),
            in_specs=[pl.BlockSpec((TM_E, f), lambda j, t, *_: (t, 0)),
                      pl.BlockSpec(memory_space=pl.ANY)],
            out_specs=pl.BlockSpec((TM_E, tn // LANES, LANES), lambda j, t, *_: (t, j, 0)),
            scratch_shapes=[pltpu.VMEM((1, f, tn), F32), pltpu.VMEM((f, tn), BF16),
                            pltpu.SemaphoreType.DMA(())]),
        out_shape=jax.ShapeDtypeStruct((n_tiles * TM_E, d // LANES, LANES), F32),
        compiler_params=_params(2, 40),
        name="moe_down",
    )(*plan, hid, w_down)


def _combine_kernel(pos_ref, h_ref, cw_ref, y_hbm, g_ref, ho_ref, xo_ref, buf_ref, sem):
    i = pl.program_id(0)
    tm, d = h_ref.shape

    def row_copy(slot, k, r, src_row):
        return pltpu.make_async_copy(y_hbm.at[pl.ds(src_row, 1)], buf_ref.at[slot, k, pl.ds(r, 1)], sem.at[slot])

    def issue_step(step, slot):
        def body(r, carry):
            for k in range(TOP_K):
                row_copy(slot, k, r, pos_ref[(step * tm + r) * TOP_K + k]).start(priority=k % 2)
            return carry
        lax.fori_loop(0, tm, body, 0, unroll=4)

    @pl.when(i == 0)
    def _():
        issue_step(0, 0)

    @pl.when(i + 1 < pl.num_programs(0))
    def _():
        issue_step(i + 1, (i + 1) % 2)

    slot = i % 2
    for k in range(TOP_K):
        pltpu.make_async_copy(y_hbm.at[pl.ds(0, tm)], buf_ref.at[slot, k], sem.at[slot]).wait()
    cw = cw_ref[...]
    y1 = buf_ref[slot, 0].reshape(tm, d)
    y2 = buf_ref[slot, 1].reshape(tm, d)
    h2 = h_ref[...] + (cw[:, 0:1] * y1 + cw[:, 1:2] * y2)
    ho_ref[...] = h2
    xo_ref[...] = _rms_rows(h2, g_ref[...]).astype(xo_ref.dtype)


def _combine(pos, h, cw, y3, g_ple):
    n, d = h.shape
    row_spec = pl.BlockSpec((TM_ROW, d), lambda i, p: (i, 0))
    return pl.pallas_call(
        _combine_kernel,
        grid_spec=pltpu.PrefetchScalarGridSpec(
            num_scalar_prefetch=1,
            grid=(n // TM_ROW,),
            in_specs=[row_spec,
                      pl.BlockSpec((TM_ROW, LANES), lambda i, p: (i, 0)),
                      pl.BlockSpec(memory_space=pl.ANY),
                      pl.BlockSpec((1, d), lambda i, p: (0, 0))],
            out_specs=[row_spec, row_spec],
            scratch_shapes=[pltpu.VMEM((2, TOP_K, TM_ROW) + y3.shape[1:], F32), pltpu.SemaphoreType.DMA((2,))]),
        out_shape=[jax.ShapeDtypeStruct((n, d), F32), jax.ShapeDtypeStruct((n, d), BF16)],
        compiler_params=_params(1, 56),
        name="moe_combine",
    )(pos, h, cw, y3, g_ple.reshape(1, d))


def _ple_kernel(xn_ref, wg_hbm, bg_ref, pa_ref, pb_ref, wp_hbm, h_ref, o_ref,
                wgf_ref, wpf_ref, wgb_ref, wpb_ref, sem, *, nb_first):
    _column_weights_step((wg_hbm, wp_hbm), (0, 0), (wgf_ref, wpf_ref), (wgb_ref, wpb_ref), sem)
    p = jnp.where(pl.program_id(1) < nb_first, pa_ref[...], pb_ref[...]).astype(BF16)
    tn = o_ref.shape[1]
    step = min(tn, TN)
    for c in range(0, tn, step):
        cols = slice(c, c + step)
        gate = jax.nn.sigmoid(jnp.dot(xn_ref[...], wgb_ref[:, cols], preferred_element_type=F32) + bg_ref[:, cols])
        ple = jnp.dot(p, wpb_ref[:, cols], preferred_element_type=F32)
        o_ref[:, cols] = h_ref[:, cols] + gate * ple


def _ple(xn, w_gate, b_gate, pa, pb, w_proj, h):
    n, d = h.shape
    pd = pa.shape[1]
    tm, tn = TM_WIDE, TN_WIDE
    nb_first = pa.shape[0] // tm
    p_first = pl.BlockSpec((tm, pd), lambda j, i: (jnp.minimum(i, nb_first - 1), 0))
    p_last = pl.BlockSpec((tm, pd), lambda j, i: (jnp.maximum(i - nb_first, 0), 0))
    any_spec = pl.BlockSpec(memory_space=pl.ANY)
    return pl.pallas_call(
        functools.partial(_ple_kernel, nb_first=nb_first),
        grid=(d // tn, n // tm),
        in_specs=[pl.BlockSpec((tm, d), lambda j, i: (i, 0)),
                  any_spec,
                  pl.BlockSpec((1, tn), lambda j, i: (0, j)),
                  p_first, p_last,
                  any_spec,
                  pl.BlockSpec((tm, tn), lambda j, i: (i, j))],
        out_specs=pl.BlockSpec((tm, tn), lambda j, i: (i, j)),
        out_shape=jax.ShapeDtypeStruct((n, d), F32),
        scratch_shapes=_weight_scratch([(d, tn), (pd, tn)]),
        compiler_params=_params(2, 58),
        name="ple",
    )(xn, w_gate, b_gate.reshape(1, d), pa, pb, w_proj, h)


def _layer(xn, h_p, h_s, p_p, p_s, cache_k, cache_v, state_pool, lw):
    n, d = xn.shape
    n_prompt = h_p.shape[0]
    n_seq, n_cache, n_kv, _ = cache_k.shape
    t_new = (n - n_prompt) // n_seq
    kv_w = n_kv * HEAD_DIM
    attn_w = lw['w_attn_up'].shape[0]
    pool_w = lw['w_pool_up'].shape[0]
    n_pool = state_pool.shape[1]

    qkv = _in_proj(xn, lw['w_in'], 0, attn_w + 2 * kv_w, "in_proj_qkv")
    z = _in_proj(xn, lw['w_in'], attn_w + 2 * kv_w, pool_w, "in_proj_pool")
    gates = _in_proj(xn, lw['w_in'], attn_w + 2 * kv_w + pool_w, 2 * d, "in_proj_gates")

    tab_p = _rope_tables(jnp.arange(n_prompt, dtype=F32))
    tab_s = tuple(jnp.tile(a, (SEQS_PER_STEP, 1))
                  for a in _rope_tables(PAST_LEN + jnp.arange(t_new, dtype=F32)))
    tab_p_t = _rope_tables_t(jnp.arange(n_prompt, dtype=F32))
    attn, k_rot_p = _attn_prompt(qkv, lw['attn_sinks'], tab_p, tab_p_t, n_prompt, attn_w, kv_w)
    attn, new_k_s, new_v_s = _attn_sample(
        qkv, attn, n_prompt, cache_k.reshape(n_seq, n_cache, kv_w), cache_v.reshape(n_seq, n_cache, kv_w),
        lw['attn_sinks'], tab_s, n_seq, t_new, attn_w, kv_w)

    mixed = _pool_prompt(z, n_prompt, lw['w_pool_grp'], lw['pool_scale'])
    mixed, new_z_s = _pool_sample(state_pool, z, mixed, n_prompt, lw['w_pool_grp'], lw['pool_scale'], t_new)

    merged = _merge(attn, mixed, lw['w_attn_up'], lw['w_pool_up'], gates, lw['b_gate'].reshape(1, 2 * d))
    h1 = _out_proj(merged, lw['w_out'], h_p, h_s)

    n_tiles = (n * TOP_K) // TM_E + N_EXPERTS
    info, cw, xf3, counts = _router(h1, lw['g_ffn'], lw['w_route_group'], lw['b_route_group'],
                                    lw['w_route_expert'], lw['b_route_expert'])
    pos, plan, last_tile_row = _dispatch_plan(info, counts, n_tiles)
    xs3 = _dispatch(pos, last_tile_row, plan[1], xf3, n_tiles)
    hid = _moe_up(plan, xs3, lw['w_exp_gate'], lw['w_exp_up'], n_tiles)
    y3 = _moe_down(plan, hid, lw['w_exp_down'], n_tiles)
    h2, xn2 = _combine(pos, h1, cw, y3, lw['g_ple'])

    h3 = _ple(xn2, lw['w_ple_gate'], lw['b_ple_gate'], p_p, p_s, lw['w_ple_proj'], h2)

    new_k_p = k_rot_p[n_prompt - n_cache:].reshape(1, n_cache, n_kv, HEAD_DIM)
    new_v_p = qkv[n_prompt - n_cache:n_prompt, attn_w + kv_w:].reshape(1, n_cache, n_kv, HEAD_DIM)
    new_z_p = z[n_prompt - n_pool:n_prompt].reshape(1, n_pool, pool_w)
    return (h3, new_k_p, new_v_p, new_z_p,
            new_k_s.reshape(cache_k.shape), new_v_s.reshape(cache_v.shape), new_z_s)


def kernel(x_prompt, x_sample, cache_k, cache_v, state_pool, p_prompt, p_sample, g_mix, w_in, b_gate, attn_sinks, w_pool_grp, pool_scale, w_attn_up, w_pool_up, w_out, g_ffn, w_route_group, b_route_group, w_route_expert, b_route_expert, w_exp_gate, w_exp_up, w_exp_down, g_ple, w_ple_gate, b_ple_gate, w_ple_proj, g_final):
    batch, seq, d = x_prompt.shape
    n_seq, t_new, _ = x_sample.shape
    depth = w_in.shape[0]
    assert batch == 1, "prompt rows are treated as one sequence"
    n_prompt = batch * seq
    n_sample = n_seq * t_new
    n = n_prompt + n_sample
    assert n_prompt % TM == 0 and n_sample % TM == 0 and (n * TOP_K) % TM_E == 0 and n % TM_DISPATCH == 0
    assert n_seq % SEQS_PER_STEP == 0 and n_seq % POOL_SEQS == 0 and t_new & (t_new - 1) == 0
    assert cache_k.shape[2] == WINDOW and state_pool.shape[2] == POOL_HALO - 1
    assert depth == 1, "one layer per step"

    weights = dict(g_mix=g_mix, w_in=w_in, b_gate=b_gate, attn_sinks=attn_sinks, w_pool_grp=w_pool_grp,
                   pool_scale=pool_scale, w_attn_up=w_attn_up, w_pool_up=w_pool_up, w_out=w_out, g_ffn=g_ffn,
                   w_route_group=w_route_group, b_route_group=b_route_group, w_route_expert=w_route_expert,
                   b_route_expert=b_route_expert, w_exp_gate=w_exp_gate, w_exp_up=w_exp_up,
                   w_exp_down=w_exp_down, g_ple=g_ple, w_ple_gate=w_ple_gate, b_ple_gate=b_ple_gate,
                   w_ple_proj=w_ple_proj)
    lw = {name: w[0] for name, w in weights.items()}
    h_p = x_prompt.reshape(n_prompt, d)
    h_s = x_sample.reshape(n_sample, d)
    xn = _rmsnorm_in2(h_p, h_s, lw['g_mix'], BF16, "norm_mix")
    out = _layer(xn, h_p, h_s, p_prompt[0].reshape(n_prompt, -1), p_sample[0].reshape(n_sample, -1),
                 cache_k[0], cache_v[0], state_pool[0], lw)
    y_p, y_s = _rmsnorm_out2(out[0], g_final, n_prompt, "norm_final")
    return (y_p.reshape(batch, seq, d), y_s.reshape(n_seq, t_new, d)) + tuple(piece[None] for piece in out[1:])
```

```python
import functools

import jax
import jax.numpy as jnp
from jax import lax
from jax.experimental import pallas as pl
from jax.experimental.pallas import tpu as pltpu

F32 = jnp.float32
BF16 = jnp.bfloat16
I32 = jnp.int32

HEAD_DIM = 64
Q_PER_KV = 4
WINDOW = 128
PAST_LEN = 8192
ROT_DIM = HEAD_DIM // 4
ROPE_THETA = 500000.0
POOL_WINDOWS = (2, 4, 8, 16)
POOL_HALO = 16
N_GROUPS = 4
EXPERTS_PER_GROUP = 4
N_EXPERTS = N_GROUPS * EXPERTS_PER_GROUP
TOP_K = 2
RMS_EPS = 1e-6
NEG_INF = -1e30
LOG2_E = 1.4426950408889634
LANES = 128
SUBLANES = 8
MIB = 1024 * 1024

TM = 1024
TN = 512
TN_PROJ = 1024
CAST_ROWS = 512
TM_WIDE = 512
TN_WIDE = 1024
TM_ROW = 256
TM_E = 512
TM_DISPATCH = 512
SEQS_PER_STEP = 8
POOL_SEQS = 16


def _params(n_axes, vmem_mib):
    return pltpu.CompilerParams(dimension_semantics=("arbitrary",) * n_axes,
                                vmem_limit_bytes=vmem_mib * MIB)


def _rms_rows(x, g):
    ms = jnp.mean(x * x, axis=-1, keepdims=True)
    return x * lax.rsqrt(ms + RMS_EPS) * g


def _first_last_specs(nb_first, block):
    zeros = (0,) * (len(block) - 1)
    first = pl.BlockSpec(block, lambda i, *_: (jnp.minimum(i, nb_first - 1),) + zeros)
    last = pl.BlockSpec(block, lambda i, *_: (jnp.maximum(i - nb_first, 0),) + zeros)
    return first, last


def _norm_in2_kernel(xa_ref, xb_ref, g_ref, o_ref, *, nb_first):
    i = pl.program_id(0)

    @pl.when(i < nb_first)
    def _():
        o_ref[...] = _rms_rows(xa_ref[...], g_ref[...]).astype(o_ref.dtype)

    @pl.when(i >= nb_first)
    def _():
        o_ref[...] = _rms_rows(xb_ref[...], g_ref[...]).astype(o_ref.dtype)


def _rmsnorm_in2(xa, xb, g, out_dtype, name):
    (na, d), nb = xa.shape, xb.shape[0]
    nb_first = na // TM_ROW
    spec_a, spec_b = _first_last_specs(nb_first, (TM_ROW, d))
    return pl.pallas_call(
        functools.partial(_norm_in2_kernel, nb_first=nb_first),
        grid=((na + nb) // TM_ROW,),
        in_specs=[spec_a, spec_b, pl.BlockSpec((1, d), lambda i: (0, 0))],
        out_specs=pl.BlockSpec((TM_ROW, d), lambda i: (i, 0)),
        out_shape=jax.ShapeDtypeStruct((na + nb, d), out_dtype),
        compiler_params=_params(1, 40),
        name=name,
    )(xa, xb, g.reshape(1, d))


def _norm_out2_kernel(x_ref, g_ref, oa_ref, ob_ref, *, nb_first):
    i = pl.program_id(0)

    @pl.when(i < nb_first)
    def _():
        oa_ref[...] = _rms_rows(x_ref[...], g_ref[...])

    @pl.when(i >= nb_first)
    def _():
        ob_ref[...] = _rms_rows(x_ref[...], g_ref[...])


def _rmsnorm_out2(x, g, na, name):
    n, d = x.shape
    nb_first = na // TM_ROW
    spec_a, spec_b = _first_last_specs(nb_first, (TM_ROW, d))
    return pl.pallas_call(
        functools.partial(_norm_out2_kernel, nb_first=nb_first),
        grid=(n // TM_ROW,),
        in_specs=[pl.BlockSpec((TM_ROW, d), lambda i: (i, 0)), pl.BlockSpec((1, d), lambda i: (0, 0))],
        out_specs=[spec_a, spec_b],
        out_shape=[jax.ShapeDtypeStruct((na, d), F32), jax.ShapeDtypeStruct((n - na, d), F32)],
        compiler_params=_params(1, 40),
        name=name,
    )(x, g.reshape(1, d))


def _cast_rows(src_ref, dst_ref):
    rows = src_ref.shape[0]
    step = min(rows, CAST_ROWS)
    for r in range(0, rows, step):
        dst_ref[r:r + step] = src_ref[r:r + step].astype(dst_ref.dtype)


def _column_weights_step(w_hbms, col0s, wbuf_refs, wb_refs, sem):
    j, i = pl.program_id(0), pl.program_id(1)

    def copies(jj):
        out = []
        for w, col0, buf in zip(w_hbms, col0s, wbuf_refs):
            tn = buf.shape[1]
            out.append(pltpu.make_async_copy(w.at[:, pl.ds(pl.multiple_of(col0 + jj * tn, tn), tn)], buf, sem))
        return out

    @pl.when(i == 0)
    def _():
        @pl.when(j == 0)
        def _():
            for c in copies(j):
                c.start()

        for c in copies(j):
            c.wait()
        for buf, wb_ref in zip(wbuf_refs, wb_refs):
            _cast_rows(buf, wb_ref)

        @pl.when(j + 1 < pl.num_programs(0))
        def _():
            for c in copies(j + 1):
                c.start()


def _weight_scratch(shapes):
    return ([pltpu.VMEM(s, F32) for s in shapes] + [pltpu.VMEM(s, BF16) for s in shapes]
            + [pltpu.SemaphoreType.DMA(())])


def _proj_kernel(a_ref, w_hbm, o_ref, wbuf_ref, wb_ref, sem, *, col0):
    _column_weights_step((w_hbm,), (col0,), (wbuf_ref,), (wb_ref,), sem)
    o_ref[...] = jnp.dot(a_ref[...], wb_ref[...], preferred_element_type=F32)


def _in_proj(xn, w, col0, width, name):
    n, d = xn.shape
    tn = TN_PROJ
    assert col0 % tn == 0 and width % tn == 0
    return pl.pallas_call(
        functools.partial(_proj_kernel, col0=col0),
        grid=(width // tn, n // TM),
        in_specs=[pl.BlockSpec((TM, d), lambda j, i: (i, 0)),
                  pl.BlockSpec(memory_space=pl.ANY)],
        out_specs=pl.BlockSpec((TM, tn), lambda j, i: (i, j)),
        out_shape=jax.ShapeDtypeStruct((n, width), F32),
        scratch_shapes=_weight_scratch([(d, tn)]),
        compiler_params=_params(2, 56),
        name=name,
    )(xn, w)


def _rope_tables(pos):
    half = ROT_DIM // 2
    inv_freq = ROPE_THETA ** (-jnp.arange(half, dtype=F32) * (2.0 / ROT_DIM))
    ang = pos[:, None] * inv_freq[None, :]
    cos, sin = jnp.cos(ang), jnp.sin(ang)
    t = pos.shape[0]
    pad = jnp.zeros((t, HEAD_DIM - ROT_DIM), F32)
    zeros = jnp.zeros((t, half), F32)
    cos_h = jnp.concatenate([cos, cos, pad + 1.0], axis=1)
    sa_h = jnp.concatenate([zeros, sin, pad], axis=1)
    sb_h = jnp.concatenate([-sin, zeros, pad], axis=1)
    reps = LANES // HEAD_DIM
    return tuple(jnp.tile(a, (1, reps)) for a in (cos_h, sa_h, sb_h))


def _rope_tables_t(pos):
    half = ROT_DIM // 2
    inv_freq = ROPE_THETA ** (-jnp.arange(half, dtype=F32) * (2.0 / ROT_DIM))
    ang = pos[:, None] * inv_freq[None, :]
    return jnp.cos(ang).T, jnp.sin(ang).T


def _rope(x, cos_t, sa_t, sb_t):
    pieces = []
    for c in range(x.shape[1] // LANES):
        xc = x[:, c * LANES:(c + 1) * LANES]
        pieces.append(xc * cos_t
                      + pltpu.roll(xc, ROT_DIM // 2, 1) * sa_t
                      + pltpu.roll(xc, LANES - ROT_DIM // 2, 1) * sb_t)
    return jnp.concatenate(pieces, axis=1)


def _softmax2_with_sink(s, sink):
    m = jnp.maximum(jnp.max(s, axis=-1, keepdims=True), sink)
    p = jnp.exp2(s - m)
    den = jnp.sum(p, axis=-1, keepdims=True) + jnp.exp2(sink - m)
    return p, 1.0 / den


def _attn_prompt_kernel(sink_ref, q_ref, k_ref, v_ref, cos_ref, sa_ref, sb_ref, cost_ref, sint_ref,
                        o_ref, ko_ref, kprev_ref, vprevt_ref, *, n_kv, nb):
    b = pl.program_id(0)

    @pl.when(b == 0)
    def _():
        kprev_ref[...] = jnp.zeros_like(kprev_ref)
        vprevt_ref[...] = jnp.zeros_like(vprevt_ref)

    @pl.when(b >= nb)
    def _():
        o_ref[...] = jnp.zeros_like(o_ref)

    @pl.when(b < nb)
    def _():
        _attn_prompt_block(sink_ref, q_ref, k_ref, v_ref, cos_ref, sa_ref, sb_ref, cost_ref, sint_ref,
                           o_ref, ko_ref, kprev_ref, vprevt_ref, n_kv=n_kv)


def _attn_prompt_block(sink_ref, q_ref, k_ref, v_ref, cos_ref, sa_ref, sb_ref, cost_ref, sint_ref,
                       o_ref, ko_ref, kprev_ref, vprevt_ref, *, n_kv):
    b = pl.program_id(0)
    half = ROT_DIM // 2
    k = _rope(k_ref[...], cos_ref[...], sa_ref[...], sb_ref[...])
    ko_ref[...] = k
    kcat = jnp.concatenate([kprev_ref[...], k], axis=0).astype(BF16)
    vt = v_ref[...].T
    vcat_t = jnp.concatenate([vprevt_ref[...], vt], axis=1).astype(BF16)
    kprev_ref[...] = k
    vprevt_ref[...] = vt

    qt = q_ref[...].T
    cos_t, sin_t = cost_ref[...], sint_ref[...]

    def roped_head_t(h):
        x = qt[h * HEAD_DIM:(h + 1) * HEAD_DIM]
        x1, x2 = x[0:half], x[half:ROT_DIM]
        rot = jnp.concatenate([x1 * cos_t - x2 * sin_t, x2 * cos_t + x1 * sin_t, x[ROT_DIM:]], axis=0)
        return (rot * (HEAD_DIM ** -0.5 * LOG2_E)).astype(BF16)

    n_heads = n_kv * Q_PER_KV
    zeros_head = jnp.zeros((HEAD_DIM, WINDOW), BF16)
    heads_per_tile = LANES // HEAD_DIM

    key = lax.broadcasted_iota(I32, (2 * WINDOW, Q_PER_KV * WINDOW), 0)
    qry = lax.broadcasted_iota(I32, (2 * WINDOW, Q_PER_KV * WINDOW), 1) & (WINDOW - 1)
    first_key = jnp.where(b > 0, 0, WINDOW)
    mask = (key > qry) & (key <= qry + WINDOW) & (key >= first_key)

    scores = []
    for g in range(n_kv):
        tile, slot = g // heads_per_tile, g % heads_per_tile
        k2 = kcat[:, tile * LANES:(tile + 1) * LANES]
        cols = []
        for r in range(Q_PER_KV):
            parts = [zeros_head] * heads_per_tile
            parts[slot] = roped_head_t(g * Q_PER_KV + r)
            cols.append(jnp.concatenate(parts, axis=0))
        rhs = jnp.concatenate(cols, axis=1)
        scores.append(jnp.where(mask, jnp.dot(k2, rhs, preferred_element_type=F32), NEG_INF))
    st = jnp.concatenate(scores, axis=1)
    sink = jnp.concatenate([jnp.full((1, WINDOW), sink_ref[h] * LOG2_E, F32) for h in range(n_heads)], axis=1)
    m = jnp.maximum(jnp.max(st, axis=0, keepdims=True), sink)
    p = jnp.exp2(st - m)
    den = jnp.sum(p, axis=0, keepdims=True) + jnp.exp2(sink - m)
    inv_den = 1.0 / den
    probs_t = p.astype(BF16)

    for g in range(n_kv):
        cols = slice(g * Q_PER_KV * WINDOW, (g + 1) * Q_PER_KV * WINDOW)
        ot = jnp.dot(vcat_t[g * HEAD_DIM:(g + 1) * HEAD_DIM], probs_t[:, cols],
                     preferred_element_type=F32) * inv_den[:, cols]
        for pr in range(Q_PER_KV // heads_per_tile):
            pair = jnp.concatenate(
                [ot[:, (pr * heads_per_tile + u) * WINDOW:(pr * heads_per_tile + u + 1) * WINDOW]
                 for u in range(heads_per_tile)], axis=0)
            c = (g * Q_PER_KV) // heads_per_tile + pr
            o_ref[:, c * LANES:(c + 1) * LANES] = pair.T.astype(o_ref.dtype)


def _attn_prompt(qkv, sinks, tables, tables_t, n_rows, attn_w, kv_w):
    nb = n_rows // WINDOW
    kb, vb = attn_w // kv_w, attn_w // kv_w + 1

    def blk(b, s):
        return jnp.minimum(b, nb - 1)

    tab_spec = pl.BlockSpec((WINDOW, LANES), lambda b, s: (blk(b, s), 0))
    tab_t_spec = pl.BlockSpec((ROT_DIM // 2, WINDOW), lambda b, s: (0, blk(b, s)))
    return pl.pallas_call(
        functools.partial(_attn_prompt_kernel, n_kv=kv_w // HEAD_DIM, nb=nb),
        grid_spec=pltpu.PrefetchScalarGridSpec(
            num_scalar_prefetch=1,
            grid=(qkv.shape[0] // WINDOW,),
            in_specs=[pl.BlockSpec((WINDOW, attn_w), lambda b, s: (blk(b, s), 0)),
                      pl.BlockSpec((WINDOW, kv_w), lambda b, s: (blk(b, s), kb)),
                      pl.BlockSpec((WINDOW, kv_w), lambda b, s: (blk(b, s), vb)),
                      tab_spec, tab_spec, tab_spec, tab_t_spec, tab_t_spec],
            out_specs=[pl.BlockSpec((WINDOW, attn_w), lambda b, s: (b, 0)),
                       pl.BlockSpec((WINDOW, kv_w), lambda b, s: (blk(b, s), 0))],
            scratch_shapes=[pltpu.VMEM((WINDOW, kv_w), F32), pltpu.VMEM((kv_w, WINDOW), F32)]),
        out_shape=[jax.ShapeDtypeStruct((qkv.shape[0], attn_w), BF16),
                   jax.ShapeDtypeStruct((n_rows, kv_w), F32)],
        compiler_params=_params(1, 32),
        name="attn_prompt",
    )(sinks, qkv, qkv, qkv, *tables, *tables_t)


def _attn_sample_kernel(sink_ref, q_ref, k_ref, v_ref, ck_ref, cv_ref, cos_ref, sa_ref, sb_ref, attn_in_ref,
                        o_ref, nk_ref, nv_ref, *, n_kv, t_new):
    del attn_in_ref
    assert LANES // HEAD_DIM == 2 and Q_PER_KV % 2 == 0
    cos_t, sa_t, sb_t = cos_ref[...], sa_ref[...], sb_ref[...]
    q = _rope(q_ref[...], cos_t, sa_t, sb_t) * (HEAD_DIM ** -0.5 * LOG2_E)
    k = _rope(k_ref[...], cos_t, sa_t, sb_t)
    v = v_ref[...]
    n_cache = ck_ref.shape[1]
    n_keys = 2 * WINDOW
    n_seq = q.shape[0] // t_new
    n_heads = n_kv * Q_PER_KV
    zpad = jnp.zeros((n_keys - n_cache - t_new, k.shape[1]), F32)
    low = lax.broadcasted_iota(I32, (t_new, LANES), 1) < HEAD_DIM
    high = jnp.logical_not(low)

    qt = lax.broadcasted_iota(I32, (Q_PER_KV * t_new, n_keys), 0) & (t_new - 1)
    kj = lax.broadcasted_iota(I32, (Q_PER_KV * t_new, n_keys), 1)
    mask = (kj > qt + (n_cache - WINDOW)) & (kj <= qt + n_cache)

    def swap_halves(x):
        return pltpu.roll(x, HEAD_DIM, 1)

    scores, values = [], []
    for s_i in range(n_seq):
        ck, cv = ck_ref[s_i], cv_ref[s_i]
        kn, vn = k[s_i * t_new:(s_i + 1) * t_new], v[s_i * t_new:(s_i + 1) * t_new]
        nk_ref[s_i] = jnp.concatenate([ck[t_new:], kn], axis=0)
        nv_ref[s_i] = jnp.concatenate([cv[t_new:], vn], axis=0)
        kall = jnp.concatenate([ck, kn, zpad], axis=0).astype(BF16)
        values.append(jnp.concatenate([cv, vn, zpad], axis=0).astype(BF16))
        qs = q[s_i * t_new:(s_i + 1) * t_new]
        for g in range(n_kv):
            tile, odd = g // 2, g % 2
            parts = []
            for r in range(Q_PER_KV):
                h = g * Q_PER_KV + r
                x = qs[:, (h // 2) * LANES:(h // 2 + 1) * LANES]
                if h % 2 != odd:
                    x = swap_halves(x)
                parts.append(jnp.where(high if odd else low, x, 0.0))
            qg = jnp.concatenate(parts, axis=0).astype(BF16)
            k2 = kall[:, tile * LANES:(tile + 1) * LANES]
            sg = lax.dot_general(qg, k2, (((1,), (1,)), ((), ())), preferred_element_type=F32)
            scores.append(jnp.where(mask, sg, NEG_INF))
    s = jnp.concatenate(scores, axis=0)
    sink_seq = jnp.concatenate([jnp.full((t_new, 1), sink_ref[h] * LOG2_E, F32) for h in range(n_heads)], axis=0)
    sink = jnp.concatenate([sink_seq] * n_seq, axis=0)
    p, inv_den = _softmax2_with_sink(s, sink)
    probs = p.astype(BF16)

    seq_outs = []
    for s_i in range(n_seq):
        out_tiles = []
        for g in range(n_kv):
            tile, odd = g // 2, g % 2
            r0 = (s_i * n_kv + g) * Q_PER_KV * t_new
            v2 = values[s_i][:, tile * LANES:(tile + 1) * LANES]
            rows = slice(r0, r0 + Q_PER_KV * t_new)
            og = jnp.dot(probs[rows], v2, preferred_element_type=F32) * inv_den[rows]
            for pr in range(Q_PER_KV // 2):
                even = og[(2 * pr) * t_new:(2 * pr + 1) * t_new]
                oddh = og[(2 * pr + 1) * t_new:(2 * pr + 2) * t_new]
                if odd:
                    even = swap_halves(even)
                else:
                    oddh = swap_halves(oddh)
                out_tiles.append(jnp.where(low, even, oddh))
        seq_outs.append(jnp.concatenate(out_tiles, axis=1))
    o_ref[...] = jnp.concatenate(seq_outs, axis=0).astype(o_ref.dtype)


def _attn_sample(qkv, attn, row0, cache_k, cache_v, sinks, tables, n_seq, t_new, attn_w, kv_w):
    rows = SEQS_PER_STEP * t_new
    rb0 = row0 // rows
    kb, vb = attn_w // kv_w, attn_w // kv_w + 1
    n_cache = cache_k.shape[1]
    tab_spec = pl.BlockSpec((rows, LANES), lambda i, s: (0, 0))
    cache_spec = pl.BlockSpec((SEQS_PER_STEP, n_cache, kv_w), lambda i, s: (i, 0, 0))
    return pl.pallas_call(
        functools.partial(_attn_sample_kernel, n_kv=kv_w // HEAD_DIM, t_new=t_new),
        grid_spec=pltpu.PrefetchScalarGridSpec(
            num_scalar_prefetch=1,
            grid=(n_seq // SEQS_PER_STEP,),
            in_specs=[pl.BlockSpec((rows, attn_w), lambda i, s: (rb0 + i, 0)),
                      pl.BlockSpec((rows, kv_w), lambda i, s: (rb0 + i, kb)),
                      pl.BlockSpec((rows, kv_w), lambda i, s: (rb0 + i, vb)),
                      cache_spec, cache_spec, tab_spec, tab_spec, tab_spec,
                      pl.BlockSpec(memory_space=pl.ANY)],
            out_specs=[pl.BlockSpec((rows, attn_w), lambda i, s: (rb0 + i, 0)), cache_spec, cache_spec]),
        out_shape=[jax.ShapeDtypeStruct(attn.shape, attn.dtype),
                   jax.ShapeDtypeStruct(cache_k.shape, F32),
                   jax.ShapeDtypeStruct(cache_v.shape, F32)],
        input_output_aliases={9: 0},
        compiler_params=_params(1, 32),
        name="attn_sample",
    )(sinks, qkv, qkv, qkv, cache_k, cache_v, *tables, attn)


def _pool_prompt_kernel(z_ref, halo_ref, w_ref, sc_ref, o_ref, ext_ref, *, tm, pg, nb):
    i = pl.program_id(0)

    @pl.when(i >= nb)
    def _():
        o_ref[...] = jnp.zeros_like(o_ref)

    @pl.when(i < nb)
    def _():
        _pool_prompt_block(z_ref, halo_ref, w_ref, sc_ref, o_ref, ext_ref, tm=tm, pg=pg)


def _pool_prompt_block(z_ref, halo_ref, w_ref, sc_ref, o_ref, ext_ref, *, tm, pg):
    i = pl.program_id(0)
    ext_ref[0:POOL_HALO, :] = jnp.where(i > 0, halo_ref[...], 0.0)
    ext_ref[POOL_HALO:, :] = z_ref[...]
    pos = i * tm + lax.broadcasted_iota(I32, (tm, 1), 0)
    for g, w in enumerate(POOL_WINDOWS):
        c0, c1 = g * pg, (g + 1) * pg
        cur = ext_ref[POOL_HALO:POOL_HALO + tm, c0:c1]
        acc = cur
        for d in range(1, w):
            acc = acc + ext_ref[POOL_HALO - d:POOL_HALO - d + tm, c0:c1]
        cnt = jnp.minimum(pos + 1, w).astype(F32)
        pooled = (acc / cnt - cur).astype(BF16)
        mixed = jnp.dot(pooled, w_ref[g].astype(BF16), preferred_element_type=F32) * sc_ref[:, c0:c1]
        o_ref[:, c0:c1] = mixed.astype(o_ref.dtype)


def _pool_prompt(z, n_rows, w_grp, scale):
    n, pw = z.shape
    tm = TM_ROW
    hb = tm // POOL_HALO
    return pl.pallas_call(
        functools.partial(_pool_prompt_kernel, tm=tm, pg=pw // len(POOL_WINDOWS), nb=n_rows // tm),
        grid=(n // tm,),
        in_specs=[pl.BlockSpec((tm, pw), lambda i: (i, 0)),
                  pl.BlockSpec((POOL_HALO, pw), lambda i: (jnp.maximum(i * hb - 1, 0), 0)),
                  pl.BlockSpec(w_grp.shape, lambda i: (0, 0, 0)),
                  pl.BlockSpec((1, pw), lambda i: (0, 0))],
        out_specs=pl.BlockSpec((tm, pw), lambda i: (i, 0)),
        out_shape=jax.ShapeDtypeStruct((n, pw), BF16),
        scratch_shapes=[pltpu.VMEM((tm + POOL_HALO, pw), F32)],
        compiler_params=_params(1, 40),
        name="pool_prompt",
    )(z, z, w_grp, scale.reshape(1, pw))


def _pool_sample_kernel(st_ref, z_ref, w_ref, sc_ref, mixed_in_ref, o_ref, ns_ref, ext_ref, *, pg, t_new):
    del mixed_in_ref
    n_st = st_ref.shape[1]
    seqs = st_ref.shape[0]
    hist = ext_ref.shape[1] - t_new
    ext_ref[:, hist - n_st:hist, :] = st_ref[...]
    ext_ref[:, hist:, :] = z_ref[...]
    ns_ref[...] = ext_ref[:, hist + t_new - n_st:, :]
    for g, w in enumerate(POOL_WINDOWS):
        c0, c1 = g * pg, (g + 1) * pg
        cur = ext_ref[:, hist:, c0:c1]
        acc = cur
        for d in range(1, w):
            acc = acc + ext_ref[:, hist - d:hist - d + t_new, c0:c1]
        pooled = (acc / float(w) - cur).reshape(seqs * t_new, pg).astype(BF16)
        mixed = jnp.dot(pooled, w_ref[g].astype(BF16), preferred_element_type=F32) * sc_ref[:, c0:c1]
        o_ref[:, c0:c1] = mixed.astype(o_ref.dtype)


def _pool_sample(state, z, mixed, row0, w_grp, scale, t_new):
    n_seq, n_st, pw = state.shape
    assert t_new == SUBLANES and n_st <= POOL_HALO - 1
    z3 = z.reshape(z.shape[0] // t_new, t_new, pw)
    sb0 = row0 // t_new // POOL_SEQS
    rows = POOL_SEQS * t_new
    st_spec = pl.BlockSpec((POOL_SEQS, n_st, pw), lambda i: (i, 0, 0))
    return pl.pallas_call(
        functools.partial(_pool_sample_kernel, pg=pw // len(POOL_WINDOWS), t_new=t_new),
        grid=(n_seq // POOL_SEQS,),
        in_specs=[st_spec,
                  pl.BlockSpec((POOL_SEQS, t_new, pw), lambda i: (sb0 + i, 0, 0)),
                  pl.BlockSpec(w_grp.shape, lambda i: (0, 0, 0)),
                  pl.BlockSpec((1, pw), lambda i: (0, 0)),
                  pl.BlockSpec(memory_space=pl.ANY)],
        out_specs=[pl.BlockSpec((rows, pw), lambda i: (row0 // rows + i, 0)), st_spec],
        out_shape=[jax.ShapeDtypeStruct(mixed.shape, mixed.dtype), jax.ShapeDtypeStruct(state.shape, F32)],
        scratch_shapes=[pltpu.VMEM((POOL_SEQS, POOL_HALO + t_new, pw), F32)],
        input_output_aliases={4: 0},
        compiler_params=_params(1, 40),
        name="pool_sample",
    )(state, z3, w_grp, scale.reshape(1, pw), mixed)


def _merge_kernel(a1_ref, a2_ref, w1_hbm, w2_hbm, ga_ref, gb_ref, ba_ref, bb_ref, o_ref,
                  w1f_ref, w2f_ref, w1b_ref, w2b_ref, sem):
    _column_weights_step((w1_hbm, w2_hbm), (0, 0), (w1f_ref, w2f_ref), (w1b_ref, w2b_ref), sem)
    a_up = jnp.dot(a1_ref[...], w1b_ref[...], preferred_element_type=F32)
    b_up = jnp.dot(a2_ref[...], w2b_ref[...], preferred_element_type=F32)
    merged = (jax.nn.sigmoid(ga_ref[...] + ba_ref[...]) * a_up
              + jax.nn.sigmoid(gb_ref[...] + bb_ref[...]) * b_up)
    o_ref[...] = merged.astype(o_ref.dtype)


def _merge(attn, mixed, w_attn_up, w_pool_up, gates, b_gate):
    n, ka = attn.shape
    kp = mixed.shape[1]
    d = w_attn_up.shape[1]
    tm, tn = TM_WIDE, TN_WIDE
    nj = d // tn
    any_spec = pl.BlockSpec(memory_space=pl.ANY)
    return pl.pallas_call(
        _merge_kernel,
        grid=(nj, n // tm),
        in_specs=[pl.BlockSpec((tm, ka), lambda j, i: (i, 0)),
                  pl.BlockSpec((tm, kp), lambda j, i: (i, 0)),
                  any_spec, any_spec,
                  pl.BlockSpec((tm, tn), lambda j, i: (i, j)),
                  pl.BlockSpec((tm, tn), lambda j, i: (i, nj + j)),
                  pl.BlockSpec((1, tn), lambda j, i: (0, j)),
                  pl.BlockSpec((1, tn), lambda j, i: (0, nj + j))],
        out_specs=pl.BlockSpec((tm, tn), lambda j, i: (i, j)),
        out_shape=jax.ShapeDtypeStruct((n, d), BF16),
        scratch_shapes=_weight_scratch([(ka, tn), (kp, tn)]),
        compiler_params=_params(2, 56),
        name="merge",
    )(attn, mixed, w_attn_up, w_pool_up, gates, gates, b_gate, b_gate)


def _out_proj_kernel(a_ref, w_hbm, ha_ref, hb_ref, o_ref, wf_ref, wb_ref, sem, *, nb_first):
    _column_weights_step((w_hbm,), (0,), (wf_ref,), (wb_ref,), sem)
    h = jnp.where(pl.program_id(1) < nb_first, ha_ref[...], hb_ref[...])
    o_ref[...] = h + jnp.dot(a_ref[...], wb_ref[...], preferred_element_type=F32)


def _out_proj(merged, w_out, ha, hb):
    n, d = merged.shape
    tm, tn = TM_WIDE, TN_WIDE
    nb_first = ha.shape[0] // tm
    h_first = pl.BlockSpec((tm, tn), lambda j, i: (jnp.minimum(i, nb_first - 1), j))
    h_last = pl.BlockSpec((tm, tn), lambda j, i: (jnp.maximum(i - nb_first, 0), j))
    return pl.pallas_call(
        functools.partial(_out_proj_kernel, nb_first=nb_first),
        grid=(d // tn, n // tm),
        in_specs=[pl.BlockSpec((tm, d), lambda j, i: (i, 0)),
                  pl.BlockSpec(memory_space=pl.ANY),
                  h_first, h_last],
        out_specs=pl.BlockSpec((tm, tn), lambda j, i: (i, j)),
        out_shape=jax.ShapeDtypeStruct((n, d), F32),
        scratch_shapes=_weight_scratch([(d, tn)]),
        compiler_params=_params(2, 56),
        name="out_proj",
    )(merged, w_out, ha, hb)


def _router_kernel(h_ref, g_ref, wr_ref, br_ref, info_ref, cw_ref, xf_ref, cnt_ref, base_ref):
    tm = h_ref.shape[0]

    @pl.when(pl.program_id(0) == 0)
    def _():
        base_ref[...] = jnp.zeros_like(base_ref)

    xf = _rms_rows(h_ref[...], g_ref[...]).astype(BF16)
    xf_ref[...] = xf.reshape(xf_ref.shape)
    logits = jnp.dot(xf, wr_ref[...].astype(BF16), preferred_element_type=F32) + br_ref[...]
    lane = lax.broadcasted_iota(I32, logits.shape, 1)
    big = jnp.int32(1 << 20)
    ninf = jnp.float32(-jnp.inf)

    is_g = lane < N_GROUPS
    gl = jnp.where(is_g, logits, ninf)
    gmax = jnp.max(gl, axis=-1, keepdims=True)
    gden = jnp.sum(jnp.exp(gl - gmax), axis=-1, keepdims=True)
    g_p = 1.0 / gden
    g_idx = jnp.min(jnp.where(gl == gmax, lane, big), axis=-1, keepdims=True)

    e_lane = lane - N_GROUPS
    in_grp = (e_lane >= g_idx * EXPERTS_PER_GROUP) & (e_lane < (g_idx + 1) * EXPERTS_PER_GROUP)
    el = jnp.where(in_grp, logits, ninf)
    m1 = jnp.max(el, axis=-1, keepdims=True)
    i1 = jnp.min(jnp.where(in_grp & (el == m1), e_lane, big), axis=-1, keepdims=True)
    el2 = jnp.where(e_lane == i1, ninf, el)
    m2 = jnp.max(el2, axis=-1, keepdims=True)
    i2 = jnp.min(jnp.where(in_grp & (el2 == m2), e_lane, big), axis=-1, keepdims=True)
    t = jnp.exp(m2 - m1)
    den = 1.0 + t
    c1 = g_p * (1.0 / den)
    c2 = g_p * (t / den)
    cw_ref[...] = jnp.where(lane == 0, c1, jnp.where(lane == 1, c2, 0.0))

    hit1, hit2 = lane == i1, lane == i2
    onehot = (hit1 | hit2).astype(BF16)
    earlier = (lax.broadcasted_iota(I32, (tm, tm), 1) < lax.broadcasted_iota(I32, (tm, tm), 0)).astype(BF16)
    before = jnp.dot(earlier, onehot, preferred_element_type=F32) + base_ref[...]
    r1 = jnp.sum(jnp.where(hit1, before, 0.0), axis=-1, keepdims=True).astype(I32)
    r2 = jnp.sum(jnp.where(hit2, before, 0.0), axis=-1, keepdims=True).astype(I32)
    base_ref[...] += jnp.sum(onehot.astype(F32), axis=0, keepdims=True)
    cnt_ref[...] = base_ref[...].astype(I32)
    info_ref[...] = jnp.where(lane == 0, i1, jnp.where(lane == 1, i2,
                              jnp.where(lane == 2, r1, jnp.where(lane == 3, r2, 0))))


def _router(h, g_ffn, w_rg, b_rg, w_re, b_re):
    n, d = h.shape
    n_log = N_GROUPS + N_EXPERTS
    wr = jnp.concatenate([w_rg, w_re, jnp.zeros((d, LANES - n_log), F32)], axis=1)
    br = jnp.concatenate([b_rg, b_re, jnp.zeros((LANES - n_log,), F32)]).reshape(1, LANES)
    lane_spec = pl.BlockSpec((TM_ROW, LANES), lambda i: (i, 0))
    return pl.pallas_call(
        _router_kernel,
        grid=(n // TM_ROW,),
        in_specs=[pl.BlockSpec((TM_ROW, d), lambda i: (i, 0)),
                  pl.BlockSpec((1, d), lambda i: (0, 0)),
                  pl.BlockSpec((d, LANES), lambda i: (0, 0)),
                  pl.BlockSpec((1, LANES), lambda i: (0, 0))],
        out_specs=[lane_spec, lane_spec,
                   pl.BlockSpec((TM_ROW, d // LANES, LANES), lambda i: (i, 0, 0)),
                   pl.BlockSpec((1, LANES), lambda i: (0, 0))],
        out_shape=[jax.ShapeDtypeStruct((n, LANES), I32), jax.ShapeDtypeStruct((n, LANES), F32),
                   jax.ShapeDtypeStruct((n, d // LANES, LANES), BF16), jax.ShapeDtypeStruct((1, LANES), I32)],
        scratch_shapes=[pltpu.VMEM((1, LANES), F32)],
        compiler_params=_params(1, 40),
        name="router",
    )(h, g_ffn.reshape(1, d), wr, br)


def _dispatch_plan(info, counts, n_tiles):
    counts = counts[0, :N_EXPERTS]
    tiles_per = (counts + TM_E - 1) // TM_E
    tiles_end = jnp.cumsum(tiles_per)
    row_start = (tiles_end - tiles_per) * TM_E
    eid, rank = info[:, :TOP_K], info[:, TOP_K:2 * TOP_K]
    onehot = (eid[:, :, None] == jnp.arange(N_EXPERTS, dtype=I32)[None, None, :]).astype(I32)
    pos = (jnp.sum(onehot * row_start[None, None, :], axis=-1) + rank).reshape(-1).astype(I32)
    n_used = tiles_end[-1:].astype(I32)
    tile_ids = jnp.minimum(jnp.arange(n_tiles, dtype=I32), n_used - 1)
    tile_expert = jnp.sum((tile_ids[:, None] >= tiles_end[None, :]).astype(I32), axis=1).astype(I32)
    last_tile_row = (jnp.clip(tiles_end - 1, 0, n_tiles - 1) * TM_E).astype(I32)
    starts = jnp.concatenate([jnp.ones((1,), I32), (tile_expert[1:] != tile_expert[:-1]).astype(I32)])
    run_idx = jnp.cumsum(starts) - 1
    n_runs = run_idx[-1:] + 1
    run_expert = jnp.zeros((n_tiles,), I32).at[run_idx].set(tile_expert)
    next_expert = run_expert[(run_idx + 1) % n_runs]
    rows_end = row_start + counts
    valid = jnp.clip(rows_end[tile_expert] - jnp.arange(n_tiles, dtype=I32) * TM_E, 0, TM_E)
    plan = (tile_expert, n_used, run_idx.astype(I32), next_expert.astype(I32), n_runs.astype(I32),
            valid.astype(I32))
    return pos, plan, last_tile_row


def _dispatch_kernel(pos_ref, ltr_ref, nu_ref, xf_ref, xs_hbm, zero_ref, sem):
    i = pl.program_id(0)
    n_tiles = xs_hbm.shape[0] // TM_E

    def clear_tile(row):
        return pltpu.make_async_copy(zero_ref, xs_hbm.at[pl.ds(pl.multiple_of(row, TM_E), TM_E)], sem)

    @pl.when(i == 0)
    def _():
        zero_ref[...] = jnp.zeros_like(zero_ref)
        for e in range(N_EXPERTS):
            clear_tile(ltr_ref[e]).start()

        def start_unused(t, carry):
            clear_tile(t * TM_E).start()
            return carry

        def wait_one(t, carry):
            clear_tile(0).wait()
            return carry

        lax.fori_loop(nu_ref[0], n_tiles, start_unused, 0)
        lax.fori_loop(nu_ref[0] - N_EXPERTS, n_tiles, wait_one, 0)

    base = i * TM_DISPATCH

    def issue(r, carry):
        for k in range(TOP_K):
            dst_row = pos_ref[(base + r) * TOP_K + k]
            pltpu.make_async_copy(xf_ref.at[pl.ds(r, 1)], xs_hbm.at[pl.ds(dst_row, 1)], sem).start(priority=k % 2)
        return carry

    lax.fori_loop(0, TM_DISPATCH, issue, 0, unroll=4)
    for k in range(TOP_K):
        pltpu.make_async_copy(xf_ref, xs_hbm.at[pl.ds(0, TM_DISPATCH)], sem).wait()


def _dispatch(pos, last_tile_row, n_used, xf3, n_tiles):
    n, c, l = xf3.shape
    return pl.pallas_call(
        _dispatch_kernel,
        grid_spec=pltpu.PrefetchScalarGridSpec(
            num_scalar_prefetch=3,
            grid=(n // TM_DISPATCH,),
            in_specs=[pl.BlockSpec((TM_DISPATCH, c, l), lambda i, *_: (i, 0, 0))],
            out_specs=pl.BlockSpec(memory_space=pl.ANY),
            scratch_shapes=[pltpu.VMEM((TM_E, c, l), xf3.dtype), pltpu.SemaphoreType.DMA(())]),
        out_shape=jax.ShapeDtypeStruct((n_tiles * TM_E, c, l), xf3.dtype),
        compiler_params=_params(1, 24),
        name="moe_dispatch",
    )(pos, last_tile_row, n_used, xf3)


def _expert_weights_step(plan_refs, w_hbms, wbuf_ref, wb_refs, sem, tn):
    te_ref, _, run_ref, nxt_ref, nr_ref = plan_refs
    j, t = pl.program_id(0), pl.program_id(1)

    def copies(e, jj):
        col = pl.multiple_of(jj * tn, tn)
        return [pltpu.make_async_copy(w.at[e, :, pl.ds(col, tn)], wbuf_ref.at[m], sem)
                for m, w in enumerate(w_hbms)]

    @pl.when((t == 0) | (te_ref[t] != te_ref[jnp.maximum(t - 1, 0)]))
    def _():
        @pl.when((j == 0) & (t == 0))
        def _():
            for c in copies(te_ref[t], j):
                c.start()

        for c in copies(te_ref[t], j):
            c.wait()
        for m, wb_ref in enumerate(wb_refs):
            _cast_rows(wbuf_ref.at[m], wb_ref)

        next_j = jnp.where(run_ref[t] + 1 == nr_ref[0], j + 1, j)

        @pl.when(next_j < pl.num_programs(0))
        def _():
            for c in copies(nxt_ref[t], next_j):
                c.start()


def _by_valid_rows(valid, full_fn, zero_fn):
    half = TM_E // 2

    @pl.when(valid > half)
    def _():
        full_fn(TM_E)

    @pl.when((valid > 0) & (valid <= half))
    def _():
        full_fn(half)
        zero_fn(half)

    @pl.when(valid == 0)
    def _():
        zero_fn(0)


def _moe_up_kernel(te_ref, nu_ref, run_ref, nxt_ref, nr_ref, valid_ref, xs_ref, wg_hbm, wu_hbm, o_ref,
                   wbuf_ref, wgb_ref, wub_ref, sem):
    t = pl.program_id(1)
    _expert_weights_step((te_ref, nu_ref, run_ref, nxt_ref, nr_ref), (wg_hbm, wu_hbm), wbuf_ref,
                         (wgb_ref, wub_ref), sem, o_ref.shape[1])

    def compute(rows):
        x = xs_ref[0:rows].reshape(rows, -1)
        gate = jnp.dot(x, wgb_ref[...], preferred_element_type=F32)
        up = jnp.dot(x, wub_ref[...], preferred_element_type=F32)
        o_ref[0:rows] = (jax.nn.silu(gate) * up).astype(o_ref.dtype)

    def clear(row0):
        o_ref[row0:] = jnp.zeros((TM_E - row0,) + o_ref.shape[1:], o_ref.dtype)

    _by_valid_rows(valid_ref[t], compute, clear)


def _moe_up(plan, xs3, w_gate, w_up, n_tiles):
    d, f = w_gate.shape[1], w_gate.shape[2]
    tf = TN
    any_spec = pl.BlockSpec(memory_space=pl.ANY)
    return pl.pallas_call(
        _moe_up_kernel,
        grid_spec=pltpu.PrefetchScalarGridSpec(
            num_scalar_prefetch=len(plan),
            grid=(f // tf, n_tiles),
            in_specs=[pl.BlockSpec((TM_E,) + xs3.shape[1:], lambda j, t, te, nu, *_: (jnp.minimum(t, nu[0] - 1), 0, 0)),
                      any_spec, any_spec],
            out_specs=pl.BlockSpec((TM_E, tf), lambda j, t, *_: (t, j)),
            scratch_shapes=[pltpu.VMEM((2, d, tf), F32), pltpu.VMEM((d, tf), BF16), pltpu.VMEM((d, tf), BF16),
                            pltpu.SemaphoreType.DMA(())]),
        out_shape=jax.ShapeDtypeStruct((n_tiles * TM_E, f), BF16),
        compiler_params=_params(2, 56),
        name="moe_up",
    )(*plan, xs3, w_gate, w_up)


def _moe_down_kernel(te_ref, nu_ref, run_ref, nxt_ref, nr_ref, valid_ref, hid_ref, wd_hbm, o_ref,
                     wbuf_ref, wdb_ref, sem):
    t = pl.program_id(1)
    _expert_weights_step((te_ref, nu_ref, run_ref, nxt_ref, nr_ref), (wd_hbm,), wbuf_ref, (wdb_ref,), sem,
                         wdb_ref.shape[1])

    def compute(rows):
        y = jnp.dot(hid_ref[0:rows], wdb_ref[...], preferred_element_type=F32)
        o_ref[0:rows] = y.reshape((rows,) + o_ref.shape[1:])

    def clear(row0):
        o_ref[row0:] = jnp.zeros((TM_E - row0,) + o_ref.shape[1:], o_ref.dtype)

    _by_valid_rows(valid_ref[t], compute, clear)


def _moe_down(plan, hid, w_down, n_tiles):
    f, d = w_down.shape[1], w_down.shape[2]
    tn = 4 * TN
    return pl.pallas_call(
        _moe_down_kernel,
        grid_spec=pltpu.PrefetchScalarGridSpec(
            num_scalar_prefetch=len(plan),
            grid=(d // tn, n_tiles),
            in_specs=[pl.BlockSpec((TM_E, f), lambda j, t, *_: (t, 0)),
                      pl.BlockSpec(memory_space=pl.ANY)],
            out_specs=pl.BlockSpec((TM_E, tn // LANES, LANES), lambda j, t, *_: (t, j, 0)),
            scratch_shapes=[pltpu.VMEM((1, f, tn), F32), pltpu.VMEM((f, tn), BF16),
                            pltpu.SemaphoreType.DMA(())]),
        out_shape=jax.ShapeDtypeStruct((n_tiles * TM_E, d // LANES, LANES), F32),
        compiler_params=_params(2, 40),
        name="moe_down",
    )(*plan, hid, w_down)


def _combine_kernel(pos_ref, h_ref, cw_ref, y_hbm, g_ref, ho_ref, xo_ref, buf_ref, sem):
    i = pl.program_id(0)
    tm, d = h_ref.shape

    def row_copy(slot, k, r, src_row):
        return pltpu.make_async_copy(y_hbm.at[pl.ds(src_row, 1)], buf_ref.at[slot, k, pl.ds(r, 1)], sem.at[slot])

    def issue_step(step, slot):
        def body(r, carry):
            for k in range(TOP_K):
                row_copy(slot, k, r, pos_ref[(step * tm + r) * TOP_K + k]).start(priority=k % 2)
            return carry
        lax.fori_loop(0, tm, body, 0, unroll=4)

    @pl.when(i == 0)
    def _():
        issue_step(0, 0)

    @pl.when(i + 1 < pl.num_programs(0))
    def _():
        issue_step(i + 1, (i + 1) % 2)

    slot = i % 2
    for k in range(TOP_K):
        pltpu.make_async_copy(y_hbm.at[pl.ds(0, tm)], buf_ref.at[slot, k], sem.at[slot]).wait()
    cw = cw_ref[...]
    y1 = buf_ref[slot, 0].reshape(tm, d)
    y2 = buf_ref[slot, 1].reshape(tm, d)
    h2 = h_ref[...] + (cw[:, 0:1] * y1 + cw[:, 1:2] * y2)
    ho_ref[...] = h2
    xo_ref[...] = _rms_rows(h2, g_ref[...]).astype(xo_ref.dtype)


def _combine(pos, h, cw, y3, g_ple):
    n, d = h.shape
    row_spec = pl.BlockSpec((TM_ROW, d), lambda i, p: (i, 0))
    return pl.pallas_call(
        _combine_kernel,
        grid_spec=pltpu.PrefetchScalarGridSpec(
            num_scalar_prefetch=1,
            grid=(n // TM_ROW,),
            in_specs=[row_spec,
                      pl.BlockSpec((TM_ROW, LANES), lambda i, p: (i, 0)),
                      pl.BlockSpec(memory_space=pl.ANY),
                      pl.BlockSpec((1, d), lambda i, p: (0, 0))],
            out_specs=[row_spec, row_spec],
            scratch_shapes=[pltpu.VMEM((2, TOP_K, TM_ROW) + y3.shape[1:], F32), pltpu.SemaphoreType.DMA((2,))]),
        out_shape=[jax.ShapeDtypeStruct((n, d), F32), jax.ShapeDtypeStruct((n, d), BF16)],
        compiler_params=_params(1, 56),
        name="moe_combine",
    )(pos, h, cw, y3, g_ple.reshape(1, d))


def _ple_kernel(xn_ref, wg_hbm, bg_ref, pa_ref, pb_ref, wp_hbm, h_ref, o_ref,
                wgf_ref, wpf_ref, wgb_ref, wpb_ref, sem, *, nb_first):
    _column_weights_step((wg_hbm, wp_hbm), (0, 0), (wgf_ref, wpf_ref), (wgb_ref, wpb_ref), sem)
    p = jnp.where(pl.program_id(1) < nb_first, pa_ref[...], pb_ref[...]).astype(BF16)
    tn = o_ref.shape[1]
    step = min(tn, TN)
    for c in range(0, tn, step):
        cols = slice(c, c + step)
        gate = jax.nn.sigmoid(jnp.dot(xn_ref[...], wgb_ref[:, cols], preferred_element_type=F32) + bg_ref[:, cols])
        ple = jnp.dot(p, wpb_ref[:, cols], preferred_element_type=F32)
        o_ref[:, cols] = h_ref[:, cols] + gate * ple


def _ple(xn, w_gate, b_gate, pa, pb, w_proj, h):
    n, d = h.shape
    pd = pa.shape[1]
    tm, tn = TM_WIDE, TN_WIDE
    nb_first = pa.shape[0] // tm
    p_first = pl.BlockSpec((tm, pd), lambda j, i: (jnp.minimum(i, nb_first - 1), 0))
    p_last = pl.BlockSpec((tm, pd), lambda j, i: (jnp.maximum(i - nb_first, 0), 0))
    any_spec = pl.BlockSpec(memory_space=pl.ANY)
    return pl.pallas_call(
        functools.partial(_ple_kernel, nb_first=nb_first),
        grid=(d // tn, n // tm),
        in_specs=[pl.BlockSpec((tm, d), lambda j, i: (i, 0)),
                  any_spec,
                  pl.BlockSpec((1, tn), lambda j, i: (0, j)),
                  p_first, p_last,
                  any_spec,
                  pl.BlockSpec((tm, tn), lambda j, i: (i, j))],
        out_specs=pl.BlockSpec((tm, tn), lambda j, i: (i, j)),
        out_shape=jax.ShapeDtypeStruct((n, d), F32),
        scratch_shapes=_weight_scratch([(d, tn), (pd, tn)]),
        compiler_params=_params(2, 58),
        name="ple",
    )(xn, w_gate, b_gate.reshape(1, d), pa, pb, w_proj, h)


def _layer(xn, h_p, h_s, p_p, p_s, cache_k, cache_v, state_pool, lw):
    n, d = xn.shape
    n_prompt = h_p.shape[0]
    n_seq, n_cache, n_kv, _ = cache_k.shape
    t_new = (n - n_prompt) // n_seq
    kv_w = n_kv * HEAD_DIM
    attn_w = lw['w_attn_up'].shape[0]
    pool_w = lw['w_pool_up'].shape[0]
    n_pool = state_pool.shape[1]

    qkv = _in_proj(xn, lw['w_in'], 0, attn_w + 2 * kv_w, "in_proj_qkv")
    z = _in_proj(xn, lw['w_in'], attn_w + 2 * kv_w, pool_w, "in_proj_pool")
    gates = _in_proj(xn, lw['w_in'], attn_w + 2 * kv_w + pool_w, 2 * d, "in_proj_gates")

    tab_p = _rope_tables(jnp.arange(n_prompt, dtype=F32))
    tab_s = tuple(jnp.tile(a, (SEQS_PER_STEP, 1))
                  for a in _rope_tables(PAST_LEN + jnp.arange(t_new, dtype=F32)))
    tab_p_t = _rope_tables_t(jnp.arange(n_prompt, dtype=F32))
    attn, k_rot_p = _attn_prompt(qkv, lw['attn_sinks'], tab_p, tab_p_t, n_prompt, attn_w, kv_w)
    attn, new_k_s, new_v_s = _attn_sample(
        qkv, attn, n_prompt, cache_k.reshape(n_seq, n_cache, kv_w), cache_v.reshape(n_seq, n_cache, kv_w),
        lw['attn_sinks'], tab_s, n_seq, t_new, attn_w, kv_w)

    mixed = _pool_prompt(z, n_prompt, lw['w_pool_grp'], lw['pool_scale'])
    mixed, new_z_s = _pool_sample(state_pool, z, mixed, n_prompt, lw['w_pool_grp'], lw['pool_scale'], t_new)

    merged = _merge(attn, mixed, lw['w_attn_up'], lw['w_pool_up'], gates, lw['b_gate'].reshape(1, 2 * d))
    h1 = _out_proj(merged, lw['w_out'], h_p, h_s)

    n_tiles = (n * TOP_K) // TM_E + N_EXPERTS
    info, cw, xf3, counts = _router(h1, lw['g_ffn'], lw['w_route_group'], lw['b_route_group'],
                                    lw['w_route_expert'], lw['b_route_expert'])
    pos, plan, last_tile_row = _dispatch_plan(info, counts, n_tiles)
    xs3 = _dispatch(pos, last_tile_row, plan[1], xf3, n_tiles)
    hid = _moe_up(plan, xs3, lw['w_exp_gate'], lw['w_exp_up'], n_tiles)
    y3 = _moe_down(plan, hid, lw['w_exp_down'], n_tiles)
    h2, xn2 = _combine(pos, h1, cw, y3, lw['g_ple'])

    h3 = _ple(xn2, lw['w_ple_gate'], lw['b_ple_gate'], p_p, p_s, lw['w_ple_proj'], h2)

    new_k_p = k_rot_p[n_prompt - n_cache:].reshape(1, n_cache, n_kv, HEAD_DIM)
    new_v_p = qkv[n_prompt - n_cache:n_prompt, attn_w + kv_w:].reshape(1, n_cache, n_kv, HEAD_DIM)
    new_z_p = z[n_prompt - n_pool:n_prompt].reshape(1, n_pool, pool_w)
    return (h3, new_k_p, new_v_p, new_z_p,
            new_k_s.reshape(cache_k.shape), new_v_s.reshape(cache_v.shape), new_z_s)


def kernel(x_prompt, x_sample, cache_k, cache_v, state_pool, p_prompt, p_sample, g_mix, w_in, b_gate, attn_sinks, w_pool_grp, pool_scale, w_attn_up, w_pool_up, w_out, g_ffn, w_route_group, b_route_group, w_route_expert, b_route_expert, w_exp_gate, w_exp_up, w_exp_down, g_ple, w_ple_gate, b_ple_gate, w_ple_proj, g_final):
    batch, seq, d = x_prompt.shape
    n_seq, t_new, _ = x_sample.shape
    depth = w_in.shape[0]
    assert batch == 1, "prompt rows are treated as one sequence"
    n_prompt = batch * seq
    n_sample = n_seq * t_new
    n = n_prompt + n_sample
    assert n_prompt % TM == 0 and n_sample % TM == 0 and (n * TOP_K) % TM_E == 0 and n % TM_DISPATCH == 0
    assert n_seq % SEQS_PER_STEP == 0 and n_seq % POOL_SEQS == 0 and t_new & (t_new - 1) == 0
    assert cache_k.shape[2] == WINDOW and state_pool.shape[2] == POOL_HALO - 1
    assert depth == 1, "one layer per step"

    weights = dict(g_mix=g_mix, w_in=w_in, b_gate=b_gate, attn_sinks=attn_sinks, w_pool_grp=w_pool_grp,
                   pool_scale=pool_scale, w_attn_up=w_attn_up, w_pool_up=w_pool_up, w_out=w_out, g_ffn=g_ffn,
                   w_route_group=w_route_group, b_route_group=b_route_group, w_route_expert=w_route_expert,
                   b_route_expert=b_route_expert, w_exp_gate=w_exp_gate, w_exp_up=w_exp_up,
                   w_exp_down=w_exp_down, g_ple=g_ple, w_ple_gate=w_ple_gate, b_ple_gate=b_ple_gate,
                   w_ple_proj=w_ple_proj)
    lw = {name: w[0] for name, w in weights.items()}
    h_p = x_prompt.reshape(n_prompt, d)
    h_s = x_sample.reshape(n_sample, d)
    xn = _rmsnorm_in2(h_p, h_s, lw['g_mix'], BF16, "norm_mix")
    out = _layer(xn, h_p, h_s, p_prompt[0].reshape(n_prompt, -1), p_sample[0].reshape(n_sample, -1),
                 cache_k[0], cache_v[0], state_pool[0], lw)
    y_p, y_s = _rmsnorm_out2(out[0], g_final, n_prompt, "norm_final")
    return (y_p.reshape(batch, seq, d), y_s.reshape(n_seq, t_new, d)) + tuple(piece[None] for piece in out[1:])
```

```python
import functools

import jax
import jax.numpy as jnp
from jax import lax
from jax.experimental import pallas as pl
from jax.experimental.pallas import tpu as pltpu

F32 = jnp.float32
BF16 = jnp.bfloat16
I32 = jnp.int32

HEAD_DIM = 64
Q_PER_KV = 4
WINDOW = 128
PAST_LEN = 8192
ROT_DIM = HEAD_DIM // 4
ROPE_THETA = 500000.0
POOL_WINDOWS = (2, 4, 8, 16)
POOL_HALO = 16
N_GROUPS = 4
EXPERTS_PER_GROUP = 4
N_EXPERTS = N_GROUPS * EXPERTS_PER_GROUP
TOP_K = 2
RMS_EPS = 1e-6
NEG_INF = -1e30
LOG2_E = 1.4426950408889634
LANES = 128
SUBLANES = 8
MIB = 1024 * 1024

TM = 1024
TN = 512
TN_PROJ = 1024
CAST_ROWS = 512
TM_WIDE = 512
TN_WIDE = 1024
TM_ROW = 256
TM_E = 512
TM_DISPATCH = 512
SEQS_PER_STEP = 8
POOL_SEQS = 16


def _params(n_axes, vmem_mib):
    return pltpu.CompilerParams(dimension_semantics=("arbitrary",) * n_axes,
                                vmem_limit_bytes=vmem_mib * MIB)


def _rms_rows(x, g):
    ms = jnp.mean(x * x, axis=-1, keepdims=True)
    return x * lax.rsqrt(ms + RMS_EPS) * g


def _first_last_specs(nb_first, block):
    zeros = (0,) * (len(block) - 1)
    first = pl.BlockSpec(block, lambda i, *_: (jnp.minimum(i, nb_first - 1),) + zeros)
    last = pl.BlockSpec(block, lambda i, *_: (jnp.maximum(i - nb_first, 0),) + zeros)
    return first, last


def _norm_in2_kernel(xa_ref, xb_ref, g_ref, o_ref, *, nb_first):
    i = pl.program_id(0)

    @pl.when(i < nb_first)
    def _():
        o_ref[...] = _rms_rows(xa_ref[...], g_ref[...]).astype(o_ref.dtype)

    @pl.when(i >= nb_first)
    def _():
        o_ref[...] = _rms_rows(xb_ref[...], g_ref[...]).astype(o_ref.dtype)


def _rmsnorm_in2(xa, xb, g, out_dtype, name):
    (na, d), nb = xa.shape, xb.shape[0]
    nb_first = na // TM_ROW
    spec_a, spec_b = _first_last_specs(nb_first, (TM_ROW, d))
    return pl.pallas_call(
        functools.partial(_norm_in2_kernel, nb_first=nb_first),
        grid=((na + nb) // TM_ROW,),
        in_specs=[spec_a, spec_b, pl.BlockSpec((1, d), lambda i: (0, 0))],
        out_specs=pl.BlockSpec((TM_ROW, d), lambda i: (i, 0)),
        out_shape=jax.ShapeDtypeStruct((na + nb, d), out_dtype),
        compiler_params=_params(1, 40),
        name=name,
    )(xa, xb, g.reshape(1, d))


def _norm_out2_kernel(x_ref, g_ref, oa_ref, ob_ref, *, nb_first):
    i = pl.program_id(0)

    @pl.when(i < nb_first)
    def _():
        oa_ref[...] = _rms_rows(x_ref[...], g_ref[...])

    @pl.when(i >= nb_first)
    def _():
        ob_ref[...] = _rms_rows(x_ref[...], g_ref[...])


def _rmsnorm_out2(x, g, na, name):
    n, d = x.shape
    nb_first = na // TM_ROW
    spec_a, spec_b = _first_last_specs(nb_first, (TM_ROW, d))
    return pl.pallas_call(
        functools.partial(_norm_out2_kernel, nb_first=nb_first),
        grid=(n // TM_ROW,),
        in_specs=[pl.BlockSpec((TM_ROW, d), lambda i: (i, 0)), pl.BlockSpec((1, d), lambda i: (0, 0))],
        out_specs=[spec_a, spec_b],
        out_shape=[jax.ShapeDtypeStruct((na, d), F32), jax.ShapeDtypeStruct((n - na, d), F32)],
        compiler_params=_params(1, 40),
        name=name,
    )(x, g.reshape(1, d))


def _cast_rows(src_ref, dst_ref):
    rows = src_ref.shape[0]
    step = min(rows, CAST_ROWS)
    for r in range(0, rows, step):
        dst_ref[r:r + step] = src_ref[r:r + step].astype(dst_ref.dtype)


def _column_weights_step(w_hbms, col0s, wbuf_refs, wb_refs, sem):
    j, i = pl.program_id(0), pl.program_id(1)

    def copies(jj):
        out = []
        for w, col0, buf in zip(w_hbms, col0s, wbuf_refs):
            tn = buf.shape[1]
            out.append(pltpu.make_async_copy(w.at[:, pl.ds(pl.multiple_of(col0 + jj * tn, tn), tn)], buf, sem))
        return out

    @pl.when(i == 0)
    def _():
        @pl.when(j == 0)
        def _():
            for c in copies(j):
                c.start()

        for c in copies(j):
            c.wait()
        for buf, wb_ref in zip(wbuf_refs, wb_refs):
            _cast_rows(buf, wb_ref)

        @pl.when(j + 1 < pl.num_programs(0))
        def _():
            for c in copies(j + 1):
                c.start()


def _weight_scratch(shapes):
    return ([pltpu.VMEM(s, F32) for s in shapes] + [pltpu.VMEM(s, BF16) for s in shapes]
            + [pltpu.SemaphoreType.DMA(())])


def _proj_kernel(a_ref, w_hbm, o_ref, wbuf_ref, wb_ref, sem, *, col0):
    _column_weights_step((w_hbm,), (col0,), (wbuf_ref,), (wb_ref,), sem)
    o_ref[...] = jnp.dot(a_ref[...], wb_ref[...], preferred_element_type=F32)


def _in_proj(xn, w, col0, width, name):
    n, d = xn.shape
    tn = TN_PROJ
    assert col0 % tn == 0 and width % tn == 0
    return pl.pallas_call(
        functools.partial(_proj_kernel, col0=col0),
        grid=(width // tn, n // TM),
        in_specs=[pl.BlockSpec((TM, d), lambda j, i: (i, 0)),
                  pl.BlockSpec(memory_space=pl.ANY)],
        out_specs=pl.BlockSpec((TM, tn), lambda j, i: (i, j)),
        out_shape=jax.ShapeDtypeStruct((n, width), F32),
        scratch_shapes=_weight_scratch([(d, tn)]),
        compiler_params=_params(2, 56),
        name=name,
    )(xn, w)


def _rope_tables(pos):
    half = ROT_DIM // 2
    inv_freq = ROPE_THETA ** (-jnp.arange(half, dtype=F32) * (2.0 / ROT_DIM))
    ang = pos[:, None] * inv_freq[None, :]
    cos, sin = jnp.cos(ang), jnp.sin(ang)
    t = pos.shape[0]
    pad = jnp.zeros((t, HEAD_DIM - ROT_DIM), F32)
    zeros = jnp.zeros((t, half), F32)
    cos_h = jnp.concatenate([cos, cos, pad + 1.0], axis=1)
    sa_h = jnp.concatenate([zeros, sin, pad], axis=1)
    sb_h = jnp.concatenate([-sin, zeros, pad], axis=1)
    reps = LANES // HEAD_DIM
    return tuple(jnp.tile(a, (1, reps)) for a in (cos_h, sa_h, sb_h))


def _rope_tables_t(pos):
    half = ROT_DIM // 2
    inv_freq = ROPE_THETA ** (-jnp.arange(half, dtype=F32) * (2.0 / ROT_DIM))
    ang = pos[:, None] * inv_freq[None, :]
    return jnp.cos(ang).T, jnp.sin(ang).T


def _rope(x, cos_t, sa_t, sb_t):
    pieces = []
    for c in range(x.shape[1] // LANES):
        xc = x[:, c * LANES:(c + 1) * LANES]
        pieces.append(xc * cos_t
                      + pltpu.roll(xc, ROT_DIM // 2, 1) * sa_t
                      + pltpu.roll(xc, LANES - ROT_DIM // 2, 1) * sb_t)
    return jnp.concatenate(pieces, axis=1)


def _softmax2_with_sink(s, sink):
    m = jnp.maximum(jnp.max(s, axis=-1, keepdims=True), sink)
    p = jnp.exp2(s - m)
    den = jnp.sum(p, axis=-1, keepdims=True) + jnp.exp2(sink - m)
    return p, 1.0 / den


def _attn_prompt_kernel(sink_ref, q_ref, k_ref, v_ref, cost_ref, sint_ref,
                        o_ref, ko_ref, kprev_ref, vprevt_ref, *, n_kv, nb):
    b = pl.program_id(0)

    @pl.when(b == 0)
    def _():
        kprev_ref[...] = jnp.zeros_like(kprev_ref)
        vprevt_ref[...] = jnp.zeros_like(vprevt_ref)

    @pl.when(b >= nb)
    def _():
        o_ref[...] = jnp.zeros_like(o_ref)

    @pl.when(b < nb)
    def _():
        _attn_prompt_block(sink_ref, q_ref, k_ref, v_ref, cost_ref, sint_ref,
                           o_ref, ko_ref, kprev_ref, vprevt_ref, n_kv=n_kv)


def _attn_prompt_block(sink_ref, q_ref, k_ref, v_ref, cost_ref, sint_ref,
                       o_ref, ko_ref, kprev_ref, vprevt_ref, *, n_kv):
    b = pl.program_id(0)
    half = ROT_DIM // 2
    cos_t, sin_t = cost_ref[...], sint_ref[...]

    def rope_t(x):
        x1, x2 = x[0:half], x[half:ROT_DIM]
        return jnp.concatenate([x1 * cos_t - x2 * sin_t, x2 * cos_t + x1 * sin_t, x[ROT_DIM:]], axis=0)

    kt = k_ref[...].T
    k = jnp.concatenate([rope_t(kt[h * HEAD_DIM:(h + 1) * HEAD_DIM]) for h in range(n_kv)], axis=0).T
    ko_ref[...] = k
    kcat = jnp.concatenate([kprev_ref[...], k], axis=0).astype(BF16)
    vt = v_ref[...].T
    vcat_t = jnp.concatenate([vprevt_ref[...], vt], axis=1).astype(BF16)
    kprev_ref[...] = k
    vprevt_ref[...] = vt

    qt = q_ref[...].T

    def roped_head_t(h):
        return (rope_t(qt[h * HEAD_DIM:(h + 1) * HEAD_DIM]) * (HEAD_DIM ** -0.5 * LOG2_E)).astype(BF16)

    n_heads = n_kv * Q_PER_KV
    zeros_head = jnp.zeros((HEAD_DIM, WINDOW), BF16)
    heads_per_tile = LANES // HEAD_DIM

    key = lax.broadcasted_iota(I32, (2 * WINDOW, Q_PER_KV * WINDOW), 0)
    qry = lax.broadcasted_iota(I32, (2 * WINDOW, Q_PER_KV * WINDOW), 1) & (WINDOW - 1)
    first_key = jnp.where(b > 0, 0, WINDOW)
    mask = (key > qry) & (key <= qry + WINDOW) & (key >= first_key)

    scores = []
    for g in range(n_kv):
        tile, slot = g // heads_per_tile, g % heads_per_tile
        k2 = kcat[:, tile * LANES:(tile + 1) * LANES]
        cols = []
        for r in range(Q_PER_KV):
            parts = [zeros_head] * heads_per_tile
            parts[slot] = roped_head_t(g * Q_PER_KV + r)
            cols.append(jnp.concatenate(parts, axis=0))
        rhs = jnp.concatenate(cols, axis=1)
        scores.append(jnp.where(mask, jnp.dot(k2, rhs, preferred_element_type=F32), NEG_INF))
    st = jnp.concatenate(scores, axis=1)
    sink = jnp.concatenate([jnp.full((1, WINDOW), sink_ref[h] * LOG2_E, F32) for h in range(n_heads)], axis=1)
    m = jnp.maximum(jnp.max(st, axis=0, keepdims=True), sink)
    p = jnp.exp2(st - m)
    den = jnp.sum(p, axis=0, keepdims=True) + jnp.exp2(sink - m)
    inv_den = 1.0 / den
    probs_t = p.astype(BF16)

    for g in range(n_kv):
        cols = slice(g * Q_PER_KV * WINDOW, (g + 1) * Q_PER_KV * WINDOW)
        ot = jnp.dot(vcat_t[g * HEAD_DIM:(g + 1) * HEAD_DIM], probs_t[:, cols],
                     preferred_element_type=F32) * inv_den[:, cols]
        for pr in range(Q_PER_KV // heads_per_tile):
            pair = jnp.concatenate(
                [ot[:, (pr * heads_per_tile + u) * WINDOW:(pr * heads_per_tile + u + 1) * WINDOW]
                 for u in range(heads_per_tile)], axis=0)
            c = (g * Q_PER_KV) // heads_per_tile + pr
            o_ref[:, c * LANES:(c + 1) * LANES] = pair.T.astype(o_ref.dtype)


def _attn_prompt(qkv, sinks, tables_t, n_rows, attn_w, kv_w):
    nb = n_rows // WINDOW
    kb, vb = attn_w // kv_w, attn_w // kv_w + 1

    def blk(b, s):
        return jnp.minimum(b, nb - 1)

    tab_t_spec = pl.BlockSpec((ROT_DIM // 2, WINDOW), lambda b, s: (0, blk(b, s)))
    return pl.pallas_call(
        functools.partial(_attn_prompt_kernel, n_kv=kv_w // HEAD_DIM, nb=nb),
        grid_spec=pltpu.PrefetchScalarGridSpec(
            num_scalar_prefetch=1,
            grid=(qkv.shape[0] // WINDOW,),
            in_specs=[pl.BlockSpec((WINDOW, attn_w), lambda b, s: (blk(b, s), 0)),
                      pl.BlockSpec((WINDOW, kv_w), lambda b, s: (blk(b, s), kb)),
                      pl.BlockSpec((WINDOW, kv_w), lambda b, s: (blk(b, s), vb)),
                      tab_t_spec, tab_t_spec],
            out_specs=[pl.BlockSpec((WINDOW, attn_w), lambda b, s: (b, 0)),
                       pl.BlockSpec((WINDOW, kv_w), lambda b, s: (blk(b, s), 0))],
            scratch_shapes=[pltpu.VMEM((WINDOW, kv_w), F32), pltpu.VMEM((kv_w, WINDOW), F32)]),
        out_shape=[jax.ShapeDtypeStruct((qkv.shape[0], attn_w), BF16),
                   jax.ShapeDtypeStruct((n_rows, kv_w), F32)],
        compiler_params=_params(1, 32),
        name="attn_prompt",
    )(sinks, qkv, qkv, qkv, *tables_t)


def _attn_sample_kernel(sink_ref, q_ref, k_ref, v_ref, ck_ref, cv_ref, cos_ref, sa_ref, sb_ref, attn_in_ref,
                        o_ref, nk_ref, nv_ref, *, n_kv, t_new):
    del attn_in_ref
    assert LANES // HEAD_DIM == 2 and Q_PER_KV % 2 == 0
    cos_t, sa_t, sb_t = cos_ref[...], sa_ref[...], sb_ref[...]
    q = _rope(q_ref[...], cos_t, sa_t, sb_t) * (HEAD_DIM ** -0.5 * LOG2_E)
    k = _rope(k_ref[...], cos_t, sa_t, sb_t)
    v = v_ref[...]
    n_cache = ck_ref.shape[1]
    n_keys = 2 * WINDOW
    n_seq = q.shape[0] // t_new
    n_heads = n_kv * Q_PER_KV
    zpad = jnp.zeros((n_keys - n_cache - t_new, k.shape[1]), F32)
    low = lax.broadcasted_iota(I32, (t_new, LANES), 1) < HEAD_DIM
    high = jnp.logical_not(low)

    qt = lax.broadcasted_iota(I32, (Q_PER_KV * t_new, n_keys), 0) & (t_new - 1)
    kj = lax.broadcasted_iota(I32, (Q_PER_KV * t_new, n_keys), 1)
    mask = (kj > qt + (n_cache - WINDOW)) & (kj <= qt + n_cache)

    def swap_halves(x):
        return pltpu.roll(x, HEAD_DIM, 1)

    scores, values = [], []
    for s_i in range(n_seq):
        ck, cv = ck_ref[s_i], cv_ref[s_i]
        kn, vn = k[s_i * t_new:(s_i + 1) * t_new], v[s_i * t_new:(s_i + 1) * t_new]
        nk_ref[s_i] = jnp.concatenate([ck[t_new:], kn], axis=0)
        nv_ref[s_i] = jnp.concatenate([cv[t_new:], vn], axis=0)
        kall = jnp.concatenate([ck, kn, zpad], axis=0).astype(BF16)
        values.append(jnp.concatenate([cv, vn, zpad], axis=0).astype(BF16))
        qs = q[s_i * t_new:(s_i + 1) * t_new]
        for g in range(n_kv):
            tile, odd = g // 2, g % 2
            parts = []
            for r in range(Q_PER_KV):
                h = g * Q_PER_KV + r
                x = qs[:, (h // 2) * LANES:(h // 2 + 1) * LANES]
                if h % 2 != odd:
                    x = swap_halves(x)
                parts.append(jnp.where(high if odd else low, x, 0.0))
            qg = jnp.concatenate(parts, axis=0).astype(BF16)
            k2 = kall[:, tile * LANES:(tile + 1) * LANES]
            sg = lax.dot_general(qg, k2, (((1,), (1,)), ((), ())), preferred_element_type=F32)
            scores.append(jnp.where(mask, sg, NEG_INF))
    s = jnp.concatenate(scores, axis=0)
    sink_seq = jnp.concatenate([jnp.full((t_new, 1), sink_ref[h] * LOG2_E, F32) for h in range(n_heads)], axis=0)
    sink = jnp.concatenate([sink_seq] * n_seq, axis=0)
    p, inv_den = _softmax2_with_sink(s, sink)
    probs = p.astype(BF16)

    seq_outs = []
    for s_i in range(n_seq):
        out_tiles = []
        for g in range(n_kv):
            tile, odd = g // 2, g % 2
            r0 = (s_i * n_kv + g) * Q_PER_KV * t_new
            v2 = values[s_i][:, tile * LANES:(tile + 1) * LANES]
            rows = slice(r0, r0 + Q_PER_KV * t_new)
            og = jnp.dot(probs[rows], v2, preferred_element_type=F32) * inv_den[rows]
            for pr in range(Q_PER_KV // 2):
                even = og[(2 * pr) * t_new:(2 * pr + 1) * t_new]
                oddh = og[(2 * pr + 1) * t_new:(2 * pr + 2) * t_new]
                if odd:
                    even = swap_halves(even)
                else:
                    oddh = swap_halves(oddh)
                out_tiles.append(jnp.where(low, even, oddh))
        seq_outs.append(jnp.concatenate(out_tiles, axis=1))
    o_ref[...] = jnp.concatenate(seq_outs, axis=0).astype(o_ref.dtype)


def _attn_sample(qkv, attn, row0, cache_k, cache_v, sinks, tables, n_seq, t_new, attn_w, kv_w):
    rows = SEQS_PER_STEP * t_new
    rb0 = row0 // rows
    kb, vb = attn_w // kv_w, attn_w // kv_w + 1
    n_cache = cache_k.shape[1]
    tab_spec = pl.BlockSpec((rows, LANES), lambda i, s: (0, 0))
    cache_spec = pl.BlockSpec((SEQS_PER_STEP, n_cache, kv_w), lambda i, s: (i, 0, 0))
    return pl.pallas_call(
        functools.partial(_attn_sample_kernel, n_kv=kv_w // HEAD_DIM, t_new=t_new),
        grid_spec=pltpu.PrefetchScalarGridSpec(
            num_scalar_prefetch=1,
            grid=(n_seq // SEQS_PER_STEP,),
            in_specs=[pl.BlockSpec((rows, attn_w), lambda i, s: (rb0 + i, 0)),
                      pl.BlockSpec((rows, kv_w), lambda i, s: (rb0 + i, kb)),
                      pl.BlockSpec((rows, kv_w), lambda i, s: (rb0 + i, vb)),
                      cache_spec, cache_spec, tab_spec, tab_spec, tab_spec,
                      pl.BlockSpec(memory_space=pl.ANY)],
            out_specs=[pl.BlockSpec((rows, attn_w), lambda i, s: (rb0 + i, 0)), cache_spec, cache_spec]),
        out_shape=[jax.ShapeDtypeStruct(attn.shape, attn.dtype),
                   jax.ShapeDtypeStruct(cache_k.shape, F32),
                   jax.ShapeDtypeStruct(cache_v.shape, F32)],
        input_output_aliases={9: 0},
        compiler_params=_params(1, 32),
        name="attn_sample",
    )(sinks, qkv, qkv, qkv, cache_k, cache_v, *tables, attn)


def _pool_prompt_kernel(z_ref, halo_ref, w_ref, sc_ref, o_ref, ext_ref, *, tm, pg, nb):
    i = pl.program_id(0)

    @pl.when(i >= nb)
    def _():
        o_ref[...] = jnp.zeros_like(o_ref)

    @pl.when(i < nb)
    def _():
        _pool_prompt_block(z_ref, halo_ref, w_ref, sc_ref, o_ref, ext_ref, tm=tm, pg=pg)


def _pool_prompt_block(z_ref, halo_ref, w_ref, sc_ref, o_ref, ext_ref, *, tm, pg):
    i = pl.program_id(0)
    ext_ref[0:POOL_HALO, :] = jnp.where(i > 0, halo_ref[...], 0.0)
    ext_ref[POOL_HALO:, :] = z_ref[...]
    pos = i * tm + lax.broadcasted_iota(I32, (tm, 1), 0)
    for g, w in enumerate(POOL_WINDOWS):
        c0, c1 = g * pg, (g + 1) * pg
        cur = ext_ref[POOL_HALO:POOL_HALO + tm, c0:c1]
        acc = cur
        for d in range(1, w):
            acc = acc + ext_ref[POOL_HALO - d:POOL_HALO - d + tm, c0:c1]
        cnt = jnp.minimum(pos + 1, w).astype(F32)
        pooled = (acc / cnt - cur).astype(BF16)
        mixed = jnp.dot(pooled, w_ref[g].astype(BF16), preferred_element_type=F32) * sc_ref[:, c0:c1]
        o_ref[:, c0:c1] = mixed.astype(o_ref.dtype)


def _pool_prompt(z, n_rows, w_grp, scale):
    n, pw = z.shape
    tm = TM_ROW
    hb = tm // POOL_HALO
    return pl.pallas_call(
        functools.partial(_pool_prompt_kernel, tm=tm, pg=pw // len(POOL_WINDOWS), nb=n_rows // tm),
        grid=(n // tm,),
        in_specs=[pl.BlockSpec((tm, pw), lambda i: (i, 0)),
                  pl.BlockSpec((POOL_HALO, pw), lambda i: (jnp.maximum(i * hb - 1, 0), 0)),
                  pl.BlockSpec(w_grp.shape, lambda i: (0, 0, 0)),
                  pl.BlockSpec((1, pw), lambda i: (0, 0))],
        out_specs=pl.BlockSpec((tm, pw), lambda i: (i, 0)),
        out_shape=jax.ShapeDtypeStruct((n, pw), BF16),
        scratch_shapes=[pltpu.VMEM((tm + POOL_HALO, pw), F32)],
        compiler_params=_params(1, 40),
        name="pool_prompt",
    )(z, z, w_grp, scale.reshape(1, pw))


def _pool_sample_kernel(st_ref, z_ref, w_ref, sc_ref, mixed_in_ref, o_ref, ns_ref, ext_ref, *, pg, t_new):
    del mixed_in_ref
    n_st = st_ref.shape[1]
    seqs = st_ref.shape[0]
    hist = ext_ref.shape[1] - t_new
    ext_ref[:, hist - n_st:hist, :] = st_ref[...]
    ext_ref[:, hist:, :] = z_ref[...]
    ns_ref[...] = ext_ref[:, hist + t_new - n_st:, :]
    for g, w in enumerate(POOL_WINDOWS):
        c0, c1 = g * pg, (g + 1) * pg
        cur = ext_ref[:, hist:, c0:c1]
        acc = cur
        for d in range(1, w):
            acc = acc + ext_ref[:, hist - d:hist - d + t_new, c0:c1]
        pooled = (acc / float(w) - cur).reshape(seqs * t_new, pg).astype(BF16)
        mixed = jnp.dot(pooled, w_ref[g].astype(BF16), preferred_element_type=F32) * sc_ref[:, c0:c1]
        o_ref[:, c0:c1] = mixed.astype(o_ref.dtype)


def _pool_sample(state, z, mixed, row0, w_grp, scale, t_new):
    n_seq, n_st, pw = state.shape
    assert t_new == SUBLANES and n_st <= POOL_HALO - 1
    z3 = z.reshape(z.shape[0] // t_new, t_new, pw)
    sb0 = row0 // t_new // POOL_SEQS
    rows = POOL_SEQS * t_new
    st_spec = pl.BlockSpec((POOL_SEQS, n_st, pw), lambda i: (i, 0, 0))
    return pl.pallas_call(
        functools.partial(_pool_sample_kernel, pg=pw // len(POOL_WINDOWS), t_new=t_new),
        grid=(n_seq // POOL_SEQS,),
        in_specs=[st_spec,
                  pl.BlockSpec((POOL_SEQS, t_new, pw), lambda i: (sb0 + i, 0, 0)),
                  pl.BlockSpec(w_grp.shape, lambda i: (0, 0, 0)),
                  pl.BlockSpec((1, pw), lambda i: (0, 0)),
                  pl.BlockSpec(memory_space=pl.ANY)],
        out_specs=[pl.BlockSpec((rows, pw), lambda i: (row0 // rows + i, 0)), st_spec],
        out_shape=[jax.ShapeDtypeStruct(mixed.shape, mixed.dtype), jax.ShapeDtypeStruct(state.shape, F32)],
        scratch_shapes=[pltpu.VMEM((POOL_SEQS, POOL_HALO + t_new, pw), F32)],
        input_output_aliases={4: 0},
        compiler_params=_params(1, 40),
        name="pool_sample",
    )(state, z3, w_grp, scale.reshape(1, pw), mixed)


def _merge_kernel(a1_ref, a2_ref, w1_hbm, w2_hbm, ga_ref, gb_ref, ba_ref, bb_ref, o_ref,
                  w1f_ref, w2f_ref, w1b_ref, w2b_ref, sem):
    _column_weights_step((w1_hbm, w2_hbm), (0, 0), (w1f_ref, w2f_ref), (w1b_ref, w2b_ref), sem)
    a_up = jnp.dot(a1_ref[...], w1b_ref[...], preferred_element_type=F32)
    b_up = jnp.dot(a2_ref[...], w2b_ref[...], preferred_element_type=F32)
    merged = (jax.nn.sigmoid(ga_ref[...] + ba_ref[...]) * a_up
              + jax.nn.sigmoid(gb_ref[...] + bb_ref[...]) * b_up)
    o_ref[...] = merged.astype(o_ref.dtype)


def _merge(attn, mixed, w_attn_up, w_pool_up, gates, b_gate):
    n, ka = attn.shape
    kp = mixed.shape[1]
    d = w_attn_up.shape[1]
    tm, tn = TM_WIDE, TN_WIDE
    nj = d // tn
    any_spec = pl.BlockSpec(memory_space=pl.ANY)
    return pl.pallas_call(
        _merge_kernel,
        grid=(nj, n // tm),
        in_specs=[pl.BlockSpec((tm, ka), lambda j, i: (i, 0)),
                  pl.BlockSpec((tm, kp), lambda j, i: (i, 0)),
                  any_spec, any_spec,
                  pl.BlockSpec((tm, tn), lambda j, i: (i, j)),
                  pl.BlockSpec((tm, tn), lambda j, i: (i, nj + j)),
                  pl.BlockSpec((1, tn), lambda j, i: (0, j)),
                  pl.BlockSpec((1, tn), lambda j, i: (0, nj + j))],
        out_specs=pl.BlockSpec((tm, tn), lambda j, i: (i, j)),
        out_shape=jax.ShapeDtypeStruct((n, d), BF16),
        scratch_shapes=_weight_scratch([(ka, tn), (kp, tn)]),
        compiler_params=_params(2, 56),
        name="merge",
    )(attn, mixed, w_attn_up, w_pool_up, gates, gates, b_gate, b_gate)


def _out_proj_kernel(a_ref, w_hbm, ha_ref, hb_ref, o_ref, wf_ref, wb_ref, sem, *, nb_first):
    _column_weights_step((w_hbm,), (0,), (wf_ref,), (wb_ref,), sem)
    h = jnp.where(pl.program_id(1) < nb_first, ha_ref[...], hb_ref[...])
    o_ref[...] = h + jnp.dot(a_ref[...], wb_ref[...], preferred_element_type=F32)


def _out_proj(merged, w_out, ha, hb):
    n, d = merged.shape
    tm, tn = TM_WIDE, TN_WIDE
    nb_first = ha.shape[0] // tm
    h_first = pl.BlockSpec((tm, tn), lambda j, i: (jnp.minimum(i, nb_first - 1), j))
    h_last = pl.BlockSpec((tm, tn), lambda j, i: (jnp.maximum(i - nb_first, 0), j))
    return pl.pallas_call(
        functools.partial(_out_proj_kernel, nb_first=nb_first),
        grid=(d // tn, n // tm),
        in_specs=[pl.BlockSpec((tm, d), lambda j, i: (i, 0)),
                  pl.BlockSpec(memory_space=pl.ANY),
                  h_first, h_last],
        out_specs=pl.BlockSpec((tm, tn), lambda j, i: (i, j)),
        out_shape=jax.ShapeDtypeStruct((n, d), F32),
        scratch_shapes=_weight_scratch([(d, tn)]),
        compiler_params=_params(2, 56),
        name="out_proj",
    )(merged, w_out, ha, hb)


def _router_kernel(h_ref, g_ref, wr_ref, br_ref, info_ref, cw_ref, xf_ref, cnt_ref, base_ref):
    tm = h_ref.shape[0]

    @pl.when(pl.program_id(0) == 0)
    def _():
        base_ref[...] = jnp.zeros_like(base_ref)

    xf = _rms_rows(h_ref[...], g_ref[...]).astype(BF16)
    xf_ref[...] = xf.reshape(xf_ref.shape)
    logits = jnp.dot(xf, wr_ref[...].astype(BF16), preferred_element_type=F32) + br_ref[...]
    lane = lax.broadcasted_iota(I32, logits.shape, 1)
    big = jnp.int32(1 << 20)
    ninf = jnp.float32(-jnp.inf)

    is_g = lane < N_GROUPS
    gl = jnp.where(is_g, logits, ninf)
    gmax = jnp.max(gl, axis=-1, keepdims=True)
    gden = jnp.sum(jnp.exp(gl - gmax), axis=-1, keepdims=True)
    g_p = 1.0 / gden
    g_idx = jnp.min(jnp.where(gl == gmax, lane, big), axis=-1, keepdims=True)

    e_lane = lane - N_GROUPS
    in_grp = (e_lane >= g_idx * EXPERTS_PER_GROUP) & (e_lane < (g_idx + 1) * EXPERTS_PER_GROUP)
    el = jnp.where(in_grp, logits, ninf)
    m1 = jnp.max(el, axis=-1, keepdims=True)
    i1 = jnp.min(jnp.where(in_grp & (el == m1), e_lane, big), axis=-1, keepdims=True)
    el2 = jnp.where(e_lane == i1, ninf, el)
    m2 = jnp.max(el2, axis=-1, keepdims=True)
    i2 = jnp.min(jnp.where(in_grp & (el2 == m2), e_lane, big), axis=-1, keepdims=True)
    t = jnp.exp(m2 - m1)
    den = 1.0 + t
    c1 = g_p * (1.0 / den)
    c2 = g_p * (t / den)
    cw_ref[...] = jnp.where(lane == 0, c1, jnp.where(lane == 1, c2, 0.0))

    hit1, hit2 = lane == i1, lane == i2
    onehot = (hit1 | hit2).astype(BF16)
    earlier = (lax.broadcasted_iota(I32, (tm, tm), 1) < lax.broadcasted_iota(I32, (tm, tm), 0)).astype(BF16)
    before = jnp.dot(earlier, onehot, preferred_element_type=F32) + base_ref[...]
    r1 = jnp.sum(jnp.where(hit1, before, 0.0), axis=-1, keepdims=True).astype(I32)
    r2 = jnp.sum(jnp.where(hit2, before, 0.0), axis=-1, keepdims=True).astype(I32)
    base_ref[...] += jnp.sum(onehot.astype(F32), axis=0, keepdims=True)
    cnt_ref[...] = base_ref[...].astype(I32)
    info_ref[...] = jnp.where(lane == 0, i1, jnp.where(lane == 1, i2,
                              jnp.where(lane == 2, r1, jnp.where(lane == 3, r2, 0))))


def _router(h, g_ffn, w_rg, b_rg, w_re, b_re):
    n, d = h.shape
    n_log = N_GROUPS + N_EXPERTS
    wr = jnp.concatenate([w_rg, w_re, jnp.zeros((d, LANES - n_log), F32)], axis=1)
    br = jnp.concatenate([b_rg, b_re, jnp.zeros((LANES - n_log,), F32)]).reshape(1, LANES)
    lane_spec = pl.BlockSpec((TM_ROW, LANES), lambda i: (i, 0))
    return pl.pallas_call(
        _router_kernel,
        grid=(n // TM_ROW,),
        in_specs=[pl.BlockSpec((TM_ROW, d), lambda i: (i, 0)),
                  pl.BlockSpec((1, d), lambda i: (0, 0)),
                  pl.BlockSpec((d, LANES), lambda i: (0, 0)),
                  pl.BlockSpec((1, LANES), lambda i: (0, 0))],
        out_specs=[lane_spec, lane_spec,
                   pl.BlockSpec((TM_ROW, d // LANES, LANES), lambda i: (i, 0, 0)),
                   pl.BlockSpec((1, LANES), lambda i: (0, 0))],
        out_shape=[jax.ShapeDtypeStruct((n, LANES), I32), jax.ShapeDtypeStruct((n, LANES), F32),
                   jax.ShapeDtypeStruct((n, d // LANES, LANES), BF16), jax.ShapeDtypeStruct((1, LANES), I32)],
        scratch_shapes=[pltpu.VMEM((1, LANES), F32)],
        compiler_params=_params(1, 40),
        name="router",
    )(h, g_ffn.reshape(1, d), wr, br)


def _dispatch_plan(info, counts, n_tiles):
    counts = counts[0, :N_EXPERTS]
    tiles_per = (counts + TM_E - 1) // TM_E
    tiles_end = jnp.cumsum(tiles_per)
    row_start = (tiles_end - tiles_per) * TM_E
    eid, rank = info[:, :TOP_K], info[:, TOP_K:2 * TOP_K]
    onehot = (eid[:, :, None] == jnp.arange(N_EXPERTS, dtype=I32)[None, None, :]).astype(I32)
    pos = (jnp.sum(onehot * row_start[None, None, :], axis=-1) + rank).reshape(-1).astype(I32)
    n_used = tiles_end[-1:].astype(I32)
    tile_ids = jnp.minimum(jnp.arange(n_tiles, dtype=I32), n_used - 1)
    tile_expert = jnp.sum((tile_ids[:, None] >= tiles_end[None, :]).astype(I32), axis=1).astype(I32)
    last_tile_row = (jnp.clip(tiles_end - 1, 0, n_tiles - 1) * TM_E).astype(I32)
    starts = jnp.concatenate([jnp.ones((1,), I32), (tile_expert[1:] != tile_expert[:-1]).astype(I32)])
    run_idx = jnp.cumsum(starts) - 1
    n_runs = run_idx[-1:] + 1
    run_expert = jnp.zeros((n_tiles,), I32).at[run_idx].set(tile_expert)
    next_expert = run_expert[(run_idx + 1) % n_runs]
    rows_end = row_start + counts
    valid = jnp.clip(rows_end[tile_expert] - jnp.arange(n_tiles, dtype=I32) * TM_E, 0, TM_E)
    plan = (tile_expert, n_used, run_idx.astype(I32), next_expert.astype(I32), n_runs.astype(I32),
            valid.astype(I32))
    return pos, plan, last_tile_row


def _dispatch_kernel(pos_ref, ltr_ref, nu_ref, xf_ref, xs_hbm, zero_ref, sem):
    i = pl.program_id(0)
    n_tiles = xs_hbm.shape[0] // TM_E

    def clear_tile(row):
        return pltpu.make_async_copy(zero_ref, xs_hbm.at[pl.ds(pl.multiple_of(row, TM_E), TM_E)], sem)

    @pl.when(i == 0)
    def _():
        zero_ref[...] = jnp.zeros_like(zero_ref)
        for e in range(N_EXPERTS):
            clear_tile(ltr_ref[e]).start()

        def start_unused(t, carry):
            clear_tile(t * TM_E).start()
            return carry

        def wait_one(t, carry):
            clear_tile(0).wait()
            return carry

        lax.fori_loop(nu_ref[0], n_tiles, start_unused, 0)
        lax.fori_loop(nu_ref[0] - N_EXPERTS, n_tiles, wait_one, 0)

    base = i * TM_DISPATCH

    def issue(r, carry):
        for k in range(TOP_K):
            dst_row = pos_ref[(base + r) * TOP_K + k]
            pltpu.make_async_copy(xf_ref.at[pl.ds(r, 1)], xs_hbm.at[pl.ds(dst_row, 1)], sem).start(priority=k % 2)
        return carry

    lax.fori_loop(0, TM_DISPATCH, issue, 0, unroll=4)
    for k in range(TOP_K):
        pltpu.make_async_copy(xf_ref, xs_hbm.at[pl.ds(0, TM_DISPATCH)], sem).wait()


def _dispatch(pos, last_tile_row, n_used, xf3, n_tiles):
    n, c, l = xf3.shape
    return pl.pallas_call(
        _dispatch_kernel,
        grid_spec=pltpu.PrefetchScalarGridSpec(
            num_scalar_prefetch=3,
            grid=(n // TM_DISPATCH,),
            in_specs=[pl.BlockSpec((TM_DISPATCH, c, l), lambda i, *_: (i, 0, 0))],
            out_specs=pl.BlockSpec(memory_space=pl.ANY),
            scratch_shapes=[pltpu.VMEM((TM_E, c, l), xf3.dtype), pltpu.SemaphoreType.DMA(())]),
        out_shape=jax.ShapeDtypeStruct((n_tiles * TM_E, c, l), xf3.dtype),
        compiler_params=_params(1, 24),
        name="moe_dispatch",
    )(pos, last_tile_row, n_used, xf3)


def _expert_weights_step(plan_refs, w_hbms, wbuf_ref, wb_refs, sem, tn):
    te_ref, _, run_ref, nxt_ref, nr_ref = plan_refs
    j, t = pl.program_id(0), pl.program_id(1)

    def copies(e, jj):
        col = pl.multiple_of(jj * tn, tn)
        return [pltpu.make_async_copy(w.at[e, :, pl.ds(col, tn)], wbuf_ref.at[m], sem)
                for m, w in enumerate(w_hbms)]

    @pl.when((t == 0) | (te_ref[t] != te_ref[jnp.maximum(t - 1, 0)]))
    def _():
        @pl.when((j == 0) & (t == 0))
        def _():
            for c in copies(te_ref[t], j):
                c.start()

        for c in copies(te_ref[t], j):
            c.wait()
        for m, wb_ref in enumerate(wb_refs):
            _cast_rows(wbuf_ref.at[m], wb_ref)

        next_j = jnp.where(run_ref[t] + 1 == nr_ref[0], j + 1, j)

        @pl.when(next_j < pl.num_programs(0))
        def _():
            for c in copies(nxt_ref[t], next_j):
                c.start()


def _by_valid_rows(valid, full_fn, zero_fn):
    heights = (TM_E // 2, 3 * TM_E // 4, TM_E)
    lo = 0
    for rows in heights:
        @pl.when((valid > lo) & (valid <= rows))
        def _(rows=rows):
            full_fn(rows)
            if rows < TM_E:
                zero_fn(rows)
        lo = rows

    @pl.when(valid == 0)
    def _():
        zero_fn(0)


def _moe_up_kernel(te_ref, nu_ref, run_ref, nxt_ref, nr_ref, valid_ref, xs_ref, wg_hbm, wu_hbm, o_ref,
                   wbuf_ref, wgb_ref, wub_ref, sem):
    t = pl.program_id(1)
    _expert_weights_step((te_ref, nu_ref, run_ref, nxt_ref, nr_ref), (wg_hbm, wu_hbm), wbuf_ref,
                         (wgb_ref, wub_ref), sem, o_ref.shape[1])

    def compute(rows):
        x = xs_ref[0:rows].reshape(rows, -1)
        gate = jnp.dot(x, wgb_ref[...], preferred_element_type=F32)
        up = jnp.dot(x, wub_ref[...], preferred_element_type=F32)
        o_ref[0:rows] = (jax.nn.silu(gate) * up).astype(o_ref.dtype)

    def clear(row0):
        o_ref[row0:] = jnp.zeros((TM_E - row0,) + o_ref.shape[1:], o_ref.dtype)

    _by_valid_rows(valid_ref[t], compute, clear)


def _moe_up(plan, xs3, w_gate, w_up, n_tiles):
    d, f = w_gate.shape[1], w_gate.shape[2]
    tf = TN
    any_spec = pl.BlockSpec(memory_space=pl.ANY)
    return pl.pallas_call(
        _moe_up_kernel,
        grid_spec=pltpu.PrefetchScalarGridSpec(
            num_scalar_prefetch=len(plan),
            grid=(f // tf, n_tiles),
            in_specs=[pl.BlockSpec((TM_E,) + xs3.shape[1:], lambda j, t, te, nu, *_: (jnp.minimum(t, nu[0] - 1), 0, 0)),
                      any_spec, any_spec],
            out_specs=pl.BlockSpec((TM_E, tf), lambda j, t, *_: (t, j)),
            scratch_shapes=[pltpu.VMEM((2, d, tf), F32), pltpu.VMEM((d, tf), BF16), pltpu.VMEM((d, tf), BF16),
                            pltpu.SemaphoreType.DMA(())]),
        out_shape=jax.ShapeDtypeStruct((n_tiles * TM_E, f), BF16),
        compiler_params=_params(2, 56),
        name="moe_up",
    )(*plan, xs3, w_gate, w_up)


def _moe_down_kernel(te_ref, nu_ref, run_ref, nxt_ref, nr_ref, valid_ref, hid_ref, wd_hbm, o_ref,
                     wbuf_ref, wdb_ref, sem):
    t = pl.program_id(1)
    _expert_weights_step((te_ref, nu_ref, run_ref, nxt_ref, nr_ref), (wd_hbm,), wbuf_ref, (wdb_ref,), sem,
                         wdb_ref.shape[1])

    def compute(rows):
        y = jnp.dot(hid_ref[0:rows], wdb_ref[...], preferred_element_type=F32)
        o_ref[0:rows] = y.reshape((rows,) + o_ref.shape[1:])

    def clear(row0):
        o_ref[row0:] = jnp.zeros((TM_E - row0,) + o_ref.shape[1:], o_ref.dtype)

    _by_valid_rows(valid_ref[t], compute, clear)


def _moe_down(plan, hid, w_down, n_tiles):
    f, d = w_down.shape[1], w_down.shape[2]
    tn = 4 * TN
    return pl.pallas_call(
        _moe_down_kernel,
        grid_spec=pltpu.PrefetchScalarGridSpec(
            num_scalar_prefetch=len(plan),
            grid=(d // tn, n_tiles),
            in_specs=[pl.BlockSpec((TM_E, f), lambda j, t, *_: (t, 0)),
                      pl.BlockSpec(memory_space=pl.ANY)],
            out_specs=pl.BlockSpec((TM_E, tn // LANES, LANES), lambda j, t, *_: (t, j, 0)),
            scratch_shapes=[pltpu.VMEM((1, f, tn), F32), pltpu.VMEM((f, tn), BF16),
                            pltpu.SemaphoreType.DMA(())]),
        out_shape=jax.ShapeDtypeStruct((n_tiles * TM_E, d // LANES, LANES), F32),
        compiler_params=_params(2, 40),
        name="moe_down",
    )(*plan, hid, w_down)


def _combine_kernel(pos_ref, h_ref, cw_ref, y_hbm, g_ref, ho_ref, xo_ref, buf_ref, sem):
    i = pl.program_id(0)
    tm, d = h_ref.shape

    def row_copy(slot, k, r, src_row):
        return pltpu.make_async_copy(y_hbm.at[pl.ds(src_row, 1)], buf_ref.at[slot, k, pl.ds(r, 1)], sem.at[slot])

    def issue_step(step, slot):
        def body(r, carry):
            for k in range(TOP_K):
                row_copy(slot, k, r, pos_ref[(step * tm + r) * TOP_K + k]).start(priority=k % 2)
            return carry
        lax.fori_loop(0, tm, body, 0, unroll=4)

    @pl.when(i == 0)
    def _():
        issue_step(0, 0)

    @pl.when(i + 1 < pl.num_programs(0))
    def _():
        issue_step(i + 1, (i + 1) % 2)

    slot = i % 2
    for k in range(TOP_K):
        pltpu.make_async_copy(y_hbm.at[pl.ds(0, tm)], buf_ref.at[slot, k], sem.at[slot]).wait()
    cw = cw_ref[...]
    y1 = buf_ref[slot, 0].reshape(tm, d)
    y2 = buf_ref[slot, 1].reshape(tm, d)
    h2 = h_ref[...] + (cw[:, 0:1] * y1 + cw[:, 1:2] * y2)
    ho_ref[...] = h2
    xo_ref[...] = _rms_rows(h2, g_ref[...]).astype(xo_ref.dtype)


def _combine(pos, h, cw, y3, g_ple):
    n, d = h.shape
    row_spec = pl.BlockSpec((TM_ROW, d), lambda i, p: (i, 0))
    return pl.pallas_call(
        _combine_kernel,
        grid_spec=pltpu.PrefetchScalarGridSpec(
            num_scalar_prefetch=1,
            grid=(n // TM_ROW,),
            in_specs=[row_spec,
                      pl.BlockSpec((TM_ROW, LANES), lambda i, p: (i, 0)),
                      pl.BlockSpec(memory_space=pl.ANY),
                      pl.BlockSpec((1, d), lambda i, p: (0, 0))],
            out_specs=[row_spec, row_spec],
            scratch_shapes=[pltpu.VMEM((2, TOP_K, TM_ROW) + y3.shape[1:], F32), pltpu.SemaphoreType.DMA((2,))]),
        out_shape=[jax.ShapeDtypeStruct((n, d), F32), jax.ShapeDtypeStruct((n, d), BF16)],
        compiler_params=_params(1, 56),
        name="moe_combine",
    )(pos, h, cw, y3, g_ple.reshape(1, d))


def _ple_kernel(xn_ref, wg_hbm, bg_ref, pa_ref, pb_ref, wp_hbm, h_ref, o_ref,
                wgf_ref, wpf_ref, wgb_ref, wpb_ref, sem, *, nb_first):
    _column_weights_step((wg_hbm, wp_hbm), (0, 0), (wgf_ref, wpf_ref), (wgb_ref, wpb_ref), sem)
    p = jnp.where(pl.program_id(1) < nb_first, pa_ref[...], pb_ref[...]).astype(BF16)
    tn = o_ref.shape[1]
    step = min(tn, TN)
    for c in range(0, tn, step):
        cols = slice(c, c + step)
        gate = jax.nn.sigmoid(jnp.dot(xn_ref[...], wgb_ref[:, cols], preferred_element_type=F32) + bg_ref[:, cols])
        ple = jnp.dot(p, wpb_ref[:, cols], preferred_element_type=F32)
        o_ref[:, cols] = h_ref[:, cols] + gate * ple


def _ple(xn, w_gate, b_gate, pa, pb, w_proj, h):
    n, d = h.shape
    pd = pa.shape[1]
    tm, tn = TM_WIDE, TN_WIDE
    nb_first = pa.shape[0] // tm
    p_first = pl.BlockSpec((tm, pd), lambda j, i: (jnp.minimum(i, nb_first - 1), 0))
    p_last = pl.BlockSpec((tm, pd), lambda j, i: (jnp.maximum(i - nb_first, 0), 0))
    any_spec = pl.BlockSpec(memory_space=pl.ANY)
    return pl.pallas_call(
        functools.partial(_ple_kernel, nb_first=nb_first),
        grid=(d // tn, n // tm),
        in_specs=[pl.BlockSpec((tm, d), lambda j, i: (i, 0)),
                  any_spec,
                  pl.BlockSpec((1, tn), lambda j, i: (0, j)),
                  p_first, p_last,
                  any_spec,
                  pl.BlockSpec((tm, tn), lambda j, i: (i, j))],
        out_specs=pl.BlockSpec((tm, tn), lambda j, i: (i, j)),
        out_shape=jax.ShapeDtypeStruct((n, d), F32),
        scratch_shapes=_weight_scratch([(d, tn), (pd, tn)]),
        compiler_params=_params(2, 58),
        name="ple",
    )(xn, w_gate, b_gate.reshape(1, d), pa, pb, w_proj, h)


def _layer(xn, h_p, h_s, p_p, p_s, cache_k, cache_v, state_pool, lw):
    n, d = xn.shape
    n_prompt = h_p.shape[0]
    n_seq, n_cache, n_kv, _ = cache_k.shape
    t_new = (n - n_prompt) // n_seq
    kv_w = n_kv * HEAD_DIM
    attn_w = lw['w_attn_up'].shape[0]
    pool_w = lw['w_pool_up'].shape[0]
    n_pool = state_pool.shape[1]

    qkv = _in_proj(xn, lw['w_in'], 0, attn_w + 2 * kv_w, "in_proj_qkv")
    z = _in_proj(xn, lw['w_in'], attn_w + 2 * kv_w, pool_w, "in_proj_pool")
    gates = _in_proj(xn, lw['w_in'], attn_w + 2 * kv_w + pool_w, 2 * d, "in_proj_gates")

    tab_s = tuple(jnp.tile(a, (SEQS_PER_STEP, 1))
                  for a in _rope_tables(PAST_LEN + jnp.arange(t_new, dtype=F32)))
    tab_p_t = _rope_tables_t(jnp.arange(n_prompt, dtype=F32))
    attn, k_rot_p = _attn_prompt(qkv, lw['attn_sinks'], tab_p_t, n_prompt, attn_w, kv_w)
    attn, new_k_s, new_v_s = _attn_sample(
        qkv, attn, n_prompt, cache_k.reshape(n_seq, n_cache, kv_w), cache_v.reshape(n_seq, n_cache, kv_w),
        lw['attn_sinks'], tab_s, n_seq, t_new, attn_w, kv_w)

    mixed = _pool_prompt(z, n_prompt, lw['w_pool_grp'], lw['pool_scale'])
    mixed, new_z_s = _pool_sample(state_pool, z, mixed, n_prompt, lw['w_pool_grp'], lw['pool_scale'], t_new)

    merged = _merge(attn, mixed, lw['w_attn_up'], lw['w_pool_up'], gates, lw['b_gate'].reshape(1, 2 * d))
    h1 = _out_proj(merged, lw['w_out'], h_p, h_s)

    n_tiles = (n * TOP_K) // TM_E + N_EXPERTS
    info, cw, xf3, counts = _router(h1, lw['g_ffn'], lw['w_route_group'], lw['b_route_group'],
                                    lw['w_route_expert'], lw['b_route_expert'])
    pos, plan, last_tile_row = _dispatch_plan(info, counts, n_tiles)
    xs3 = _dispatch(pos, last_tile_row, plan[1], xf3, n_tiles)
    hid = _moe_up(plan, xs3, lw['w_exp_gate'], lw['w_exp_up'], n_tiles)
    y3 = _moe_down(plan, hid, lw['w_exp_down'], n_tiles)
    h2, xn2 = _combine(pos, h1, cw, y3, lw['g_ple'])

    h3 = _ple(xn2, lw['w_ple_gate'], lw['b_ple_gate'], p_p, p_s, lw['w_ple_proj'], h2)

    new_k_p = k_rot_p[n_prompt - n_cache:].reshape(1, n_cache, n_kv, HEAD_DIM)
    new_v_p = qkv[n_prompt - n_cache:n_prompt, attn_w + kv_w:].reshape(1, n_cache, n_kv, HEAD_DIM)
    new_z_p = z[n_prompt - n_pool:n_prompt].reshape(1, n_pool, pool_w)
    return (h3, new_k_p, new_v_p, new_z_p,
            new_k_s.reshape(cache_k.shape), new_v_s.reshape(cache_v.shape), new_z_s)


def kernel(x_prompt, x_sample, cache_k, cache_v, state_pool, p_prompt, p_sample, g_mix, w_in, b_gate, attn_sinks, w_pool_grp, pool_scale, w_attn_up, w_pool_up, w_out, g_ffn, w_route_group, b_route_group, w_route_expert, b_route_expert, w_exp_gate, w_exp_up, w_exp_down, g_ple, w_ple_gate, b_ple_gate, w_ple_proj, g_final):
    batch, seq, d = x_prompt.shape
    n_seq, t_new, _ = x_sample.shape
    depth = w_in.shape[0]
    assert batch == 1, "prompt rows are treated as one sequence"
    n_prompt = batch * seq
    n_sample = n_seq * t_new
    n = n_prompt + n_sample
    assert n_prompt % TM == 0 and n_sample % TM == 0 and (n * TOP_K) % TM_E == 0 and n % TM_DISPATCH == 0
    assert n_seq % SEQS_PER_STEP == 0 and n_seq % POOL_SEQS == 0 and t_new & (t_new - 1) == 0
    assert cache_k.shape[2] == WINDOW and state_pool.shape[2] == POOL_HALO - 1
    assert depth == 1, "one layer per step"

    weights = dict(g_mix=g_mix, w_in=w_in, b_gate=b_gate, attn_sinks=attn_sinks, w_pool_grp=w_pool_grp,
                   pool_scale=pool_scale, w_attn_up=w_attn_up, w_pool_up=w_pool_up, w_out=w_out, g_ffn=g_ffn,
                   w_route_group=w_route_group, b_route_group=b_route_group, w_route_expert=w_route_expert,
                   b_route_expert=b_route_expert, w_exp_gate=w_exp_gate, w_exp_up=w_exp_up,
                   w_exp_down=w_exp_down, g_ple=g_ple, w_ple_gate=w_ple_gate, b_ple_gate=b_ple_gate,
                   w_ple_proj=w_ple_proj)
    lw = {name: w[0] for name, w in weights.items()}
    h_p = x_prompt.reshape(n_prompt, d)
    h_s = x_sample.reshape(n_sample, d)
    xn = _rmsnorm_in2(h_p, h_s, lw['g_mix'], BF16, "norm_mix")
    out = _layer(xn, h_p, h_s, p_prompt[0].reshape(n_prompt, -1), p_sample[0].reshape(n_sample, -1),
                 cache_k[0], cache_v[0], state_pool[0], lw)
    y_p, y_s = _rmsnorm_out2(out[0], g_final, n_prompt, "norm_final")
    return (y_p.reshape(batch, seq, d), y_s.reshape(n_seq, t_new, d)) + tuple(piece[None] for piece in out[1:])
```

```python
import functools

import jax
import jax.numpy as jnp
from jax import lax
from jax.experimental import pallas as pl
from jax.experimental.pallas import tpu as pltpu

F32 = jnp.float32
BF16 = jnp.bfloat16
I32 = jnp.int32

HEAD_DIM = 64
Q_PER_KV = 4
WINDOW = 128
PAST_LEN = 8192
ROT_DIM = HEAD_DIM // 4
ROPE_THETA = 500000.0
POOL_WINDOWS = (2, 4, 8, 16)
POOL_HALO = 16
N_GROUPS = 4
EXPERTS_PER_GROUP = 4
N_EXPERTS = N_GROUPS * EXPERTS_PER_GROUP
TOP_K = 2
RMS_EPS = 1e-6
NEG_INF = -1e30
LOG2_E = 1.4426950408889634
LANES = 128
SUBLANES = 8
MIB = 1024 * 1024

TM = 1024
TN = 512
TN_PROJ = 1024
CAST_ROWS = 512
TM_WIDE = 512
TN_WIDE = 1024
TM_ROW = 256
TM_E = 512
TM_DISPATCH = 512
SEQS_PER_STEP = 8
POOL_SEQS = 16


def _params(n_axes, vmem_mib):
    return pltpu.CompilerParams(dimension_semantics=("arbitrary",) * n_axes,
                                vmem_limit_bytes=vmem_mib * MIB)


def _rms_rows(x, g):
    ms = jnp.mean(x * x, axis=-1, keepdims=True)
    return x * lax.rsqrt(ms + RMS_EPS) * g


def _first_last_specs(nb_first, block):
    zeros = (0,) * (len(block) - 1)
    first = pl.BlockSpec(block, lambda i, *_: (jnp.minimum(i, nb_first - 1),) + zeros)
    last = pl.BlockSpec(block, lambda i, *_: (jnp.maximum(i - nb_first, 0),) + zeros)
    return first, last


def _norm_in2_kernel(xa_ref, xb_ref, g_ref, o_ref, *, nb_first):
    i = pl.program_id(0)

    @pl.when(i < nb_first)
    def _():
        o_ref[...] = _rms_rows(xa_ref[...], g_ref[...]).astype(o_ref.dtype)

    @pl.when(i >= nb_first)
    def _():
        o_ref[...] = _rms_rows(xb_ref[...], g_ref[...]).astype(o_ref.dtype)


def _rmsnorm_in2(xa, xb, g, out_dtype, name):
    (na, d), nb = xa.shape, xb.shape[0]
    nb_first = na // TM_ROW
    spec_a, spec_b = _first_last_specs(nb_first, (TM_ROW, d))
    return pl.pallas_call(
        functools.partial(_norm_in2_kernel, nb_first=nb_first),
        grid=((na + nb) // TM_ROW,),
        in_specs=[spec_a, spec_b, pl.BlockSpec((1, d), lambda i: (0, 0))],
        out_specs=pl.BlockSpec((TM_ROW, d), lambda i: (i, 0)),
        out_shape=jax.ShapeDtypeStruct((na + nb, d), out_dtype),
        compiler_params=_params(1, 40),
        name=name,
    )(xa, xb, g.reshape(1, d))


def _norm_out2_kernel(x_ref, g_ref, oa_ref, ob_ref, *, nb_first):
    i = pl.program_id(0)

    @pl.when(i < nb_first)
    def _():
        oa_ref[...] = _rms_rows(x_ref[...], g_ref[...])

    @pl.when(i >= nb_first)
    def _():
        ob_ref[...] = _rms_rows(x_ref[...], g_ref[...])


def _rmsnorm_out2(x, g, na, name):
    n, d = x.shape
    nb_first = na // TM_ROW
    spec_a, spec_b = _first_last_specs(nb_first, (TM_ROW, d))
    return pl.pallas_call(
        functools.partial(_norm_out2_kernel, nb_first=nb_first),
        grid=(n // TM_ROW,),
        in_specs=[pl.BlockSpec((TM_ROW, d), lambda i: (i, 0)), pl.BlockSpec((1, d), lambda i: (0, 0))],
        out_specs=[spec_a, spec_b],
        out_shape=[jax.ShapeDtypeStruct((na, d), F32), jax.ShapeDtypeStruct((n - na, d), F32)],
        compiler_params=_params(1, 40),
        name=name,
    )(x, g.reshape(1, d))


def _cast_rows(src_ref, dst_ref):
    rows = src_ref.shape[0]
    step = min(rows, CAST_ROWS)
    for r in range(0, rows, step):
        dst_ref[r:r + step] = src_ref[r:r + step].astype(dst_ref.dtype)


def _column_weights_step(w_hbms, col0s, wbuf_refs, wb_refs, sem):
    j, i = pl.program_id(0), pl.program_id(1)

    def copies(jj):
        out = []
        for w, col0, buf in zip(w_hbms, col0s, wbuf_refs):
            tn = buf.shape[1]
            out.append(pltpu.make_async_copy(w.at[:, pl.ds(pl.multiple_of(col0 + jj * tn, tn), tn)], buf, sem))
        return out

    @pl.when(i == 0)
    def _():
        @pl.when(j == 0)
        def _():
            for c in copies(j):
                c.start()

        for c in copies(j):
            c.wait()
        for buf, wb_ref in zip(wbuf_refs, wb_refs):
            _cast_rows(buf, wb_ref)

        @pl.when(j + 1 < pl.num_programs(0))
        def _():
            for c in copies(j + 1):
                c.start()


def _weight_scratch(shapes):
    return ([pltpu.VMEM(s, F32) for s in shapes] + [pltpu.VMEM(s, BF16) for s in shapes]
            + [pltpu.SemaphoreType.DMA(())])


def _proj_kernel(a_ref, w_hbm, o_ref, wbuf_ref, wb_ref, sem, *, col0):
    _column_weights_step((w_hbm,), (col0,), (wbuf_ref,), (wb_ref,), sem)
    o_ref[...] = jnp.dot(a_ref[...], wb_ref[...], preferred_element_type=F32)


def _in_proj(xn, w, col0, width, name):
    n, d = xn.shape
    tn = TN_PROJ
    assert col0 % tn == 0 and width % tn == 0
    return pl.pallas_call(
        functools.partial(_proj_kernel, col0=col0),
        grid=(width // tn, n // TM),
        in_specs=[pl.BlockSpec((TM, d), lambda j, i: (i, 0)),
                  pl.BlockSpec(memory_space=pl.ANY)],
        out_specs=pl.BlockSpec((TM, tn), lambda j, i: (i, j)),
        out_shape=jax.ShapeDtypeStruct((n, width), F32),
        scratch_shapes=_weight_scratch([(d, tn)]),
        compiler_params=_params(2, 56),
        name=name,
    )(xn, w)


def _rope_tables(pos):
    half = ROT_DIM // 2
    inv_freq = ROPE_THETA ** (-jnp.arange(half, dtype=F32) * (2.0 / ROT_DIM))
    ang = pos[:, None] * inv_freq[None, :]
    cos, sin = jnp.cos(ang), jnp.sin(ang)
    t = pos.shape[0]
    pad = jnp.zeros((t, HEAD_DIM - ROT_DIM), F32)
    zeros = jnp.zeros((t, half), F32)
    cos_h = jnp.concatenate([cos, cos, pad + 1.0], axis=1)
    sa_h = jnp.concatenate([zeros, sin, pad], axis=1)
    sb_h = jnp.concatenate([-sin, zeros, pad], axis=1)
    reps = LANES // HEAD_DIM
    return tuple(jnp.tile(a, (1, reps)) for a in (cos_h, sa_h, sb_h))


def _rope_tables_t(pos):
    half = ROT_DIM // 2
    inv_freq = ROPE_THETA ** (-jnp.arange(half, dtype=F32) * (2.0 / ROT_DIM))
    ang = pos[:, None] * inv_freq[None, :]
    return jnp.cos(ang).T, jnp.sin(ang).T


def _rope(x, cos_t, sa_t, sb_t):
    pieces = []
    for c in range(x.shape[1] // LANES):
        xc = x[:, c * LANES:(c + 1) * LANES]
        pieces.append(xc * cos_t
                      + pltpu.roll(xc, ROT_DIM // 2, 1) * sa_t
                      + pltpu.roll(xc, LANES - ROT_DIM // 2, 1) * sb_t)
    return jnp.concatenate(pieces, axis=1)


def _softmax2_with_sink(s, sink):
    m = jnp.maximum(jnp.max(s, axis=-1, keepdims=True), sink)
    p = jnp.exp2(s - m)
    den = jnp.sum(p, axis=-1, keepdims=True) + jnp.exp2(sink - m)
    return p, 1.0 / den


def _attn_prompt_kernel(sink_ref, q_ref, k_ref, v_ref, cost_ref, sint_ref,
                        o_ref, ko_ref, kprev_ref, vprevt_ref, *, n_kv, nb):
    b = pl.program_id(0)

    @pl.when(b == 0)
    def _():
        kprev_ref[...] = jnp.zeros_like(kprev_ref)
        vprevt_ref[...] = jnp.zeros_like(vprevt_ref)

    @pl.when(b >= nb)
    def _():
        o_ref[...] = jnp.zeros_like(o_ref)

    @pl.when(b < nb)
    def _():
        _attn_prompt_block(sink_ref, q_ref, k_ref, v_ref, cost_ref, sint_ref,
                           o_ref, ko_ref, kprev_ref, vprevt_ref, n_kv=n_kv)


def _attn_prompt_block(sink_ref, q_ref, k_ref, v_ref, cost_ref, sint_ref,
                       o_ref, ko_ref, kprev_ref, vprevt_ref, *, n_kv):
    b = pl.program_id(0)
    half = ROT_DIM // 2
    cos_t, sin_t = cost_ref[...], sint_ref[...]

    def rope_t(x):
        x1, x2 = x[0:half], x[half:ROT_DIM]
        return jnp.concatenate([x1 * cos_t - x2 * sin_t, x2 * cos_t + x1 * sin_t, x[ROT_DIM:]], axis=0)

    kt = k_ref[...].T
    k = jnp.concatenate([rope_t(kt[h * HEAD_DIM:(h + 1) * HEAD_DIM]) for h in range(n_kv)], axis=0).T
    ko_ref[...] = k
    kcat = jnp.concatenate([kprev_ref[...], k], axis=0).astype(BF16)
    vt = v_ref[...].T
    vcat_t = jnp.concatenate([vprevt_ref[...], vt], axis=1).astype(BF16)
    kprev_ref[...] = k
    vprevt_ref[...] = vt

    qt = q_ref[...].T

    def roped_head_t(h):
        return (rope_t(qt[h * HEAD_DIM:(h + 1) * HEAD_DIM]) * (HEAD_DIM ** -0.5 * LOG2_E)).astype(BF16)

    n_heads = n_kv * Q_PER_KV
    zeros_head = jnp.zeros((HEAD_DIM, WINDOW), BF16)
    heads_per_tile = LANES // HEAD_DIM

    key = lax.broadcasted_iota(I32, (2 * WINDOW, Q_PER_KV * WINDOW), 0)
    qry = lax.broadcasted_iota(I32, (2 * WINDOW, Q_PER_KV * WINDOW), 1) & (WINDOW - 1)
    first_key = jnp.where(b > 0, 0, WINDOW)
    mask = (key > qry) & (key <= qry + WINDOW) & (key >= first_key)

    scores = []
    for g in range(n_kv):
        tile, slot = g // heads_per_tile, g % heads_per_tile
        k2 = kcat[:, tile * LANES:(tile + 1) * LANES]
        cols = []
        for r in range(Q_PER_KV):
            parts = [zeros_head] * heads_per_tile
            parts[slot] = roped_head_t(g * Q_PER_KV + r)
            cols.append(jnp.concatenate(parts, axis=0))
        rhs = jnp.concatenate(cols, axis=1)
        scores.append(jnp.where(mask, jnp.dot(k2, rhs, preferred_element_type=F32), NEG_INF))
    st = jnp.concatenate(scores, axis=1)
    sink = jnp.concatenate([jnp.full((1, WINDOW), sink_ref[h] * LOG2_E, F32) for h in range(n_heads)], axis=1)
    m = jnp.maximum(jnp.max(st, axis=0, keepdims=True), sink)
    p = jnp.exp2(st - m)
    den = jnp.sum(p, axis=0, keepdims=True) + jnp.exp2(sink - m)
    inv_den = 1.0 / den
    probs_t = p.astype(BF16)

    for g in range(n_kv):
        cols = slice(g * Q_PER_KV * WINDOW, (g + 1) * Q_PER_KV * WINDOW)
        ot = jnp.dot(vcat_t[g * HEAD_DIM:(g + 1) * HEAD_DIM], probs_t[:, cols],
                     preferred_element_type=F32) * inv_den[:, cols]
        for pr in range(Q_PER_KV // heads_per_tile):
            pair = jnp.concatenate(
                [ot[:, (pr * heads_per_tile + u) * WINDOW:(pr * heads_per_tile + u + 1) * WINDOW]
                 for u in range(heads_per_tile)], axis=0)
            c = (g * Q_PER_KV) // heads_per_tile + pr
            o_ref[:, c * LANES:(c + 1) * LANES] = pair.T.astype(o_ref.dtype)


def _attn_prompt(qkv, sinks, tables_t, n_rows, attn_w, kv_w):
    nb = n_rows // WINDOW
    kb, vb = attn_w // kv_w, attn_w // kv_w + 1

    def blk(b, s):
        return jnp.minimum(b, nb - 1)

    tab_t_spec = pl.BlockSpec((ROT_DIM // 2, WINDOW), lambda b, s: (0, blk(b, s)))
    return pl.pallas_call(
        functools.partial(_attn_prompt_kernel, n_kv=kv_w // HEAD_DIM, nb=nb),
        grid_spec=pltpu.PrefetchScalarGridSpec(
            num_scalar_prefetch=1,
            grid=(qkv.shape[0] // WINDOW,),
            in_specs=[pl.BlockSpec((WINDOW, attn_w), lambda b, s: (blk(b, s), 0)),
                      pl.BlockSpec((WINDOW, kv_w), lambda b, s: (blk(b, s), kb)),
                      pl.BlockSpec((WINDOW, kv_w), lambda b, s: (blk(b, s), vb)),
                      tab_t_spec, tab_t_spec],
            out_specs=[pl.BlockSpec((WINDOW, attn_w), lambda b, s: (b, 0)),
                       pl.BlockSpec((WINDOW, kv_w), lambda b, s: (blk(b, s), 0))],
            scratch_shapes=[pltpu.VMEM((WINDOW, kv_w), F32), pltpu.VMEM((kv_w, WINDOW), F32)]),
        out_shape=[jax.ShapeDtypeStruct((qkv.shape[0], attn_w), BF16),
                   jax.ShapeDtypeStruct((n_rows, kv_w), F32)],
        compiler_params=_params(1, 32),
        name="attn_prompt",
    )(sinks, qkv, qkv, qkv, *tables_t)


def _attn_sample_kernel(sink_ref, q_ref, k_ref, v_ref, ck_ref, cv_ref, cos_ref, sa_ref, sb_ref, attn_in_ref,
                        o_ref, nk_ref, nv_ref, *, n_kv, t_new):
    del attn_in_ref
    assert LANES // HEAD_DIM == 2 and Q_PER_KV % 2 == 0
    cos_t, sa_t, sb_t = cos_ref[...], sa_ref[...], sb_ref[...]
    q = _rope(q_ref[...], cos_t, sa_t, sb_t) * (HEAD_DIM ** -0.5 * LOG2_E)
    k = _rope(k_ref[...], cos_t, sa_t, sb_t)
    v = v_ref[...]
    n_cache = ck_ref.shape[1]
    n_keys = 2 * WINDOW
    n_seq = q.shape[0] // t_new
    n_heads = n_kv * Q_PER_KV
    zpad = jnp.zeros((n_keys - n_cache - t_new, k.shape[1]), F32)
    low = lax.broadcasted_iota(I32, (t_new, LANES), 1) < HEAD_DIM
    high = jnp.logical_not(low)

    qt = lax.broadcasted_iota(I32, (Q_PER_KV * t_new, n_keys), 0) & (t_new - 1)
    kj = lax.broadcasted_iota(I32, (Q_PER_KV * t_new, n_keys), 1)
    mask = (kj > qt + (n_cache - WINDOW)) & (kj <= qt + n_cache)

    def swap_halves(x):
        return pltpu.roll(x, HEAD_DIM, 1)

    scores, values = [], []
    for s_i in range(n_seq):
        ck, cv = ck_ref[s_i], cv_ref[s_i]
        kn, vn = k[s_i * t_new:(s_i + 1) * t_new], v[s_i * t_new:(s_i + 1) * t_new]
        nk_ref[s_i] = jnp.concatenate([ck[t_new:], kn], axis=0)
        nv_ref[s_i] = jnp.concatenate([cv[t_new:], vn], axis=0)
        kall = jnp.concatenate([ck, kn, zpad], axis=0).astype(BF16)
        values.append(jnp.concatenate([cv, vn, zpad], axis=0).astype(BF16))
        qs = q[s_i * t_new:(s_i + 1) * t_new]
        for g in range(n_kv):
            tile, odd = g // 2, g % 2
            parts = []
            for r in range(Q_PER_KV):
                h = g * Q_PER_KV + r
                x = qs[:, (h // 2) * LANES:(h // 2 + 1) * LANES]
                if h % 2 != odd:
                    x = swap_halves(x)
                parts.append(jnp.where(high if odd else low, x, 0.0))
            qg = jnp.concatenate(parts, axis=0).astype(BF16)
            k2 = kall[:, tile * LANES:(tile + 1) * LANES]
            sg = lax.dot_general(qg, k2, (((1,), (1,)), ((), ())), preferred_element_type=F32)
            scores.append(jnp.where(mask, sg, NEG_INF))
    s = jnp.concatenate(scores, axis=0)
    sink_seq = jnp.concatenate([jnp.full((t_new, 1), sink_ref[h] * LOG2_E, F32) for h in range(n_heads)], axis=0)
    sink = jnp.concatenate([sink_seq] * n_seq, axis=0)
    p, inv_den = _softmax2_with_sink(s, sink)
    probs = p.astype(BF16)

    seq_outs = []
    for s_i in range(n_seq):
        out_tiles = []
        for g in range(n_kv):
            tile, odd = g // 2, g % 2
            r0 = (s_i * n_kv + g) * Q_PER_KV * t_new
            v2 = values[s_i][:, tile * LANES:(tile + 1) * LANES]
            rows = slice(r0, r0 + Q_PER_KV * t_new)
            og = jnp.dot(probs[rows], v2, preferred_element_type=F32) * inv_den[rows]
            for pr in range(Q_PER_KV // 2):
                even = og[(2 * pr) * t_new:(2 * pr + 1) * t_new]
                oddh = og[(2 * pr + 1) * t_new:(2 * pr + 2) * t_new]
                if odd:
                    even = swap_halves(even)
                else:
                    oddh = swap_halves(oddh)
                out_tiles.append(jnp.where(low, even, oddh))
        seq_outs.append(jnp.concatenate(out_tiles, axis=1))
    o_ref[...] = jnp.concatenate(seq_outs, axis=0).astype(o_ref.dtype)


def _attn_sample(qkv, attn, row0, cache_k, cache_v, sinks, tables, n_seq, t_new, attn_w, kv_w):
    rows = SEQS_PER_STEP * t_new
    rb0 = row0 // rows
    kb, vb = attn_w // kv_w, attn_w // kv_w + 1
    n_cache = cache_k.shape[1]
    tab_spec = pl.BlockSpec((rows, LANES), lambda i, s: (0, 0))
    cache_spec = pl.BlockSpec((SEQS_PER_STEP, n_cache, kv_w), lambda i, s: (i, 0, 0))
    return pl.pallas_call(
        functools.partial(_attn_sample_kernel, n_kv=kv_w // HEAD_DIM, t_new=t_new),
        grid_spec=pltpu.PrefetchScalarGridSpec(
            num_scalar_prefetch=1,
            grid=(n_seq // SEQS_PER_STEP,),
            in_specs=[pl.BlockSpec((rows, attn_w), lambda i, s: (rb0 + i, 0)),
                      pl.BlockSpec((rows, kv_w), lambda i, s: (rb0 + i, kb)),
                      pl.BlockSpec((rows, kv_w), lambda i, s: (rb0 + i, vb)),
                      cache_spec, cache_spec, tab_spec, tab_spec, tab_spec,
                      pl.BlockSpec(memory_space=pl.ANY)],
            out_specs=[pl.BlockSpec((rows, attn_w), lambda i, s: (rb0 + i, 0)), cache_spec, cache_spec]),
        out_shape=[jax.ShapeDtypeStruct(attn.shape, attn.dtype),
                   jax.ShapeDtypeStruct(cache_k.shape, F32),
                   jax.ShapeDtypeStruct(cache_v.shape, F32)],
        input_output_aliases={9: 0},
        compiler_params=_params(1, 32),
        name="attn_sample",
    )(sinks, qkv, qkv, qkv, cache_k, cache_v, *tables, attn)


def _pool_prompt_kernel(z_ref, halo_ref, w_ref, sc_ref, o_ref, ext_ref, lvla_ref, lvlb_ref, *, tm, pg, nb):
    i = pl.program_id(0)

    @pl.when(i >= nb)
    def _():
        o_ref[...] = jnp.zeros_like(o_ref)

    @pl.when(i < nb)
    def _():
        _pool_prompt_block(z_ref, halo_ref, w_ref, sc_ref, o_ref, ext_ref, lvla_ref, lvlb_ref, tm=tm, pg=pg)


def _pool_prompt_block(z_ref, halo_ref, w_ref, sc_ref, o_ref, ext_ref, lvla_ref, lvlb_ref, *, tm, pg):
    i = pl.program_id(0)
    data0 = 2 * POOL_HALO
    total = data0 + tm
    ext_ref[0:POOL_HALO, :] = jnp.zeros((POOL_HALO, ext_ref.shape[1]), F32)
    ext_ref[POOL_HALO:data0, :] = jnp.where(i > 0, halo_ref[...], 0.0)
    ext_ref[data0:, :] = z_ref[...]
    pos = i * tm + lax.broadcasted_iota(I32, (tm, 1), 0)
    for g, w in enumerate(POOL_WINDOWS):
        c0, c1 = g * pg, (g + 1) * pg
        cur = ext_ref[data0:, c0:c1]
        prev_ref, pc0, start, m = ext_ref, c0, 0, 1
        levels = (lvla_ref, lvlb_ref)
        for k in range(w.bit_length() - 1):
            start += SUBLANES
            s = prev_ref[start:total, pc0:pc0 + pg] + prev_ref[start - m:total - m, pc0:pc0 + pg]
            m *= 2
            if m < w:
                levels[k % 2][start:total, :] = s
                prev_ref, pc0 = levels[k % 2], 0
        acc = s[data0 - start:]
        cnt = jnp.minimum(pos + 1, w).astype(F32)
        pooled = (acc / cnt - cur).astype(BF16)
        mixed = jnp.dot(pooled, w_ref[g].astype(BF16), preferred_element_type=F32) * sc_ref[:, c0:c1]
        o_ref[:, c0:c1] = mixed.astype(o_ref.dtype)


def _pool_prompt(z, n_rows, w_grp, scale):
    n, pw = z.shape
    tm = TM_ROW
    hb = tm // POOL_HALO
    return pl.pallas_call(
        functools.partial(_pool_prompt_kernel, tm=tm, pg=pw // len(POOL_WINDOWS), nb=n_rows // tm),
        grid=(n // tm,),
        in_specs=[pl.BlockSpec((tm, pw), lambda i: (i, 0)),
                  pl.BlockSpec((POOL_HALO, pw), lambda i: (jnp.maximum(i * hb - 1, 0), 0)),
                  pl.BlockSpec(w_grp.shape, lambda i: (0, 0, 0)),
                  pl.BlockSpec((1, pw), lambda i: (0, 0))],
        out_specs=pl.BlockSpec((tm, pw), lambda i: (i, 0)),
        out_shape=jax.ShapeDtypeStruct((n, pw), BF16),
        scratch_shapes=[pltpu.VMEM((tm + 2 * POOL_HALO, pw), F32),
                        pltpu.VMEM((tm + 2 * POOL_HALO, pw // len(POOL_WINDOWS)), F32),
                        pltpu.VMEM((tm + 2 * POOL_HALO, pw // len(POOL_WINDOWS)), F32)],
        compiler_params=_params(1, 40),
        name="pool_prompt",
    )(z, z, w_grp, scale.reshape(1, pw))


def _pool_sample_kernel(st_ref, z_ref, w_ref, sc_ref, mixed_in_ref, o_ref, ns_ref, ext_ref, *, pg, t_new):
    del mixed_in_ref
    n_st = st_ref.shape[1]
    seqs = st_ref.shape[0]
    hist = ext_ref.shape[1] - t_new
    ext_ref[:, hist - n_st:hist, :] = st_ref[...]
    ext_ref[:, hist:, :] = z_ref[...]
    ns_ref[...] = ext_ref[:, hist + t_new - n_st:, :]
    for g, w in enumerate(POOL_WINDOWS):
        c0, c1 = g * pg, (g + 1) * pg
        cur = ext_ref[:, hist:, c0:c1]
        acc = cur
        for d in range(1, w):
            acc = acc + ext_ref[:, hist - d:hist - d + t_new, c0:c1]
        pooled = (acc / float(w) - cur).reshape(seqs * t_new, pg).astype(BF16)
        mixed = jnp.dot(pooled, w_ref[g].astype(BF16), preferred_element_type=F32) * sc_ref[:, c0:c1]
        o_ref[:, c0:c1] = mixed.astype(o_ref.dtype)


def _pool_sample(state, z, mixed, row0, w_grp, scale, t_new):
    n_seq, n_st, pw = state.shape
    assert t_new == SUBLANES and n_st <= POOL_HALO - 1
    z3 = z.reshape(z.shape[0] // t_new, t_new, pw)
    sb0 = row0 // t_new // POOL_SEQS
    rows = POOL_SEQS * t_new
    st_spec = pl.BlockSpec((POOL_SEQS, n_st, pw), lambda i: (i, 0, 0))
    return pl.pallas_call(
        functools.partial(_pool_sample_kernel, pg=pw // len(POOL_WINDOWS), t_new=t_new),
        grid=(n_seq // POOL_SEQS,),
        in_specs=[st_spec,
                  pl.BlockSpec((POOL_SEQS, t_new, pw), lambda i: (sb0 + i, 0, 0)),
                  pl.BlockSpec(w_grp.shape, lambda i: (0, 0, 0)),
                  pl.BlockSpec((1, pw), lambda i: (0, 0)),
                  pl.BlockSpec(memory_space=pl.ANY)],
        out_specs=[pl.BlockSpec((rows, pw), lambda i: (row0 // rows + i, 0)), st_spec],
        out_shape=[jax.ShapeDtypeStruct(mixed.shape, mixed.dtype), jax.ShapeDtypeStruct(state.shape, F32)],
        scratch_shapes=[pltpu.VMEM((POOL_SEQS, POOL_HALO + t_new, pw), F32)],
        input_output_aliases={4: 0},
        compiler_params=_params(1, 40),
        name="pool_sample",
    )(state, z3, w_grp, scale.reshape(1, pw), mixed)


def _merge_kernel(a1_ref, a2_ref, w1_hbm, w2_hbm, ga_ref, gb_ref, ba_ref, bb_ref, o_ref,
                  w1f_ref, w2f_ref, w1b_ref, w2b_ref, sem):
    _column_weights_step((w1_hbm, w2_hbm), (0, 0), (w1f_ref, w2f_ref), (w1b_ref, w2b_ref), sem)
    a_up = jnp.dot(a1_ref[...], w1b_ref[...], preferred_element_type=F32)
    b_up = jnp.dot(a2_ref[...], w2b_ref[...], preferred_element_type=F32)
    merged = (jax.nn.sigmoid(ga_ref[...] + ba_ref[...]) * a_up
              + jax.nn.sigmoid(gb_ref[...] + bb_ref[...]) * b_up)
    o_ref[...] = merged.astype(o_ref.dtype)


def _merge(attn, mixed, w_attn_up, w_pool_up, gates, b_gate):
    n, ka = attn.shape
    kp = mixed.shape[1]
    d = w_attn_up.shape[1]
    tm, tn = TM_WIDE, TN_WIDE
    nj = d // tn
    any_spec = pl.BlockSpec(memory_space=pl.ANY)
    return pl.pallas_call(
        _merge_kernel,
        grid=(nj, n // tm),
        in_specs=[pl.BlockSpec((tm, ka), lambda j, i: (i, 0)),
                  pl.BlockSpec((tm, kp), lambda j, i: (i, 0)),
                  any_spec, any_spec,
                  pl.BlockSpec((tm, tn), lambda j, i: (i, j)),
                  pl.BlockSpec((tm, tn), lambda j, i: (i, nj + j)),
                  pl.BlockSpec((1, tn), lambda j, i: (0, j)),
                  pl.BlockSpec((1, tn), lambda j, i: (0, nj + j))],
        out_specs=pl.BlockSpec((tm, tn), lambda j, i: (i, j)),
        out_shape=jax.ShapeDtypeStruct((n, d), BF16),
        scratch_shapes=_weight_scratch([(ka, tn), (kp, tn)]),
        compiler_params=_params(2, 56),
        name="merge",
    )(attn, mixed, w_attn_up, w_pool_up, gates, gates, b_gate, b_gate)


def _out_proj_kernel(a_ref, w_hbm, ha_ref, hb_ref, o_ref, wf_ref, wb_ref, sem, *, nb_first):
    _column_weights_step((w_hbm,), (0,), (wf_ref,), (wb_ref,), sem)
    h = jnp.where(pl.program_id(1) < nb_first, ha_ref[...], hb_ref[...])
    o_ref[...] = h + jnp.dot(a_ref[...], wb_ref[...], preferred_element_type=F32)


def _out_proj(merged, w_out, ha, hb):
    n, d = merged.shape
    tm, tn = TM_WIDE, TN_WIDE
    nb_first = ha.shape[0] // tm
    h_first = pl.BlockSpec((tm, tn), lambda j, i: (jnp.minimum(i, nb_first - 1), j))
    h_last = pl.BlockSpec((tm, tn), lambda j, i: (jnp.maximum(i - nb_first, 0), j))
    return pl.pallas_call(
        functools.partial(_out_proj_kernel, nb_first=nb_first),
        grid=(d // tn, n // tm),
        in_specs=[pl.BlockSpec((tm, d), lambda j, i: (i, 0)),
                  pl.BlockSpec(memory_space=pl.ANY),
                  h_first, h_last],
        out_specs=pl.BlockSpec((tm, tn), lambda j, i: (i, j)),
        out_shape=jax.ShapeDtypeStruct((n, d), F32),
        scratch_shapes=_weight_scratch([(d, tn)]),
        compiler_params=_params(2, 56),
        name="out_proj",
    )(merged, w_out, ha, hb)


def _router_kernel(h_ref, g_ref, wr_ref, br_ref, info_ref, cw_ref, xf_ref, cnt_ref, base_ref):
    tm = h_ref.shape[0]

    @pl.when(pl.program_id(0) == 0)
    def _():
        base_ref[...] = jnp.zeros_like(base_ref)

    xf = _rms_rows(h_ref[...], g_ref[...]).astype(BF16)
    xf_ref[...] = xf.reshape(xf_ref.shape)
    logits = jnp.dot(xf, wr_ref[...].astype(BF16), preferred_element_type=F32) + br_ref[...]
    lane = lax.broadcasted_iota(I32, logits.shape, 1)
    big = jnp.int32(1 << 20)
    ninf = jnp.float32(-jnp.inf)

    is_g = lane < N_GROUPS
    gl = jnp.where(is_g, logits, ninf)
    gmax = jnp.max(gl, axis=-1, keepdims=True)
    gden = jnp.sum(jnp.exp(gl - gmax), axis=-1, keepdims=True)
    g_p = 1.0 / gden
    g_idx = jnp.min(jnp.where(gl == gmax, lane, big), axis=-1, keepdims=True)

    e_lane = lane - N_GROUPS
    in_grp = (e_lane >= g_idx * EXPERTS_PER_GROUP) & (e_lane < (g_idx + 1) * EXPERTS_PER_GROUP)
    el = jnp.where(in_grp, logits, ninf)
    m1 = jnp.max(el, axis=-1, keepdims=True)
    i1 = jnp.min(jnp.where(in_grp & (el == m1), e_lane, big), axis=-1, keepdims=True)
    el2 = jnp.where(e_lane == i1, ninf, el)
    m2 = jnp.max(el2, axis=-1, keepdims=True)
    i2 = jnp.min(jnp.where(in_grp & (el2 == m2), e_lane, big), axis=-1, keepdims=True)
    t = jnp.exp(m2 - m1)
    den = 1.0 + t
    c1 = g_p * (1.0 / den)
    c2 = g_p * (t / den)
    cw_ref[...] = jnp.where(lane == 0, c1, jnp.where(lane == 1, c2, 0.0))

    hit1, hit2 = lane == i1, lane == i2
    onehot = (hit1 | hit2).astype(BF16)
    earlier = (lax.broadcasted_iota(I32, (tm, tm), 1) < lax.broadcasted_iota(I32, (tm, tm), 0)).astype(BF16)
    before = jnp.dot(earlier, onehot, preferred_element_type=F32) + base_ref[...]
    r1 = jnp.sum(jnp.where(hit1, before, 0.0), axis=-1, keepdims=True).astype(I32)
    r2 = jnp.sum(jnp.where(hit2, before, 0.0), axis=-1, keepdims=True).astype(I32)
    base_ref[...] += jnp.sum(onehot.astype(F32), axis=0, keepdims=True)
    cnt_ref[...] = base_ref[...].astype(I32)
    info_ref[...] = jnp.where(lane == 0, i1, jnp.where(lane == 1, i2,
                              jnp.where(lane == 2, r1, jnp.where(lane == 3, r2, 0))))


def _router(h, g_ffn, w_rg, b_rg, w_re, b_re):
    n, d = h.shape
    n_log = N_GROUPS + N_EXPERTS
    wr = jnp.concatenate([w_rg, w_re, jnp.zeros((d, LANES - n_log), F32)], axis=1)
    br = jnp.concatenate([b_rg, b_re, jnp.zeros((LANES - n_log,), F32)]).reshape(1, LANES)
    lane_spec = pl.BlockSpec((TM_ROW, LANES), lambda i: (i, 0))
    return pl.pallas_call(
        _router_kernel,
        grid=(n // TM_ROW,),
        in_specs=[pl.BlockSpec((TM_ROW, d), lambda i: (i, 0)),
                  pl.BlockSpec((1, d), lambda i: (0, 0)),
                  pl.BlockSpec((d, LANES), lambda i: (0, 0)),
                  pl.BlockSpec((1, LANES), lambda i: (0, 0))],
        out_specs=[lane_spec, lane_spec,
                   pl.BlockSpec((TM_ROW, d // LANES, LANES), lambda i: (i, 0, 0)),
                   pl.BlockSpec((1, LANES), lambda i: (0, 0))],
        out_shape=[jax.ShapeDtypeStruct((n, LANES), I32), jax.ShapeDtypeStruct((n, LANES), F32),
                   jax.ShapeDtypeStruct((n, d // LANES, LANES), BF16), jax.ShapeDtypeStruct((1, LANES), I32)],
        scratch_shapes=[pltpu.VMEM((1, LANES), F32)],
        compiler_params=_params(1, 40),
        name="router",
    )(h, g_ffn.reshape(1, d), wr, br)


def _dispatch_plan(info, counts, n_tiles):
    counts = counts[0, :N_EXPERTS]
    tiles_per = (counts + TM_E - 1) // TM_E
    tiles_end = jnp.cumsum(tiles_per)
    row_start = (tiles_end - tiles_per) * TM_E
    eid, rank = info[:, :TOP_K], info[:, TOP_K:2 * TOP_K]
    onehot = (eid[:, :, None] == jnp.arange(N_EXPERTS, dtype=I32)[None, None, :]).astype(I32)
    pos = (jnp.sum(onehot * row_start[None, None, :], axis=-1) + rank).reshape(-1).astype(I32)
    n_used = tiles_end[-1:].astype(I32)
    tile_ids = jnp.minimum(jnp.arange(n_tiles, dtype=I32), n_used - 1)
    tile_expert = jnp.sum((tile_ids[:, None] >= tiles_end[None, :]).astype(I32), axis=1).astype(I32)
    last_tile_row = (jnp.clip(tiles_end - 1, 0, n_tiles - 1) * TM_E).astype(I32)
    starts = jnp.concatenate([jnp.ones((1,), I32), (tile_expert[1:] != tile_expert[:-1]).astype(I32)])
    run_idx = jnp.cumsum(starts) - 1
    n_runs = run_idx[-1:] + 1
    run_expert = jnp.zeros((n_tiles,), I32).at[run_idx].set(tile_expert)
    next_expert = run_expert[(run_idx + 1) % n_runs]
    rows_end = row_start + counts
    valid = jnp.clip(rows_end[tile_expert] - jnp.arange(n_tiles, dtype=I32) * TM_E, 0, TM_E)
    plan = (tile_expert, n_used, run_idx.astype(I32), next_expert.astype(I32), n_runs.astype(I32),
            valid.astype(I32))
    return pos, plan, last_tile_row


def _dispatch_kernel(pos_ref, ltr_ref, nu_ref, xf_ref, xs_hbm, zero_ref, sem):
    i = pl.program_id(0)
    n_tiles = xs_hbm.shape[0] // TM_E

    def clear_tile(row):
        return pltpu.make_async_copy(zero_ref, xs_hbm.at[pl.ds(pl.multiple_of(row, TM_E), TM_E)], sem)

    @pl.when(i == 0)
    def _():
        zero_ref[...] = jnp.zeros_like(zero_ref)
        for e in range(N_EXPERTS):
            clear_tile(ltr_ref[e]).start()

        def start_unused(t, carry):
            clear_tile(t * TM_E).start()
            return carry

        def wait_one(t, carry):
            clear_tile(0).wait()
            return carry

        lax.fori_loop(nu_ref[0], n_tiles, start_unused, 0)
        lax.fori_loop(nu_ref[0] - N_EXPERTS, n_tiles, wait_one, 0)

    base = i * TM_DISPATCH

    def issue(r, carry):
        for k in range(TOP_K):
            dst_row = pos_ref[(base + r) * TOP_K + k]
            pltpu.make_async_copy(xf_ref.at[pl.ds(r, 1)], xs_hbm.at[pl.ds(dst_row, 1)], sem).start(priority=k % 2)
        return carry

    lax.fori_loop(0, TM_DISPATCH, issue, 0, unroll=4)
    for k in range(TOP_K):
        pltpu.make_async_copy(xf_ref, xs_hbm.at[pl.ds(0, TM_DISPATCH)], sem).wait()


def _dispatch(pos, last_tile_row, n_used, xf3, n_tiles):
    n, c, l = xf3.shape
    return pl.pallas_call(
        _dispatch_kernel,
        grid_spec=pltpu.PrefetchScalarGridSpec(
            num_scalar_prefetch=3,
            grid=(n // TM_DISPATCH,),
            in_specs=[pl.BlockSpec((TM_DISPATCH, c, l), lambda i, *_: (i, 0, 0))],
            out_specs=pl.BlockSpec(memory_space=pl.ANY),
            scratch_shapes=[pltpu.VMEM((TM_E, c, l), xf3.dtype), pltpu.SemaphoreType.DMA(())]),
        out_shape=jax.ShapeDtypeStruct((n_tiles * TM_E, c, l), xf3.dtype),
        compiler_params=_params(1, 24),
        name="moe_dispatch",
    )(pos, last_tile_row, n_used, xf3)


def _expert_weights_step(plan_refs, w_hbms, wbuf_ref, wb_refs, sem, tn):
    te_ref, _, run_ref, nxt_ref, nr_ref = plan_refs
    j, t = pl.program_id(0), pl.program_id(1)

    def copies(e, jj):
        col = pl.multiple_of(jj * tn, tn)
        return [pltpu.make_async_copy(w.at[e, :, pl.ds(col, tn)], wbuf_ref.at[m], sem)
                for m, w in enumerate(w_hbms)]

    @pl.when((t == 0) | (te_ref[t] != te_ref[jnp.maximum(t - 1, 0)]))
    def _():
        @pl.when((j == 0) & (t == 0))
        def _():
            for c in copies(te_ref[t], j):
                c.start()

        for c in copies(te_ref[t], j):
            c.wait()
        for m, wb_ref in enumerate(wb_refs):
            _cast_rows(wbuf_ref.at[m], wb_ref)

        next_j = jnp.where(run_ref[t] + 1 == nr_ref[0], j + 1, j)

        @pl.when(next_j < pl.num_programs(0))
        def _():
            for c in copies(nxt_ref[t], next_j):
                c.start()


def _by_valid_rows(valid, full_fn, zero_fn):
    heights = (TM_E // 2, 3 * TM_E // 4, TM_E)
    lo = 0
    for rows in heights:
        @pl.when((valid > lo) & (valid <= rows))
        def _(rows=rows):
            full_fn(rows)
            if rows < TM_E:
                zero_fn(rows)
        lo = rows

    @pl.when(valid == 0)
    def _():
        zero_fn(0)


def _moe_up_kernel(te_ref, nu_ref, run_ref, nxt_ref, nr_ref, valid_ref, xs_ref, wg_hbm, wu_hbm, o_ref,
                   wbuf_ref, wgb_ref, wub_ref, sem):
    t = pl.program_id(1)
    _expert_weights_step((te_ref, nu_ref, run_ref, nxt_ref, nr_ref), (wg_hbm, wu_hbm), wbuf_ref,
                         (wgb_ref, wub_ref), sem, o_ref.shape[1])

    def compute(rows):
        x = xs_ref[0:rows].reshape(rows, -1)
        gate = jnp.dot(x, wgb_ref[...], preferred_element_type=F32)
        up = jnp.dot(x, wub_ref[...], preferred_element_type=F32)
        o_ref[0:rows] = (jax.nn.silu(gate) * up).astype(o_ref.dtype)

    def clear(row0):
        o_ref[row0:] = jnp.zeros((TM_E - row0,) + o_ref.shape[1:], o_ref.dtype)

    _by_valid_rows(valid_ref[t], compute, clear)


def _moe_up(plan, xs3, w_gate, w_up, n_tiles):
    d, f = w_gate.shape[1], w_gate.shape[2]
    tf = TN
    any_spec = pl.BlockSpec(memory_space=pl.ANY)
    return pl.pallas_call(
        _moe_up_kernel,
        grid_spec=pltpu.PrefetchScalarGridSpec(
            num_scalar_prefetch=len(plan),
            grid=(f // tf, n_tiles),
            in_specs=[pl.BlockSpec((TM_E,) + xs3.shape[1:], lambda j, t, te, nu, *_: (jnp.minimum(t, nu[0] - 1), 0, 0)),
                      any_spec, any_spec],
            out_specs=pl.BlockSpec((TM_E, tf), lambda j, t, *_: (t, j)),
            scratch_shapes=[pltpu.VMEM((2, d, tf), F32), pltpu.VMEM((d, tf), BF16), pltpu.VMEM((d, tf), BF16),
                            pltpu.SemaphoreType.DMA(())]),
        out_shape=jax.ShapeDtypeStruct((n_tiles * TM_E, f), BF16),
        compiler_params=_params(2, 56),
        name="moe_up",
    )(*plan, xs3, w_gate, w_up)


def _moe_down_kernel(te_ref, nu_ref, run_ref, nxt_ref, nr_ref, valid_ref, hid_ref, wd_hbm, o_ref,
                     wbuf_ref, wdb_ref, sem):
    t = pl.program_id(1)
    _expert_weights_step((te_ref, nu_ref, run_ref, nxt_ref, nr_ref), (wd_hbm,), wbuf_ref, (wdb_ref,), sem,
                         wdb_ref.shape[1])

    def compute(rows):
        y = jnp.dot(hid_ref[0:rows], wdb_ref[...], preferred_element_type=F32)
        o_ref[0:rows] = y.reshape((rows,) + o_ref.shape[1:])

    def clear(row0):
        o_ref[row0:] = jnp.zeros((TM_E - row0,) + o_ref.shape[1:], o_ref.dtype)

    _by_valid_rows(valid_ref[t], compute, clear)


def _moe_down(plan, hid, w_down, n_tiles):
    f, d = w_down.shape[1], w_down.shape[2]
    tn = 4 * TN
    return pl.pallas_call(
        _moe_down_kernel,
        grid_spec=pltpu.PrefetchScalarGridSpec(
            num_scalar_prefetch=len(plan),
            grid=(d // tn, n_tiles),
            in_specs=[pl.BlockSpec((TM_E, f), lambda j, t, *_: (t, 0)),
                      pl.BlockSpec(memory_space=pl.ANY)],
            out_specs=pl.BlockSpec((TM_E, tn // LANES, LANES), lambda j, t, *_: (t, j, 0)),
            scratch_shapes=[pltpu.VMEM((1, f, tn), F32), pltpu.VMEM((f, tn), BF16),
                            pltpu.SemaphoreType.DMA(())]),
        out_shape=jax.ShapeDtypeStruct((n_tiles * TM_E, d // LANES, LANES), F32),
        compiler_params=_params(2, 40),
        name="moe_down",
    )(*plan, hid, w_down)


def _combine_kernel(pos_ref, h_ref, cw_ref, y_hbm, g_ref, ho_ref, xo_ref, buf_ref, sem):
    i = pl.program_id(0)
    tm, d = h_ref.shape

    def row_copy(slot, k, r, src_row):
        return pltpu.make_async_copy(y_hbm.at[pl.ds(src_row, 1)], buf_ref.at[slot, k, pl.ds(r, 1)], sem.at[slot])

    def issue_step(step, slot):
        def body(r, carry):
            for k in range(TOP_K):
                row_copy(slot, k, r, pos_ref[(step * tm + r) * TOP_K + k]).start(priority=k % 2)
            return carry
        lax.fori_loop(0, tm, body, 0, unroll=4)

    @pl.when(i == 0)
    def _():
        issue_step(0, 0)

    @pl.when(i + 1 < pl.num_programs(0))
    def _():
        issue_step(i + 1, (i + 1) % 2)

    slot = i % 2
    for k in range(TOP_K):
        pltpu.make_async_copy(y_hbm.at[pl.ds(0, tm)], buf_ref.at[slot, k], sem.at[slot]).wait()
    cw = cw_ref[...]
    y1 = buf_ref[slot, 0].reshape(tm, d)
    y2 = buf_ref[slot, 1].reshape(tm, d)
    h2 = h_ref[...] + (cw[:, 0:1] * y1 + cw[:, 1:2] * y2)
    ho_ref[...] = h2
    xo_ref[...] = _rms_rows(h2, g_ref[...]).astype(xo_ref.dtype)


def _combine(pos, h, cw, y3, g_ple):
    n, d = h.shape
    row_spec = pl.BlockSpec((TM_ROW, d), lambda i, p: (i, 0))
    return pl.pallas_call(
        _combine_kernel,
        grid_spec=pltpu.PrefetchScalarGridSpec(
            num_scalar_prefetch=1,
            grid=(n // TM_ROW,),
            in_specs=[row_spec,
                      pl.BlockSpec((TM_ROW, LANES), lambda i, p: (i, 0)),
                      pl.BlockSpec(memory_space=pl.ANY),
                      pl.BlockSpec((1, d), lambda i, p: (0, 0))],
            out_specs=[row_spec, row_spec],
            scratch_shapes=[pltpu.VMEM((2, TOP_K, TM_ROW) + y3.shape[1:], F32), pltpu.SemaphoreType.DMA((2,))]),
        out_shape=[jax.ShapeDtypeStruct((n, d), F32), jax.ShapeDtypeStruct((n, d), BF16)],
        compiler_params=_params(1, 56),
        name="moe_combine",
    )(pos, h, cw, y3, g_ple.reshape(1, d))


def _ple_kernel(xn_ref, wg_hbm, bg_ref, pa_ref, pb_ref, wp_hbm, h_ref, o_ref,
                wgf_ref, wpf_ref, wgb_ref, wpb_ref, sem, *, nb_first):
    _column_weights_step((wg_hbm, wp_hbm), (0, 0), (wgf_ref, wpf_ref), (wgb_ref, wpb_ref), sem)
    p = jnp.where(pl.program_id(1) < nb_first, pa_ref[...], pb_ref[...]).astype(BF16)
    tn = o_ref.shape[1]
    step = min(tn, TN)
    for c in range(0, tn, step):
        cols = slice(c, c + step)
        gate = jax.nn.sigmoid(jnp.dot(xn_ref[...], wgb_ref[:, cols], preferred_element_type=F32) + bg_ref[:, cols])
        ple = jnp.dot(p, wpb_ref[:, cols], preferred_element_type=F32)
        o_ref[:, cols] = h_ref[:, cols] + gate * ple


def _ple(xn, w_gate, b_gate, pa, pb, w_proj, h):
    n, d = h.shape
    pd = pa.shape[1]
    tm, tn = TM_WIDE, TN_WIDE
    nb_first = pa.shape[0] // tm
    p_first = pl.BlockSpec((tm, pd), lambda j, i: (jnp.minimum(i, nb_first - 1), 0))
    p_last = pl.BlockSpec((tm, pd), lambda j, i: (jnp.maximum(i - nb_first, 0), 0))
    any_spec = pl.BlockSpec(memory_space=pl.ANY)
    return pl.pallas_call(
        functools.partial(_ple_kernel, nb_first=nb_first),
        grid=(d // tn, n // tm),
        in_specs=[pl.BlockSpec((tm, d), lambda j, i: (i, 0)),
                  any_spec,
                  pl.BlockSpec((1, tn), lambda j, i: (0, j)),
                  p_first, p_last,
                  any_spec,
                  pl.BlockSpec((tm, tn), lambda j, i: (i, j))],
        out_specs=pl.BlockSpec((tm, tn), lambda j, i: (i, j)),
        out_shape=jax.ShapeDtypeStruct((n, d), F32),
        scratch_shapes=_weight_scratch([(d, tn), (pd, tn)]),
        compiler_params=_params(2, 58),
        name="ple",
    )(xn, w_gate, b_gate.reshape(1, d), pa, pb, w_proj, h)


def _layer(xn, h_p, h_s, p_p, p_s, cache_k, cache_v, state_pool, lw):
    n, d = xn.shape
    n_prompt = h_p.shape[0]
    n_seq, n_cache, n_kv, _ = cache_k.shape
    t_new = (n - n_prompt) // n_seq
    kv_w = n_kv * HEAD_DIM
    attn_w = lw['w_attn_up'].shape[0]
    pool_w = lw['w_pool_up'].shape[0]
    n_pool = state_pool.shape[1]

    qkv = _in_proj(xn, lw['w_in'], 0, attn_w + 2 * kv_w, "in_proj_qkv")
    z = _in_proj(xn, lw['w_in'], attn_w + 2 * kv_w, pool_w, "in_proj_pool")
    gates = _in_proj(xn, lw['w_in'], attn_w + 2 * kv_w + pool_w, 2 * d, "in_proj_gates")

    tab_s = tuple(jnp.tile(a, (SEQS_PER_STEP, 1))
                  for a in _rope_tables(PAST_LEN + jnp.arange(t_new, dtype=F32)))
    tab_p_t = _rope_tables_t(jnp.arange(n_prompt, dtype=F32))
    attn, k_rot_p = _attn_prompt(qkv, lw['attn_sinks'], tab_p_t, n_prompt, attn_w, kv_w)
    attn, new_k_s, new_v_s = _attn_sample(
        qkv, attn, n_prompt, cache_k.reshape(n_seq, n_cache, kv_w), cache_v.reshape(n_seq, n_cache, kv_w),
        lw['attn_sinks'], tab_s, n_seq, t_new, attn_w, kv_w)

    mixed = _pool_prompt(z, n_prompt, lw['w_pool_grp'], lw['pool_scale'])
    mixed, new_z_s = _pool_sample(state_pool, z, mixed, n_prompt, lw['w_pool_grp'], lw['pool_scale'], t_new)

    merged = _merge(attn, mixed, lw['w_attn_up'], lw['w_pool_up'], gates, lw['b_gate'].reshape(1, 2 * d))
    h1 = _out_proj(merged, lw['w_out'], h_p, h_s)

    n_tiles = (n * TOP_K) // TM_E + N_EXPERTS
    info, cw, xf3, counts = _router(h1, lw['g_ffn'], lw['w_route_group'], lw['b_route_group'],
                                    lw['w_route_expert'], lw['b_route_expert'])
    pos, plan, last_tile_row = _dispatch_plan(info, counts, n_tiles)
    xs3 = _dispatch(pos, last_tile_row, plan[1], xf3, n_tiles)
    hid = _moe_up(plan, xs3, lw['w_exp_gate'], lw['w_exp_up'], n_tiles)
    y3 = _moe_down(plan, hid, lw['w_exp_down'], n_tiles)
    h2, xn2 = _combine(pos, h1, cw, y3, lw['g_ple'])

    h3 = _ple(xn2, lw['w_ple_gate'], lw['b_ple_gate'], p_p, p_s, lw['w_ple_proj'], h2)

    new_k_p = k_rot_p[n_prompt - n_cache:].reshape(1, n_cache, n_kv, HEAD_DIM)
    new_v_p = qkv[n_prompt - n_cache:n_prompt, attn_w + kv_w:].reshape(1, n_cache, n_kv, HEAD_DIM)
    new_z_p = z[n_prompt - n_pool:n_prompt].reshape(1, n_pool, pool_w)
    return (h3, new_k_p, new_v_p, new_z_p,
            new_k_s.reshape(cache_k.shape), new_v_s.reshape(cache_v.shape), new_z_s)


def kernel(x_prompt, x_sample, cache_k, cache_v, state_pool, p_prompt, p_sample, g_mix, w_in, b_gate, attn_sinks, w_pool_grp, pool_scale, w_attn_up, w_pool_up, w_out, g_ffn, w_route_group, b_route_group, w_route_expert, b_route_expert, w_exp_gate, w_exp_up, w_exp_down, g_ple, w_ple_gate, b_ple_gate, w_ple_proj, g_final):
    batch, seq, d = x_prompt.shape
    n_seq, t_new, _ = x_sample.shape
    depth = w_in.shape[0]
    assert batch == 1, "prompt rows are treated as one sequence"
    n_prompt = batch * seq
    n_sample = n_seq * t_new
    n = n_prompt + n_sample
    assert n_prompt % TM == 0 and n_sample % TM == 0 and (n * TOP_K) % TM_E == 0 and n % TM_DISPATCH == 0
    assert n_seq % SEQS_PER_STEP == 0 and n_seq % POOL_SEQS == 0 and t_new & (t_new - 1) == 0
    assert cache_k.shape[2] == WINDOW and state_pool.shape[2] == POOL_HALO - 1
    assert depth == 1, "one layer per step"

    weights = dict(g_mix=g_mix, w_in=w_in, b_gate=b_gate, attn_sinks=attn_sinks, w_pool_grp=w_pool_grp,
                   pool_scale=pool_scale, w_attn_up=w_attn_up, w_pool_up=w_pool_up, w_out=w_out, g_ffn=g_ffn,
                   w_route_group=w_route_group, b_route_group=b_route_group, w_route_expert=w_route_expert,
                   b_route_expert=b_route_expert, w_exp_gate=w_exp_gate, w_exp_up=w_exp_up,
                   w_exp_down=w_exp_down, g_ple=g_ple, w_ple_gate=w_ple_gate, b_ple_gate=b_ple_gate,
                   w_ple_proj=w_ple_proj)
    lw = {name: w[0] for name, w in weights.items()}
    h_p = x_prompt.reshape(n_prompt, d)
    h_s = x_sample.reshape(n_sample, d)
    xn = _rmsnorm_in2(h_p, h_s, lw['g_mix'], BF16, "norm_mix")
    out = _layer(xn, h_p, h_s, p_prompt[0].reshape(n_prompt, -1), p_sample[0].reshape(n_sample, -1),
                 cache_k[0], cache_v[0], state_pool[0], lw)
    y_p, y_s = _rmsnorm_out2(out[0], g_final, n_prompt, "norm_final")
    return (y_p.reshape(batch, seq, d), y_s.reshape(n_seq, t_new, d)) + tuple(piece[None] for piece in out[1:])
```

```python
import functools

import jax
import jax.numpy as jnp
from jax import lax
from jax.experimental import pallas as pl
from jax.experimental.pallas import tpu as pltpu

F32 = jnp.float32
BF16 = jnp.bfloat16
I32 = jnp.int32

HEAD_DIM = 64
Q_PER_KV = 4
WINDOW = 128
PAST_LEN = 8192
ROT_DIM = HEAD_DIM // 4
ROPE_THETA = 500000.0
POOL_WINDOWS = (2, 4, 8, 16)
POOL_HALO = 16
N_GROUPS = 4
EXPERTS_PER_GROUP = 4
N_EXPERTS = N_GROUPS * EXPERTS_PER_GROUP
TOP_K = 2
RMS_EPS = 1e-6
NEG_INF = -1e30
LOG2_E = 1.4426950408889634
LANES = 128
SUBLANES = 8
MIB = 1024 * 1024

TM = 1024
TN = 512
TN_PROJ = 1024
CAST_ROWS = 512
TM_WIDE = 512
TN_WIDE = 1024
TM_ROW = 256
TM_E = 512
TM_DISPATCH = 1024
TM_NORM = 512
SEQS_PER_STEP = 8
POOL_SEQS = 16


def _params(n_axes, vmem_mib):
    return pltpu.CompilerParams(dimension_semantics=("arbitrary",) * n_axes,
                                vmem_limit_bytes=vmem_mib * MIB)


def _rms_rows(x, g):
    ms = jnp.mean(x * x, axis=-1, keepdims=True)
    return x * lax.rsqrt(ms + RMS_EPS) * g


def _first_last_specs(nb_first, block):
    zeros = (0,) * (len(block) - 1)
    first = pl.BlockSpec(block, lambda i, *_: (jnp.minimum(i, nb_first - 1),) + zeros)
    last = pl.BlockSpec(block, lambda i, *_: (jnp.maximum(i - nb_first, 0),) + zeros)
    return first, last


def _norm_in2_kernel(xa_ref, xb_ref, g_ref, o_ref, *, nb_first):
    i = pl.program_id(0)

    @pl.when(i < nb_first)
    def _():
        o_ref[...] = _rms_rows(xa_ref[...], g_ref[...]).astype(o_ref.dtype)

    @pl.when(i >= nb_first)
    def _():
        o_ref[...] = _rms_rows(xb_ref[...], g_ref[...]).astype(o_ref.dtype)


def _rmsnorm_in2(xa, xb, g, out_dtype, name):
    (na, d), nb = xa.shape, xb.shape[0]
    nb_first = na // TM_NORM
    spec_a, spec_b = _first_last_specs(nb_first, (TM_NORM, d))
    return pl.pallas_call(
        functools.partial(_norm_in2_kernel, nb_first=nb_first),
        grid=((na + nb) // TM_NORM,),
        in_specs=[spec_a, spec_b, pl.BlockSpec((1, d), lambda i: (0, 0))],
        out_specs=pl.BlockSpec((TM_NORM, d), lambda i: (i, 0)),
        out_shape=jax.ShapeDtypeStruct((na + nb, d), out_dtype),
        compiler_params=_params(1, 56),
        name=name,
    )(xa, xb, g.reshape(1, d))


def _norm_out2_kernel(x_ref, g_ref, oa_ref, ob_ref, *, nb_first):
    i = pl.program_id(0)

    @pl.when(i < nb_first)
    def _():
        oa_ref[...] = _rms_rows(x_ref[...], g_ref[...])

    @pl.when(i >= nb_first)
    def _():
        ob_ref[...] = _rms_rows(x_ref[...], g_ref[...])


def _rmsnorm_out2(x, g, na, name):
    n, d = x.shape
    nb_first = na // TM_NORM
    spec_a, spec_b = _first_last_specs(nb_first, (TM_NORM, d))
    return pl.pallas_call(
        functools.partial(_norm_out2_kernel, nb_first=nb_first),
        grid=(n // TM_NORM,),
        in_specs=[pl.BlockSpec((TM_NORM, d), lambda i: (i, 0)), pl.BlockSpec((1, d), lambda i: (0, 0))],
        out_specs=[spec_a, spec_b],
        out_shape=[jax.ShapeDtypeStruct((na, d), F32), jax.ShapeDtypeStruct((n - na, d), F32)],
        compiler_params=_params(1, 58),
        name=name,
    )(x, g.reshape(1, d))


def _cast_rows(src_ref, dst_ref):
    rows = src_ref.shape[0]
    step = min(rows, CAST_ROWS)
    for r in range(0, rows, step):
        dst_ref[r:r + step] = src_ref[r:r + step].astype(dst_ref.dtype)


def _column_weights_step(w_hbms, col0s, wbuf_refs, wb_refs, sem):
    j, i = pl.program_id(0), pl.program_id(1)

    def copies(jj):
        out = []
        for w, col0, buf in zip(w_hbms, col0s, wbuf_refs):
            tn = buf.shape[1]
            out.append(pltpu.make_async_copy(w.at[:, pl.ds(pl.multiple_of(col0 + jj * tn, tn), tn)], buf, sem))
        return out

    @pl.when(i == 0)
    def _():
        @pl.when(j == 0)
        def _():
            for c in copies(j):
                c.start()

        for c in copies(j):
            c.wait()
        for buf, wb_ref in zip(wbuf_refs, wb_refs):
            _cast_rows(buf, wb_ref)

        @pl.when(j + 1 < pl.num_programs(0))
        def _():
            for c in copies(j + 1):
                c.start()


def _weight_scratch(shapes):
    return ([pltpu.VMEM(s, F32) for s in shapes] + [pltpu.VMEM(s, BF16) for s in shapes]
            + [pltpu.SemaphoreType.DMA(())])


def _proj_kernel(a_ref, w_hbm, o_ref, wbuf_ref, wb_ref, sem, *, col0):
    _column_weights_step((w_hbm,), (col0,), (wbuf_ref,), (wb_ref,), sem)
    o_ref[...] = jnp.dot(a_ref[...], wb_ref[...], preferred_element_type=F32)


def _in_proj(xn, w, col0, width, name):
    n, d = xn.shape
    tn = TN_PROJ
    assert col0 % tn == 0 and width % tn == 0
    return pl.pallas_call(
        functools.partial(_proj_kernel, col0=col0),
        grid=(width // tn, n // TM),
        in_specs=[pl.BlockSpec((TM, d), lambda j, i: (i, 0)),
                  pl.BlockSpec(memory_space=pl.ANY)],
        out_specs=pl.BlockSpec((TM, tn), lambda j, i: (i, j)),
        out_shape=jax.ShapeDtypeStruct((n, width), F32),
        scratch_shapes=_weight_scratch([(d, tn)]),
        compiler_params=_params(2, 56),
        name=name,
    )(xn, w)


def _rope_tables(pos):
    half = ROT_DIM // 2
    inv_freq = ROPE_THETA ** (-jnp.arange(half, dtype=F32) * (2.0 / ROT_DIM))
    ang = pos[:, None] * inv_freq[None, :]
    cos, sin = jnp.cos(ang), jnp.sin(ang)
    t = pos.shape[0]
    pad = jnp.zeros((t, HEAD_DIM - ROT_DIM), F32)
    zeros = jnp.zeros((t, half), F32)
    cos_h = jnp.concatenate([cos, cos, pad + 1.0], axis=1)
    sa_h = jnp.concatenate([zeros, sin, pad], axis=1)
    sb_h = jnp.concatenate([-sin, zeros, pad], axis=1)
    reps = LANES // HEAD_DIM
    return tuple(jnp.tile(a, (1, reps)) for a in (cos_h, sa_h, sb_h))


def _rope_tables_t(pos):
    half = ROT_DIM // 2
    inv_freq = ROPE_THETA ** (-jnp.arange(half, dtype=F32) * (2.0 / ROT_DIM))
    ang = pos[:, None] * inv_freq[None, :]
    return jnp.cos(ang).T, jnp.sin(ang).T


def _rope(x, cos_t, sa_t, sb_t):
    pieces = []
    for c in range(x.shape[1] // LANES):
        xc = x[:, c * LANES:(c + 1) * LANES]
        pieces.append(xc * cos_t
                      + pltpu.roll(xc, ROT_DIM // 2, 1) * sa_t
                      + pltpu.roll(xc, LANES - ROT_DIM // 2, 1) * sb_t)
    return jnp.concatenate(pieces, axis=1)


def _softmax2_with_sink(s, sink):
    m = jnp.maximum(jnp.max(s, axis=-1, keepdims=True), sink)
    p = jnp.exp2(s - m)
    den = jnp.sum(p, axis=-1, keepdims=True) + jnp.exp2(sink - m)
    return p, 1.0 / den


def _attn_prompt_kernel(sink_ref, q_ref, k_ref, v_ref, cost_ref, sint_ref,
                        o_ref, ko_ref, kprev_ref, vprevt_ref, *, n_kv, nb):
    b = pl.program_id(0)

    @pl.when(b == 0)
    def _():
        kprev_ref[...] = jnp.zeros_like(kprev_ref)
        vprevt_ref[...] = jnp.zeros_like(vprevt_ref)

    @pl.when(b >= nb)
    def _():
        o_ref[...] = jnp.zeros_like(o_ref)

    @pl.when(b < nb)
    def _():
        _attn_prompt_block(sink_ref, q_ref, k_ref, v_ref, cost_ref, sint_ref,
                           o_ref, ko_ref, kprev_ref, vprevt_ref, n_kv=n_kv)


def _attn_prompt_block(sink_ref, q_ref, k_ref, v_ref, cost_ref, sint_ref,
                       o_ref, ko_ref, kprev_ref, vprevt_ref, *, n_kv):
    b = pl.program_id(0)
    half = ROT_DIM // 2
    cos_t, sin_t = cost_ref[...], sint_ref[...]

    def rope_t(x):
        x1, x2 = x[0:half], x[half:ROT_DIM]
        return jnp.concatenate([x1 * cos_t - x2 * sin_t, x2 * cos_t + x1 * sin_t, x[ROT_DIM:]], axis=0)

    kt = k_ref[...].T
    k = jnp.concatenate([rope_t(kt[h * HEAD_DIM:(h + 1) * HEAD_DIM]) for h in range(n_kv)], axis=0).T
    ko_ref[...] = k
    kcat = jnp.concatenate([kprev_ref[...], k], axis=0).astype(BF16)
    vt = v_ref[...].T
    vcat_t = jnp.concatenate([vprevt_ref[...], vt], axis=1).astype(BF16)
    kprev_ref[...] = k
    vprevt_ref[...] = vt

    qt = q_ref[...].T

    def roped_head_t(h):
        return (rope_t(qt[h * HEAD_DIM:(h + 1) * HEAD_DIM]) * (HEAD_DIM ** -0.5 * LOG2_E)).astype(BF16)

    n_heads = n_kv * Q_PER_KV
    zeros_head = jnp.zeros((HEAD_DIM, WINDOW), BF16)
    heads_per_tile = LANES // HEAD_DIM

    key = lax.broadcasted_iota(I32, (2 * WINDOW, Q_PER_KV * WINDOW), 0)
    qry = lax.broadcasted_iota(I32, (2 * WINDOW, Q_PER_KV * WINDOW), 1) & (WINDOW - 1)
    first_key = jnp.where(b > 0, 0, WINDOW)
    mask = (key > qry) & (key <= qry + WINDOW) & (key >= first_key)

    scores = []
    for g in range(n_kv):
        tile, slot = g // heads_per_tile, g % heads_per_tile
        k2 = kcat[:, tile * LANES:(tile + 1) * LANES]
        cols = []
        for r in range(Q_PER_KV):
            parts = [zeros_head] * heads_per_tile
            parts[slot] = roped_head_t(g * Q_PER_KV + r)
            cols.append(jnp.concatenate(parts, axis=0))
        rhs = jnp.concatenate(cols, axis=1)
        scores.append(jnp.where(mask, jnp.dot(k2, rhs, preferred_element_type=F32), NEG_INF))
    st = jnp.concatenate(scores, axis=1)
    sink = jnp.concatenate([jnp.full((1, WINDOW), sink_ref[h] * LOG2_E, F32) for h in range(n_heads)], axis=1)
    m = jnp.maximum(jnp.max(st, axis=0, keepdims=True), sink)
    p = jnp.exp2(st - m)
    den = jnp.sum(p, axis=0, keepdims=True) + jnp.exp2(sink - m)
    inv_den = 1.0 / den
    probs_t = p.astype(BF16)

    for g in range(n_kv):
        cols = slice(g * Q_PER_KV * WINDOW, (g + 1) * Q_PER_KV * WINDOW)
        ot = jnp.dot(vcat_t[g * HEAD_DIM:(g + 1) * HEAD_DIM], probs_t[:, cols],
                     preferred_element_type=F32) * inv_den[:, cols]
        for pr in range(Q_PER_KV // heads_per_tile):
            pair = jnp.concatenate(
                [ot[:, (pr * heads_per_tile + u) * WINDOW:(pr * heads_per_tile + u + 1) * WINDOW]
                 for u in range(heads_per_tile)], axis=0)
            c = (g * Q_PER_KV) // heads_per_tile + pr
            o_ref[:, c * LANES:(c + 1) * LANES] = pair.T.astype(o_ref.dtype)


def _attn_prompt(qkv, sinks, tables_t, n_rows, attn_w, kv_w):
    nb = n_rows // WINDOW
    kb, vb = attn_w // kv_w, attn_w // kv_w + 1

    def blk(b, s):
        return jnp.minimum(b, nb - 1)

    tab_t_spec = pl.BlockSpec((ROT_DIM // 2, WINDOW), lambda b, s: (0, blk(b, s)))
    return pl.pallas_call(
        functools.partial(_attn_prompt_kernel, n_kv=kv_w // HEAD_DIM, nb=nb),
        grid_spec=pltpu.PrefetchScalarGridSpec(
            num_scalar_prefetch=1,
            grid=(qkv.shape[0] // WINDOW,),
            in_specs=[pl.BlockSpec((WINDOW, attn_w), lambda b, s: (blk(b, s), 0)),
                      pl.BlockSpec((WINDOW, kv_w), lambda b, s: (blk(b, s), kb)),
                      pl.BlockSpec((WINDOW, kv_w), lambda b, s: (blk(b, s), vb)),
                      tab_t_spec, tab_t_spec],
            out_specs=[pl.BlockSpec((WINDOW, attn_w), lambda b, s: (b, 0)),
                       pl.BlockSpec((WINDOW, kv_w), lambda b, s: (blk(b, s), 0))],
            scratch_shapes=[pltpu.VMEM((WINDOW, kv_w), F32), pltpu.VMEM((kv_w, WINDOW), F32)]),
        out_shape=[jax.ShapeDtypeStruct((qkv.shape[0], attn_w), BF16),
                   jax.ShapeDtypeStruct((n_rows, kv_w), F32)],
        compiler_params=_params(1, 32),
        name="attn_prompt",
    )(sinks, qkv, qkv, qkv, *tables_t)


def _attn_sample_kernel(sink_ref, q_ref, k_ref, v_ref, ck_ref, cv_ref, cos_ref, sa_ref, sb_ref, attn_in_ref,
                        o_ref, nk_ref, nv_ref, *, n_kv, t_new):
    del attn_in_ref
    assert LANES // HEAD_DIM == 2 and Q_PER_KV % 2 == 0
    cos_t, sa_t, sb_t = cos_ref[...], sa_ref[...], sb_ref[...]
    q = _rope(q_ref[...], cos_t, sa_t, sb_t) * (HEAD_DIM ** -0.5 * LOG2_E)
    k = _rope(k_ref[...], cos_t, sa_t, sb_t)
    v = v_ref[...]
    n_cache = ck_ref.shape[1]
    n_keys = 2 * WINDOW
    n_seq = q.shape[0] // t_new
    n_heads = n_kv * Q_PER_KV
    zpad = jnp.zeros((n_keys - n_cache - t_new, k.shape[1]), F32)
    low = lax.broadcasted_iota(I32, (t_new, LANES), 1) < HEAD_DIM
    high = jnp.logical_not(low)

    qt = lax.broadcasted_iota(I32, (Q_PER_KV * t_new, n_keys), 0) & (t_new - 1)
    kj = lax.broadcasted_iota(I32, (Q_PER_KV * t_new, n_keys), 1)
    mask = (kj > qt + (n_cache - WINDOW)) & (kj <= qt + n_cache)

    def swap_halves(x):
        return pltpu.roll(x, HEAD_DIM, 1)

    scores, values = [], []
    for s_i in range(n_seq):
        ck, cv = ck_ref[s_i], cv_ref[s_i]
        kn, vn = k[s_i * t_new:(s_i + 1) * t_new], v[s_i * t_new:(s_i + 1) * t_new]
        nk_ref[s_i] = jnp.concatenate([ck[t_new:], kn], axis=0)
        nv_ref[s_i] = jnp.concatenate([cv[t_new:], vn], axis=0)
        kall = jnp.concatenate([ck, kn, zpad], axis=0).astype(BF16)
        values.append(jnp.concatenate([cv, vn, zpad], axis=0).astype(BF16))
        qs = q[s_i * t_new:(s_i + 1) * t_new]
        for g in range(n_kv):
            tile, odd = g // 2, g % 2
            parts = []
            for r in range(Q_PER_KV):
                h = g * Q_PER_KV + r
                x = qs[:, (h // 2) * LANES:(h // 2 + 1) * LANES]
                if h % 2 != odd:
                    x = swap_halves(x)
                parts.append(jnp.where(high if odd else low, x, 0.0))
            qg = jnp.concatenate(parts, axis=0).astype(BF16)
            k2 = kall[:, tile * LANES:(tile + 1) * LANES]
            sg = lax.dot_general(qg, k2, (((1,), (1,)), ((), ())), preferred_element_type=F32)
            scores.append(jnp.where(mask, sg, NEG_INF))
    s = jnp.concatenate(scores, axis=0)
    sink_seq = jnp.concatenate([jnp.full((t_new, 1), sink_ref[h] * LOG2_E, F32) for h in range(n_heads)], axis=0)
    sink = jnp.concatenate([sink_seq] * n_seq, axis=0)
    p, inv_den = _softmax2_with_sink(s, sink)
    probs = p.astype(BF16)

    seq_outs = []
    for s_i in range(n_seq):
        out_tiles = []
        for g in range(n_kv):
            tile, odd = g // 2, g % 2
            r0 = (s_i * n_kv + g) * Q_PER_KV * t_new
            v2 = values[s_i][:, tile * LANES:(tile + 1) * LANES]
            rows = slice(r0, r0 + Q_PER_KV * t_new)
            og = jnp.dot(probs[rows], v2, preferred_element_type=F32) * inv_den[rows]
            for pr in range(Q_PER_KV // 2):
                even = og[(2 * pr) * t_new:(2 * pr + 1) * t_new]
                oddh = og[(2 * pr + 1) * t_new:(2 * pr + 2) * t_new]
                if odd:
                    even = swap_halves(even)
                else:
                    oddh = swap_halves(oddh)
                out_tiles.append(jnp.where(low, even, oddh))
        seq_outs.append(jnp.concatenate(out_tiles, axis=1))
    o_ref[...] = jnp.concatenate(seq_outs, axis=0).astype(o_ref.dtype)


def _attn_sample(qkv, attn, row0, cache_k, cache_v, sinks, tables, n_seq, t_new, attn_w, kv_w):
    rows = SEQS_PER_STEP * t_new
    rb0 = row0 // rows
    kb, vb = attn_w // kv_w, attn_w // kv_w + 1
    n_cache = cache_k.shape[1]
    tab_spec = pl.BlockSpec((rows, LANES), lambda i, s: (0, 0))
    cache_spec = pl.BlockSpec((SEQS_PER_STEP, n_cache, kv_w), lambda i, s: (i, 0, 0))
    return pl.pallas_call(
        functools.partial(_attn_sample_kernel, n_kv=kv_w // HEAD_DIM, t_new=t_new),
        grid_spec=pltpu.PrefetchScalarGridSpec(
            num_scalar_prefetch=1,
            grid=(n_seq // SEQS_PER_STEP,),
            in_specs=[pl.BlockSpec((rows, attn_w), lambda i, s: (rb0 + i, 0)),
                      pl.BlockSpec((rows, kv_w), lambda i, s: (rb0 + i, kb)),
                      pl.BlockSpec((rows, kv_w), lambda i, s: (rb0 + i, vb)),
                      cache_spec, cache_spec, tab_spec, tab_spec, tab_spec,
                      pl.BlockSpec(memory_space=pl.ANY)],
            out_specs=[pl.BlockSpec((rows, attn_w), lambda i, s: (rb0 + i, 0)), cache_spec, cache_spec]),
        out_shape=[jax.ShapeDtypeStruct(attn.shape, attn.dtype),
                   jax.ShapeDtypeStruct(cache_k.shape, F32),
                   jax.ShapeDtypeStruct(cache_v.shape, F32)],
        input_output_aliases={9: 0},
        compiler_params=_params(1, 32),
        name="attn_sample",
    )(sinks, qkv, qkv, qkv, cache_k, cache_v, *tables, attn)


def _pool_prompt_kernel(z_ref, halo_ref, w_ref, sc_ref, o_ref, ext_ref, lvla_ref, lvlb_ref, *, tm, pg, nb):
    i = pl.program_id(0)

    @pl.when(i >= nb)
    def _():
        o_ref[...] = jnp.zeros_like(o_ref)

    @pl.when(i < nb)
    def _():
        _pool_prompt_block(z_ref, halo_ref, w_ref, sc_ref, o_ref, ext_ref, lvla_ref, lvlb_ref, tm=tm, pg=pg)


def _pool_prompt_block(z_ref, halo_ref, w_ref, sc_ref, o_ref, ext_ref, lvla_ref, lvlb_ref, *, tm, pg):
    i = pl.program_id(0)
    data0 = 2 * POOL_HALO
    total = data0 + tm
    ext_ref[0:POOL_HALO, :] = jnp.zeros((POOL_HALO, ext_ref.shape[1]), F32)
    ext_ref[POOL_HALO:data0, :] = jnp.where(i > 0, halo_ref[...], 0.0)
    ext_ref[data0:, :] = z_ref[...]
    pos = i * tm + lax.broadcasted_iota(I32, (tm, 1), 0)
    for g, w in enumerate(POOL_WINDOWS):
        c0, c1 = g * pg, (g + 1) * pg
        cur = ext_ref[data0:, c0:c1]
        prev_ref, pc0, start, m = ext_ref, c0, 0, 1
        levels = (lvla_ref, lvlb_ref)
        for k in range(w.bit_length() - 1):
            start += SUBLANES
            s = prev_ref[start:total, pc0:pc0 + pg] + prev_ref[start - m:total - m, pc0:pc0 + pg]
            m *= 2
            if m < w:
                levels[k % 2][start:total, :] = s
                prev_ref, pc0 = levels[k % 2], 0
        acc = s[data0 - start:]
        cnt = jnp.minimum(pos + 1, w).astype(F32)
        pooled = (acc / cnt - cur).astype(BF16)
        mixed = jnp.dot(pooled, w_ref[g].astype(BF16), preferred_element_type=F32) * sc_ref[:, c0:c1]
        o_ref[:, c0:c1] = mixed.astype(o_ref.dtype)


def _pool_prompt(z, n_rows, w_grp, scale):
    n, pw = z.shape
    tm = TM_ROW
    hb = tm // POOL_HALO
    return pl.pallas_call(
        functools.partial(_pool_prompt_kernel, tm=tm, pg=pw // len(POOL_WINDOWS), nb=n_rows // tm),
        grid=(n // tm,),
        in_specs=[pl.BlockSpec((tm, pw), lambda i: (i, 0)),
                  pl.BlockSpec((POOL_HALO, pw), lambda i: (jnp.maximum(i * hb - 1, 0), 0)),
                  pl.BlockSpec(w_grp.shape, lambda i: (0, 0, 0)),
                  pl.BlockSpec((1, pw), lambda i: (0, 0))],
        out_specs=pl.BlockSpec((tm, pw), lambda i: (i, 0)),
        out_shape=jax.ShapeDtypeStruct((n, pw), BF16),
        scratch_shapes=[pltpu.VMEM((tm + 2 * POOL_HALO, pw), F32),
                        pltpu.VMEM((tm + 2 * POOL_HALO, pw // len(POOL_WINDOWS)), F32),
                        pltpu.VMEM((tm + 2 * POOL_HALO, pw // len(POOL_WINDOWS)), F32)],
        compiler_params=_params(1, 40),
        name="pool_prompt",
    )(z, z, w_grp, scale.reshape(1, pw))


def _pool_sample_kernel(st_ref, z_ref, w_ref, sc_ref, mixed_in_ref, o_ref, ns_ref, ext_ref, *, pg, t_new):
    del mixed_in_ref
    n_st = st_ref.shape[1]
    seqs = st_ref.shape[0]
    hist = ext_ref.shape[1] - t_new
    ext_ref[:, hist - n_st:hist, :] = st_ref[...]
    ext_ref[:, hist:, :] = z_ref[...]
    ns_ref[...] = ext_ref[:, hist + t_new - n_st:, :]
    for g, w in enumerate(POOL_WINDOWS):
        c0, c1 = g * pg, (g + 1) * pg
        cur = ext_ref[:, hist:, c0:c1]
        acc = cur
        for d in range(1, w):
            acc = acc + ext_ref[:, hist - d:hist - d + t_new, c0:c1]
        pooled = (acc / float(w) - cur).reshape(seqs * t_new, pg).astype(BF16)
        mixed = jnp.dot(pooled, w_ref[g].astype(BF16), preferred_element_type=F32) * sc_ref[:, c0:c1]
        o_ref[:, c0:c1] = mixed.astype(o_ref.dtype)


def _pool_sample(state, z, mixed, row0, w_grp, scale, t_new):
    n_seq, n_st, pw = state.shape
    assert t_new == SUBLANES and n_st <= POOL_HALO - 1
    z3 = z.reshape(z.shape[0] // t_new, t_new, pw)
    sb0 = row0 // t_new // POOL_SEQS
    rows = POOL_SEQS * t_new
    st_spec = pl.BlockSpec((POOL_SEQS, n_st, pw), lambda i: (i, 0, 0))
    return pl.pallas_call(
        functools.partial(_pool_sample_kernel, pg=pw // len(POOL_WINDOWS), t_new=t_new),
        grid=(n_seq // POOL_SEQS,),
        in_specs=[st_spec,
                  pl.BlockSpec((POOL_SEQS, t_new, pw), lambda i: (sb0 + i, 0, 0)),
                  pl.BlockSpec(w_grp.shape, lambda i: (0, 0, 0)),
                  pl.BlockSpec((1, pw), lambda i: (0, 0)),
                  pl.BlockSpec(memory_space=pl.ANY)],
        out_specs=[pl.BlockSpec((rows, pw), lambda i: (row0 // rows + i, 0)), st_spec],
        out_shape=[jax.ShapeDtypeStruct(mixed.shape, mixed.dtype), jax.ShapeDtypeStruct(state.shape, F32)],
        scratch_shapes=[pltpu.VMEM((POOL_SEQS, POOL_HALO + t_new, pw), F32)],
        input_output_aliases={4: 0},
        compiler_params=_params(1, 40),
        name="pool_sample",
    )(state, z3, w_grp, scale.reshape(1, pw), mixed)


def _merge_kernel(a1_ref, a2_ref, w1_hbm, w2_hbm, ga_ref, gb_ref, ba_ref, bb_ref, o_ref,
                  w1f_ref, w2f_ref, w1b_ref, w2b_ref, sem):
    _column_weights_step((w1_hbm, w2_hbm), (0, 0), (w1f_ref, w2f_ref), (w1b_ref, w2b_ref), sem)
    a_up = jnp.dot(a1_ref[...], w1b_ref[...], preferred_element_type=F32)
    b_up = jnp.dot(a2_ref[...], w2b_ref[...], preferred_element_type=F32)
    merged = (jax.nn.sigmoid(ga_ref[...] + ba_ref[...]) * a_up
              + jax.nn.sigmoid(gb_ref[...] + bb_ref[...]) * b_up)
    o_ref[...] = merged.astype(o_ref.dtype)


def _merge(attn, mixed, w_attn_up, w_pool_up, gates, b_gate):
    n, ka = attn.shape
    kp = mixed.shape[1]
    d = w_attn_up.shape[1]
    tm, tn = TM_WIDE, TN_WIDE
    nj = d // tn
    any_spec = pl.BlockSpec(memory_space=pl.ANY)
    return pl.pallas_call(
        _merge_kernel,
        grid=(nj, n // tm),
        in_specs=[pl.BlockSpec((tm, ka), lambda j, i: (i, 0)),
                  pl.BlockSpec((tm, kp), lambda j, i: (i, 0)),
                  any_spec, any_spec,
                  pl.BlockSpec((tm, tn), lambda j, i: (i, j)),
                  pl.BlockSpec((tm, tn), lambda j, i: (i, nj + j)),
                  pl.BlockSpec((1, tn), lambda j, i: (0, j)),
                  pl.BlockSpec((1, tn), lambda j, i: (0, nj + j))],
        out_specs=pl.BlockSpec((tm, tn), lambda j, i: (i, j)),
        out_shape=jax.ShapeDtypeStruct((n, d), BF16),
        scratch_shapes=_weight_scratch([(ka, tn), (kp, tn)]),
        compiler_params=_params(2, 56),
        name="merge",
    )(attn, mixed, w_attn_up, w_pool_up, gates, gates, b_gate, b_gate)


def _out_proj_kernel(a_ref, w_hbm, ha_ref, hb_ref, o_ref, wf_ref, wb_ref, sem, *, nb_first):
    _column_weights_step((w_hbm,), (0,), (wf_ref,), (wb_ref,), sem)
    h = jnp.where(pl.program_id(1) < nb_first, ha_ref[...], hb_ref[...])
    o_ref[...] = h + jnp.dot(a_ref[...], wb_ref[...], preferred_element_type=F32)


def _out_proj(merged, w_out, ha, hb):
    n, d = merged.shape
    tm, tn = TM_WIDE, TN_WIDE
    nb_first = ha.shape[0] // tm
    h_first = pl.BlockSpec((tm, tn), lambda j, i: (jnp.minimum(i, nb_first - 1), j))
    h_last = pl.BlockSpec((tm, tn), lambda j, i: (jnp.maximum(i - nb_first, 0), j))
    return pl.pallas_call(
        functools.partial(_out_proj_kernel, nb_first=nb_first),
        grid=(d // tn, n // tm),
        in_specs=[pl.BlockSpec((tm, d), lambda j, i: (i, 0)),
                  pl.BlockSpec(memory_space=pl.ANY),
                  h_first, h_last],
        out_specs=pl.BlockSpec((tm, tn), lambda j, i: (i, j)),
        out_shape=jax.ShapeDtypeStruct((n, d), F32),
        scratch_shapes=_weight_scratch([(d, tn)]),
        compiler_params=_params(2, 56),
        name="out_proj",
    )(merged, w_out, ha, hb)


def _router_kernel(h_ref, g_ref, wr_ref, br_ref, info_ref, cw_ref, xf_ref, cnt_ref, base_ref):
    tm = h_ref.shape[0]

    @pl.when(pl.program_id(0) == 0)
    def _():
        base_ref[...] = jnp.zeros_like(base_ref)

    xf = _rms_rows(h_ref[...], g_ref[...]).astype(BF16)
    xf_ref[...] = xf.reshape(xf_ref.shape)
    logits = jnp.dot(xf, wr_ref[...].astype(BF16), preferred_element_type=F32) + br_ref[...]
    lane = lax.broadcasted_iota(I32, logits.shape, 1)
    big = jnp.int32(1 << 20)
    ninf = jnp.float32(-jnp.inf)

    is_g = lane < N_GROUPS
    gl = jnp.where(is_g, logits, ninf)
    gmax = jnp.max(gl, axis=-1, keepdims=True)
    gden = jnp.sum(jnp.exp(gl - gmax), axis=-1, keepdims=True)
    g_p = 1.0 / gden
    g_idx = jnp.min(jnp.where(gl == gmax, lane, big), axis=-1, keepdims=True)

    e_lane = lane - N_GROUPS
    in_grp = (e_lane >= g_idx * EXPERTS_PER_GROUP) & (e_lane < (g_idx + 1) * EXPERTS_PER_GROUP)
    el = jnp.where(in_grp, logits, ninf)
    m1 = jnp.max(el, axis=-1, keepdims=True)
    i1 = jnp.min(jnp.where(in_grp & (el == m1), e_lane, big), axis=-1, keepdims=True)
    el2 = jnp.where(e_lane == i1, ninf, el)
    m2 = jnp.max(el2, axis=-1, keepdims=True)
    i2 = jnp.min(jnp.where(in_grp & (el2 == m2), e_lane, big), axis=-1, keepdims=True)
    t = jnp.exp(m2 - m1)
    den = 1.0 + t
    c1 = g_p * (1.0 / den)
    c2 = g_p * (t / den)
    cw_ref[...] = jnp.where(lane == 0, c1, jnp.where(lane == 1, c2, 0.0))

    hit1, hit2 = lane == i1, lane == i2
    onehot = (hit1 | hit2).astype(BF16)
    earlier = (lax.broadcasted_iota(I32, (tm, tm), 1) < lax.broadcasted_iota(I32, (tm, tm), 0)).astype(BF16)
    before = jnp.dot(earlier, onehot, preferred_element_type=F32) + base_ref[...]
    r1 = jnp.sum(jnp.where(hit1, before, 0.0), axis=-1, keepdims=True).astype(I32)
    r2 = jnp.sum(jnp.where(hit2, before, 0.0), axis=-1, keepdims=True).astype(I32)
    base_ref[...] += jnp.sum(onehot.astype(F32), axis=0, keepdims=True)
    cnt_ref[...] = base_ref[...].astype(I32)
    info_ref[...] = jnp.where(lane == 0, i1, jnp.where(lane == 1, i2,
                              jnp.where(lane == 2, r1, jnp.where(lane == 3, r2, 0))))


def _router(h, g_ffn, w_rg, b_rg, w_re, b_re):
    n, d = h.shape
    n_log = N_GROUPS + N_EXPERTS
    wr = jnp.concatenate([w_rg, w_re, jnp.zeros((d, LANES - n_log), F32)], axis=1)
    br = jnp.concatenate([b_rg, b_re, jnp.zeros((LANES - n_log,), F32)]).reshape(1, LANES)
    lane_spec = pl.BlockSpec((TM_ROW, LANES), lambda i: (i, 0))
    return pl.pallas_call(
        _router_kernel,
        grid=(n // TM_ROW,),
        in_specs=[pl.BlockSpec((TM_ROW, d), lambda i: (i, 0)),
                  pl.BlockSpec((1, d), lambda i: (0, 0)),
                  pl.BlockSpec((d, LANES), lambda i: (0, 0)),
                  pl.BlockSpec((1, LANES), lambda i: (0, 0))],
        out_specs=[lane_spec, lane_spec,
                   pl.BlockSpec((TM_ROW, d // LANES, LANES), lambda i: (i, 0, 0)),
                   pl.BlockSpec((1, LANES), lambda i: (0, 0))],
        out_shape=[jax.ShapeDtypeStruct((n, LANES), I32), jax.ShapeDtypeStruct((n, LANES), F32),
                   jax.ShapeDtypeStruct((n, d // LANES, LANES), BF16), jax.ShapeDtypeStruct((1, LANES), I32)],
        scratch_shapes=[pltpu.VMEM((1, LANES), F32)],
        compiler_params=_params(1, 40),
        name="router",
    )(h, g_ffn.reshape(1, d), wr, br)


def _dispatch_plan(info, counts, n_tiles):
    counts = counts[0, :N_EXPERTS]
    tiles_per = (counts + TM_E - 1) // TM_E
    tiles_end = jnp.cumsum(tiles_per)
    row_start = (tiles_end - tiles_per) * TM_E
    eid, rank = info[:, :TOP_K], info[:, TOP_K:2 * TOP_K]
    onehot = (eid[:, :, None] == jnp.arange(N_EXPERTS, dtype=I32)[None, None, :]).astype(I32)
    pos = (jnp.sum(onehot * row_start[None, None, :], axis=-1) + rank).reshape(-1).astype(I32)
    n_used = tiles_end[-1:].astype(I32)
    tile_ids = jnp.minimum(jnp.arange(n_tiles, dtype=I32), n_used - 1)
    tile_expert = jnp.sum((tile_ids[:, None] >= tiles_end[None, :]).astype(I32), axis=1).astype(I32)
    last_tile_row = (jnp.clip(tiles_end - 1, 0, n_tiles - 1) * TM_E).astype(I32)
    starts = jnp.concatenate([jnp.ones((1,), I32), (tile_expert[1:] != tile_expert[:-1]).astype(I32)])
    run_idx = jnp.cumsum(starts) - 1
    n_runs = run_idx[-1:] + 1
    run_expert = jnp.zeros((n_tiles,), I32).at[run_idx].set(tile_expert)
    next_expert = run_expert[(run_idx + 1) % n_runs]
    rows_end = row_start + counts
    valid = jnp.clip(rows_end[tile_expert] - jnp.arange(n_tiles, dtype=I32) * TM_E, 0, TM_E)
    plan = (tile_expert, n_used, run_idx.astype(I32), next_expert.astype(I32), n_runs.astype(I32),
            valid.astype(I32))
    return pos, plan, last_tile_row


def _dispatch_kernel(pos_ref, ltr_ref, nu_ref, xf_ref, xs_hbm, zero_ref, sem):
    i = pl.program_id(0)
    n_tiles = xs_hbm.shape[0] // TM_E

    def clear_tile(row):
        return pltpu.make_async_copy(zero_ref, xs_hbm.at[pl.ds(pl.multiple_of(row, TM_E), TM_E)], sem)

    @pl.when(i == 0)
    def _():
        zero_ref[...] = jnp.zeros_like(zero_ref)
        for e in range(N_EXPERTS):
            clear_tile(ltr_ref[e]).start()

        def start_unused(t, carry):
            clear_tile(t * TM_E).start()
            return carry

        def wait_one(t, carry):
            clear_tile(0).wait()
            return carry

        lax.fori_loop(nu_ref[0], n_tiles, start_unused, 0)
        lax.fori_loop(nu_ref[0] - N_EXPERTS, n_tiles, wait_one, 0)

    base = i * TM_DISPATCH

    def issue(r, carry):
        for k in range(TOP_K):
            dst_row = pos_ref[(base + r) * TOP_K + k]
            pltpu.make_async_copy(xf_ref.at[pl.ds(r, 1)], xs_hbm.at[pl.ds(dst_row, 1)], sem).start(priority=k % 2)
        return carry

    lax.fori_loop(0, TM_DISPATCH, issue, 0, unroll=4)
    for k in range(TOP_K):
        pltpu.make_async_copy(xf_ref, xs_hbm.at[pl.ds(0, TM_DISPATCH)], sem).wait()


def _dispatch(pos, last_tile_row, n_used, xf3, n_tiles):
    n, c, l = xf3.shape
    return pl.pallas_call(
        _dispatch_kernel,
        grid_spec=pltpu.PrefetchScalarGridSpec(
            num_scalar_prefetch=3,
            grid=(n // TM_DISPATCH,),
            in_specs=[pl.BlockSpec((TM_DISPATCH, c, l), lambda i, *_: (i, 0, 0))],
            out_specs=pl.BlockSpec(memory_space=pl.ANY),
            scratch_shapes=[pltpu.VMEM((TM_E, c, l), xf3.dtype), pltpu.SemaphoreType.DMA(())]),
        out_shape=jax.ShapeDtypeStruct((n_tiles * TM_E, c, l), xf3.dtype),
        compiler_params=_params(1, 24),
        name="moe_dispatch",
    )(pos, last_tile_row, n_used, xf3)


def _expert_weights_step(plan_refs, w_hbms, wbuf_ref, wb_refs, sem, tn):
    te_ref, _, run_ref, nxt_ref, nr_ref = plan_refs
    j, t = pl.program_id(0), pl.program_id(1)

    def copies(e, jj):
        col = pl.multiple_of(jj * tn, tn)
        return [pltpu.make_async_copy(w.at[e, :, pl.ds(col, tn)], wbuf_ref.at[m], sem)
                for m, w in enumerate(w_hbms)]

    @pl.when((t == 0) | (te_ref[t] != te_ref[jnp.maximum(t - 1, 0)]))
    def _():
        @pl.when((j == 0) & (t == 0))
        def _():
            for c in copies(te_ref[t], j):
                c.start()

        for c in copies(te_ref[t], j):
            c.wait()
        for m, wb_ref in enumerate(wb_refs):
            _cast_rows(wbuf_ref.at[m], wb_ref)

        next_j = jnp.where(run_ref[t] + 1 == nr_ref[0], j + 1, j)

        @pl.when(next_j < pl.num_programs(0))
        def _():
            for c in copies(nxt_ref[t], next_j):
                c.start()


def _by_valid_rows(valid, full_fn, zero_fn):
    heights = (TM_E // 2, 3 * TM_E // 4, TM_E)
    lo = 0
    for rows in heights:
        @pl.when((valid > lo) & (valid <= rows))
        def _(rows=rows):
            full_fn(rows)
            if rows < TM_E:
                zero_fn(rows)
        lo = rows

    @pl.when(valid == 0)
    def _():
        zero_fn(0)


def _moe_up_kernel(te_ref, nu_ref, run_ref, nxt_ref, nr_ref, valid_ref, xs_ref, wg_hbm, wu_hbm, o_ref,
                   wbuf_ref, wgb_ref, wub_ref, sem):
    t = pl.program_id(1)
    _expert_weights_step((te_ref, nu_ref, run_ref, nxt_ref, nr_ref), (wg_hbm, wu_hbm), wbuf_ref,
                         (wgb_ref, wub_ref), sem, o_ref.shape[1])

    def compute(rows):
        x = xs_ref[0:rows].reshape(rows, -1)
        gate = jnp.dot(x, wgb_ref[...], preferred_element_type=F32)
        up = jnp.dot(x, wub_ref[...], preferred_element_type=F32)
        o_ref[0:rows] = (jax.nn.silu(gate) * up).astype(o_ref.dtype)

    def clear(row0):
        o_ref[row0:] = jnp.zeros((TM_E - row0,) + o_ref.shape[1:], o_ref.dtype)

    _by_valid_rows(valid_ref[t], compute, clear)


def _moe_up(plan, xs3, w_gate, w_up, n_tiles):
    d, f = w_gate.shape[1], w_gate.shape[2]
    tf = TN
    any_spec = pl.BlockSpec(memory_space=pl.ANY)
    return pl.pallas_call(
        _moe_up_kernel,
        grid_spec=pltpu.PrefetchScalarGridSpec(
            num_scalar_prefetch=len(plan),
            grid=(f // tf, n_tiles),
            in_specs=[pl.BlockSpec((TM_E,) + xs3.shape[1:], lambda j, t, te, nu, *_: (jnp.minimum(t, nu[0] - 1), 0, 0)),
                      any_spec, any_spec],
            out_specs=pl.BlockSpec((TM_E, tf), lambda j, t, *_: (t, j)),
            scratch_shapes=[pltpu.VMEM((2, d, tf), F32), pltpu.VMEM((d, tf), BF16), pltpu.VMEM((d, tf), BF16),
                            pltpu.SemaphoreType.DMA(())]),
        out_shape=jax.ShapeDtypeStruct((n_tiles * TM_E, f), BF16),
        compiler_params=_params(2, 56),
        name="moe_up",
    )(*plan, xs3, w_gate, w_up)


def _moe_down_kernel(te_ref, nu_ref, run_ref, nxt_ref, nr_ref, valid_ref, hid_ref, wd_hbm, o_ref,
                     wbuf_ref, wdb_ref, sem):
    t = pl.program_id(1)
    _expert_weights_step((te_ref, nu_ref, run_ref, nxt_ref, nr_ref), (wd_hbm,), wbuf_ref, (wdb_ref,), sem,
                         wdb_ref.shape[1])

    def compute(rows):
        y = jnp.dot(hid_ref[0:rows], wdb_ref[...], preferred_element_type=F32)
        o_ref[0:rows] = y.reshape((rows,) + o_ref.shape[1:])

    def clear(row0):
        o_ref[row0:] = jnp.zeros((TM_E - row0,) + o_ref.shape[1:], o_ref.dtype)

    _by_valid_rows(valid_ref[t], compute, clear)


def _moe_down(plan, hid, w_down, n_tiles):
    f, d = w_down.shape[1], w_down.shape[2]
    tn = 4 * TN
    return pl.pallas_call(
        _moe_down_kernel,
        grid_spec=pltpu.PrefetchScalarGridSpec(
            num_scalar_prefetch=len(plan),
            grid=(d // tn, n_tiles),
            in_specs=[pl.BlockSpec((TM_E, f), lambda j, t, *_: (t, 0)),
                      pl.BlockSpec(memory_space=pl.ANY)],
            out_specs=pl.BlockSpec((TM_E, tn // LANES, LANES), lambda j, t, *_: (t, j, 0)),
            scratch_shapes=[pltpu.VMEM((1, f, tn), F32), pltpu.VMEM((f, tn), BF16),
                            pltpu.SemaphoreType.DMA(())]),
        out_shape=jax.ShapeDtypeStruct((n_tiles * TM_E, d // LANES, LANES), F32),
        compiler_params=_params(2, 40),
        name="moe_down",
    )(*plan, hid, w_down)


def _combine_kernel(pos_ref, h_ref, cw_ref, y_hbm, g_ref, ho_ref, xo_ref, buf_ref, sem):
    i = pl.program_id(0)
    tm, d = h_ref.shape

    def row_copy(slot, k, r, src_row):
        return pltpu.make_async_copy(y_hbm.at[pl.ds(src_row, 1)], buf_ref.at[slot, k, pl.ds(r, 1)], sem.at[slot])

    def issue_step(step, slot):
        def body(r, carry):
            for k in range(TOP_K):
                row_copy(slot, k, r, pos_ref[(step * tm + r) * TOP_K + k]).start(priority=k % 2)
            return carry
        lax.fori_loop(0, tm, body, 0, unroll=4)

    @pl.when(i == 0)
    def _():
        issue_step(0, 0)

    @pl.when(i + 1 < pl.num_programs(0))
    def _():
        issue_step(i + 1, (i + 1) % 2)

    slot = i % 2
    for k in range(TOP_K):
        pltpu.make_async_copy(y_hbm.at[pl.ds(0, tm)], buf_ref.at[slot, k], sem.at[slot]).wait()
    cw = cw_ref[...]
    y1 = buf_ref[slot, 0].reshape(tm, d)
    y2 = buf_ref[slot, 1].reshape(tm, d)
    h2 = h_ref[...] + (cw[:, 0:1] * y1 + cw[:, 1:2] * y2)
    ho_ref[...] = h2
    xo_ref[...] = _rms_rows(h2, g_ref[...]).astype(xo_ref.dtype)


def _combine(pos, h, cw, y3, g_ple):
    n, d = h.shape
    row_spec = pl.BlockSpec((TM_ROW, d), lambda i, p: (i, 0))
    return pl.pallas_call(
        _combine_kernel,
        grid_spec=pltpu.PrefetchScalarGridSpec(
            num_scalar_prefetch=1,
            grid=(n // TM_ROW,),
            in_specs=[row_spec,
                      pl.BlockSpec((TM_ROW, LANES), lambda i, p: (i, 0)),
                      pl.BlockSpec(memory_space=pl.ANY),
                      pl.BlockSpec((1, d), lambda i, p: (0, 0))],
            out_specs=[row_spec, row_spec],
            scratch_shapes=[pltpu.VMEM((2, TOP_K, TM_ROW) + y3.shape[1:], F32), pltpu.SemaphoreType.DMA((2,))]),
        out_shape=[jax.ShapeDtypeStruct((n, d), F32), jax.ShapeDtypeStruct((n, d), BF16)],
        compiler_params=_params(1, 56),
        name="moe_combine",
    )(pos, h, cw, y3, g_ple.reshape(1, d))


def _ple_kernel(xn_ref, wg_hbm, bg_ref, pa_ref, pb_ref, wp_hbm, h_ref, o_ref,
                wgf_ref, wpf_ref, wgb_ref, wpb_ref, sem, *, nb_first):
    _column_weights_step((wg_hbm, wp_hbm), (0, 0), (wgf_ref, wpf_ref), (wgb_ref, wpb_ref), sem)
    p = jnp.where(pl.program_id(1) < nb_first, pa_ref[...], pb_ref[...]).astype(BF16)
    tn = o_ref.shape[1]
    step = min(tn, TN)
    for c in range(0, tn, step):
        cols = slice(c, c + step)
        gate = jax.nn.sigmoid(jnp.dot(xn_ref[...], wgb_ref[:, cols], preferred_element_type=F32) + bg_ref[:, cols])
        ple = jnp.dot(p, wpb_ref[:, cols], preferred_element_type=F32)
        o_ref[:, cols] = h_ref[:, cols] + gate * ple


def _ple(xn, w_gate, b_gate, pa, pb, w_proj, h):
    n, d = h.shape
    pd = pa.shape[1]
    tm, tn = TM_WIDE, TN_WIDE
    nb_first = pa.shape[0] // tm
    p_first = pl.BlockSpec((tm, pd), lambda j, i: (jnp.minimum(i, nb_first - 1), 0))
    p_last = pl.BlockSpec((tm, pd), lambda j, i: (jnp.maximum(i - nb_first, 0), 0))
    any_spec = pl.BlockSpec(memory_space=pl.ANY)
    return pl.pallas_call(
        functools.partial(_ple_kernel, nb_first=nb_first),
        grid=(d // tn, n // tm),
        in_specs=[pl.BlockSpec((tm, d), lambda j, i: (i, 0)),
                  any_spec,
                  pl.BlockSpec((1, tn), lambda j, i: (0, j)),
                  p_first, p_last,
                  any_spec,
                  pl.BlockSpec((tm, tn), lambda j, i: (i, j))],
        out_specs=pl.BlockSpec((tm, tn), lambda j, i: (i, j)),
        out_shape=jax.ShapeDtypeStruct((n, d), F32),
        scratch_shapes=_weight_scratch([(d, tn), (pd, tn)]),
        compiler_params=_params(2, 58),
        name="ple",
    )(xn, w_gate, b_gate.reshape(1, d), pa, pb, w_proj, h)


def _layer(xn, h_p, h_s, p_p, p_s, cache_k, cache_v, state_pool, lw):
    n, d = xn.shape
    n_prompt = h_p.shape[0]
    n_seq, n_cache, n_kv, _ = cache_k.shape
    t_new = (n - n_prompt) // n_seq
    kv_w = n_kv * HEAD_DIM
    attn_w = lw['w_attn_up'].shape[0]
    pool_w = lw['w_pool_up'].shape[0]
    n_pool = state_pool.shape[1]

    qkv = _in_proj(xn, lw['w_in'], 0, attn_w + 2 * kv_w, "in_proj_qkv")
    z = _in_proj(xn, lw['w_in'], attn_w + 2 * kv_w, pool_w, "in_proj_pool")
    gates = _in_proj(xn, lw['w_in'], attn_w + 2 * kv_w + pool_w, 2 * d, "in_proj_gates")

    tab_s = tuple(jnp.tile(a, (SEQS_PER_STEP, 1))
                  for a in _rope_tables(PAST_LEN + jnp.arange(t_new, dtype=F32)))
    tab_p_t = _rope_tables_t(jnp.arange(n_prompt, dtype=F32))
    attn, k_rot_p = _attn_prompt(qkv, lw['attn_sinks'], tab_p_t, n_prompt, attn_w, kv_w)
    attn, new_k_s, new_v_s = _attn_sample(
        qkv, attn, n_prompt, cache_k.reshape(n_seq, n_cache, kv_w), cache_v.reshape(n_seq, n_cache, kv_w),
        lw['attn_sinks'], tab_s, n_seq, t_new, attn_w, kv_w)

    mixed = _pool_prompt(z, n_prompt, lw['w_pool_grp'], lw['pool_scale'])
    mixed, new_z_s = _pool_sample(state_pool, z, mixed, n_prompt, lw['w_pool_grp'], lw['pool_scale'], t_new)

    merged = _merge(attn, mixed, lw['w_attn_up'], lw['w_pool_up'], gates, lw['b_gate'].reshape(1, 2 * d))
    h1 = _out_proj(merged, lw['w_out'], h_p, h_s)

    n_tiles = (n * TOP_K) // TM_E + N_EXPERTS
    info, cw, xf3, counts = _router(h1, lw['g_ffn'], lw['w_route_group'], lw['b_route_group'],
                                    lw['w_route_expert'], lw['b_route_expert'])
    pos, plan, last_tile_row = _dispatch_plan(info, counts, n_tiles)
    xs3 = _dispatch(pos, last_tile_row, plan[1], xf3, n_tiles)
    hid = _moe_up(plan, xs3, lw['w_exp_gate'], lw['w_exp_up'], n_tiles)
    y3 = _moe_down(plan, hid, lw['w_exp_down'], n_tiles)
    h2, xn2 = _combine(pos, h1, cw, y3, lw['g_ple'])

    h3 = _ple(xn2, lw['w_ple_gate'], lw['b_ple_gate'], p_p, p_s, lw['w_ple_proj'], h2)

    new_k_p = k_rot_p[n_prompt - n_cache:].reshape(1, n_cache, n_kv, HEAD_DIM)
    new_v_p = qkv[n_prompt - n_cache:n_prompt, attn_w + kv_w:].reshape(1, n_cache, n_kv, HEAD_DIM)
    new_z_p = z[n_prompt - n_pool:n_prompt].reshape(1, n_pool, pool_w)
    return (h3, new_k_p, new_v_p, new_z_p,
            new_k_s.reshape(cache_k.shape), new_v_s.reshape(cache_v.shape), new_z_s)


def kernel(x_prompt, x_sample, cache_k, cache_v, state_pool, p_prompt, p_sample, g_mix, w_in, b_gate, attn_sinks, w_pool_grp, pool_scale, w_attn_up, w_pool_up, w_out, g_ffn, w_route_group, b_route_group, w_route_expert, b_route_expert, w_exp_gate, w_exp_up, w_exp_down, g_ple, w_ple_gate, b_ple_gate, w_ple_proj, g_final):
    batch, seq, d = x_prompt.shape
    n_seq, t_new, _ = x_sample.shape
    depth = w_in.shape[0]
    assert batch == 1, "prompt rows are treated as one sequence"
    n_prompt = batch * seq
    n_sample = n_seq * t_new
    n = n_prompt + n_sample
    assert n_prompt % TM == 0 and n_sample % TM == 0 and (n * TOP_K) % TM_E == 0 and n % TM_DISPATCH == 0
    assert n_seq % SEQS_PER_STEP == 0 and n_seq % POOL_SEQS == 0 and t_new & (t_new - 1) == 0
    assert cache_k.shape[2] == WINDOW and state_pool.shape[2] == POOL_HALO - 1
    assert depth == 1, "one layer per step"

    weights = dict(g_mix=g_mix, w_in=w_in, b_gate=b_gate, attn_sinks=attn_sinks, w_pool_grp=w_pool_grp,
                   pool_scale=pool_scale, w_attn_up=w_attn_up, w_pool_up=w_pool_up, w_out=w_out, g_ffn=g_ffn,
                   w_route_group=w_route_group, b_route_group=b_route_group, w_route_expert=w_route_expert,
                   b_route_expert=b_route_expert, w_exp_gate=w_exp_gate, w_exp_up=w_exp_up,
                   w_exp_down=w_exp_down, g_ple=g_ple, w_ple_gate=w_ple_gate, b_ple_gate=b_ple_gate,
                   w_ple_proj=w_ple_proj)
    lw = {name: w[0] for name, w in weights.items()}
    h_p = x_prompt.reshape(n_prompt, d)
    h_s = x_sample.reshape(n_sample, d)
    xn = _rmsnorm_in2(h_p, h_s, lw['g_mix'], BF16, "norm_mix")
    out = _layer(xn, h_p, h_s, p_prompt[0].reshape(n_prompt, -1), p_sample[0].reshape(n_sample, -1),
                 cache_k[0], cache_v[0], state_pool[0], lw)
    y_p, y_s = _rmsnorm_out2(out[0], g_final, n_prompt, "norm_final")
    return (y_p.reshape(batch, seq, d), y_s.reshape(n_seq, t_new, d)) + tuple(piece[None] for piece in out[1:])
```
